```python
import math
import jax, jax.numpy as jnp
from jax import lax
import numpy as np

D_MODEL = 1024
BATCH = 2
SEQ = 16384
DEPTH = 2

CHUNK = 64
N_META = 16
META_PAD = CHUNK - N_META
EPS = 1e-6
ROPE_THETA = 500000.0
Q_BLOCK = 128

A_HEADS = 4
A_DK = 64
A_DV = 2 * A_DK
A_ROT = A_DK // 4
A_QK_W = A_HEADS * 2 * A_DK
A_W = A_HEADS * A_DV

B_HEADS = 4
B_DK = 128
B_DV = 128
B_W = B_HEADS * B_DV
CONV_K = 4

C_HEADS = 8
C_DK = D_MODEL // C_HEADS
C_DV = D_MODEL // C_HEADS

D_FF = 2816
N_EXPERTS = 8
TOP_K = 2
D_FF_EXPERT = 3584
MOE_BLOCK = 256

N_EVEN = (DEPTH + 1) // 2
N_ODD = DEPTH // 2
EVEN_IN = 2 * A_QK_W + A_W + 4 * B_W + 2 * B_HEADS
ODD_IN = 4 * D_MODEL

kernel_name = 'hybrid_diffattn_gdn_hgrn2_moe'


def rmsnorm(x, g):
    xf = x.astype(jnp.float32)
    y = xf * lax.rsqrt(jnp.mean(xf * xf, axis=-1, keepdims=True) + EPS)
    return (y * g.astype(jnp.float32)).astype(x.dtype)


def l2norm(x):
    return x * lax.rsqrt(jnp.sum(x * x, axis=-1, keepdims=True) + EPS)


def chunk_ids(n):
    return (jnp.arange(n) + META_PAD) // CHUNK


def rope_partial(x, pos):
    half = A_ROT // 2
    inv_freq = ROPE_THETA ** (-jnp.arange(half, dtype=jnp.float32) * 2.0 / A_ROT)
    ang = pos.astype(jnp.float32)[:, None] * inv_freq[None, :]
    cos = jnp.cos(ang)[:, None, :].astype(x.dtype)
    sin = jnp.sin(ang)[:, None, :].astype(x.dtype)
    x1, x2, rest = x[..., :half], x[..., half:A_ROT], x[..., A_ROT:]
    return jnp.concatenate([x1 * cos - x2 * sin, x2 * cos + x1 * sin, rest], axis=-1)


def causal_conv(x, w):
    c = x.shape[-1]
    return lax.conv_general_dilated(x, w[:, None, :].astype(x.dtype), window_strides=(1,),
                                    padding=[(CONV_K - 1, 0)],
                                    dimension_numbers=('NWC', 'WIO', 'NWC'),
                                    feature_group_count=c)


def pad_front(t, n):
    return jnp.pad(t, [(0, 0), (n, 0)] + [(0, 0)] * (t.ndim - 2))


def to_chunks(t):
    b, lc, h, d = t.shape
    return t.reshape(b, lc // CHUNK, CHUNK, h, d).transpose(1, 0, 3, 2, 4)


def from_chunks(t):
    n, b, h, c, d = t.shape
    return t.transpose(1, 0, 3, 2, 4).reshape(b, n * c, h, d)


def diff_attention(q, k, v, q_gain, k_gain, lq1, lk1, lq2, lk2, sub_gain, layer):
    bn, L = q.shape[0], q.shape[1]
    pos = jnp.arange(L)
    q = rope_partial(rmsnorm(q, q_gain), pos)
    k = rope_partial(rmsnorm(k, k_gain), pos)
    n_blk = -(-L // Q_BLOCK)
    lp = n_blk * Q_BLOCK
    pad = [(0, 0), (0, lp - L), (0, 0), (0, 0)]
    q, k, v = jnp.pad(q, pad), jnp.pad(k, pad), jnp.pad(v, pad)
    cid = chunk_ids(lp)
    qh = q.transpose(0, 2, 1, 3)
    kh = k.transpose(0, 2, 1, 3)
    vh = v.transpose(0, 2, 1, 3)
    lam_init = 0.8 - 0.6 * math.exp(-0.3 * layer)
    f32 = jnp.float32
    lam = (jnp.exp(jnp.sum(lq1.astype(f32) * lk1.astype(f32)))
           - jnp.exp(jnp.sum(lq2.astype(f32) * lk2.astype(f32))) + lam_init)
    scale = A_DK ** -0.5
    q_blocks = qh.reshape(bn, 2 * A_HEADS, n_blk, Q_BLOCK, A_DK).transpose(2, 0, 1, 3, 4)
    c_blocks = cid.reshape(n_blk, Q_BLOCK)

    def block(args):
        qb, cb = args
        s = jnp.einsum('bhqd,bhkd->bhqk', qb, kh).astype(f32) * scale
        s = jnp.where(cid[None, :] <= cb[:, None], s, -jnp.inf)
        p = jax.nn.softmax(s, axis=-1).reshape(bn, A_HEADS, 2, Q_BLOCK, lp)
        attn = (p[:, :, 0] - lam * p[:, :, 1]).astype(vh.dtype)
        return jnp.einsum('bhqk,bhkd->bhqd', attn, vh)

    o = lax.map(block, (q_blocks, c_blocks))
    o = o.transpose(1, 0, 3, 2, 4).reshape(bn, lp, A_HEADS, A_DV)[:, :L]
    o = rmsnorm(o, sub_gain) * (1.0 - lam_init)
    return o.reshape(bn, L, A_W)


def gated_delta_rule(q, k, v, beta, g):
    f32 = jnp.float32
    q, k, v = (pad_front(t.astype(f32), META_PAD) for t in (q, k, v))
    beta, g = (pad_front(t.astype(f32), META_PAD)[..., None] for t in (beta, g))
    qc, kc, vc = to_chunks(q), to_chunks(k), to_chunks(v)
    bc, gc = to_chunks(beta)[..., 0], to_chunks(g)[..., 0]
    dv = vc.shape[-1]
    gcum = jnp.cumsum(gc, axis=-1)
    idx = jnp.arange(CHUNK)
    incl = idx[:, None] >= idx[None, :]
    strict = idx[:, None] > idx[None, :]
    dec = jnp.exp(jnp.where(incl, gcum[..., :, None] - gcum[..., None, :], -jnp.inf))
    kb = kc * bc[..., None]
    m = jnp.einsum('nbhid,nbhjd->nbhij', kb, kc) * jnp.where(strict, dec, 0.0)
    a = m + jnp.eye(CHUNK, dtype=f32)
    rhs = jnp.concatenate([vc * bc[..., None], kb * jnp.exp(gcum)[..., None]], axis=-1)
    sol = lax.linalg.triangular_solve(a, rhs, left_side=True, lower=True, unit_diagonal=True)
    u, w = sol[..., :dv], sol[..., dv:]
    qk = jnp.einsum('nbhid,nbhjd->nbhij', qc, kc) * dec
    qg = qc * jnp.exp(gcum)[..., None]
    glast = gcum[..., -1]
    kdec = kc * jnp.exp(glast[..., None] - gcum)[..., None]

    def step(S, xs):
        u_n, w_n, qk_n, qg_n, kdec_n, gl_n = xs
        v_new = u_n - jnp.einsum('bhcd,bhde->bhce', w_n, S)
        o = jnp.einsum('bhcd,bhde->bhce', qg_n, S) + jnp.einsum('bhij,bhje->bhie', qk_n, v_new)
        S = S * jnp.exp(gl_n)[..., None, None] + jnp.einsum('bhcd,bhce->bhde', kdec_n, v_new)
        return S, o

    s0 = jnp.zeros(qc.shape[1:3] + (qc.shape[-1], dv), f32)
    _, o = lax.scan(step, s0, (u, w, qk, qg, kdec, glast))
    return from_chunks(o)[:, META_PAD:]


def hgrn2_recurrence(q, k, i, logf):
    f32 = jnp.float32
    q, k, i, logf = (pad_front(t.astype(f32), META_PAD) for t in (q, k, i, logf))
    qc, kc, ic, lc = to_chunks(q), to_chunks(k), to_chunks(i), to_chunks(logf)
    gcum = jnp.cumsum(lc, axis=-2)
    glast = gcum[..., -1, :]
    qg = qc * jnp.exp(gcum)
    kdec = kc * jnp.exp(glast[..., None, :] - gcum)
    idx = jnp.arange(CHUNK)
    incl = (idx[:, None] >= idx[None, :])[:, :, None]

    def step(S, xs):
        q_n, k_n, i_n, gc_n, qg_n, kdec_n, gl_n = xs
        dec = jnp.exp(jnp.where(incl, gc_n[:, :, :, None, :] - gc_n[:, :, None, :, :], -jnp.inf))
        att = jnp.einsum('bhtsd,bhsd->bhts', dec * q_n[:, :, :, None, :], k_n)
        o = jnp.einsum('bhtd,bhde->bhte', qg_n, S) + jnp.einsum('bhts,bhse->bhte', att, i_n)
        S = jnp.exp(gl_n)[..., None] * S + jnp.einsum('bhsd,bhse->bhde', kdec_n, i_n)
        return S, o

    s0 = jnp.zeros(qc.shape[1:3] + (qc.shape[-1], ic.shape[-1]), f32)
    _, o = lax.scan(step, s0, (qc, kc, ic, gcum, qg, kdec, glast))
    return from_chunks(o)[:, META_PAD:]


def even_mixer(h, w_in, a_q_gain, a_k_gain, lq1, lk1, lq2, lk2, a_sub_gain,
               b_conv, b_a_log, b_dt_bias, b_out_gain, w_out, layer):
    bn, L, _ = h.shape
    proj = h @ w_in
    cuts = [A_QK_W, 2 * A_QK_W, 2 * A_QK_W + A_W, 2 * A_QK_W + A_W + 3 * B_W,
            2 * A_QK_W + A_W + 4 * B_W, 2 * A_QK_W + A_W + 4 * B_W + B_HEADS]
    aq, ak, av, bqkv, bz, bb, ba = jnp.split(proj, cuts, axis=-1)
    o_a = diff_attention(aq.reshape(bn, L, 2 * A_HEADS, A_DK), ak.reshape(bn, L, 2 * A_HEADS, A_DK),
                         av.reshape(bn, L, A_HEADS, A_DV), a_q_gain, a_k_gain, lq1, lk1, lq2, lk2,
                         a_sub_gain, layer)
    bqkv = jax.nn.silu(causal_conv(bqkv, b_conv))
    bq, bk, bv = jnp.split(bqkv, 3, axis=-1)
    f32 = jnp.float32
    shp = (bn, L, B_HEADS, B_DK)
    bq = l2norm(bq.reshape(shp).astype(f32)) * (B_DK ** -0.5)
    bk = l2norm(bk.reshape(shp).astype(f32))
    beta = jax.nn.sigmoid(bb.astype(f32))
    g = -jnp.exp(b_a_log.astype(f32)) * jax.nn.softplus(ba.astype(f32) + b_dt_bias.astype(f32))
    o_b = gated_delta_rule(bq, bk, bv.reshape(bn, L, B_HEADS, B_DV), beta, g).astype(h.dtype)
    o_b = rmsnorm(o_b, b_out_gain) * jax.nn.silu(bz.reshape(bn, L, B_HEADS, B_DV))
    o = jnp.concatenate([o_a, o_b.reshape(bn, L, B_W)], axis=-1)
    return o @ w_out


def odd_mixer(h, w_in, c_out_gain, w_out, lb):
    bn, L, _ = h.shape
    q, f, i, z = jnp.split(h @ w_in, 4, axis=-1)
    shp = (bn, L, C_HEADS, C_DK)
    lbh = lb.reshape(C_HEADS, C_DK)
    fg = lbh + (1.0 - lbh) * jax.nn.sigmoid(f.reshape(shp).astype(jnp.float32))
    o = hgrn2_recurrence(q.reshape(shp), 1.0 - fg, i.reshape(bn, L, C_HEADS, C_DV), jnp.log(fg))
    o = rmsnorm(o.astype(h.dtype), c_out_gain) * jax.nn.silu(z.reshape(bn, L, C_HEADS, C_DV))
    return o.reshape(bn, L, D_MODEL) @ w_out


def swiglu(x, w1, w3, w2):
    return (jax.nn.silu(x @ w1) * (x @ w3)) @ w2


def moe_swiglu(x, router, w1, w3, w2):
    bn, L, d = x.shape
    xf = x.reshape(-1, d)
    t = xf.shape[0]
    na = t * TOP_K
    logits = (xf @ router).astype(jnp.float32)
    top_v, top_i = lax.top_k(logits, TOP_K)
    gate = jax.nn.softmax(top_v, axis=-1)
    flat_e = top_i.reshape(-1).astype(jnp.int32)
    flat_tok = jnp.arange(na, dtype=jnp.int32) // TOP_K
    flat_g = gate.reshape(-1)
    order = jnp.argsort(flat_e)
    sorted_e = flat_e[order]
    counts = jnp.zeros(N_EXPERTS, jnp.int32).at[flat_e].add(1)
    padded = (counts + MOE_BLOCK - 1) // MOE_BLOCK * MOE_BLOCK
    start = jnp.cumsum(counts) - counts
    pend = jnp.cumsum(padded)
    pstart = pend - padded
    dest = pstart[sorted_e] + (jnp.arange(na, dtype=jnp.int32) - start[sorted_e])
    n_rows = -(-(na + N_EXPERTS * (MOE_BLOCK - 1)) // MOE_BLOCK) * MOE_BLOCK
    n_blk = n_rows // MOE_BLOCK
    row_tok = jnp.zeros(n_rows, jnp.int32).at[dest].set(flat_tok[order])
    row_gate = jnp.zeros(n_rows, jnp.float32).at[dest].set(flat_g[order])
    blk_e = jnp.minimum(jnp.searchsorted(pend, jnp.arange(n_blk, dtype=jnp.int32) * MOE_BLOCK,
                                         side='right'), N_EXPERTS - 1)

    def block(args):
        tok, gb, e = args
        xb = xf[tok]
        hb = jax.nn.silu(xb @ w1[e]) * (xb @ w3[e])
        return (hb @ w2[e]) * gb[:, None].astype(x.dtype)

    y = lax.map(block, (row_tok.reshape(n_blk, MOE_BLOCK), row_gate.reshape(n_blk, MOE_BLOCK), blk_e))
    out = jnp.zeros_like(xf).at[row_tok].add(y.reshape(n_rows, d))
    return out.reshape(bn, L, d)


def setup_inputs(seed: int = 0) -> dict:
    key = jax.random.key(seed)
    keys = jax.random.split(key, 40)
    ctr = [0]

    def nk():
        ctr[0] += 1
        return keys[ctr[0] - 1]

    def nrm(shape, scale):
        return jax.random.normal(nk(), shape, jnp.float32) * scale

    def gain(shape):
        return 1.0 + nrm(shape, 0.02)

    D = D_MODEL
    dt = jnp.exp(jax.random.uniform(nk(), (N_EVEN, B_HEADS), jnp.float32,
                                    minval=math.log(1e-3), maxval=math.log(1e-1)))
    return {
        'x': nrm((BATCH, SEQ, D), 1.0),
        'meta_tokens': nrm((N_META, D), 1.0),
        'lb_logits': nrm((DEPTH, D), 0.5),
        'mix_norm': gain((DEPTH, D)),
        'ffn_norm': gain((DEPTH, D)),
        'e_w_in': nrm((N_EVEN, D, EVEN_IN), D ** -0.5),
        'a_q_gain': gain((N_EVEN, A_DK)),
        'a_k_gain': gain((N_EVEN, A_DK)),
        'a_lam_q1': nrm((N_EVEN, A_DK), 0.1),
        'a_lam_k1': nrm((N_EVEN, A_DK), 0.1),
        'a_lam_q2': nrm((N_EVEN, A_DK), 0.1),
        'a_lam_k2': nrm((N_EVEN, A_DK), 0.1),
        'a_sub_gain': gain((N_EVEN, A_DV)),
        'b_conv': nrm((N_EVEN, CONV_K, 3 * B_W), CONV_K ** -0.5),
        'b_a_log': jnp.log(jax.random.uniform(nk(), (N_EVEN, B_HEADS), jnp.float32, minval=1.0, maxval=16.0)),
        'b_dt_bias': dt + jnp.log(-jnp.expm1(-dt)),
        'b_out_gain': gain((N_EVEN, B_DV)),
        'e_w_out': nrm((N_EVEN, A_W + B_W, D), (A_W + B_W) ** -0.5),
        'ffn_w1': nrm((N_EVEN, D, D_FF), D ** -0.5),
        'ffn_w3': nrm((N_EVEN, D, D_FF), D ** -0.5),
        'ffn_w2': nrm((N_EVEN, D_FF, D), D_FF ** -0.5),
        'o_w_in': nrm((N_ODD, D, ODD_IN), D ** -0.5),
        'c_out_gain': gain((N_ODD, C_DV)),
        'o_w_out': nrm((N_ODD, D, D), D ** -0.5),
        'router': nrm((N_ODD, D, N_EXPERTS), D ** -0.5),
        'moe_w1': nrm((N_ODD, N_EXPERTS, D, D_FF_EXPERT), D ** -0.5),
        'moe_w3': nrm((N_ODD, N_EXPERTS, D, D_FF_EXPERT), D ** -0.5),
        'moe_w2': nrm((N_ODD, N_EXPERTS, D_FF_EXPERT, D), D_FF_EXPERT ** -0.5),
    }


def reference(x, meta_tokens, lb_logits, mix_norm, ffn_norm, e_w_in, a_q_gain, a_k_gain,
              a_lam_q1, a_lam_k1, a_lam_q2, a_lam_k2, a_sub_gain, b_conv, b_a_log, b_dt_bias,
              b_out_gain, e_w_out, ffn_w1, ffn_w3, ffn_w2, o_w_in, c_out_gain, o_w_out,
              router, moe_w1, moe_w3, moe_w2):
    bn = x.shape[0]
    meta = jnp.broadcast_to(meta_tokens[None].astype(x.dtype), (bn, N_META, D_MODEL))
    h = jnp.concatenate([meta, x], axis=1)
    lb_all = jnp.cumsum(jax.nn.softmax(lb_logits.astype(jnp.float32), axis=0), axis=0)
    lb_all = lb_all - lb_all[0]
    for layer in range(DEPTH):
        j = layer // 2
        hn = rmsnorm(h, mix_norm[layer])
        if layer % 2 == 0:
            h = h + even_mixer(hn, e_w_in[j], a_q_gain[j], a_k_gain[j], a_lam_q1[j], a_lam_k1[j],
                               a_lam_q2[j], a_lam_k2[j], a_sub_gain[j], b_conv[j], b_a_log[j],
                               b_dt_bias[j], b_out_gain[j], e_w_out[j], layer)
            h = h + swiglu(rmsnorm(h, ffn_norm[layer]), ffn_w1[j], ffn_w3[j], ffn_w2[j])
        else:
            h = h + odd_mixer(hn, o_w_in[j], c_out_gain[j], o_w_out[j], lb_all[layer])
            h = h + moe_swiglu(rmsnorm(h, ffn_norm[layer]), router[j], moe_w1[j], moe_w3[j], moe_w2[j])
    return h[:, N_META:]
```

```python
import functools
import math

import jax
import jax.numpy as jnp
from jax import lax
from jax.experimental import pallas as pl
from jax.experimental.pallas import tpu as pltpu

D_MODEL = 1024
CHUNK = 64
N_META = 16
FRONT = CHUNK - N_META
EPS = 1e-6
ROPE_THETA = 500000.0

A_HEADS = 4
A_DK = 64
A_DV = 128
A_ROT = A_DK // 4
A_W = A_HEADS * A_DV

B_HEADS = 4
B_DK = 128
B_W = B_HEADS * B_DK
CONV_K = 4

C_HEADS = 8
C_DK = 128

D_FF = 2816
N_EXPERTS = 8
D_FF_EXPERT = 3584

LANES = 128
ROW_TILE = 256
VMEM_LIMIT = 56 * 1024 * 1024

BF16 = jnp.bfloat16
F32 = jnp.float32
HIGHEST = lax.Precision.HIGHEST


def _const_spec(shape):
    return pl.BlockSpec(shape, lambda *_: (0,) * len(shape), pipeline_mode=pl.Buffered(1))


def _rms(x, gain):
    return x * lax.rsqrt(jnp.mean(x * x, axis=-1, keepdims=True) + EPS) * gain


def _silu(x):
    return x * (1.0 / (1.0 + jnp.exp(-x)))


def _sigmoid(x):
    return 1.0 / (1.0 + jnp.exp(-x))


def _row_valid(tile_rows, lp, seq):
    r = pl.program_id(0) * tile_rows + lax.broadcasted_iota(jnp.int32, (tile_rows, 1), 0)
    p = r % lp
    return (p >= FRONT) & (p < CHUNK + seq)


def _head_norm_rope(a, ones_bd, gain, cos, sin_lo, sin_hi):
    ssq = jnp.dot((a * a).astype(BF16), ones_bd, preferred_element_type=F32)
    y = a * lax.rsqrt(ssq * (1.0 / A_DK) + EPS) * gain
    up = pltpu.roll(y, A_W - A_ROT // 2, axis=1)
    dn = pltpu.roll(y, A_ROT // 2, axis=1)
    return y * cos + up * sin_lo + dn * sin_hi


def _even_in_kernel(h_ref, g_ref, wq_ref, wk_ref, wv_ref, wb_ref, wz_ref, wg_ref, bd_ref,
                    qg_ref, kg_ref, cos_ref, slo_ref, shi_ref,
                    q_ref, k_ref, v_ref, b_ref, z_ref, gp_ref):
    xn = _rms(h_ref[...], g_ref[...]).astype(BF16)
    rep = lambda t: jnp.concatenate([t] * (A_W // LANES), axis=1)
    cos, slo, shi = rep(cos_ref[...]), rep(slo_ref[...]), rep(shi_ref[...])
    bd = bd_ref[...]
    aq = jnp.dot(xn, wq_ref[...], preferred_element_type=F32)
    q_ref[...] = (_head_norm_rope(aq, bd, qg_ref[...], cos, slo, shi) * (A_DK ** -0.5)).astype(BF16)
    ak = jnp.dot(xn, wk_ref[...], preferred_element_type=F32)
    k_ref[...] = _head_norm_rope(ak, bd, kg_ref[...], cos, slo, shi).astype(BF16)
    v_ref[...] = jnp.dot(xn, wv_ref[...], preferred_element_type=F32).astype(BF16)
    b_ref[...] = jnp.dot(xn, wb_ref[...], preferred_element_type=F32).astype(BF16)
    z_ref[...] = jnp.dot(xn, wz_ref[...], preferred_element_type=F32).astype(BF16)
    gp_ref[...] = jnp.dot(xn, wg_ref[...], preferred_element_type=F32)


def _even_in(h, gain, w_in, q_gain, k_gain, lp, tm):
    rows = h.shape[0]
    wb16 = w_in.astype(BF16)
    c0, c1, c2, c3, c4 = A_W, 2 * A_W, 3 * A_W, 3 * A_W + 3 * B_W, 3 * A_W + 4 * B_W
    wq, wk, wv, wb, wz = wb16[:, :c0], wb16[:, c0:c1], wb16[:, c1:c2], wb16[:, c2:c3], wb16[:, c3:c4]
    wg = jnp.pad(wb16[:, c4:], ((0, 0), (0, LANES - 2 * B_HEADS)))
    d = jnp.arange(A_W)
    ones_bd = (d[:, None] // A_DK == d[None, :] // A_DK).astype(BF16)
    half = A_ROT // 2
    inv_freq = ROPE_THETA ** (-jnp.arange(half, dtype=F32) * 2.0 / A_ROT)
    pos = (jnp.arange(lp) - FRONT).astype(F32)
    ang = pos[:, None] * inv_freq[None, :]
    cosv, sinv = jnp.cos(ang), jnp.sin(ang)
    zeros = jnp.zeros((lp, A_DK - A_ROT), F32)
    one = jnp.ones((lp, A_DK - A_ROT), F32)
    zh = jnp.zeros((lp, half), F32)
    cos64 = jnp.concatenate([cosv, cosv, one], axis=1)
    slo64 = jnp.concatenate([-sinv, zh, zeros], axis=1)
    shi64 = jnp.concatenate([zh, sinv, zeros], axis=1)
    two = lambda t: jnp.concatenate([t, t], axis=1)
    cos_t, slo_t, shi_t = two(cos64), two(slo64), two(shi64)
    tile8 = lambda g: jnp.tile(g.astype(F32), 2 * A_HEADS)[None, :]
    n_pos = lp // tm
    row_spec = lambda w: pl.BlockSpec((tm, w), lambda i: (i, 0))
    pos_spec = pl.BlockSpec((tm, LANES), lambda i: (i % n_pos, 0))
    outs = pl.pallas_call(
        _even_in_kernel,
        grid=(rows // tm,),
        in_specs=[row_spec(D_MODEL), _const_spec((1, D_MODEL)),
                  _const_spec(wq.shape), _const_spec(wk.shape), _const_spec(wv.shape),
                  _const_spec(wb.shape), _const_spec(wz.shape), _const_spec(wg.shape),
                  _const_spec(ones_bd.shape), _const_spec((1, A_W)), _const_spec((1, A_W)),
                  pos_spec, pos_spec, pos_spec],
        out_specs=(row_spec(A_W), row_spec(A_W), row_spec(A_W), row_spec(3 * B_W), row_spec(B_W),
                   row_spec(LANES)),
        out_shape=(jax.ShapeDtypeStruct((rows, A_W), BF16), jax.ShapeDtypeStruct((rows, A_W), BF16),
                   jax.ShapeDtypeStruct((rows, A_W), BF16), jax.ShapeDtypeStruct((rows, 3 * B_W), BF16),
                   jax.ShapeDtypeStruct((rows, B_W), BF16), jax.ShapeDtypeStruct((rows, LANES), F32)),
        compiler_params=pltpu.CompilerParams(dimension_semantics=("arbitrary",),
                                             vmem_limit_bytes=VMEM_LIMIT),
        name="even_in_proj",
    )(h, gain[None, :].astype(F32), wq, wk, wv, wb, wz, wg, ones_bd, tile8(q_gain), tile8(k_gain),
      cos_t, slo_t, shi_t)
    return outs


def _attn_kernel(q_ref, k_ref, v_ref, sg_ref, lam_ref, o_ref, qs_ref, *, tq, lam_init):
    qi = pl.program_id(2)
    q = q_ref[...]
    lane = lax.broadcasted_iota(jnp.int32, q.shape, 1)
    zero = jnp.zeros_like(q)
    qs_ref[:tq, :] = jnp.where(lane < A_DK, q, zero)
    qs_ref[tq:, :] = jnp.where(lane >= A_DK, q, zero)

    def tile(ki, carry, masked):
        m, l, acc = carry
        start = pl.multiple_of(ki * tq, tq)
        k = k_ref[pl.ds(start, tq), :]
        v = v_ref[pl.ds(start, tq), :]
        s = lax.dot_general(qs_ref[...], k, (((1,), (1,)), ((), ())), preferred_element_type=F32)
        if masked:
            qpos = qi * tq + lax.broadcasted_iota(jnp.int32, s.shape, 0) % tq
            kpos = start + lax.broadcasted_iota(jnp.int32, s.shape, 1)
            s = jnp.where((kpos // CHUNK <= qpos // CHUNK) & (kpos >= FRONT), s, -jnp.inf)
        m_new = jnp.maximum(m, jnp.max(s, axis=-1, keepdims=True))
        alpha = jnp.exp(m - m_new)
        p = jnp.exp(s - m_new)
        l = alpha * l + jnp.sum(p, axis=-1, keepdims=True)
        acc = alpha * acc + jnp.dot(p.astype(BF16), v, preferred_element_type=F32)
        return m_new, l, acc

    init = (jnp.full((2 * tq, 1), -jnp.inf, F32), jnp.zeros((2 * tq, 1), F32),
            jnp.zeros((2 * tq, A_DV), F32))
    carry = tile(qi, init, True)
    carry = lax.cond(qi > 0, lambda c: tile(0, c, True), lambda c: c, carry)
    m, l, acc = lax.fori_loop(1, qi, lambda ki, c: tile(ki, c, False), carry)
    o = acc / l
    o = o[:tq] - lam_ref[...] * o[tq:]
    o_ref[...] = (_rms(o, sg_ref[...]) * (1.0 - lam_init)).astype(o_ref.dtype)


def _diff_attention(q, k, v, sub_gain, lam, lam_init, batch, lp, tq):
    nq = lp // tq
    kv_spec = pl.BlockSpec((lp, LANES), lambda b, h, i: (b, h))
    q_spec = pl.BlockSpec((tq, LANES), lambda b, h, i: (b * nq + i, h))
    vec = pl.BlockSpec((1, LANES), lambda b, h, i: (0, 0))
    return pl.pallas_call(
        functools.partial(_attn_kernel, tq=tq, lam_init=lam_init),
        grid=(batch, A_HEADS, nq),
        in_specs=[q_spec, kv_spec, kv_spec, vec, vec],
        out_specs=q_spec,
        out_shape=jax.ShapeDtypeStruct(q.shape, BF16),
        scratch_shapes=[pltpu.VMEM((2 * tq, LANES), BF16)],
        compiler_params=pltpu.CompilerParams(dimension_semantics=("arbitrary",) * 3,
                                             vmem_limit_bytes=VMEM_LIMIT),
        name="diff_attention",
    )(q, k, v, sub_gain[None, :].astype(F32), jnp.full((1, LANES), lam, F32))


def _chunk_masks(n):
    r = lax.broadcasted_iota(jnp.int32, (n, n), 0)
    c = lax.broadcasted_iota(jnp.int32, (n, n), 1)
    same = (r // CHUNK) == (c // CHUNK)
    return r, c, same


def _bdot(a, b):
    return jnp.dot(a.astype(BF16), b.astype(BF16), preferred_element_type=F32)


def _bdot_nt(a, b):
    return lax.dot_general(a.astype(BF16), b.astype(BF16), (((1,), (1,)), ((), ())),
                           preferred_element_type=F32)


def _bdot_tn(a, b):
    return lax.dot_general(a.astype(BF16), b.astype(BF16), (((0,), (0,)), ((), ())),
                           preferred_element_type=F32)


def _gdn_kernel(x_ref, z_ref, gp_ref, cw_ref, alog_ref, dtb_ref, og_ref, o_ref,
                xbuf_ref, s_ref, *, steps_per_batch):
    n = ROW_TILE
    step = pl.program_id(0) % steps_per_batch

    @pl.when(step == 0)
    def _():
        xbuf_ref[:8, :] = jnp.zeros((8, 3 * B_W), F32)
        s_ref[...] = jnp.zeros_like(s_ref)

    xbuf_ref[8:, :] = x_ref[...].astype(F32)
    conv = cw_ref[CONV_K - 1:CONV_K, :] * xbuf_ref[8:, :]
    for j in range(CONV_K - 1):
        conv = conv + cw_ref[j:j + 1, :] * xbuf_ref[5 + j:5 + j + n, :]
    xbuf_ref[:8, :] = xbuf_ref[n:n + 8, :]
    conv = _silu(conv)

    r, c, same = _chunk_masks(n)
    incl = same & (r >= c)
    strict = same & (r > c)
    eye = (r == c).astype(F32)
    gp = gp_ref[...]
    beta_all = _sigmoid(gp)
    gpre = gp + dtb_ref[...]
    softplus = jnp.maximum(gpre, 0.0) + jnp.log(1.0 + jnp.exp(-jnp.abs(gpre)))
    row = step * n + lax.broadcasted_iota(jnp.int32, (n, 1), 0)
    g_all = jnp.where(row >= FRONT, -jnp.exp(alog_ref[...]) * softplus, 0.0)
    gc_all = jnp.dot(incl.astype(F32), g_all, precision=HIGHEST, preferred_element_type=F32)
    last = (same & (c % CHUNK == CHUNK - 1)).astype(F32)
    gl_all = jnp.dot(last, gc_all, precision=HIGHEST, preferred_element_type=F32)
    gc_rows = gc_all.T

    for h in range(B_HEADS):
        sl = slice(h * B_DK, (h + 1) * B_DK)
        qh = conv[:, sl]
        kh = conv[:, B_W + h * B_DK:B_W + (h + 1) * B_DK]
        vh = conv[:, 2 * B_W + h * B_DK:2 * B_W + (h + 1) * B_DK]
        qh = qh * lax.rsqrt(jnp.sum(qh * qh, axis=-1, keepdims=True) + EPS) * (B_DK ** -0.5)
        kh = kh * lax.rsqrt(jnp.sum(kh * kh, axis=-1, keepdims=True) + EPS)
        beta = beta_all[:, h:h + 1]
        gcol = gc_all[:, B_HEADS + h:B_HEADS + h + 1]
        glcol = gl_all[:, B_HEADS + h:B_HEADS + h + 1]
        grow = gc_rows[B_HEADS + h:B_HEADS + h + 1, :]
        dec = jnp.exp(jnp.where(incl, gcol - grow, -jnp.inf))
        kb = kh * beta
        egc = jnp.exp(gcol)
        p = -(_bdot_nt(kb, kh) * jnp.where(strict, dec, 0.0))
        t = eye + p
        for _ in range(5):
            p = _bdot(p, p)
            t = t + _bdot(t, p)
        sol = _bdot(t, jnp.concatenate([vh * beta, kb * egc], axis=1))
        u, w = sol[:, :B_DK], sol[:, B_DK:]
        qk = _bdot_nt(qh, kh) * dec
        qg = qh * egc
        kdec = kh * jnp.exp(glcol - gcol)
        state = s_ref[h]
        vnew, inter = [], []
        for ci in range(n // CHUNK):
            rs = slice(ci * CHUNK, (ci + 1) * CHUNK)
            vn = u[rs] - _bdot(w[rs], state)
            inter.append(_bdot(qg[rs], state))
            state = state * jnp.exp(glcol[ci * CHUNK:ci * CHUNK + 1, :]) + _bdot_tn(kdec[rs], vn)
            vnew.append(vn)
        s_ref[h] = state
        o = jnp.concatenate(inter, axis=0) + _bdot(qk, jnp.concatenate(vnew, axis=0))
        o_ref[:, sl] = (_rms(o, og_ref[...]) * _silu(z_ref[:, sl].astype(F32))).astype(o_ref.dtype)


def _gated_deltanet(bqkv, bz, gp, conv_w, a_log, dt_bias, out_gain, lp):
    rows = bqkv.shape[0]
    n = ROW_TILE
    head_vec = lambda p: jnp.zeros((1, LANES), F32).at[0, B_HEADS:2 * B_HEADS].set(p.astype(F32))
    row_spec = lambda w: pl.BlockSpec((n, w), lambda i: (i, 0))
    return pl.pallas_call(
        functools.partial(_gdn_kernel, steps_per_batch=lp // n),
        grid=(rows // n,),
        in_specs=[row_spec(3 * B_W), row_spec(B_W), row_spec(LANES), _const_spec((CONV_K, 3 * B_W)),
                  _const_spec((1, LANES)), _const_spec((1, LANES)), _const_spec((1, B_DK))],
        out_specs=row_spec(B_W),
        out_shape=jax.ShapeDtypeStruct((rows, B_W), BF16),
        scratch_shapes=[pltpu.VMEM((n + 8, 3 * B_W), F32), pltpu.VMEM((B_HEADS, B_DK, B_DK), F32)],
        compiler_params=pltpu.CompilerParams(dimension_semantics=("arbitrary",),
                                             vmem_limit_bytes=VMEM_LIMIT),
        name="gated_deltanet",
    )(bqkv, bz, gp, conv_w.astype(F32), head_vec(a_log), head_vec(dt_bias), out_gain[None, :].astype(F32))


FF_COLS = 256


def _even_out_kernel(h_ref, oa_ref, ob_ref, woa_ref, wob_ref, g_ref, w1_ref, w3_ref, w2_ref, o_ref,
                     act_ref, *, tm, lp, seq):
    h1 = (h_ref[...] + jnp.dot(oa_ref[...], woa_ref[...], preferred_element_type=F32)
          + jnp.dot(ob_ref[...], wob_ref[...], preferred_element_type=F32))
    xn = _rms(h1, g_ref[...]).astype(BF16)
    for c in range(D_FF // FF_COLS):
        cs = slice(c * FF_COLS, (c + 1) * FF_COLS)
        a = jnp.dot(xn, w1_ref[:, cs], preferred_element_type=F32)
        b = jnp.dot(xn, w3_ref[:, cs], preferred_element_type=F32)
        act_ref[:, cs] = (_silu(a) * b).astype(BF16)
    out = h1 + jnp.dot(act_ref[...], w2_ref[...], preferred_element_type=F32)
    o_ref[...] = jnp.where(_row_valid(tm, lp, seq), out, 0.0)


def _even_out(h, oa, ob, w_out, gain, w1, w3, w2, lp, seq, tm):
    rows = h.shape[0]
    wo = w_out.astype(BF16)
    row_spec = lambda w: pl.BlockSpec((tm, w), lambda i: (i, 0))
    return pl.pallas_call(
        functools.partial(_even_out_kernel, tm=tm, lp=lp, seq=seq),
        grid=(rows // tm,),
        in_specs=[row_spec(D_MODEL), row_spec(A_W), row_spec(B_W),
                  _const_spec((A_W, D_MODEL)), _const_spec((B_W, D_MODEL)), _const_spec((1, D_MODEL)),
                  _const_spec(w1.shape), _const_spec(w3.shape), _const_spec(w2.shape)],
        out_specs=row_spec(D_MODEL),
        out_shape=jax.ShapeDtypeStruct((rows, D_MODEL), F32),
        scratch_shapes=[pltpu.VMEM((tm, D_FF), BF16)],
        compiler_params=pltpu.CompilerParams(dimension_semantics=("arbitrary",),
                                             vmem_limit_bytes=VMEM_LIMIT),
        name="even_out_ffn",
    )(h, oa, ob, wo[:A_W], wo[A_W:], gain[None, :].astype(F32),
      w1.astype(BF16), w3.astype(BF16), w2.astype(BF16))


def _odd_in_kernel(h_ref, g_ref, wq_ref, wf_ref, wi_ref, wz_ref, lb_ref,
                   q_ref, k_ref, lf_ref, i_ref, z_ref, *, tm, lp, seq):
    xn = _rms(h_ref[...], g_ref[...]).astype(BF16)
    valid = _row_valid(tm, lp, seq)
    q_ref[...] = jnp.dot(xn, wq_ref[...], preferred_element_type=F32).astype(BF16)
    f = jnp.dot(xn, wf_ref[...], preferred_element_type=F32)
    lb = lb_ref[...]
    fg = lb + (1.0 - lb) * _sigmoid(f)
    k_ref[...] = jnp.where(valid, 1.0 - fg, 0.0).astype(BF16)
    lf_ref[...] = jnp.where(valid, jnp.log(fg), 0.0)
    i_ref[...] = jnp.dot(xn, wi_ref[...], preferred_element_type=F32).astype(BF16)
    z_ref[...] = jnp.dot(xn, wz_ref[...], preferred_element_type=F32).astype(BF16)


def _odd_in(h, gain, w_in, lb, lp, seq, tm):
    rows = h.shape[0]
    wb16 = w_in.astype(BF16)
    ws = [wb16[:, j * D_MODEL:(j + 1) * D_MODEL] for j in range(4)]
    row_spec = pl.BlockSpec((tm, D_MODEL), lambda i: (i, 0))
    wspec = _const_spec((D_MODEL, D_MODEL))
    vec = _const_spec((1, D_MODEL))
    sd = lambda dt: jax.ShapeDtypeStruct((rows, D_MODEL), dt)
    return pl.pallas_call(
        functools.partial(_odd_in_kernel, tm=tm, lp=lp, seq=seq),
        grid=(rows // tm,),
        in_specs=[row_spec, vec, wspec, wspec, wspec, wspec, vec],
        out_specs=(row_spec,) * 5,
        out_shape=(sd(BF16), sd(BF16), sd(F32), sd(BF16), sd(BF16)),
        compiler_params=pltpu.CompilerParams(dimension_semantics=("arbitrary",),
                                             vmem_limit_bytes=VMEM_LIMIT),
        name="odd_in_proj",
    )(h, gain[None, :].astype(F32), *ws, lb[None, :].astype(F32))


SUB = 16


def _hgrn_kernel(q_ref, k_ref, lf_ref, i_ref, z_ref, og_ref, o_ref, s_ref, *, steps_per_batch):
    n = ROW_TILE
    nsub = CHUNK // SUB
    step = pl.program_id(0) % steps_per_batch

    @pl.when(step == 0)
    def _():
        s_ref[...] = jnp.zeros_like(s_ref)

    r, c, same = _chunk_masks(n)
    incl = same & (r >= c)
    gc_all = jnp.dot(incl.astype(F32), lf_ref[...], precision=HIGHEST, preferred_element_type=F32)
    rrow = lax.broadcasted_iota(jnp.int32, (n, 1), 0)
    subpos = (rrow % CHUNK) // SUB
    inner = rrow % SUB

    def chunk_rows(a, off):
        return jnp.concatenate(
            [jnp.broadcast_to(a[ci * CHUNK + off:ci * CHUNK + off + 1, :], (CHUNK, a.shape[1]))
             for ci in range(n // CHUNK)], axis=0)

    def sub_rows(a, j):
        a3 = a.reshape(n // SUB, SUB, a.shape[1])
        return jnp.broadcast_to(a3[:, j:j + 1, :], a3.shape).reshape(a.shape)

    for h in range(C_HEADS):
        sl = slice(h * C_DK, (h + 1) * C_DK)
        q = q_ref[:, sl].astype(F32)
        k = k_ref[:, sl].astype(F32)
        iv = i_ref[:, sl]
        gc = gc_all[:, sl]
        gl = chunk_rows(gc, CHUNK - 1)
        qg = q * jnp.exp(gc)
        kdec = k * jnp.exp(gl - gc)
        grefs = [chunk_rows(gc, p * SUB - 1) for p in range(1, nsub)]
        gown = grefs[0]
        for p in range(2, nsub):
            gown = jnp.where(subpos == p, grefs[p - 1], gown)
        qt = q * jnp.exp(jnp.where(subpos >= 1, gc - gown, -jnp.inf))
        qts = [jnp.where(subpos == p, qt, 0.0) for p in range(1, nsub)]
        kts = [k * jnp.exp(jnp.where(subpos < p, grefs[p - 1] - gc, -jnp.inf)) for p in range(1, nsub)]
        att = jnp.where(same, _bdot_nt(jnp.concatenate(qts, axis=1), jnp.concatenate(kts, axis=1)), 0.0)
        cbase = (r // SUB) * SUB
        for j in range(SUB):
            kj = sub_rows(k, j)
            gj = sub_rows(gc, j)
            pj = q * kj * jnp.exp(jnp.where(inner >= j, gc - gj, -jnp.inf))
            att = att + jnp.where(c == cbase + j, jnp.sum(pj, axis=-1, keepdims=True), 0.0)
        st = s_ref[h]
        inter = []
        for ci in range(n // CHUNK):
            rs = slice(ci * CHUNK, (ci + 1) * CHUNK)
            inter.append(_bdot_nt(qg[rs], st))
            st = st * jnp.exp(gl[ci * CHUNK:ci * CHUNK + 1, :]) + _bdot_tn(iv[rs], kdec[rs])
        s_ref[h] = st
        o = jnp.concatenate(inter, axis=0) + _bdot(att, iv)
        o_ref[:, sl] = (_rms(o, og_ref[...]) * _silu(z_ref[:, sl].astype(F32))).astype(o_ref.dtype)


def _hgrn2(q, k, lf, iv, z, out_gain, lp):
    rows = q.shape[0]
    n = ROW_TILE
    row_spec = pl.BlockSpec((n, D_MODEL), lambda i: (i, 0))
    return pl.pallas_call(
        functools.partial(_hgrn_kernel, steps_per_batch=lp // n),
        grid=(rows // n,),
        in_specs=[row_spec] * 5 + [_const_spec((1, C_DK))],
        out_specs=row_spec,
        out_shape=jax.ShapeDtypeStruct((rows, D_MODEL), BF16),
        scratch_shapes=[pltpu.VMEM((C_HEADS, C_DK, C_DK), F32)],
        compiler_params=pltpu.CompilerParams(dimension_semantics=("arbitrary",),
                                             vmem_limit_bytes=VMEM_LIMIT),
        name="hgrn2",
    )(q, k, lf, iv, z, out_gain[None, :].astype(F32))


def _odd_out_kernel(h_ref, o_ref, wo_ref, g_ref, wr_ref, h3_ref, xn_ref, route_ref, *, tm, lp, seq):
    h3 = h_ref[...] + jnp.dot(o_ref[...], wo_ref[...], preferred_element_type=F32)
    h3 = jnp.where(_row_valid(tm, lp, seq), h3, 0.0)
    h3_ref[...] = h3
    xn = _rms(h3, g_ref[...])
    xn_ref[...] = xn
    logits = jnp.dot(xn, wr_ref[...], precision=HIGHEST, preferred_element_type=F32)
    lane = lax.broadcasted_iota(jnp.int32, logits.shape, 1)
    logits = jnp.where(lane < N_EXPERTS, logits, -jnp.inf)
    v1 = jnp.max(logits, axis=-1, keepdims=True)
    e1 = jnp.min(jnp.where(logits == v1, lane, LANES), axis=-1, keepdims=True)
    rest = jnp.where(lane == e1, -jnp.inf, logits)
    v2 = jnp.max(rest, axis=-1, keepdims=True)
    e2 = jnp.min(jnp.where(rest == v2, lane, LANES), axis=-1, keepdims=True)
    t = jnp.exp(v2 - v1)
    g1 = 1.0 / (1.0 + t)
    g2 = t * g1
    route_ref[...] = jnp.where(lane == 0, e1.astype(F32),
                               jnp.where(lane == 1, e2.astype(F32),
                                         jnp.where(lane == 2, g1, jnp.where(lane == 3, g2, 0.0))))


def _odd_out(h, o, w_out, gain, router, lp, seq, tm):
    rows = h.shape[0]
    wr = jnp.pad(router.astype(F32), ((0, 0), (0, LANES - N_EXPERTS)))
    row_spec = lambda w: pl.BlockSpec((tm, w), lambda i: (i, 0))
    return pl.pallas_call(
        functools.partial(_odd_out_kernel, tm=tm, lp=lp, seq=seq),
        grid=(rows // tm,),
        in_specs=[row_spec(D_MODEL), row_spec(D_MODEL), _const_spec((D_MODEL, D_MODEL)),
                  _const_spec((1, D_MODEL)), _const_spec((D_MODEL, LANES))],
        out_specs=(row_spec(D_MODEL), row_spec(D_MODEL), row_spec(LANES)),
        out_shape=(jax.ShapeDtypeStruct((rows, D_MODEL), F32), jax.ShapeDtypeStruct((rows, D_MODEL), F32),
                   jax.ShapeDtypeStruct((rows, LANES), F32)),
        compiler_params=pltpu.CompilerParams(dimension_semantics=("arbitrary",),
                                             vmem_limit_bytes=VMEM_LIMIT),
        name="odd_out_route",
    )(h, o, w_out.astype(BF16), gain[None, :].astype(F32), wr)


MOE_TILE = 512
MOE_FF = 512


def _moe_plan(route, tile):
    n_tok = route.shape[0]
    flat_e = route[:, :2].astype(jnp.int32).reshape(-1)
    na = flat_e.shape[0]
    onehot = (flat_e[:, None] == jnp.arange(N_EXPERTS, dtype=jnp.int32)[None, :]).astype(jnp.int32)
    csum = jnp.cumsum(onehot, axis=0)
    rank = jnp.take_along_axis(csum, flat_e[:, None], axis=1)[:, 0] - 1
    counts = csum[-1]
    padded = (counts + tile - 1) // tile * tile
    pend = jnp.cumsum(padded)
    dest = (pend - padded)[flat_e] + rank
    n_tiles = -(-(na + N_EXPERTS * (tile - 1)) // tile)
    row_tok = jnp.zeros((n_tiles * tile,), jnp.int32).at[dest].set(jnp.arange(na, dtype=jnp.int32) // 2)
    tile_e = jnp.minimum(jnp.searchsorted(pend, jnp.arange(n_tiles, dtype=jnp.int32) * tile, side='right'),
                         N_EXPERTS - 1).astype(jnp.int32)
    n_used = (pend[-1] // tile).astype(jnp.int32).reshape(1)
    return dest.reshape(n_tok, 2), row_tok, tile_e, n_used, n_tiles


def _moe_gather_kernel(tok_ref, nused_ref, x_hbm, o_ref, buf_ref, sem):
    i = pl.program_id(0)
    tile = buf_ref.shape[0]

    @pl.when(i < nused_ref[0])
    def _():
        def issue(r, carry):
            tok = tok_ref[0, 0, r]
            pltpu.make_async_copy(x_hbm.at[pl.ds(tok, 1), :], buf_ref.at[pl.ds(r, 1), :], sem).start()
            return carry
        lax.fori_loop(0, tile, issue, 0)
        pltpu.make_async_copy(x_hbm.at[pl.ds(0, tile), :], buf_ref, sem).wait()
        o_ref[...] = buf_ref[...].astype(o_ref.dtype)

    @pl.when(i >= nused_ref[0])
    def _():
        o_ref[...] = jnp.zeros_like(o_ref)


def _moe_gather(xn, row_tok, n_used, n_tiles, tile):
    return pl.pallas_call(
        _moe_gather_kernel,
        grid=(n_tiles,),
        in_specs=[pl.BlockSpec((1, 1, tile), lambda i: (i, 0, 0), memory_space=pltpu.SMEM),
                  pl.BlockSpec(memory_space=pltpu.SMEM),
                  pl.BlockSpec(memory_space=pl.ANY)],
        out_specs=pl.BlockSpec((tile, D_MODEL), lambda i: (i, 0)),
        out_shape=jax.ShapeDtypeStruct((n_tiles * tile, D_MODEL), BF16),
        scratch_shapes=[pltpu.VMEM((tile, D_MODEL), F32), pltpu.SemaphoreType.DMA(())],
        compiler_params=pltpu.CompilerParams(dimension_semantics=("arbitrary",),
                                             vmem_limit_bytes=VMEM_LIMIT),
        name="moe_gather",
    )(row_tok.reshape(n_tiles, 1, tile), n_used, xn)


def _moe_ffn_kernel(te_ref, nused_ref, x_ref, w1_ref, w3_ref, w2_ref, y_ref):
    i, f = pl.program_id(0), pl.program_id(1)

    @pl.when(f == 0)
    def _():
        y_ref[...] = jnp.zeros_like(y_ref)

    @pl.when(i < nused_ref[0])
    def _():
        x = x_ref[...]
        a = jnp.dot(x, w1_ref[0], preferred_element_type=F32)
        b = jnp.dot(x, w3_ref[0], preferred_element_type=F32)
        y_ref[...] += jnp.dot((_silu(a) * b).astype(BF16), w2_ref[0], preferred_element_type=F32)


def _moe_ffn(xs, w1, w3, w2, tile_e, n_used, n_tiles, tile):
    nf = D_FF_EXPERT // MOE_FF
    live = lambda i, nu: i < nu[0]
    xi = lambda i, f, te, nu: (jnp.where(live(i, nu), i, nu[0] - 1), 0)
    fi = lambda i, f, nu: jnp.where(live(i, nu), f, nf - 1)
    grid_spec = pltpu.PrefetchScalarGridSpec(
        num_scalar_prefetch=2,
        grid=(n_tiles, nf),
        in_specs=[pl.BlockSpec((tile, D_MODEL), xi),
                  pl.BlockSpec((1, D_MODEL, MOE_FF), lambda i, f, te, nu: (te[i], 0, fi(i, f, nu))),
                  pl.BlockSpec((1, D_MODEL, MOE_FF), lambda i, f, te, nu: (te[i], 0, fi(i, f, nu))),
                  pl.BlockSpec((1, MOE_FF, D_MODEL), lambda i, f, te, nu: (te[i], fi(i, f, nu), 0))],
        out_specs=pl.BlockSpec((tile, D_MODEL), lambda i, f, te, nu: (i, 0)),
    )
    return pl.pallas_call(
        _moe_ffn_kernel,
        grid_spec=grid_spec,
        out_shape=jax.ShapeDtypeStruct((n_tiles * tile, D_MODEL), F32),
        compiler_params=pltpu.CompilerParams(dimension_semantics=("arbitrary", "arbitrary"),
                                             vmem_limit_bytes=VMEM_LIMIT),
        name="moe_ffn",
    )(tile_e, n_used, xs, w1, w3, w2)


def _moe_combine_kernel(pos_ref, gate_ref, h_hbm, y_hbm, o_ref, hbuf, ybuf, sem_h, sem_y, *, tc, lp):
    b, j = pl.program_id(0), pl.program_id(1)
    h_copy = pltpu.make_async_copy(h_hbm.at[pl.ds(b * lp + CHUNK + j * tc, tc), :], hbuf, sem_h)
    h_copy.start()

    def issue(r, carry):
        p = pos_ref[0, 0, r]
        pltpu.make_async_copy(y_hbm.at[pl.ds(p, 1), :], ybuf.at[pl.ds(r, 1), :], sem_y).start()
        return carry
    lax.fori_loop(0, 2 * tc, issue, 0)
    pltpu.make_async_copy(y_hbm.at[pl.ds(0, 2 * tc), :], ybuf, sem_y).wait()
    h_copy.wait()
    g = gate_ref[0]
    o_ref[0] = hbuf[...] + g[:, 0:1] * ybuf[:tc, :] + g[:, 1:2] * ybuf[tc:, :]


def _moe_combine(h3, y, dest, gates, batch, seq, lp, tc):
    nj = seq // tc
    pos = dest.reshape(batch * nj, tc, 2).transpose(0, 2, 1).reshape(batch * nj, 1, 2 * tc)
    return pl.pallas_call(
        functools.partial(_moe_combine_kernel, tc=tc, lp=lp),
        grid=(batch, nj),
        in_specs=[pl.BlockSpec((1, 1, 2 * tc), lambda b, j: (b * nj + j, 0, 0), memory_space=pltpu.SMEM),
                  pl.BlockSpec((1, tc, 2), lambda b, j: (b, j, 0)),
                  pl.BlockSpec(memory_space=pl.ANY), pl.BlockSpec(memory_space=pl.ANY)],
        out_specs=pl.BlockSpec((1, tc, D_MODEL), lambda b, j: (b, j, 0)),
        out_shape=jax.ShapeDtypeStruct((batch, seq, D_MODEL), F32),
        scratch_shapes=[pltpu.VMEM((tc, D_MODEL), F32), pltpu.VMEM((2 * tc, D_MODEL), F32),
                        pltpu.SemaphoreType.DMA(()), pltpu.SemaphoreType.DMA(())],
        compiler_params=pltpu.CompilerParams(dimension_semantics=("arbitrary", "arbitrary"),
                                             vmem_limit_bytes=VMEM_LIMIT),
        name="moe_combine",
    )(pos, gates, h3, y)


def _layout_rows(seq):
    return -(-(CHUNK + seq) // ROW_TILE) * ROW_TILE


def _proj_tile(lp):
    return next(t for t in (640, 512, 384, ROW_TILE) if lp % t == 0)


def kernel(x, meta_tokens, lb_logits, mix_norm, ffn_norm, e_w_in, a_q_gain, a_k_gain, a_lam_q1, a_lam_k1, a_lam_q2, a_lam_k2, a_sub_gain, b_conv, b_a_log, b_dt_bias, b_out_gain, e_w_out, ffn_w1, ffn_w3, ffn_w2, o_w_in, c_out_gain, o_w_out, router, moe_w1, moe_w3, moe_w2):
    batch, seq, _ = x.shape
    lp = _layout_rows(seq)
    rows = batch * lp
    tm = _proj_tile(lp)
    meta = jnp.broadcast_to(meta_tokens[None].astype(x.dtype), (batch, N_META, D_MODEL))
    h = jnp.concatenate([jnp.zeros((batch, FRONT, D_MODEL), x.dtype), meta, x,
                         jnp.zeros((batch, lp - CHUNK - seq, D_MODEL), x.dtype)], axis=1).reshape(rows, D_MODEL)

    q, k, v, bqkv, bz, gp = _even_in(h, mix_norm[0], e_w_in[0], a_q_gain[0], a_k_gain[0], lp, tm)
    lam_init = 0.8 - 0.6 * math.exp(-0.3 * 0)
    lam = (jnp.exp(jnp.sum(a_lam_q1[0].astype(F32) * a_lam_k1[0].astype(F32)))
           - jnp.exp(jnp.sum(a_lam_q2[0].astype(F32) * a_lam_k2[0].astype(F32))) + lam_init)
    o_a = _diff_attention(q, k, v, a_sub_gain[0], lam, lam_init, batch, lp, ROW_TILE)
    o_b = _gated_deltanet(bqkv, bz, gp, b_conv[0], b_a_log[0], b_dt_bias[0], b_out_gain[0], lp)
    h = _even_out(h, o_a, o_b, e_w_out[0], ffn_norm[0], ffn_w1[0], ffn_w3[0], ffn_w2[0], lp, seq, tm)

    lb_all = jnp.cumsum(jax.nn.softmax(lb_logits.astype(F32), axis=0), axis=0)
    lb = (lb_all - lb_all[0])[1]
    cq, ck, clf, ci, cz = _odd_in(h, mix_norm[1], o_w_in[0], lb, lp, seq, tm)
    o_c = _hgrn2(cq, ck, clf, ci, cz, c_out_gain[0], lp)
    h3, xn, route = _odd_out(h, o_c, o_w_out[0], ffn_norm[1], router[0], lp, seq, tm)
    dest, row_tok, tile_e, n_used, n_tiles = _moe_plan(route, MOE_TILE)
    xs = _moe_gather(xn, row_tok, n_used, n_tiles, MOE_TILE)
    y = _moe_ffn(xs, moe_w1[0].astype(BF16), moe_w3[0].astype(BF16), moe_w2[0].astype(BF16),
                 tile_e, n_used, n_tiles, MOE_TILE)
    frames = lambda a: a.reshape(batch, lp, a.shape[-1])[:, CHUNK:CHUNK + seq]
    return _moe_combine(h3, y, frames(dest), frames(route[:, 2:4]), batch, seq, lp, ROW_TILE)
```

```python
import functools
import math

import jax
import jax.numpy as jnp
from jax import lax
from jax.experimental import pallas as pl
from jax.experimental.pallas import tpu as pltpu

D_MODEL = 1024
CHUNK = 64
N_META = 16
FRONT = CHUNK - N_META
EPS = 1e-6
ROPE_THETA = 500000.0
LOG2E = math.log2(math.e)

A_HEADS = 4
A_DK = 64
A_DV = 128
A_ROT = A_DK // 4
A_W = A_HEADS * A_DV

B_HEADS = 4
B_DK = 128
B_W = B_HEADS * B_DK
CONV_K = 4

C_HEADS = 8
C_DK = 128

D_FF = 2816
N_EXPERTS = 8
D_FF_EXPERT = 3584

LANES = 128
ROW_TILE = 256
VMEM_LIMIT = 56 * 1024 * 1024

BF16 = jnp.bfloat16
F32 = jnp.float32
HIGHEST = lax.Precision.HIGHEST


def _const_spec(shape):
    return pl.BlockSpec(shape, lambda *_: (0,) * len(shape), pipeline_mode=pl.Buffered(1))


def _rms(x, gain):
    return x * lax.rsqrt(jnp.mean(x * x, axis=-1, keepdims=True) + EPS) * gain


def _silu(x):
    return x * (1.0 / (1.0 + jnp.exp(-x)))


def _sigmoid(x):
    return 1.0 / (1.0 + jnp.exp(-x))


def _row_valid(tile_rows, lp, seq):
    r = pl.program_id(0) * tile_rows + lax.broadcasted_iota(jnp.int32, (tile_rows, 1), 0)
    p = r % lp
    return (p >= FRONT) & (p < CHUNK + seq)


def _head_norm_rope(a, ones_bd, gain, cos, sin_lo, sin_hi):
    ssq = jnp.dot((a * a).astype(BF16), ones_bd, preferred_element_type=F32)
    y = a * lax.rsqrt(ssq * (1.0 / A_DK) + EPS) * gain
    up = pltpu.roll(y, A_W - A_ROT // 2, axis=1)
    dn = pltpu.roll(y, A_ROT // 2, axis=1)
    return y * cos + up * sin_lo + dn * sin_hi


def _even_in_kernel(h_ref, g_ref, wq_ref, wk_ref, wv_ref, wb_ref, wz_ref, wg_ref, bd_ref,
                    qg_ref, kg_ref, cos_ref, slo_ref, shi_ref,
                    q_ref, k_ref, v_ref, b_ref, z_ref, gp_ref):
    xn = _rms(h_ref[...], g_ref[...]).astype(BF16)
    rep = lambda t: jnp.concatenate([t] * (A_W // LANES), axis=1)
    cos, slo, shi = rep(cos_ref[...]), rep(slo_ref[...]), rep(shi_ref[...])
    bd = bd_ref[...]
    aq = jnp.dot(xn, wq_ref[...], preferred_element_type=F32)
    q_ref[...] = (_head_norm_rope(aq, bd, qg_ref[...], cos, slo, shi) * (A_DK ** -0.5 * LOG2E)).astype(BF16)
    ak = jnp.dot(xn, wk_ref[...], preferred_element_type=F32)
    k_ref[...] = _head_norm_rope(ak, bd, kg_ref[...], cos, slo, shi).astype(BF16)
    v_ref[...] = jnp.dot(xn, wv_ref[...], preferred_element_type=F32).astype(BF16)
    b_ref[...] = jnp.dot(xn, wb_ref[...], preferred_element_type=F32).astype(BF16)
    z_ref[...] = jnp.dot(xn, wz_ref[...], preferred_element_type=F32).astype(BF16)
    gp_ref[...] = jnp.dot(xn, wg_ref[...], preferred_element_type=F32)


def _even_in(h, gain, w_in, q_gain, k_gain, lp, tm):
    rows = h.shape[0]
    wb16 = w_in.astype(BF16)
    c0, c1, c2, c3, c4 = A_W, 2 * A_W, 3 * A_W, 3 * A_W + 3 * B_W, 3 * A_W + 4 * B_W
    wq, wk, wv, wb, wz = wb16[:, :c0], wb16[:, c0:c1], wb16[:, c1:c2], wb16[:, c2:c3], wb16[:, c3:c4]
    wg = jnp.pad(wb16[:, c4:], ((0, 0), (0, LANES - 2 * B_HEADS)))
    d = jnp.arange(A_W)
    ones_bd = (d[:, None] // A_DK == d[None, :] // A_DK).astype(BF16)
    half = A_ROT // 2
    inv_freq = ROPE_THETA ** (-jnp.arange(half, dtype=F32) * 2.0 / A_ROT)
    pos = (jnp.arange(lp) - FRONT).astype(F32)
    ang = pos[:, None] * inv_freq[None, :]
    cosv, sinv = jnp.cos(ang), jnp.sin(ang)
    zeros = jnp.zeros((lp, A_DK - A_ROT), F32)
    one = jnp.ones((lp, A_DK - A_ROT), F32)
    zh = jnp.zeros((lp, half), F32)
    cos64 = jnp.concatenate([cosv, cosv, one], axis=1)
    slo64 = jnp.concatenate([-sinv, zh, zeros], axis=1)
    shi64 = jnp.concatenate([zh, sinv, zeros], axis=1)
    two = lambda t: jnp.concatenate([t, t], axis=1)
    cos_t, slo_t, shi_t = two(cos64), two(slo64), two(shi64)
    tile8 = lambda g: jnp.tile(g.astype(F32), 2 * A_HEADS)[None, :]
    n_pos = lp // tm
    row_spec = lambda w: pl.BlockSpec((tm, w), lambda i: (i, 0))
    pos_spec = pl.BlockSpec((tm, LANES), lambda i: (i % n_pos, 0))
    outs = pl.pallas_call(
        _even_in_kernel,
        grid=(rows // tm,),
        in_specs=[row_spec(D_MODEL), _const_spec((1, D_MODEL)),
                  _const_spec(wq.shape), _const_spec(wk.shape), _const_spec(wv.shape),
                  _const_spec(wb.shape), _const_spec(wz.shape), _const_spec(wg.shape),
                  _const_spec(ones_bd.shape), _const_spec((1, A_W)), _const_spec((1, A_W)),
                  pos_spec, pos_spec, pos_spec],
        out_specs=(row_spec(A_W), row_spec(A_W), row_spec(A_W), row_spec(3 * B_W), row_spec(B_W),
                   row_spec(LANES)),
        out_shape=(jax.ShapeDtypeStruct((rows, A_W), BF16), jax.ShapeDtypeStruct((rows, A_W), BF16),
                   jax.ShapeDtypeStruct((rows, A_W), BF16), jax.ShapeDtypeStruct((rows, 3 * B_W), BF16),
                   jax.ShapeDtypeStruct((rows, B_W), BF16), jax.ShapeDtypeStruct((rows, LANES), F32)),
        compiler_params=pltpu.CompilerParams(dimension_semantics=("arbitrary",),
                                             vmem_limit_bytes=VMEM_LIMIT),
        name="even_in_proj",
    )(h, gain[None, :].astype(F32), wq, wk, wv, wb, wz, wg, ones_bd, tile8(q_gain), tile8(k_gain),
      cos_t, slo_t, shi_t)
    return outs


ATTN_WIDE = 1024


def _attn_kernel(q_ref, k_ref, v_ref, sg_ref, lam_ref, o_ref, qs_ref, vx_ref, m_ref, acc_ref,
                 *, tq, width, lam_init):
    qi = pl.program_id(2)

    @pl.when(qi == 0)
    def _():
        vx_ref[:, :A_DV] = v_ref[...]
        vx_ref[:, A_DV:] = jnp.ones((vx_ref.shape[0], A_DV), BF16)

    q = q_ref[...]
    lane = lax.broadcasted_iota(jnp.int32, q.shape, 1)
    zero = jnp.zeros_like(q)
    qs_ref[:tq, :] = jnp.where(lane < A_DK, q, zero)
    qs_ref[tq:, :] = jnp.where(lane >= A_DK, q, zero)
    m_ref[...] = jnp.full(m_ref.shape, -jnp.inf, F32)
    acc_ref[...] = jnp.zeros(acc_ref.shape, F32)

    lp = k_ref.shape[0]
    qchunk = (qi * tq + lax.broadcasted_iota(jnp.int32, (2 * tq, 1), 0) % tq) // CHUNK

    def process(nominal, masked):
        start = pl.multiple_of(jnp.minimum(nominal, lp - width) if masked else nominal, tq)
        s = lax.dot_general(qs_ref[...], k_ref[pl.ds(start, width), :], (((1,), (1,)), ((), ())),
                            preferred_element_type=F32)
        if masked:
            kpos = start + lax.broadcasted_iota(jnp.int32, (1, width), 1)
            kchunk = jnp.where((kpos >= nominal) & (kpos >= FRONT), kpos // CHUNK, lp)
            s = jnp.where(kchunk <= qchunk, s, -jnp.inf)
        blocks = [s[:, c * LANES:(c + 1) * LANES] for c in range(width // LANES)]
        mx = functools.reduce(jnp.maximum, blocks)
        m_old = m_ref[...]
        m_new = jnp.maximum(m_old, jnp.max(mx, axis=-1, keepdims=True))
        alpha = jnp.exp2(m_old - m_new)
        p = jnp.concatenate([jnp.exp2((blk - m_new).astype(BF16)) for blk in blocks], axis=1)
        pv = jnp.dot(p, vx_ref[pl.ds(start, width), :], preferred_element_type=F32)
        acc_ref[:, :A_DV] = alpha * acc_ref[:, :A_DV] + pv[:, :A_DV]
        acc_ref[:, A_DV:] = alpha * acc_ref[:, A_DV:] + pv[:, A_DV:]
        m_ref[...] = m_new

    n_tiles = (qi * tq + tq + width - 1) // width
    process(0, True)

    @pl.when(n_tiles > 1)
    def _():
        process((n_tiles - 1) * width, True)

    n_mid = jnp.maximum(n_tiles - 2, 0)

    def pair(j, carry):
        process((1 + 2 * j) * width, False)
        process((2 + 2 * j) * width, False)
        return carry
    lax.fori_loop(0, n_mid // 2, pair, 0)

    @pl.when(n_mid % 2 == 1)
    def _():
        process(n_mid * width, False)

    o = acc_ref[:, :A_DV] / acc_ref[:, A_DV:]
    o = o[:tq] - lam_ref[...] * o[tq:]
    o_ref[...] = (_rms(o, sg_ref[...]) * (1.0 - lam_init)).astype(o_ref.dtype)


def _diff_attention(q, k, v, sub_gain, lam, lam_init, batch, lp, tq):
    nq = lp // tq
    kv_spec = pl.BlockSpec((lp, LANES), lambda b, h, i: (b, h))
    q_spec = pl.BlockSpec((tq, LANES), lambda b, h, i: (b * nq + i, h))
    vec = pl.BlockSpec((1, LANES), lambda b, h, i: (0, 0))
    return pl.pallas_call(
        functools.partial(_attn_kernel, tq=tq, width=min(ATTN_WIDE, lp), lam_init=lam_init),
        grid=(batch, A_HEADS, nq),
        in_specs=[q_spec, kv_spec, kv_spec, vec, vec],
        out_specs=q_spec,
        out_shape=jax.ShapeDtypeStruct(q.shape, BF16),
        scratch_shapes=[pltpu.VMEM((2 * tq, LANES), BF16), pltpu.VMEM((lp, 2 * A_DV), BF16),
                        pltpu.VMEM((2 * tq, LANES), F32), pltpu.VMEM((2 * tq, 2 * A_DV), F32)],
        compiler_params=pltpu.CompilerParams(dimension_semantics=("arbitrary",) * 3,
                                             vmem_limit_bytes=VMEM_LIMIT),
        name="diff_attention",
    )(q, k, v, sub_gain[None, :].astype(F32), jnp.full((1, LANES), lam, F32))


def _chunk_masks(n):
    r = lax.broadcasted_iota(jnp.int32, (n, n), 0)
    c = lax.broadcasted_iota(jnp.int32, (n, n), 1)
    same = (r // CHUNK) == (c // CHUNK)
    return r, c, same


def _bdot(a, b):
    return jnp.dot(a.astype(BF16), b.astype(BF16), preferred_element_type=F32)


def _bdot_nt(a, b):
    return lax.dot_general(a.astype(BF16), b.astype(BF16), (((1,), (1,)), ((), ())),
                           preferred_element_type=F32)


def _bdot_tn(a, b):
    return lax.dot_general(a.astype(BF16), b.astype(BF16), (((0,), (0,)), ((), ())),
                           preferred_element_type=F32)


def _gdn_kernel(x_ref, z_ref, gp_ref, cw_ref, alog_ref, dtb_ref, og_ref, o_ref,
                xbuf_ref, s_ref, *, steps_per_batch):
    n = ROW_TILE
    step = pl.program_id(0) % steps_per_batch

    @pl.when(step == 0)
    def _():
        xbuf_ref[:8, :] = jnp.zeros((8, 3 * B_W), F32)
        s_ref[...] = jnp.zeros_like(s_ref)

    xbuf_ref[8:, :] = x_ref[...].astype(F32)
    conv = cw_ref[CONV_K - 1:CONV_K, :] * xbuf_ref[8:, :]
    for j in range(CONV_K - 1):
        conv = conv + cw_ref[j:j + 1, :] * xbuf_ref[5 + j:5 + j + n, :]
    xbuf_ref[:8, :] = xbuf_ref[n:n + 8, :]
    conv = _silu(conv)

    r, c, same = _chunk_masks(n)
    incl = same & (r >= c)
    strict = same & (r > c)
    eye = (r == c).astype(F32)
    gp = gp_ref[...]
    beta_all = _sigmoid(gp)
    gpre = gp + dtb_ref[...]
    softplus = jnp.maximum(gpre, 0.0) + jnp.log(1.0 + jnp.exp(-jnp.abs(gpre)))
    row = step * n + lax.broadcasted_iota(jnp.int32, (n, 1), 0)
    g_all = jnp.where(row >= FRONT, -jnp.exp(alog_ref[...]) * softplus, 0.0)
    gc_all = jnp.dot(incl.astype(F32), g_all, precision=HIGHEST, preferred_element_type=F32)
    last = (same & (c % CHUNK == CHUNK - 1)).astype(F32)
    gl_all = jnp.dot(last, gc_all, precision=HIGHEST, preferred_element_type=F32)
    gc_rows = gc_all.T

    for h in range(B_HEADS):
        sl = slice(h * B_DK, (h + 1) * B_DK)
        qh = conv[:, sl]
        kh = conv[:, B_W + h * B_DK:B_W + (h + 1) * B_DK]
        vh = conv[:, 2 * B_W + h * B_DK:2 * B_W + (h + 1) * B_DK]
        qh = qh * lax.rsqrt(jnp.sum(qh * qh, axis=-1, keepdims=True) + EPS) * (B_DK ** -0.5)
        kh = kh * lax.rsqrt(jnp.sum(kh * kh, axis=-1, keepdims=True) + EPS)
        beta = beta_all[:, h:h + 1]
        gcol = gc_all[:, B_HEADS + h:B_HEADS + h + 1]
        glcol = gl_all[:, B_HEADS + h:B_HEADS + h + 1]
        grow = gc_rows[B_HEADS + h:B_HEADS + h + 1, :]
        dec = jnp.exp(jnp.where(incl, gcol - grow, -jnp.inf))
        kb = kh * beta
        egc = jnp.exp(gcol)
        p = -(_bdot_nt(kb, kh) * jnp.where(strict, dec, 0.0))
        t = eye + p
        for _ in range(5):
            p = _bdot(p, p)
            t = t + _bdot(t, p)
        sol = _bdot(t, jnp.concatenate([vh * beta, kb * egc], axis=1))
        u, w = sol[:, :B_DK], sol[:, B_DK:]
        qk = _bdot_nt(qh, kh) * dec
        qg = qh * egc
        kdec = kh * jnp.exp(glcol - gcol)
        state = s_ref[h]
        vnew, inter = [], []
        for ci in range(n // CHUNK):
            rs = slice(ci * CHUNK, (ci + 1) * CHUNK)
            vn = u[rs] - _bdot(w[rs], state)
            inter.append(_bdot(qg[rs], state))
            state = state * jnp.exp(glcol[ci * CHUNK:ci * CHUNK + 1, :]) + _bdot_tn(kdec[rs], vn)
            vnew.append(vn)
        s_ref[h] = state
        o = jnp.concatenate(inter, axis=0) + _bdot(qk, jnp.concatenate(vnew, axis=0))
        o_ref[:, sl] = (_rms(o, og_ref[...]) * _silu(z_ref[:, sl].astype(F32))).astype(o_ref.dtype)


def _gated_deltanet(bqkv, bz, gp, conv_w, a_log, dt_bias, out_gain, lp):
    rows = bqkv.shape[0]
    n = ROW_TILE
    head_vec = lambda p: jnp.zeros((1, LANES), F32).at[0, B_HEADS:2 * B_HEADS].set(p.astype(F32))
    row_spec = lambda w: pl.BlockSpec((n, w), lambda i: (i, 0))
    return pl.pallas_call(
        functools.partial(_gdn_kernel, steps_per_batch=lp // n),
        grid=(rows // n,),
        in_specs=[row_spec(3 * B_W), row_spec(B_W), row_spec(LANES), _const_spec((CONV_K, 3 * B_W)),
                  _const_spec((1, LANES)), _const_spec((1, LANES)), _const_spec((1, B_DK))],
        out_specs=row_spec(B_W),
        out_shape=jax.ShapeDtypeStruct((rows, B_W), BF16),
        scratch_shapes=[pltpu.VMEM((n + 8, 3 * B_W), F32), pltpu.VMEM((B_HEADS, B_DK, B_DK), F32)],
        compiler_params=pltpu.CompilerParams(dimension_semantics=("arbitrary",),
                                             vmem_limit_bytes=VMEM_LIMIT),
        name="gated_deltanet",
    )(bqkv, bz, gp, conv_w.astype(F32), head_vec(a_log), head_vec(dt_bias), out_gain[None, :].astype(F32))


FF_COLS = 256


def _even_out_kernel(h_ref, oa_ref, ob_ref, woa_ref, wob_ref, g_ref, w1_ref, w3_ref, w2_ref, o_ref,
                     act_ref, *, tm, lp, seq):
    h1 = (h_ref[...] + jnp.dot(oa_ref[...], woa_ref[...], preferred_element_type=F32)
          + jnp.dot(ob_ref[...], wob_ref[...], preferred_element_type=F32))
    xn = _rms(h1, g_ref[...]).astype(BF16)
    for c in range(D_FF // FF_COLS):
        cs = slice(c * FF_COLS, (c + 1) * FF_COLS)
        a = jnp.dot(xn, w1_ref[:, cs], preferred_element_type=F32)
        b = jnp.dot(xn, w3_ref[:, cs], preferred_element_type=F32)
        act_ref[:, cs] = (_silu(a) * b).astype(BF16)
    out = h1 + jnp.dot(act_ref[...], w2_ref[...], preferred_element_type=F32)
    o_ref[...] = jnp.where(_row_valid(tm, lp, seq), out, 0.0)


def _even_out(h, oa, ob, w_out, gain, w1, w3, w2, lp, seq, tm):
    rows = h.shape[0]
    wo = w_out.astype(BF16)
    row_spec = lambda w: pl.BlockSpec((tm, w), lambda i: (i, 0))
    return pl.pallas_call(
        functools.partial(_even_out_kernel, tm=tm, lp=lp, seq=seq),
        grid=(rows // tm,),
        in_specs=[row_spec(D_MODEL), row_spec(A_W), row_spec(B_W),
                  _const_spec((A_W, D_MODEL)), _const_spec((B_W, D_MODEL)), _const_spec((1, D_MODEL)),
                  _const_spec(w1.shape), _const_spec(w3.shape), _const_spec(w2.shape)],
        out_specs=row_spec(D_MODEL),
        out_shape=jax.ShapeDtypeStruct((rows, D_MODEL), F32),
        scratch_shapes=[pltpu.VMEM((tm, D_FF), BF16)],
        compiler_params=pltpu.CompilerParams(dimension_semantics=("arbitrary",),
                                             vmem_limit_bytes=VMEM_LIMIT),
        name="even_out_ffn",
    )(h, oa, ob, wo[:A_W], wo[A_W:], gain[None, :].astype(F32),
      w1.astype(BF16), w3.astype(BF16), w2.astype(BF16))


def _odd_in_kernel(h_ref, g_ref, wq_ref, wf_ref, wi_ref, wz_ref, lb_ref,
                   q_ref, k_ref, lf_ref, i_ref, z_ref, *, tm, lp, seq):
    xn = _rms(h_ref[...], g_ref[...]).astype(BF16)
    valid = _row_valid(tm, lp, seq)
    q_ref[...] = jnp.dot(xn, wq_ref[...], preferred_element_type=F32).astype(BF16)
    f = jnp.dot(xn, wf_ref[...], preferred_element_type=F32)
    lb = lb_ref[...]
    fg = lb + (1.0 - lb) * _sigmoid(f)
    k_ref[...] = jnp.where(valid, 1.0 - fg, 0.0).astype(BF16)
    lf_ref[...] = jnp.where(valid, jnp.log(fg), 0.0)
    i_ref[...] = jnp.dot(xn, wi_ref[...], preferred_element_type=F32).astype(BF16)
    z_ref[...] = jnp.dot(xn, wz_ref[...], preferred_element_type=F32).astype(BF16)


def _odd_in(h, gain, w_in, lb, lp, seq, tm):
    rows = h.shape[0]
    wb16 = w_in.astype(BF16)
    ws = [wb16[:, j * D_MODEL:(j + 1) * D_MODEL] for j in range(4)]
    row_spec = pl.BlockSpec((tm, D_MODEL), lambda i: (i, 0))
    wspec = _const_spec((D_MODEL, D_MODEL))
    vec = _const_spec((1, D_MODEL))
    sd = lambda dt: jax.ShapeDtypeStruct((rows, D_MODEL), dt)
    return pl.pallas_call(
        functools.partial(_odd_in_kernel, tm=tm, lp=lp, seq=seq),
        grid=(rows // tm,),
        in_specs=[row_spec, vec, wspec, wspec, wspec, wspec, vec],
        out_specs=(row_spec,) * 5,
        out_shape=(sd(BF16), sd(BF16), sd(F32), sd(BF16), sd(BF16)),
        compiler_params=pltpu.CompilerParams(dimension_semantics=("arbitrary",),
                                             vmem_limit_bytes=VMEM_LIMIT),
        name="odd_in_proj",
    )(h, gain[None, :].astype(F32), *ws, lb[None, :].astype(F32))


SUB = 16


def _hgrn_kernel(q_ref, k_ref, lf_ref, i_ref, z_ref, og_ref, o_ref, s_ref, *, steps_per_batch):
    n = ROW_TILE
    nsub = CHUNK // SUB
    step = pl.program_id(0) % steps_per_batch

    @pl.when(step == 0)
    def _():
        s_ref[...] = jnp.zeros_like(s_ref)

    r, c, same = _chunk_masks(n)
    incl = same & (r >= c)
    gc_all = jnp.dot(incl.astype(F32), lf_ref[...], precision=HIGHEST, preferred_element_type=F32)
    rrow = lax.broadcasted_iota(jnp.int32, (n, 1), 0)
    subpos = (rrow % CHUNK) // SUB
    inner = rrow % SUB

    def chunk_rows(a, off):
        return jnp.concatenate(
            [jnp.broadcast_to(a[ci * CHUNK + off:ci * CHUNK + off + 1, :], (CHUNK, a.shape[1]))
             for ci in range(n // CHUNK)], axis=0)

    def sub_rows(a, j):
        a3 = a.reshape(n // SUB, SUB, a.shape[1])
        return jnp.broadcast_to(a3[:, j:j + 1, :], a3.shape).reshape(a.shape)

    for h in range(C_HEADS):
        sl = slice(h * C_DK, (h + 1) * C_DK)
        q = q_ref[:, sl].astype(F32)
        k = k_ref[:, sl].astype(F32)
        iv = i_ref[:, sl]
        gc = gc_all[:, sl]
        gl = chunk_rows(gc, CHUNK - 1)
        qg = q * jnp.exp(gc)
        kdec = k * jnp.exp(gl - gc)
        grefs = [chunk_rows(gc, p * SUB - 1) for p in range(1, nsub)]
        gown = grefs[0]
        for p in range(2, nsub):
            gown = jnp.where(subpos == p, grefs[p - 1], gown)
        qt = q * jnp.exp(jnp.where(subpos >= 1, gc - gown, -jnp.inf))
        qts = [jnp.where(subpos == p, qt, 0.0) for p in range(1, nsub)]
        kts = [k * jnp.exp(jnp.where(subpos < p, grefs[p - 1] - gc, -jnp.inf)) for p in range(1, nsub)]
        att = jnp.where(same, _bdot_nt(jnp.concatenate(qts, axis=1), jnp.concatenate(kts, axis=1)), 0.0)
        cbase = (r // SUB) * SUB
        for j in range(SUB):
            kj = sub_rows(k, j)
            gj = sub_rows(gc, j)
            pj = q * kj * jnp.exp(jnp.where(inner >= j, gc - gj, -jnp.inf))
            att = att + jnp.where(c == cbase + j, jnp.sum(pj, axis=-1, keepdims=True), 0.0)
        st = s_ref[h]
        inter = []
        for ci in range(n // CHUNK):
            rs = slice(ci * CHUNK, (ci + 1) * CHUNK)
            inter.append(_bdot_nt(qg[rs], st))
            st = st * jnp.exp(gl[ci * CHUNK:ci * CHUNK + 1, :]) + _bdot_tn(iv[rs], kdec[rs])
        s_ref[h] = st
        o = jnp.concatenate(inter, axis=0) + _bdot(att, iv)
        o_ref[:, sl] = (_rms(o, og_ref[...]) * _silu(z_ref[:, sl].astype(F32))).astype(o_ref.dtype)


def _hgrn2(q, k, lf, iv, z, out_gain, lp):
    rows = q.shape[0]
    n = ROW_TILE
    row_spec = pl.BlockSpec((n, D_MODEL), lambda i: (i, 0))
    return pl.pallas_call(
        functools.partial(_hgrn_kernel, steps_per_batch=lp // n),
        grid=(rows // n,),
        in_specs=[row_spec] * 5 + [_const_spec((1, C_DK))],
        out_specs=row_spec,
        out_shape=jax.ShapeDtypeStruct((rows, D_MODEL), BF16),
        scratch_shapes=[pltpu.VMEM((C_HEADS, C_DK, C_DK), F32)],
        compiler_params=pltpu.CompilerParams(dimension_semantics=("arbitrary",),
                                             vmem_limit_bytes=VMEM_LIMIT),
        name="hgrn2",
    )(q, k, lf, iv, z, out_gain[None, :].astype(F32))


def _odd_out_kernel(h_ref, o_ref, wo_ref, g_ref, wr_ref, h3_ref, xn_ref, route_ref, *, tm, lp, seq):
    h3 = h_ref[...] + jnp.dot(o_ref[...], wo_ref[...], preferred_element_type=F32)
    h3 = jnp.where(_row_valid(tm, lp, seq), h3, 0.0)
    h3_ref[...] = h3
    xn = _rms(h3, g_ref[...])
    xn_ref[...] = xn
    logits = jnp.dot(xn, wr_ref[...], precision=HIGHEST, preferred_element_type=F32)
    lane = lax.broadcasted_iota(jnp.int32, logits.shape, 1)
    logits = jnp.where(lane < N_EXPERTS, logits, -jnp.inf)
    v1 = jnp.max(logits, axis=-1, keepdims=True)
    e1 = jnp.min(jnp.where(logits == v1, lane, LANES), axis=-1, keepdims=True)
    rest = jnp.where(lane == e1, -jnp.inf, logits)
    v2 = jnp.max(rest, axis=-1, keepdims=True)
    e2 = jnp.min(jnp.where(rest == v2, lane, LANES), axis=-1, keepdims=True)
    t = jnp.exp(v2 - v1)
    g1 = 1.0 / (1.0 + t)
    g2 = t * g1
    route_ref[...] = jnp.where(lane == 0, e1.astype(F32),
                               jnp.where(lane == 1, e2.astype(F32),
                                         jnp.where(lane == 2, g1, jnp.where(lane == 3, g2, 0.0))))


def _odd_out(h, o, w_out, gain, router, lp, seq, tm):
    rows = h.shape[0]
    wr = jnp.pad(router.astype(F32), ((0, 0), (0, LANES - N_EXPERTS)))
    row_spec = lambda w: pl.BlockSpec((tm, w), lambda i: (i, 0))
    return pl.pallas_call(
        functools.partial(_odd_out_kernel, tm=tm, lp=lp, seq=seq),
        grid=(rows // tm,),
        in_specs=[row_spec(D_MODEL), row_spec(D_MODEL), _const_spec((D_MODEL, D_MODEL)),
                  _const_spec((1, D_MODEL)), _const_spec((D_MODEL, LANES))],
        out_specs=(row_spec(D_MODEL), row_spec(D_MODEL), row_spec(LANES)),
        out_shape=(jax.ShapeDtypeStruct((rows, D_MODEL), F32), jax.ShapeDtypeStruct((rows, D_MODEL), F32),
                   jax.ShapeDtypeStruct((rows, LANES), F32)),
        compiler_params=pltpu.CompilerParams(dimension_semantics=("arbitrary",),
                                             vmem_limit_bytes=VMEM_LIMIT),
        name="odd_out_route",
    )(h, o, w_out.astype(BF16), gain[None, :].astype(F32), wr)


MOE_TILE = 512
MOE_FF = 512


def _moe_plan(route, tile):
    n_tok = route.shape[0]
    flat_e = route[:, :2].astype(jnp.int32).reshape(-1)
    na = flat_e.shape[0]
    onehot = (flat_e[:, None] == jnp.arange(N_EXPERTS, dtype=jnp.int32)[None, :]).astype(jnp.int32)
    csum = jnp.cumsum(onehot, axis=0)
    rank = jnp.take_along_axis(csum, flat_e[:, None], axis=1)[:, 0] - 1
    counts = csum[-1]
    padded = (counts + tile - 1) // tile * tile
    pend = jnp.cumsum(padded)
    dest = (pend - padded)[flat_e] + rank
    n_tiles = -(-(na + N_EXPERTS * (tile - 1)) // tile)
    row_tok = jnp.zeros((n_tiles * tile,), jnp.int32).at[dest].set(jnp.arange(na, dtype=jnp.int32) // 2)
    tile_e = jnp.minimum(jnp.searchsorted(pend, jnp.arange(n_tiles, dtype=jnp.int32) * tile, side='right'),
                         N_EXPERTS - 1).astype(jnp.int32)
    n_used = (pend[-1] // tile).astype(jnp.int32).reshape(1)
    return dest.reshape(n_tok, 2), row_tok, tile_e, n_used, n_tiles


def _moe_gather_kernel(tok_ref, nused_ref, x_hbm, o_ref, buf_ref, sem):
    i = pl.program_id(0)
    tile = buf_ref.shape[0]

    @pl.when(i < nused_ref[0])
    def _():
        def issue(r, carry):
            tok = tok_ref[0, 0, r]
            pltpu.make_async_copy(x_hbm.at[pl.ds(tok, 1), :], buf_ref.at[pl.ds(r, 1), :], sem).start()
            return carry
        lax.fori_loop(0, tile, issue, 0)
        pltpu.make_async_copy(x_hbm.at[pl.ds(0, tile), :], buf_ref, sem).wait()
        o_ref[...] = buf_ref[...].astype(o_ref.dtype)

    @pl.when(i >= nused_ref[0])
    def _():
        o_ref[...] = jnp.zeros_like(o_ref)


def _moe_gather(xn, row_tok, n_used, n_tiles, tile):
    return pl.pallas_call(
        _moe_gather_kernel,
        grid=(n_tiles,),
        in_specs=[pl.BlockSpec((1, 1, tile), lambda i: (i, 0, 0), memory_space=pltpu.SMEM),
                  pl.BlockSpec(memory_space=pltpu.SMEM),
                  pl.BlockSpec(memory_space=pl.ANY)],
        out_specs=pl.BlockSpec((tile, D_MODEL), lambda i: (i, 0)),
        out_shape=jax.ShapeDtypeStruct((n_tiles * tile, D_MODEL), BF16),
        scratch_shapes=[pltpu.VMEM((tile, D_MODEL), F32), pltpu.SemaphoreType.DMA(())],
        compiler_params=pltpu.CompilerParams(dimension_semantics=("arbitrary",),
                                             vmem_limit_bytes=VMEM_LIMIT),
        name="moe_gather",
    )(row_tok.reshape(n_tiles, 1, tile), n_used, xn)


def _moe_ffn_kernel(te_ref, nused_ref, x_ref, w1_ref, w3_ref, w2_ref, y_ref):
    i, f = pl.program_id(0), pl.program_id(1)

    @pl.when(f == 0)
    def _():
        y_ref[...] = jnp.zeros_like(y_ref)

    @pl.when(i < nused_ref[0])
    def _():
        x = x_ref[...]
        a = jnp.dot(x, w1_ref[0], preferred_element_type=F32)
        b = jnp.dot(x, w3_ref[0], preferred_element_type=F32)
        y_ref[...] += jnp.dot((_silu(a) * b).astype(BF16), w2_ref[0], preferred_element_type=F32)


def _moe_ffn(xs, w1, w3, w2, tile_e, n_used, n_tiles, tile):
    nf = D_FF_EXPERT // MOE_FF
    live = lambda i, nu: i < nu[0]
    xi = lambda i, f, te, nu: (jnp.where(live(i, nu), i, nu[0] - 1), 0)
    fi = lambda i, f, nu: jnp.where(live(i, nu), f, nf - 1)
    grid_spec = pltpu.PrefetchScalarGridSpec(
        num_scalar_prefetch=2,
        grid=(n_tiles, nf),
        in_specs=[pl.BlockSpec((tile, D_MODEL), xi),
                  pl.BlockSpec((1, D_MODEL, MOE_FF), lambda i, f, te, nu: (te[i], 0, fi(i, f, nu))),
                  pl.BlockSpec((1, D_MODEL, MOE_FF), lambda i, f, te, nu: (te[i], 0, fi(i, f, nu))),
                  pl.BlockSpec((1, MOE_FF, D_MODEL), lambda i, f, te, nu: (te[i], fi(i, f, nu), 0))],
        out_specs=pl.BlockSpec((tile, D_MODEL), lambda i, f, te, nu: (i, 0)),
    )
    return pl.pallas_call(
        _moe_ffn_kernel,
        grid_spec=grid_spec,
        out_shape=jax.ShapeDtypeStruct((n_tiles * tile, D_MODEL), F32),
        compiler_params=pltpu.CompilerParams(dimension_semantics=("arbitrary", "arbitrary"),
                                             vmem_limit_bytes=VMEM_LIMIT),
        name="moe_ffn",
    )(tile_e, n_used, xs, w1, w3, w2)


def _moe_combine_kernel(pos_ref, gate_ref, h_hbm, y_hbm, o_ref, hbuf, ybuf, sem_h, sem_y, *, tc, lp):
    b, j = pl.program_id(0), pl.program_id(1)
    h_copy = pltpu.make_async_copy(h_hbm.at[pl.ds(b * lp + CHUNK + j * tc, tc), :], hbuf, sem_h)
    h_copy.start()

    def issue(r, carry):
        p = pos_ref[0, 0, r]
        pltpu.make_async_copy(y_hbm.at[pl.ds(p, 1), :], ybuf.at[pl.ds(r, 1), :], sem_y).start()
        return carry
    lax.fori_loop(0, 2 * tc, issue, 0)
    pltpu.make_async_copy(y_hbm.at[pl.ds(0, 2 * tc), :], ybuf, sem_y).wait()
    h_copy.wait()
    g = gate_ref[0]
    o_ref[0] = hbuf[...] + g[:, 0:1] * ybuf[:tc, :] + g[:, 1:2] * ybuf[tc:, :]


def _moe_combine(h3, y, dest, gates, batch, seq, lp, tc):
    nj = seq // tc
    pos = dest.reshape(batch * nj, tc, 2).transpose(0, 2, 1).reshape(batch * nj, 1, 2 * tc)
    return pl.pallas_call(
        functools.partial(_moe_combine_kernel, tc=tc, lp=lp),
        grid=(batch, nj),
        in_specs=[pl.BlockSpec((1, 1, 2 * tc), lambda b, j: (b * nj + j, 0, 0), memory_space=pltpu.SMEM),
                  pl.BlockSpec((1, tc, 2), lambda b, j: (b, j, 0)),
                  pl.BlockSpec(memory_space=pl.ANY), pl.BlockSpec(memory_space=pl.ANY)],
        out_specs=pl.BlockSpec((1, tc, D_MODEL), lambda b, j: (b, j, 0)),
        out_shape=jax.ShapeDtypeStruct((batch, seq, D_MODEL), F32),
        scratch_shapes=[pltpu.VMEM((tc, D_MODEL), F32), pltpu.VMEM((2 * tc, D_MODEL), F32),
                        pltpu.SemaphoreType.DMA(()), pltpu.SemaphoreType.DMA(())],
        compiler_params=pltpu.CompilerParams(dimension_semantics=("arbitrary", "arbitrary"),
                                             vmem_limit_bytes=VMEM_LIMIT),
        name="moe_combine",
    )(pos, gates, h3, y)


def _layout_rows(seq):
    return -(-(CHUNK + seq) // ROW_TILE) * ROW_TILE


def _proj_tile(lp):
    return next(t for t in (640, 512, 384, ROW_TILE) if lp % t == 0)


def kernel(x, meta_tokens, lb_logits, mix_norm, ffn_norm, e_w_in, a_q_gain, a_k_gain, a_lam_q1, a_lam_k1, a_lam_q2, a_lam_k2, a_sub_gain, b_conv, b_a_log, b_dt_bias, b_out_gain, e_w_out, ffn_w1, ffn_w3, ffn_w2, o_w_in, c_out_gain, o_w_out, router, moe_w1, moe_w3, moe_w2):
    batch, seq, _ = x.shape
    lp = _layout_rows(seq)
    rows = batch * lp
    tm = _proj_tile(lp)
    meta = jnp.broadcast_to(meta_tokens[None].astype(x.dtype), (batch, N_META, D_MODEL))
    h = jnp.concatenate([jnp.zeros((batch, FRONT, D_MODEL), x.dtype), meta, x,
                         jnp.zeros((batch, lp - CHUNK - seq, D_MODEL), x.dtype)], axis=1).reshape(rows, D_MODEL)

    q, k, v, bqkv, bz, gp = _even_in(h, mix_norm[0], e_w_in[0], a_q_gain[0], a_k_gain[0], lp, tm)
    lam_init = 0.8 - 0.6 * math.exp(-0.3 * 0)
    lam = (jnp.exp(jnp.sum(a_lam_q1[0].astype(F32) * a_lam_k1[0].astype(F32)))
           - jnp.exp(jnp.sum(a_lam_q2[0].astype(F32) * a_lam_k2[0].astype(F32))) + lam_init)
    o_a = _diff_attention(q, k, v, a_sub_gain[0], lam, lam_init, batch, lp, ROW_TILE)
    o_b = _gated_deltanet(bqkv, bz, gp, b_conv[0], b_a_log[0], b_dt_bias[0], b_out_gain[0], lp)
    h = _even_out(h, o_a, o_b, e_w_out[0], ffn_norm[0], ffn_w1[0], ffn_w3[0], ffn_w2[0], lp, seq, tm)

    lb_all = jnp.cumsum(jax.nn.softmax(lb_logits.astype(F32), axis=0), axis=0)
    lb = (lb_all - lb_all[0])[1]
    cq, ck, clf, ci, cz = _odd_in(h, mix_norm[1], o_w_in[0], lb, lp, seq, tm)
    o_c = _hgrn2(cq, ck, clf, ci, cz, c_out_gain[0], lp)
    h3, xn, route = _odd_out(h, o_c, o_w_out[0], ffn_norm[1], router[0], lp, seq, tm)
    dest, row_tok, tile_e, n_used, n_tiles = _moe_plan(route, MOE_TILE)
    xs = _moe_gather(xn, row_tok, n_used, n_tiles, MOE_TILE)
    y = _moe_ffn(xs, moe_w1[0].astype(BF16), moe_w3[0].astype(BF16), moe_w2[0].astype(BF16),
                 tile_e, n_used, n_tiles, MOE_TILE)
    frames = lambda a: a.reshape(batch, lp, a.shape[-1])[:, CHUNK:CHUNK + seq]
    return _moe_combine(h3, y, frames(dest), frames(route[:, 2:4]), batch, seq, lp, ROW_TILE)
```

```python
import functools
import math

import jax
import jax.numpy as jnp
from jax import lax
from jax.experimental import pallas as pl
from jax.experimental.pallas import tpu as pltpu

D_MODEL = 1024
CHUNK = 64
N_META = 16
FRONT = CHUNK - N_META
EPS = 1e-6
ROPE_THETA = 500000.0
LOG2E = math.log2(math.e)

A_HEADS = 4
A_DK = 64
A_DV = 128
A_ROT = A_DK // 4
A_W = A_HEADS * A_DV

B_HEADS = 4
B_DK = 128
B_W = B_HEADS * B_DK
CONV_K = 4

C_HEADS = 8
C_DK = 128

D_FF = 2816
N_EXPERTS = 8
D_FF_EXPERT = 3584

LANES = 128
ROW_TILE = 256
VMEM_LIMIT = 56 * 1024 * 1024

BF16 = jnp.bfloat16
F32 = jnp.float32
HIGHEST = lax.Precision.HIGHEST


def _const_spec(shape):
    return pl.BlockSpec(shape, lambda *_: (0,) * len(shape), pipeline_mode=pl.Buffered(1))


def _rms(x, gain):
    return x * lax.rsqrt(jnp.mean(x * x, axis=-1, keepdims=True) + EPS) * gain


def _silu(x):
    return x * (1.0 / (1.0 + jnp.exp(-x)))


def _sigmoid(x):
    return 1.0 / (1.0 + jnp.exp(-x))


def _row_valid(tile_rows, lp, seq):
    r = pl.program_id(0) * tile_rows + lax.broadcasted_iota(jnp.int32, (tile_rows, 1), 0)
    p = r % lp
    return (p >= FRONT) & (p < CHUNK + seq)


def _head_norm_rope(a, ones_bd, gain, cos, sin_lo, sin_hi):
    ssq = jnp.dot((a * a).astype(BF16), ones_bd, preferred_element_type=F32)
    y = a * lax.rsqrt(ssq * (1.0 / A_DK) + EPS) * gain
    up = pltpu.roll(y, A_W - A_ROT // 2, axis=1)
    dn = pltpu.roll(y, A_ROT // 2, axis=1)
    return y * cos + up * sin_lo + dn * sin_hi


def _even_in_kernel(h_ref, g_ref, wq_ref, wk_ref, wv_ref, wb_ref, wz_ref, wg_ref, bd_ref,
                    qg_ref, kg_ref, cos_ref, slo_ref, shi_ref,
                    q_ref, k_ref, v_ref, b_ref, z_ref, gp_ref):
    xn = _rms(h_ref[...], g_ref[...]).astype(BF16)
    rep = lambda t: jnp.concatenate([t] * (A_W // LANES), axis=1)
    cos, slo, shi = rep(cos_ref[...]), rep(slo_ref[...]), rep(shi_ref[...])
    bd = bd_ref[...]
    aq = jnp.dot(xn, wq_ref[...], preferred_element_type=F32)
    q_ref[...] = (_head_norm_rope(aq, bd, qg_ref[...], cos, slo, shi) * (A_DK ** -0.5 * LOG2E)).astype(BF16)
    ak = jnp.dot(xn, wk_ref[...], preferred_element_type=F32)
    k_ref[...] = _head_norm_rope(ak, bd, kg_ref[...], cos, slo, shi).astype(BF16)
    v_ref[...] = jnp.dot(xn, wv_ref[...], preferred_element_type=F32).astype(BF16)
    b_ref[...] = jnp.dot(xn, wb_ref[...], preferred_element_type=F32).astype(BF16)
    z_ref[...] = jnp.dot(xn, wz_ref[...], preferred_element_type=F32).astype(BF16)
    gp_ref[...] = jnp.dot(xn, wg_ref[...], preferred_element_type=F32)


def _even_in(h, gain, w_in, q_gain, k_gain, lp, tm):
    rows = h.shape[0]
    wb16 = w_in.astype(BF16)
    c0, c1, c2, c3, c4 = A_W, 2 * A_W, 3 * A_W, 3 * A_W + 3 * B_W, 3 * A_W + 4 * B_W
    wq, wk, wv, wb, wz = wb16[:, :c0], wb16[:, c0:c1], wb16[:, c1:c2], wb16[:, c2:c3], wb16[:, c3:c4]
    wg = jnp.pad(wb16[:, c4:], ((0, 0), (0, LANES - 2 * B_HEADS)))
    d = jnp.arange(A_W)
    ones_bd = (d[:, None] // A_DK == d[None, :] // A_DK).astype(BF16)
    half = A_ROT // 2
    inv_freq = ROPE_THETA ** (-jnp.arange(half, dtype=F32) * 2.0 / A_ROT)
    pos = (jnp.arange(lp) - FRONT).astype(F32)
    ang = pos[:, None] * inv_freq[None, :]
    cosv, sinv = jnp.cos(ang), jnp.sin(ang)
    zeros = jnp.zeros((lp, A_DK - A_ROT), F32)
    one = jnp.ones((lp, A_DK - A_ROT), F32)
    zh = jnp.zeros((lp, half), F32)
    cos64 = jnp.concatenate([cosv, cosv, one], axis=1)
    slo64 = jnp.concatenate([-sinv, zh, zeros], axis=1)
    shi64 = jnp.concatenate([zh, sinv, zeros], axis=1)
    two = lambda t: jnp.concatenate([t, t], axis=1)
    cos_t, slo_t, shi_t = two(cos64), two(slo64), two(shi64)
    tile8 = lambda g: jnp.tile(g.astype(F32), 2 * A_HEADS)[None, :]
    n_pos = lp // tm
    row_spec = lambda w: pl.BlockSpec((tm, w), lambda i: (i, 0))
    pos_spec = pl.BlockSpec((tm, LANES), lambda i: (i % n_pos, 0))
    outs = pl.pallas_call(
        _even_in_kernel,
        grid=(rows // tm,),
        in_specs=[row_spec(D_MODEL), _const_spec((1, D_MODEL)),
                  _const_spec(wq.shape), _const_spec(wk.shape), _const_spec(wv.shape),
                  _const_spec(wb.shape), _const_spec(wz.shape), _const_spec(wg.shape),
                  _const_spec(ones_bd.shape), _const_spec((1, A_W)), _const_spec((1, A_W)),
                  pos_spec, pos_spec, pos_spec],
        out_specs=(row_spec(A_W), row_spec(A_W), row_spec(A_W), row_spec(3 * B_W), row_spec(B_W),
                   row_spec(LANES)),
        out_shape=(jax.ShapeDtypeStruct((rows, A_W), BF16), jax.ShapeDtypeStruct((rows, A_W), BF16),
                   jax.ShapeDtypeStruct((rows, A_W), BF16), jax.ShapeDtypeStruct((rows, 3 * B_W), BF16),
                   jax.ShapeDtypeStruct((rows, B_W), BF16), jax.ShapeDtypeStruct((rows, LANES), F32)),
        compiler_params=pltpu.CompilerParams(dimension_semantics=("arbitrary",),
                                             vmem_limit_bytes=VMEM_LIMIT),
        name="even_in_proj",
    )(h, gain[None, :].astype(F32), wq, wk, wv, wb, wz, wg, ones_bd, tile8(q_gain), tile8(k_gain),
      cos_t, slo_t, shi_t)
    return outs


ATTN_WIDE = 1024


def _attn_kernel(q_ref, k_ref, v_ref, sg_ref, lam_ref, o_ref, qs_ref, vx_ref, m_ref, acc_ref,
                 sa_ref, sb_ref, *, tq, width, lam_init):
    qi = pl.program_id(2)

    @pl.when(qi == 0)
    def _():
        vx_ref[:, :A_DV] = v_ref[...]
        vx_ref[:, A_DV:] = jnp.ones((vx_ref.shape[0], A_DV), BF16)

    q = q_ref[...]
    lane = lax.broadcasted_iota(jnp.int32, q.shape, 1)
    zero = jnp.zeros_like(q)
    qs_ref[:tq, :] = jnp.where(lane < A_DK, q, zero)
    qs_ref[tq:, :] = jnp.where(lane >= A_DK, q, zero)
    m_ref[...] = jnp.full(m_ref.shape, -jnp.inf, F32)
    acc_ref[...] = jnp.zeros(acc_ref.shape, F32)

    lp = k_ref.shape[0]
    qchunk = (qi * tq + lax.broadcasted_iota(jnp.int32, (2 * tq, 1), 0) % tq) // CHUNK

    def window(t):
        return pl.multiple_of(jnp.minimum(t * width, lp - width), tq)

    def scores(s_ref, t):
        s_ref[...] = lax.dot_general(qs_ref[...], k_ref[pl.ds(window(t), width), :],
                                     (((1,), (1,)), ((), ())), preferred_element_type=F32)

    def softmax_pv(s_ref, t, mask):
        start = window(t)
        kpos = start + lax.broadcasted_iota(jnp.int32, (1, width), 1)
        kchunk = jnp.where((kpos >= t * width) & (kpos >= FRONT), kpos // CHUNK, lp)
        nblk = width // LANES

        def block(c):
            cs = slice(c * LANES, (c + 1) * LANES)
            if mask == "full":
                return jnp.where(kchunk[:, cs] <= qchunk, s_ref[:, cs], -jnp.inf)
            if mask == "pad" and c == 0:
                return jnp.where(kpos[:, cs] >= FRONT, s_ref[:, cs], -jnp.inf)
            return s_ref[:, cs]

        mx = functools.reduce(jnp.maximum, [block(c) for c in range(nblk)])
        m_old = m_ref[...]
        m_new = jnp.maximum(m_old, jnp.max(mx, axis=-1, keepdims=True))
        alpha = jnp.exp2(m_old - m_new)
        p = jnp.concatenate([jnp.exp2((block(c) - m_new).astype(BF16)) for c in range(nblk)], axis=1)
        pv = jnp.dot(p, vx_ref[pl.ds(start, width), :], preferred_element_type=F32)
        acc_ref[:, :A_DV] = alpha * acc_ref[:, :A_DV] + pv[:, :A_DV]
        acc_ref[:, A_DV:] = alpha * acc_ref[:, A_DV:] + pv[:, A_DV:]
        m_ref[...] = m_new

    n_pairs = ((qi * tq + tq + width - 1) // width + 1) // 2

    def pair(j, masks):
        scores(sb_ref, 2 * j + 1)
        softmax_pv(sa_ref, 2 * j, masks[0])
        scores(sa_ref, 2 * j + 2)
        softmax_pv(sb_ref, 2 * j + 1, masks[1])

    scores(sa_ref, 0)

    @pl.when(n_pairs > 1)
    def _():
        pair(0, ("pad", "none"))

    def inner(j, carry):
        pair(j, ("none", "none"))
        return carry
    lax.fori_loop(1, n_pairs - 1, inner, 0)
    scores(sb_ref, 2 * n_pairs - 1)
    softmax_pv(sa_ref, 2 * n_pairs - 2, "full")
    softmax_pv(sb_ref, 2 * n_pairs - 1, "full")

    o = acc_ref[:, :A_DV] / acc_ref[:, A_DV:]
    o = o[:tq] - lam_ref[...] * o[tq:]
    o_ref[...] = (_rms(o, sg_ref[...]) * (1.0 - lam_init)).astype(o_ref.dtype)


def _diff_attention(q, k, v, sub_gain, lam, lam_init, batch, lp, tq):
    nq = lp // tq
    kv_spec = pl.BlockSpec((lp, LANES), lambda b, h, i: (b, h))
    q_spec = pl.BlockSpec((tq, LANES), lambda b, h, i: (b * nq + i, h))
    vec = pl.BlockSpec((1, LANES), lambda b, h, i: (0, 0))
    width = min(ATTN_WIDE, lp)
    return pl.pallas_call(
        functools.partial(_attn_kernel, tq=tq, width=width, lam_init=lam_init),
        grid=(batch, A_HEADS, nq),
        in_specs=[q_spec, kv_spec, kv_spec, vec, vec],
        out_specs=q_spec,
        out_shape=jax.ShapeDtypeStruct(q.shape, BF16),
        scratch_shapes=[pltpu.VMEM((2 * tq, LANES), BF16), pltpu.VMEM((lp, 2 * A_DV), BF16),
                        pltpu.VMEM((2 * tq, LANES), F32), pltpu.VMEM((2 * tq, 2 * A_DV), F32),
                        pltpu.VMEM((2 * tq, width), F32), pltpu.VMEM((2 * tq, width), F32)],
        compiler_params=pltpu.CompilerParams(dimension_semantics=("arbitrary",) * 3,
                                             vmem_limit_bytes=VMEM_LIMIT),
        name="diff_attention",
    )(q, k, v, sub_gain[None, :].astype(F32), jnp.full((1, LANES), lam, F32))


def _chunk_masks(n):
    r = lax.broadcasted_iota(jnp.int32, (n, n), 0)
    c = lax.broadcasted_iota(jnp.int32, (n, n), 1)
    same = (r // CHUNK) == (c // CHUNK)
    return r, c, same


def _bdot(a, b):
    return jnp.dot(a.astype(BF16), b.astype(BF16), preferred_element_type=F32)


def _bdot_nt(a, b):
    return lax.dot_general(a.astype(BF16), b.astype(BF16), (((1,), (1,)), ((), ())),
                           preferred_element_type=F32)


def _bdot_tn(a, b):
    return lax.dot_general(a.astype(BF16), b.astype(BF16), (((0,), (0,)), ((), ())),
                           preferred_element_type=F32)


def _gdn_kernel(x_ref, z_ref, gp_ref, cw_ref, alog_ref, dtb_ref, og_ref, o_ref,
                xbuf_ref, s_ref, *, steps_per_batch):
    n = ROW_TILE
    step = pl.program_id(0) % steps_per_batch

    @pl.when(step == 0)
    def _():
        xbuf_ref[:8, :] = jnp.zeros((8, 3 * B_W), F32)
        s_ref[...] = jnp.zeros_like(s_ref)

    xbuf_ref[8:, :] = x_ref[...].astype(F32)
    conv = cw_ref[CONV_K - 1:CONV_K, :] * xbuf_ref[8:, :]
    for j in range(CONV_K - 1):
        conv = conv + cw_ref[j:j + 1, :] * xbuf_ref[5 + j:5 + j + n, :]
    xbuf_ref[:8, :] = xbuf_ref[n:n + 8, :]
    conv = _silu(conv)

    r, c, same = _chunk_masks(n)
    incl = same & (r >= c)
    strict = same & (r > c)
    eye = (r == c).astype(F32)
    gp = gp_ref[...]
    beta_all = _sigmoid(gp)
    gpre = gp + dtb_ref[...]
    softplus = jnp.maximum(gpre, 0.0) + jnp.log(1.0 + jnp.exp(-jnp.abs(gpre)))
    row = step * n + lax.broadcasted_iota(jnp.int32, (n, 1), 0)
    g_all = jnp.where(row >= FRONT, -jnp.exp(alog_ref[...]) * softplus, 0.0)
    gc_all = jnp.dot(incl.astype(F32), g_all, precision=HIGHEST, preferred_element_type=F32)
    last = (same & (c % CHUNK == CHUNK - 1)).astype(F32)
    gl_all = jnp.dot(last, gc_all, precision=HIGHEST, preferred_element_type=F32)
    gc_rows = gc_all.T

    for h in range(B_HEADS):
        sl = slice(h * B_DK, (h + 1) * B_DK)
        qh = conv[:, sl]
        kh = conv[:, B_W + h * B_DK:B_W + (h + 1) * B_DK]
        vh = conv[:, 2 * B_W + h * B_DK:2 * B_W + (h + 1) * B_DK]
        qh = qh * lax.rsqrt(jnp.sum(qh * qh, axis=-1, keepdims=True) + EPS) * (B_DK ** -0.5)
        kh = kh * lax.rsqrt(jnp.sum(kh * kh, axis=-1, keepdims=True) + EPS)
        beta = beta_all[:, h:h + 1]
        gcol = gc_all[:, B_HEADS + h:B_HEADS + h + 1]
        glcol = gl_all[:, B_HEADS + h:B_HEADS + h + 1]
        grow = gc_rows[B_HEADS + h:B_HEADS + h + 1, :]
        dec = jnp.exp(jnp.where(incl, gcol - grow, -jnp.inf))
        kb = kh * beta
        egc = jnp.exp(gcol)
        p = -(_bdot_nt(kb, kh) * jnp.where(strict, dec, 0.0))
        t = eye + p
        for _ in range(5):
            p = _bdot(p, p)
            t = t + _bdot(t, p)
        sol = _bdot(t, jnp.concatenate([vh * beta, kb * egc], axis=1))
        u, w = sol[:, :B_DK], sol[:, B_DK:]
        qk = _bdot_nt(qh, kh) * dec
        qg = qh * egc
        kdec = kh * jnp.exp(glcol - gcol)
        state = s_ref[h]
        vnew, inter = [], []
        for ci in range(n // CHUNK):
            rs = slice(ci * CHUNK, (ci + 1) * CHUNK)
            vn = u[rs] - _bdot(w[rs], state)
            inter.append(_bdot(qg[rs], state))
            state = state * jnp.exp(glcol[ci * CHUNK:ci * CHUNK + 1, :]) + _bdot_tn(kdec[rs], vn)
            vnew.append(vn)
        s_ref[h] = state
        o = jnp.concatenate(inter, axis=0) + _bdot(qk, jnp.concatenate(vnew, axis=0))
        o_ref[:, sl] = (_rms(o, og_ref[...]) * _silu(z_ref[:, sl].astype(F32))).astype(o_ref.dtype)


def _gated_deltanet(bqkv, bz, gp, conv_w, a_log, dt_bias, out_gain, lp):
    rows = bqkv.shape[0]
    n = ROW_TILE
    head_vec = lambda p: jnp.zeros((1, LANES), F32).at[0, B_HEADS:2 * B_HEADS].set(p.astype(F32))
    row_spec = lambda w: pl.BlockSpec((n, w), lambda i: (i, 0))
    return pl.pallas_call(
        functools.partial(_gdn_kernel, steps_per_batch=lp // n),
        grid=(rows // n,),
        in_specs=[row_spec(3 * B_W), row_spec(B_W), row_spec(LANES), _const_spec((CONV_K, 3 * B_W)),
                  _const_spec((1, LANES)), _const_spec((1, LANES)), _const_spec((1, B_DK))],
        out_specs=row_spec(B_W),
        out_shape=jax.ShapeDtypeStruct((rows, B_W), BF16),
        scratch_shapes=[pltpu.VMEM((n + 8, 3 * B_W), F32), pltpu.VMEM((B_HEADS, B_DK, B_DK), F32)],
        compiler_params=pltpu.CompilerParams(dimension_semantics=("arbitrary",),
                                             vmem_limit_bytes=VMEM_LIMIT),
        name="gated_deltanet",
    )(bqkv, bz, gp, conv_w.astype(F32), head_vec(a_log), head_vec(dt_bias), out_gain[None, :].astype(F32))


FF_COLS = 256


def _even_out_kernel(h_ref, oa_ref, ob_ref, woa_ref, wob_ref, g_ref, w1_ref, w3_ref, w2_ref, o_ref,
                     act_ref, *, tm, lp, seq):
    h1 = (h_ref[...] + jnp.dot(oa_ref[...], woa_ref[...], preferred_element_type=F32)
          + jnp.dot(ob_ref[...], wob_ref[...], preferred_element_type=F32))
    xn = _rms(h1, g_ref[...]).astype(BF16)
    for c in range(D_FF // FF_COLS):
        cs = slice(c * FF_COLS, (c + 1) * FF_COLS)
        a = jnp.dot(xn, w1_ref[:, cs], preferred_element_type=F32)
        b = jnp.dot(xn, w3_ref[:, cs], preferred_element_type=F32)
        act_ref[:, cs] = (_silu(a) * b).astype(BF16)
    out = h1 + jnp.dot(act_ref[...], w2_ref[...], preferred_element_type=F32)
    o_ref[...] = jnp.where(_row_valid(tm, lp, seq), out, 0.0)


def _even_out(h, oa, ob, w_out, gain, w1, w3, w2, lp, seq, tm):
    rows = h.shape[0]
    wo = w_out.astype(BF16)
    row_spec = lambda w: pl.BlockSpec((tm, w), lambda i: (i, 0))
    return pl.pallas_call(
        functools.partial(_even_out_kernel, tm=tm, lp=lp, seq=seq),
        grid=(rows // tm,),
        in_specs=[row_spec(D_MODEL), row_spec(A_W), row_spec(B_W),
                  _const_spec((A_W, D_MODEL)), _const_spec((B_W, D_MODEL)), _const_spec((1, D_MODEL)),
                  _const_spec(w1.shape), _const_spec(w3.shape), _const_spec(w2.shape)],
        out_specs=row_spec(D_MODEL),
        out_shape=jax.ShapeDtypeStruct((rows, D_MODEL), F32),
        scratch_shapes=[pltpu.VMEM((tm, D_FF), BF16)],
        compiler_params=pltpu.CompilerParams(dimension_semantics=("arbitrary",),
                                             vmem_limit_bytes=VMEM_LIMIT),
        name="even_out_ffn",
    )(h, oa, ob, wo[:A_W], wo[A_W:], gain[None, :].astype(F32),
      w1.astype(BF16), w3.astype(BF16), w2.astype(BF16))


def _odd_in_kernel(h_ref, g_ref, wq_ref, wf_ref, wi_ref, wz_ref, lb_ref,
                   q_ref, k_ref, lf_ref, i_ref, z_ref, *, tm, lp, seq):
    xn = _rms(h_ref[...], g_ref[...]).astype(BF16)
    valid = _row_valid(tm, lp, seq)
    q_ref[...] = jnp.dot(xn, wq_ref[...], preferred_element_type=F32).astype(BF16)
    f = jnp.dot(xn, wf_ref[...], preferred_element_type=F32)
    lb = lb_ref[...]
    fg = lb + (1.0 - lb) * _sigmoid(f)
    k_ref[...] = jnp.where(valid, 1.0 - fg, 0.0).astype(BF16)
    lf_ref[...] = jnp.where(valid, jnp.log(fg), 0.0)
    i_ref[...] = jnp.dot(xn, wi_ref[...], preferred_element_type=F32).astype(BF16)
    z_ref[...] = jnp.dot(xn, wz_ref[...], preferred_element_type=F32).astype(BF16)


def _odd_in(h, gain, w_in, lb, lp, seq, tm):
    rows = h.shape[0]
    wb16 = w_in.astype(BF16)
    ws = [wb16[:, j * D_MODEL:(j + 1) * D_MODEL] for j in range(4)]
    row_spec = pl.BlockSpec((tm, D_MODEL), lambda i: (i, 0))
    wspec = _const_spec((D_MODEL, D_MODEL))
    vec = _const_spec((1, D_MODEL))
    sd = lambda dt: jax.ShapeDtypeStruct((rows, D_MODEL), dt)
    return pl.pallas_call(
        functools.partial(_odd_in_kernel, tm=tm, lp=lp, seq=seq),
        grid=(rows // tm,),
        in_specs=[row_spec, vec, wspec, wspec, wspec, wspec, vec],
        out_specs=(row_spec,) * 5,
        out_shape=(sd(BF16), sd(BF16), sd(F32), sd(BF16), sd(BF16)),
        compiler_params=pltpu.CompilerParams(dimension_semantics=("arbitrary",),
                                             vmem_limit_bytes=VMEM_LIMIT),
        name="odd_in_proj",
    )(h, gain[None, :].astype(F32), *ws, lb[None, :].astype(F32))


SUB = 8
HALVES = (32, 16, 8)


def _hgrn_kernel(q_ref, k_ref, lf_ref, i_ref, z_ref, og_ref, sel_ref, o_ref, s_ref, p_ref,
                 *, steps_per_batch):
    n = ROW_TILE
    step = pl.program_id(0) % steps_per_batch

    @pl.when(step == 0)
    def _():
        s_ref[...] = jnp.zeros_like(s_ref)

    r, c, same = _chunk_masks(n)
    incl = same & (r >= c)
    gc_all = jnp.dot(incl.astype(F32), lf_ref[...], precision=HIGHEST, preferred_element_type=F32)
    rrow = lax.broadcasted_iota(jnp.int32, (n, 1), 0)
    inner = rrow % SUB

    def block_rows(a, size, j):
        a3 = a.reshape(n // size, size, a.shape[1])
        return jnp.broadcast_to(a3[:, j:j + 1, :], a3.shape).reshape(a.shape)

    for h in range(C_HEADS):
        sl = slice(h * C_DK, (h + 1) * C_DK)
        q = q_ref[:, sl].astype(F32)
        k = k_ref[:, sl].astype(F32)
        iv = i_ref[:, sl]
        gc = gc_all[:, sl]
        gl = block_rows(gc, CHUNK, CHUNK - 1)
        qg = q * jnp.exp(gc)
        kdec = k * jnp.exp(gl - gc)
        att = jnp.zeros((n, n), F32)
        for half in HALVES:
            gref = block_rows(gc, 2 * half, half - 1)
            upper = (rrow % (2 * half)) >= half
            qt = q * jnp.exp(jnp.where(upper, gc - gref, -jnp.inf))
            kt = k * jnp.exp(jnp.where(upper, -jnp.inf, gref - gc))
            att = att + jnp.where(r // (2 * half) == c // (2 * half), _bdot_nt(qt, kt), 0.0)
        for j in range(SUB):
            kj = block_rows(k, SUB, j)
            gj = block_rows(gc, SUB, j)
            p_ref[:, j * C_DK:(j + 1) * C_DK] = (
                q * kj * jnp.exp(jnp.where(inner >= j, gc - gj, -jnp.inf))).astype(BF16)
        diag = jnp.dot(p_ref[...], sel_ref[...], preferred_element_type=F32)
        att = att + jnp.where(r // SUB == c // SUB, diag, 0.0)
        st = s_ref[h]
        inter = []
        for ci in range(n // CHUNK):
            rs = slice(ci * CHUNK, (ci + 1) * CHUNK)
            inter.append(_bdot_nt(qg[rs], st))
            st = st * jnp.exp(gl[ci * CHUNK:ci * CHUNK + 1, :]) + _bdot_tn(iv[rs], kdec[rs])
        s_ref[h] = st
        o = jnp.concatenate(inter, axis=0) + _bdot(att, iv)
        o_ref[:, sl] = (_rms(o, og_ref[...]) * _silu(z_ref[:, sl].astype(F32))).astype(o_ref.dtype)


def _hgrn2(q, k, lf, iv, z, out_gain, lp):
    rows = q.shape[0]
    n = ROW_TILE
    row_spec = pl.BlockSpec((n, D_MODEL), lambda i: (i, 0))
    sel = ((jnp.arange(SUB * C_DK)[:, None] // C_DK) == (jnp.arange(n)[None, :] % SUB)).astype(BF16)
    return pl.pallas_call(
        functools.partial(_hgrn_kernel, steps_per_batch=lp // n),
        grid=(rows // n,),
        in_specs=[row_spec] * 5 + [_const_spec((1, C_DK)), _const_spec(sel.shape)],
        out_specs=row_spec,
        out_shape=jax.ShapeDtypeStruct((rows, D_MODEL), BF16),
        scratch_shapes=[pltpu.VMEM((C_HEADS, C_DK, C_DK), F32), pltpu.VMEM((n, SUB * C_DK), BF16)],
        compiler_params=pltpu.CompilerParams(dimension_semantics=("arbitrary",),
                                             vmem_limit_bytes=VMEM_LIMIT),
        name="hgrn2",
    )(q, k, lf, iv, z, out_gain[None, :].astype(F32), sel)


def _odd_out_kernel(h_ref, o_ref, wo_ref, g_ref, wr_ref, h3_ref, xn_ref, route_ref, *, tm, lp, seq):
    h3 = h_ref[...] + jnp.dot(o_ref[...], wo_ref[...], preferred_element_type=F32)
    h3 = jnp.where(_row_valid(tm, lp, seq), h3, 0.0)
    h3_ref[...] = h3
    xn = _rms(h3, g_ref[...])
    xn_ref[...] = xn
    logits = jnp.dot(xn, wr_ref[...], precision=HIGHEST, preferred_element_type=F32)
    lane = lax.broadcasted_iota(jnp.int32, logits.shape, 1)
    logits = jnp.where(lane < N_EXPERTS, logits, -jnp.inf)
    v1 = jnp.max(logits, axis=-1, keepdims=True)
    e1 = jnp.min(jnp.where(logits == v1, lane, LANES), axis=-1, keepdims=True)
    rest = jnp.where(lane == e1, -jnp.inf, logits)
    v2 = jnp.max(rest, axis=-1, keepdims=True)
    e2 = jnp.min(jnp.where(rest == v2, lane, LANES), axis=-1, keepdims=True)
    t = jnp.exp(v2 - v1)
    g1 = 1.0 / (1.0 + t)
    g2 = t * g1
    route_ref[...] = jnp.where(lane == 0, e1.astype(F32),
                               jnp.where(lane == 1, e2.astype(F32),
                                         jnp.where(lane == 2, g1, jnp.where(lane == 3, g2, 0.0))))


def _odd_out(h, o, w_out, gain, router, lp, seq, tm):
    rows = h.shape[0]
    wr = jnp.pad(router.astype(F32), ((0, 0), (0, LANES - N_EXPERTS)))
    row_spec = lambda w: pl.BlockSpec((tm, w), lambda i: (i, 0))
    return pl.pallas_call(
        functools.partial(_odd_out_kernel, tm=tm, lp=lp, seq=seq),
        grid=(rows // tm,),
        in_specs=[row_spec(D_MODEL), row_spec(D_MODEL), _const_spec((D_MODEL, D_MODEL)),
                  _const_spec((1, D_MODEL)), _const_spec((D_MODEL, LANES))],
        out_specs=(row_spec(D_MODEL), row_spec(D_MODEL), row_spec(LANES)),
        out_shape=(jax.ShapeDtypeStruct((rows, D_MODEL), F32), jax.ShapeDtypeStruct((rows, D_MODEL), F32),
                   jax.ShapeDtypeStruct((rows, LANES), F32)),
        compiler_params=pltpu.CompilerParams(dimension_semantics=("arbitrary",),
                                             vmem_limit_bytes=VMEM_LIMIT),
        name="odd_out_route",
    )(h, o, w_out.astype(BF16), gain[None, :].astype(F32), wr)


MOE_TILE = 512
MOE_FF = 512


def _moe_plan(route, tile):
    n_tok = route.shape[0]
    flat_e = route[:, :2].astype(jnp.int32).reshape(-1)
    na = flat_e.shape[0]
    onehot = (flat_e[:, None] == jnp.arange(N_EXPERTS, dtype=jnp.int32)[None, :]).astype(jnp.int32)
    csum = jnp.cumsum(onehot, axis=0)
    rank = jnp.take_along_axis(csum, flat_e[:, None], axis=1)[:, 0] - 1
    counts = csum[-1]
    padded = (counts + tile - 1) // tile * tile
    pend = jnp.cumsum(padded)
    dest = (pend - padded)[flat_e] + rank
    n_tiles = -(-(na + N_EXPERTS * (tile - 1)) // tile)
    row_tok = jnp.zeros((n_tiles * tile,), jnp.int32).at[dest].set(jnp.arange(na, dtype=jnp.int32) // 2)
    tile_e = jnp.minimum(jnp.searchsorted(pend, jnp.arange(n_tiles, dtype=jnp.int32) * tile, side='right'),
                         N_EXPERTS - 1).astype(jnp.int32)
    n_used = (pend[-1] // tile).astype(jnp.int32).reshape(1)
    return dest.reshape(n_tok, 2), row_tok, tile_e, n_used, n_tiles


def _moe_gather_kernel(tok_ref, nused_ref, x_hbm, o_ref, buf_ref, sem):
    i = pl.program_id(0)
    tile = buf_ref.shape[0]

    @pl.when(i < nused_ref[0])
    def _():
        def issue(r, carry):
            tok = tok_ref[0, 0, r]
            pltpu.make_async_copy(x_hbm.at[pl.ds(tok, 1), :], buf_ref.at[pl.ds(r, 1), :], sem).start()
            return carry
        lax.fori_loop(0, tile, issue, 0)
        pltpu.make_async_copy(x_hbm.at[pl.ds(0, tile), :], buf_ref, sem).wait()
        o_ref[...] = buf_ref[...].astype(o_ref.dtype)

    @pl.when(i >= nused_ref[0])
    def _():
        o_ref[...] = jnp.zeros_like(o_ref)


def _moe_gather(xn, row_tok, n_used, n_tiles, tile):
    return pl.pallas_call(
        _moe_gather_kernel,
        grid=(n_tiles,),
        in_specs=[pl.BlockSpec((1, 1, tile), lambda i: (i, 0, 0), memory_space=pltpu.SMEM),
                  pl.BlockSpec(memory_space=pltpu.SMEM),
                  pl.BlockSpec(memory_space=pl.ANY)],
        out_specs=pl.BlockSpec((tile, D_MODEL), lambda i: (i, 0)),
        out_shape=jax.ShapeDtypeStruct((n_tiles * tile, D_MODEL), BF16),
        scratch_shapes=[pltpu.VMEM((tile, D_MODEL), F32), pltpu.SemaphoreType.DMA(())],
        compiler_params=pltpu.CompilerParams(dimension_semantics=("arbitrary",),
                                             vmem_limit_bytes=VMEM_LIMIT),
        name="moe_gather",
    )(row_tok.reshape(n_tiles, 1, tile), n_used, xn)


def _moe_ffn_kernel(te_ref, nused_ref, x_ref, w1_ref, w3_ref, w2_ref, y_ref):
    i, f = pl.program_id(0), pl.program_id(1)

    @pl.when(f == 0)
    def _():
        y_ref[...] = jnp.zeros_like(y_ref)

    @pl.when(i < nused_ref[0])
    def _():
        x = x_ref[...]
        a = jnp.dot(x, w1_ref[0], preferred_element_type=F32)
        b = jnp.dot(x, w3_ref[0], preferred_element_type=F32)
        y_ref[...] += jnp.dot((_silu(a) * b).astype(BF16), w2_ref[0], preferred_element_type=F32)


def _moe_ffn(xs, w1, w3, w2, tile_e, n_used, n_tiles, tile):
    nf = D_FF_EXPERT // MOE_FF
    live = lambda i, nu: i < nu[0]
    xi = lambda i, f, te, nu: (jnp.where(live(i, nu), i, nu[0] - 1), 0)
    fi = lambda i, f, nu: jnp.where(live(i, nu), f, nf - 1)
    grid_spec = pltpu.PrefetchScalarGridSpec(
        num_scalar_prefetch=2,
        grid=(n_tiles, nf),
        in_specs=[pl.BlockSpec((tile, D_MODEL), xi),
                  pl.BlockSpec((1, D_MODEL, MOE_FF), lambda i, f, te, nu: (te[i], 0, fi(i, f, nu))),
                  pl.BlockSpec((1, D_MODEL, MOE_FF), lambda i, f, te, nu: (te[i], 0, fi(i, f, nu))),
                  pl.BlockSpec((1, MOE_FF, D_MODEL), lambda i, f, te, nu: (te[i], fi(i, f, nu), 0))],
        out_specs=pl.BlockSpec((tile, D_MODEL), lambda i, f, te, nu: (i, 0)),
    )
    return pl.pallas_call(
        _moe_ffn_kernel,
        grid_spec=grid_spec,
        out_shape=jax.ShapeDtypeStruct((n_tiles * tile, D_MODEL), F32),
        compiler_params=pltpu.CompilerParams(dimension_semantics=("arbitrary", "arbitrary"),
                                             vmem_limit_bytes=VMEM_LIMIT),
        name="moe_ffn",
    )(tile_e, n_used, xs, w1, w3, w2)


def _moe_combine_kernel(pos_ref, gate_ref, h_hbm, y_hbm, o_ref, hbuf, ybuf, sem_h, sem_y, *, tc, lp):
    b, j = pl.program_id(0), pl.program_id(1)
    h_copy = pltpu.make_async_copy(h_hbm.at[pl.ds(b * lp + CHUNK + j * tc, tc), :], hbuf, sem_h)
    h_copy.start()

    def issue(r, carry):
        p = pos_ref[0, 0, r]
        pltpu.make_async_copy(y_hbm.at[pl.ds(p, 1), :], ybuf.at[pl.ds(r, 1), :], sem_y).start()
        return carry
    lax.fori_loop(0, 2 * tc, issue, 0)
    pltpu.make_async_copy(y_hbm.at[pl.ds(0, 2 * tc), :], ybuf, sem_y).wait()
    h_copy.wait()
    g = gate_ref[0]
    o_ref[0] = hbuf[...] + g[:, 0:1] * ybuf[:tc, :] + g[:, 1:2] * ybuf[tc:, :]


def _moe_combine(h3, y, dest, gates, batch, seq, lp, tc):
    nj = seq // tc
    pos = dest.reshape(batch * nj, tc, 2).transpose(0, 2, 1).reshape(batch * nj, 1, 2 * tc)
    return pl.pallas_call(
        functools.partial(_moe_combine_kernel, tc=tc, lp=lp),
        grid=(batch, nj),
        in_specs=[pl.BlockSpec((1, 1, 2 * tc), lambda b, j: (b * nj + j, 0, 0), memory_space=pltpu.SMEM),
                  pl.BlockSpec((1, tc, 2), lambda b, j: (b, j, 0)),
                  pl.BlockSpec(memory_space=pl.ANY), pl.BlockSpec(memory_space=pl.ANY)],
        out_specs=pl.BlockSpec((1, tc, D_MODEL), lambda b, j: (b, j, 0)),
        out_shape=jax.ShapeDtypeStruct((batch, seq, D_MODEL), F32),
        scratch_shapes=[pltpu.VMEM((tc, D_MODEL), F32), pltpu.VMEM((2 * tc, D_MODEL), F32),
                        pltpu.SemaphoreType.DMA(()), pltpu.SemaphoreType.DMA(())],
        compiler_params=pltpu.CompilerParams(dimension_semantics=("arbitrary", "arbitrary"),
                                             vmem_limit_bytes=VMEM_LIMIT),
        name="moe_combine",
    )(pos, gates, h3, y)


def _layout_rows(seq):
    return -(-(CHUNK + seq) // ROW_TILE) * ROW_TILE


def _proj_tile(lp):
    return next(t for t in (640, 512, 384, ROW_TILE) if lp % t == 0)


def kernel(x, meta_tokens, lb_logits, mix_norm, ffn_norm, e_w_in, a_q_gain, a_k_gain, a_lam_q1, a_lam_k1, a_lam_q2, a_lam_k2, a_sub_gain, b_conv, b_a_log, b_dt_bias, b_out_gain, e_w_out, ffn_w1, ffn_w3, ffn_w2, o_w_in, c_out_gain, o_w_out, router, moe_w1, moe_w3, moe_w2):
    batch, seq, _ = x.shape
    lp = _layout_rows(seq)
    rows = batch * lp
    tm = _proj_tile(lp)
    meta = jnp.broadcast_to(meta_tokens[None].astype(x.dtype), (batch, N_META, D_MODEL))
    h = jnp.concatenate([jnp.zeros((batch, FRONT, D_MODEL), x.dtype), meta, x,
                         jnp.zeros((batch, lp - CHUNK - seq, D_MODEL), x.dtype)], axis=1).reshape(rows, D_MODEL)

    q, k, v, bqkv, bz, gp = _even_in(h, mix_norm[0], e_w_in[0], a_q_gain[0], a_k_gain[0], lp, tm)
    lam_init = 0.8 - 0.6 * math.exp(-0.3 * 0)
    lam = (jnp.exp(jnp.sum(a_lam_q1[0].astype(F32) * a_lam_k1[0].astype(F32)))
           - jnp.exp(jnp.sum(a_lam_q2[0].astype(F32) * a_lam_k2[0].astype(F32))) + lam_init)
    o_a = _diff_attention(q, k, v, a_sub_gain[0], lam, lam_init, batch, lp, ROW_TILE)
    o_b = _gated_deltanet(bqkv, bz, gp, b_conv[0], b_a_log[0], b_dt_bias[0], b_out_gain[0], lp)
    h = _even_out(h, o_a, o_b, e_w_out[0], ffn_norm[0], ffn_w1[0], ffn_w3[0], ffn_w2[0], lp, seq, tm)

    lb_all = jnp.cumsum(jax.nn.softmax(lb_logits.astype(F32), axis=0), axis=0)
    lb = (lb_all - lb_all[0])[1]
    cq, ck, clf, ci, cz = _odd_in(h, mix_norm[1], o_w_in[0], lb, lp, seq, tm)
    o_c = _hgrn2(cq, ck, clf, ci, cz, c_out_gain[0], lp)
    h3, xn, route = _odd_out(h, o_c, o_w_out[0], ffn_norm[1], router[0], lp, seq, tm)
    dest, row_tok, tile_e, n_used, n_tiles = _moe_plan(route, MOE_TILE)
    xs = _moe_gather(xn, row_tok, n_used, n_tiles, MOE_TILE)
    y = _moe_ffn(xs, moe_w1[0].astype(BF16), moe_w3[0].astype(BF16), moe_w2[0].astype(BF16),
                 tile_e, n_used, n_tiles, MOE_TILE)
    frames = lambda a: a.reshape(batch, lp, a.shape[-1])[:, CHUNK:CHUNK + seq]
    return _moe_combine(h3, y, frames(dest), frames(route[:, 2:4]), batch, seq, lp, ROW_TILE)
```

```python
import functools
import math

import jax
import jax.numpy as jnp
from jax import lax
from jax.experimental import pallas as pl
from jax.experimental.pallas import tpu as pltpu

D_MODEL = 1024
CHUNK = 64
N_META = 16
FRONT = CHUNK - N_META
EPS = 1e-6
ROPE_THETA = 500000.0
LOG2E = math.log2(math.e)

A_HEADS = 4
A_DK = 64
A_DV = 128
A_ROT = A_DK // 4
A_W = A_HEADS * A_DV

B_HEADS = 4
B_DK = 128
B_W = B_HEADS * B_DK
CONV_K = 4

C_HEADS = 8
C_DK = 128

D_FF = 2816
N_EXPERTS = 8
D_FF_EXPERT = 3584

LANES = 128
ROW_TILE = 256
VMEM_LIMIT = 56 * 1024 * 1024

BF16 = jnp.bfloat16
F32 = jnp.float32
HIGHEST = lax.Precision.HIGHEST


def _const_spec(shape):
    return pl.BlockSpec(shape, lambda *_: (0,) * len(shape), pipeline_mode=pl.Buffered(1))


def _rms(x, gain):
    return x * lax.rsqrt(jnp.mean(x * x, axis=-1, keepdims=True) + EPS) * gain


def _silu(x):
    return x * (1.0 / (1.0 + jnp.exp(-x)))


def _sigmoid(x):
    return 1.0 / (1.0 + jnp.exp(-x))


def _row_valid(tile_rows, lp, seq):
    r = pl.program_id(0) * tile_rows + lax.broadcasted_iota(jnp.int32, (tile_rows, 1), 0)
    p = r % lp
    return (p >= FRONT) & (p < CHUNK + seq)


def _head_norm_rope(a, ones_bd, gain, cos, sin_lo, sin_hi):
    ssq = jnp.dot((a * a).astype(BF16), ones_bd, preferred_element_type=F32)
    y = a * lax.rsqrt(ssq * (1.0 / A_DK) + EPS) * gain
    up = pltpu.roll(y, A_W - A_ROT // 2, axis=1)
    dn = pltpu.roll(y, A_ROT // 2, axis=1)
    return y * cos + up * sin_lo + dn * sin_hi


def _even_in_kernel(h_ref, g_ref, wq_ref, wk_ref, wv_ref, wb_ref, wz_ref, wg_ref, bd_ref,
                    qg_ref, kg_ref, cos_ref, slo_ref, shi_ref,
                    q_ref, k_ref, v_ref, b_ref, z_ref, gp_ref):
    xn = _rms(h_ref[...], g_ref[...]).astype(BF16)
    rep = lambda t: jnp.concatenate([t] * (A_W // LANES), axis=1)
    cos, slo, shi = rep(cos_ref[...]), rep(slo_ref[...]), rep(shi_ref[...])
    bd = bd_ref[...]
    aq = jnp.dot(xn, wq_ref[...], preferred_element_type=F32)
    q_ref[...] = (_head_norm_rope(aq, bd, qg_ref[...], cos, slo, shi) * (A_DK ** -0.5 * LOG2E)).astype(BF16)
    ak = jnp.dot(xn, wk_ref[...], preferred_element_type=F32)
    k_ref[...] = _head_norm_rope(ak, bd, kg_ref[...], cos, slo, shi).astype(BF16)
    v_ref[...] = jnp.dot(xn, wv_ref[...], preferred_element_type=F32).astype(BF16)
    b_ref[...] = jnp.dot(xn, wb_ref[...], preferred_element_type=F32).astype(BF16)
    z_ref[...] = jnp.dot(xn, wz_ref[...], preferred_element_type=F32).astype(BF16)
    gp_ref[...] = jnp.dot(xn, wg_ref[...], preferred_element_type=F32)


def _even_in(h, gain, w_in, q_gain, k_gain, lp, tm):
    rows = h.shape[0]
    wb16 = w_in.astype(BF16)
    c0, c1, c2, c3, c4 = A_W, 2 * A_W, 3 * A_W, 3 * A_W + 3 * B_W, 3 * A_W + 4 * B_W
    wq, wk, wv, wb, wz = wb16[:, :c0], wb16[:, c0:c1], wb16[:, c1:c2], wb16[:, c2:c3], wb16[:, c3:c4]
    wg = jnp.pad(wb16[:, c4:], ((0, 0), (0, LANES - 2 * B_HEADS)))
    d = jnp.arange(A_W)
    ones_bd = (d[:, None] // A_DK == d[None, :] // A_DK).astype(BF16)
    half = A_ROT // 2
    inv_freq = ROPE_THETA ** (-jnp.arange(half, dtype=F32) * 2.0 / A_ROT)
    pos = (jnp.arange(lp) - FRONT).astype(F32)
    ang = pos[:, None] * inv_freq[None, :]
    cosv, sinv = jnp.cos(ang), jnp.sin(ang)
    zeros = jnp.zeros((lp, A_DK - A_ROT), F32)
    one = jnp.ones((lp, A_DK - A_ROT), F32)
    zh = jnp.zeros((lp, half), F32)
    cos64 = jnp.concatenate([cosv, cosv, one], axis=1)
    slo64 = jnp.concatenate([-sinv, zh, zeros], axis=1)
    shi64 = jnp.concatenate([zh, sinv, zeros], axis=1)
    two = lambda t: jnp.concatenate([t, t], axis=1)
    cos_t, slo_t, shi_t = two(cos64), two(slo64), two(shi64)
    tile8 = lambda g: jnp.tile(g.astype(F32), 2 * A_HEADS)[None, :]
    n_pos = lp // tm
    row_spec = lambda w: pl.BlockSpec((tm, w), lambda i: (i, 0))
    pos_spec = pl.BlockSpec((tm, LANES), lambda i: (i % n_pos, 0))
    outs = pl.pallas_call(
        _even_in_kernel,
        grid=(rows // tm,),
        in_specs=[row_spec(D_MODEL), _const_spec((1, D_MODEL)),
                  _const_spec(wq.shape), _const_spec(wk.shape), _const_spec(wv.shape),
                  _const_spec(wb.shape), _const_spec(wz.shape), _const_spec(wg.shape),
                  _const_spec(ones_bd.shape), _const_spec((1, A_W)), _const_spec((1, A_W)),
                  pos_spec, pos_spec, pos_spec],
        out_specs=(row_spec(A_W), row_spec(A_W), row_spec(A_W), row_spec(3 * B_W), row_spec(B_W),
                   row_spec(LANES)),
        out_shape=(jax.ShapeDtypeStruct((rows, A_W), BF16), jax.ShapeDtypeStruct((rows, A_W), BF16),
                   jax.ShapeDtypeStruct((rows, A_W), BF16), jax.ShapeDtypeStruct((rows, 3 * B_W), BF16),
                   jax.ShapeDtypeStruct((rows, B_W), BF16), jax.ShapeDtypeStruct((rows, LANES), F32)),
        compiler_params=pltpu.CompilerParams(dimension_semantics=("arbitrary",),
                                             vmem_limit_bytes=VMEM_LIMIT),
        name="even_in_proj",
    )(h, gain[None, :].astype(F32), wq, wk, wv, wb, wz, wg, ones_bd, tile8(q_gain), tile8(k_gain),
      cos_t, slo_t, shi_t)
    return outs


ATTN_WIDE = 1024


def _attn_kernel(q_ref, k_ref, v_ref, sg_ref, lam_ref, o_ref, qs_ref, vx_ref, m_ref, acc_ref,
                 sa_ref, sb_ref, *, tq, width, lam_init):
    qi = pl.program_id(2)

    @pl.when(qi == 0)
    def _():
        vx_ref[:, :A_DV] = v_ref[...]
        vx_ref[:, A_DV:] = jnp.ones((vx_ref.shape[0], A_DV), BF16)

    q = q_ref[...]
    lane = lax.broadcasted_iota(jnp.int32, q.shape, 1)
    zero = jnp.zeros_like(q)
    qs_ref[:tq, :] = jnp.where(lane < A_DK, q, zero)
    qs_ref[tq:, :] = jnp.where(lane >= A_DK, q, zero)
    m_ref[...] = jnp.full(m_ref.shape, -jnp.inf, F32)
    acc_ref[...] = jnp.zeros(acc_ref.shape, F32)

    lp = k_ref.shape[0]
    qchunk = (qi * tq + lax.broadcasted_iota(jnp.int32, (2 * tq, 1), 0) % tq) // CHUNK

    def window(t):
        return pl.multiple_of(jnp.minimum(t * width, lp - width), tq)

    def scores(s_ref, t):
        s_ref[...] = lax.dot_general(qs_ref[...], k_ref[pl.ds(window(t), width), :],
                                     (((1,), (1,)), ((), ())), preferred_element_type=F32)

    def softmax_pv(s_ref, t, mask):
        start = window(t)
        kpos = start + lax.broadcasted_iota(jnp.int32, (1, width), 1)
        kchunk = jnp.where((kpos >= t * width) & (kpos >= FRONT), kpos // CHUNK, lp)
        nblk = width // LANES

        def block(c):
            cs = slice(c * LANES, (c + 1) * LANES)
            if mask == "full":
                return jnp.where(kchunk[:, cs] <= qchunk, s_ref[:, cs], -jnp.inf)
            if mask == "pad" and c == 0:
                return jnp.where(kpos[:, cs] >= FRONT, s_ref[:, cs], -jnp.inf)
            return s_ref[:, cs]

        mx = functools.reduce(jnp.maximum, [block(c) for c in range(nblk)])
        m_old = m_ref[...]
        m_new = jnp.maximum(m_old, jnp.max(mx, axis=-1, keepdims=True))
        alpha = jnp.exp2(m_old - m_new)
        p = jnp.concatenate([jnp.exp2((block(c) - m_new).astype(BF16)) for c in range(nblk)], axis=1)
        pv = jnp.dot(p, vx_ref[pl.ds(start, width), :], preferred_element_type=F32)
        acc_ref[:, :A_DV] = alpha * acc_ref[:, :A_DV] + pv[:, :A_DV]
        acc_ref[:, A_DV:] = alpha * acc_ref[:, A_DV:] + pv[:, A_DV:]
        m_ref[...] = m_new

    n_pairs = ((qi * tq + tq + width - 1) // width + 1) // 2

    def pair(j, masks):
        scores(sb_ref, 2 * j + 1)
        softmax_pv(sa_ref, 2 * j, masks[0])
        scores(sa_ref, 2 * j + 2)
        softmax_pv(sb_ref, 2 * j + 1, masks[1])

    scores(sa_ref, 0)

    @pl.when(n_pairs > 1)
    def _():
        pair(0, ("pad", "none"))

    def inner(j, carry):
        pair(j, ("none", "none"))
        return carry
    lax.fori_loop(1, n_pairs - 1, inner, 0)
    scores(sb_ref, 2 * n_pairs - 1)
    softmax_pv(sa_ref, 2 * n_pairs - 2, "full")
    softmax_pv(sb_ref, 2 * n_pairs - 1, "full")

    o = acc_ref[:, :A_DV] / acc_ref[:, A_DV:]
    o = o[:tq] - lam_ref[...] * o[tq:]
    o_ref[...] = (_rms(o, sg_ref[...]) * (1.0 - lam_init)).astype(o_ref.dtype)


def _diff_attention(q, k, v, sub_gain, lam, lam_init, batch, lp, tq):
    nq = lp // tq
    kv_spec = pl.BlockSpec((lp, LANES), lambda b, h, i: (b, h))
    q_spec = pl.BlockSpec((tq, LANES), lambda b, h, i: (b * nq + i, h))
    vec = pl.BlockSpec((1, LANES), lambda b, h, i: (0, 0))
    width = min(ATTN_WIDE, lp)
    return pl.pallas_call(
        functools.partial(_attn_kernel, tq=tq, width=width, lam_init=lam_init),
        grid=(batch, A_HEADS, nq),
        in_specs=[q_spec, kv_spec, kv_spec, vec, vec],
        out_specs=q_spec,
        out_shape=jax.ShapeDtypeStruct(q.shape, BF16),
        scratch_shapes=[pltpu.VMEM((2 * tq, LANES), BF16), pltpu.VMEM((lp, 2 * A_DV), BF16),
                        pltpu.VMEM((2 * tq, LANES), F32), pltpu.VMEM((2 * tq, 2 * A_DV), F32),
                        pltpu.VMEM((2 * tq, width), F32), pltpu.VMEM((2 * tq, width), F32)],
        compiler_params=pltpu.CompilerParams(dimension_semantics=("arbitrary",) * 3,
                                             vmem_limit_bytes=VMEM_LIMIT),
        name="diff_attention",
    )(q, k, v, sub_gain[None, :].astype(F32), jnp.full((1, LANES), lam, F32))


def _chunk_masks(n):
    r = lax.broadcasted_iota(jnp.int32, (n, n), 0)
    c = lax.broadcasted_iota(jnp.int32, (n, n), 1)
    same = (r // CHUNK) == (c // CHUNK)
    return r, c, same


def _bdot(a, b):
    return jnp.dot(a.astype(BF16), b.astype(BF16), preferred_element_type=F32)


def _bdot_nt(a, b):
    return lax.dot_general(a.astype(BF16), b.astype(BF16), (((1,), (1,)), ((), ())),
                           preferred_element_type=F32)


def _bdot_tn(a, b):
    return lax.dot_general(a.astype(BF16), b.astype(BF16), (((0,), (0,)), ((), ())),
                           preferred_element_type=F32)


def _gdn_kernel(x_ref, z_ref, gp_ref, cw_ref, alog_ref, dtb_ref, og_ref, o_ref,
                xbuf_ref, s_ref, *, steps_per_batch):
    n = ROW_TILE
    step = pl.program_id(0) % steps_per_batch

    @pl.when(step == 0)
    def _():
        xbuf_ref[:8, :] = jnp.zeros((8, 3 * B_W), F32)
        s_ref[...] = jnp.zeros_like(s_ref)

    xbuf_ref[8:, :] = x_ref[...].astype(F32)
    conv = cw_ref[CONV_K - 1:CONV_K, :] * xbuf_ref[8:, :]
    for j in range(CONV_K - 1):
        conv = conv + cw_ref[j:j + 1, :] * xbuf_ref[5 + j:5 + j + n, :]
    xbuf_ref[:8, :] = xbuf_ref[n:n + 8, :]
    conv = _silu(conv)

    r, c, same = _chunk_masks(n)
    incl = same & (r >= c)
    strict = same & (r > c)
    eye = (r == c).astype(F32)
    gp = gp_ref[...]
    beta_all = _sigmoid(gp)
    gpre = gp + dtb_ref[...]
    softplus = jnp.maximum(gpre, 0.0) + jnp.log(1.0 + jnp.exp(-jnp.abs(gpre)))
    row = step * n + lax.broadcasted_iota(jnp.int32, (n, 1), 0)
    g_all = jnp.where(row >= FRONT, -jnp.exp(alog_ref[...]) * softplus, 0.0)
    gc_all = jnp.dot(incl.astype(F32), g_all, precision=HIGHEST, preferred_element_type=F32)
    last = (same & (c % CHUNK == CHUNK - 1)).astype(F32)
    gl_all = jnp.dot(last, gc_all, precision=HIGHEST, preferred_element_type=F32)
    gc_rows = gc_all.T

    heads = range(B_HEADS)
    chunks = [slice(ci * CHUNK, (ci + 1) * CHUNK) for ci in range(n // CHUNK)]
    p, t, rhs, qk, qg, kdec, decay = [], [], [], [], [], [], []
    for h in heads:
        qh = conv[:, h * B_DK:(h + 1) * B_DK]
        kh = conv[:, B_W + h * B_DK:B_W + (h + 1) * B_DK]
        vh = conv[:, 2 * B_W + h * B_DK:2 * B_W + (h + 1) * B_DK]
        qh = qh * lax.rsqrt(jnp.sum(qh * qh, axis=-1, keepdims=True) + EPS) * (B_DK ** -0.5)
        kh = kh * lax.rsqrt(jnp.sum(kh * kh, axis=-1, keepdims=True) + EPS)
        beta = beta_all[:, h:h + 1]
        gcol = gc_all[:, B_HEADS + h:B_HEADS + h + 1]
        glcol = gl_all[:, B_HEADS + h:B_HEADS + h + 1]
        grow = gc_rows[B_HEADS + h:B_HEADS + h + 1, :]
        dec = jnp.exp(jnp.where(incl, gcol - grow, -jnp.inf))
        kb = kh * beta
        egc = jnp.exp(gcol)
        p0 = -(_bdot_nt(kb, kh) * jnp.where(strict, dec, 0.0))
        p.append(p0)
        t.append(eye + p0)
        rhs.append(jnp.concatenate([vh * beta, kb * egc], axis=1))
        qk.append(_bdot_nt(qh, kh) * dec)
        qg.append(qh * egc)
        kdec.append(kh * jnp.exp(glcol - gcol))
        decay.append([jnp.exp(glcol[rs.stop - 1:rs.stop, :]) for rs in chunks])
    for _ in range(5):
        for h in heads:
            p[h] = _bdot(p[h], p[h])
            t[h] = t[h] + _bdot(t[h], p[h])
    sol = [_bdot(t[h], rhs[h]) for h in heads]
    u = [s[:, :B_DK] for s in sol]
    w = [s[:, B_DK:] for s in sol]
    kw = [[_bdot_tn(kdec[h][rs], w[h][rs]) for rs in chunks] for h in heads]
    ku = [[_bdot_tn(kdec[h][rs], u[h][rs]) for rs in chunks] for h in heads]
    state = [s_ref[h] for h in heads]
    before = [[] for _ in heads]
    for ci in range(len(chunks)):
        for h in heads:
            before[h].append(state[h])
            state[h] = state[h] * decay[h][ci] - _bdot(kw[h][ci], state[h]) + ku[h][ci]
    for h in heads:
        s_ref[h] = state[h]
        vnew, inter = [], []
        for ci, rs in enumerate(chunks):
            ws = _bdot(jnp.concatenate([w[h][rs], qg[h][rs]], axis=0), before[h][ci])
            vnew.append(u[h][rs] - ws[:CHUNK])
            inter.append(ws[CHUNK:])
        o = jnp.concatenate(inter, axis=0) + _bdot(qk[h], jnp.concatenate(vnew, axis=0))
        sl = slice(h * B_DK, (h + 1) * B_DK)
        o_ref[:, sl] = (_rms(o, og_ref[...]) * _silu(z_ref[:, sl].astype(F32))).astype(o_ref.dtype)


def _gated_deltanet(bqkv, bz, gp, conv_w, a_log, dt_bias, out_gain, lp):
    rows = bqkv.shape[0]
    n = ROW_TILE
    head_vec = lambda p: jnp.zeros((1, LANES), F32).at[0, B_HEADS:2 * B_HEADS].set(p.astype(F32))
    row_spec = lambda w: pl.BlockSpec((n, w), lambda i: (i, 0))
    return pl.pallas_call(
        functools.partial(_gdn_kernel, steps_per_batch=lp // n),
        grid=(rows // n,),
        in_specs=[row_spec(3 * B_W), row_spec(B_W), row_spec(LANES), _const_spec((CONV_K, 3 * B_W)),
                  _const_spec((1, LANES)), _const_spec((1, LANES)), _const_spec((1, B_DK))],
        out_specs=row_spec(B_W),
        out_shape=jax.ShapeDtypeStruct((rows, B_W), BF16),
        scratch_shapes=[pltpu.VMEM((n + 8, 3 * B_W), F32), pltpu.VMEM((B_HEADS, B_DK, B_DK), F32)],
        compiler_params=pltpu.CompilerParams(dimension_semantics=("arbitrary",),
                                             vmem_limit_bytes=VMEM_LIMIT),
        name="gated_deltanet",
    )(bqkv, bz, gp, conv_w.astype(F32), head_vec(a_log), head_vec(dt_bias), out_gain[None, :].astype(F32))


FF_COLS = 256


def _even_out_kernel(h_ref, oa_ref, ob_ref, woa_ref, wob_ref, g_ref, w1_ref, w3_ref, w2_ref, o_ref,
                     act_ref, *, tm, lp, seq):
    h1 = (h_ref[...] + jnp.dot(oa_ref[...], woa_ref[...], preferred_element_type=F32)
          + jnp.dot(ob_ref[...], wob_ref[...], preferred_element_type=F32))
    xn = _rms(h1, g_ref[...]).astype(BF16)
    for c in range(D_FF // FF_COLS):
        cs = slice(c * FF_COLS, (c + 1) * FF_COLS)
        a = jnp.dot(xn, w1_ref[:, cs], preferred_element_type=F32)
        b = jnp.dot(xn, w3_ref[:, cs], preferred_element_type=F32)
        act_ref[:, cs] = (_silu(a) * b).astype(BF16)
    out = h1 + jnp.dot(act_ref[...], w2_ref[...], preferred_element_type=F32)
    o_ref[...] = jnp.where(_row_valid(tm, lp, seq), out, 0.0)


def _even_out(h, oa, ob, w_out, gain, w1, w3, w2, lp, seq, tm):
    rows = h.shape[0]
    wo = w_out.astype(BF16)
    row_spec = lambda w: pl.BlockSpec((tm, w), lambda i: (i, 0))
    return pl.pallas_call(
        functools.partial(_even_out_kernel, tm=tm, lp=lp, seq=seq),
        grid=(rows // tm,),
        in_specs=[row_spec(D_MODEL), row_spec(A_W), row_spec(B_W),
                  _const_spec((A_W, D_MODEL)), _const_spec((B_W, D_MODEL)), _const_spec((1, D_MODEL)),
                  _const_spec(w1.shape), _const_spec(w3.shape), _const_spec(w2.shape)],
        out_specs=row_spec(D_MODEL),
        out_shape=jax.ShapeDtypeStruct((rows, D_MODEL), F32),
        scratch_shapes=[pltpu.VMEM((tm, D_FF), BF16)],
        compiler_params=pltpu.CompilerParams(dimension_semantics=("arbitrary",),
                                             vmem_limit_bytes=VMEM_LIMIT),
        name="even_out_ffn",
    )(h, oa, ob, wo[:A_W], wo[A_W:], gain[None, :].astype(F32),
      w1.astype(BF16), w3.astype(BF16), w2.astype(BF16))


def _odd_in_kernel(h_ref, g_ref, wq_ref, wf_ref, wi_ref, wz_ref, lb_ref,
                   q_ref, k_ref, lf_ref, i_ref, z_ref, *, tm, lp, seq):
    xn = _rms(h_ref[...], g_ref[...]).astype(BF16)
    valid = _row_valid(tm, lp, seq)
    q_ref[...] = jnp.dot(xn, wq_ref[...], preferred_element_type=F32).astype(BF16)
    f = jnp.dot(xn, wf_ref[...], preferred_element_type=F32)
    lb = lb_ref[...]
    fg = lb + (1.0 - lb) * _sigmoid(f)
    k_ref[...] = jnp.where(valid, 1.0 - fg, 0.0).astype(BF16)
    lf_ref[...] = jnp.where(valid, jnp.log(fg), 0.0)
    i_ref[...] = jnp.dot(xn, wi_ref[...], preferred_element_type=F32).astype(BF16)
    z_ref[...] = jnp.dot(xn, wz_ref[...], preferred_element_type=F32).astype(BF16)


def _odd_in(h, gain, w_in, lb, lp, seq, tm):
    rows = h.shape[0]
    wb16 = w_in.astype(BF16)
    ws = [wb16[:, j * D_MODEL:(j + 1) * D_MODEL] for j in range(4)]
    row_spec = pl.BlockSpec((tm, D_MODEL), lambda i: (i, 0))
    wspec = _const_spec((D_MODEL, D_MODEL))
    vec = _const_spec((1, D_MODEL))
    sd = lambda dt: jax.ShapeDtypeStruct((rows, D_MODEL), dt)
    return pl.pallas_call(
        functools.partial(_odd_in_kernel, tm=tm, lp=lp, seq=seq),
        grid=(rows // tm,),
        in_specs=[row_spec, vec, wspec, wspec, wspec, wspec, vec],
        out_specs=(row_spec,) * 5,
        out_shape=(sd(BF16), sd(BF16), sd(F32), sd(BF16), sd(BF16)),
        compiler_params=pltpu.CompilerParams(dimension_semantics=("arbitrary",),
                                             vmem_limit_bytes=VMEM_LIMIT),
        name="odd_in_proj",
    )(h, gain[None, :].astype(F32), *ws, lb[None, :].astype(F32))


SUB = 8
HALVES = (32, 16, 8)


def _hgrn_kernel(q_ref, k_ref, lf_ref, i_ref, z_ref, og_ref, sel_ref, o_ref, s_ref, p_ref,
                 *, steps_per_batch):
    n = ROW_TILE
    step = pl.program_id(0) % steps_per_batch

    @pl.when(step == 0)
    def _():
        s_ref[...] = jnp.zeros_like(s_ref)

    r, c, same = _chunk_masks(n)
    incl = same & (r >= c)
    gc_all = jnp.dot(incl.astype(F32), lf_ref[...], precision=HIGHEST, preferred_element_type=F32) * LOG2E
    rrow = lax.broadcasted_iota(jnp.int32, (n, 1), 0)
    inner = rrow % SUB
    level_mask = [(r // (2 * half) == c // (2 * half)).astype(F32) for half in HALVES]
    diag_mask = (r // SUB == c // SUB).astype(F32)

    def block_rows(a, size, j):
        a3 = a.reshape(n // size, size, a.shape[1])
        return jnp.broadcast_to(a3[:, j:j + 1, :], a3.shape).reshape(a.shape)

    for h in range(C_HEADS):
        sl = slice(h * C_DK, (h + 1) * C_DK)
        q = q_ref[:, sl].astype(F32)
        k = k_ref[:, sl].astype(F32)
        iv = i_ref[:, sl]
        gc = gc_all[:, sl]
        gl = block_rows(gc, CHUNK, CHUNK - 1)
        qg = q * jnp.exp2(gc)
        kdec = k * jnp.exp2(gl - gc)
        att = jnp.zeros((n, n), F32)
        for half, mask in zip(HALVES, level_mask):
            gref = block_rows(gc, 2 * half, half - 1)
            upper = (rrow % (2 * half)) >= half
            qt = q * jnp.exp2(jnp.where(upper, gc - gref, -jnp.inf))
            kt = k * jnp.exp2(jnp.where(upper, -jnp.inf, gref - gc))
            att = att + mask * _bdot_nt(qt, kt)
        for j in range(SUB):
            kj = block_rows(k, SUB, j)
            gj = block_rows(gc, SUB, j)
            p_ref[:, j * C_DK:(j + 1) * C_DK] = (
                q * kj * jnp.exp2(jnp.where(inner >= j, gc - gj, -jnp.inf))).astype(BF16)
        diag = jnp.dot(p_ref[...], sel_ref[...], preferred_element_type=F32)
        att = att + diag_mask * diag
        st = s_ref[h]
        inter = []
        for ci in range(n // CHUNK):
            rs = slice(ci * CHUNK, (ci + 1) * CHUNK)
            inter.append(_bdot_nt(qg[rs], st))
            st = st * jnp.exp2(gl[ci * CHUNK:ci * CHUNK + 1, :]) + _bdot_tn(iv[rs], kdec[rs])
        s_ref[h] = st
        o = jnp.concatenate(inter, axis=0) + _bdot(att, iv)
        o_ref[:, sl] = (_rms(o, og_ref[...]) * _silu(z_ref[:, sl].astype(F32))).astype(o_ref.dtype)


def _hgrn2(q, k, lf, iv, z, out_gain, lp):
    rows = q.shape[0]
    n = ROW_TILE
    row_spec = pl.BlockSpec((n, D_MODEL), lambda i: (i, 0))
    sel = ((jnp.arange(SUB * C_DK)[:, None] // C_DK) == (jnp.arange(n)[None, :] % SUB)).astype(BF16)
    return pl.pallas_call(
        functools.partial(_hgrn_kernel, steps_per_batch=lp // n),
        grid=(rows // n,),
        in_specs=[row_spec] * 5 + [_const_spec((1, C_DK)), _const_spec(sel.shape)],
        out_specs=row_spec,
        out_shape=jax.ShapeDtypeStruct((rows, D_MODEL), BF16),
        scratch_shapes=[pltpu.VMEM((C_HEADS, C_DK, C_DK), F32), pltpu.VMEM((n, SUB * C_DK), BF16)],
        compiler_params=pltpu.CompilerParams(dimension_semantics=("arbitrary",),
                                             vmem_limit_bytes=VMEM_LIMIT),
        name="hgrn2",
    )(q, k, lf, iv, z, out_gain[None, :].astype(F32), sel)


def _odd_out_kernel(h_ref, o_ref, wo_ref, g_ref, wr_ref, h3_ref, xn_ref, route_ref, *, tm, lp, seq):
    h3 = h_ref[...] + jnp.dot(o_ref[...], wo_ref[...], preferred_element_type=F32)
    h3 = jnp.where(_row_valid(tm, lp, seq), h3, 0.0)
    h3_ref[...] = h3
    xn = _rms(h3, g_ref[...])
    xn_ref[...] = xn
    logits = jnp.dot(xn, wr_ref[...], precision=HIGHEST, preferred_element_type=F32)
    lane = lax.broadcasted_iota(jnp.int32, logits.shape, 1)
    logits = jnp.where(lane < N_EXPERTS, logits, -jnp.inf)
    v1 = jnp.max(logits, axis=-1, keepdims=True)
    e1 = jnp.min(jnp.where(logits == v1, lane, LANES), axis=-1, keepdims=True)
    rest = jnp.where(lane == e1, -jnp.inf, logits)
    v2 = jnp.max(rest, axis=-1, keepdims=True)
    e2 = jnp.min(jnp.where(rest == v2, lane, LANES), axis=-1, keepdims=True)
    t = jnp.exp(v2 - v1)
    g1 = 1.0 / (1.0 + t)
    g2 = t * g1
    route_ref[...] = jnp.where(lane == 0, e1.astype(F32),
                               jnp.where(lane == 1, e2.astype(F32),
                                         jnp.where(lane == 2, g1, jnp.where(lane == 3, g2, 0.0))))


def _odd_out(h, o, w_out, gain, router, lp, seq, tm):
    rows = h.shape[0]
    wr = jnp.pad(router.astype(F32), ((0, 0), (0, LANES - N_EXPERTS)))
    row_spec = lambda w: pl.BlockSpec((tm, w), lambda i: (i, 0))
    return pl.pallas_call(
        functools.partial(_odd_out_kernel, tm=tm, lp=lp, seq=seq),
        grid=(rows // tm,),
        in_specs=[row_spec(D_MODEL), row_spec(D_MODEL), _const_spec((D_MODEL, D_MODEL)),
                  _const_spec((1, D_MODEL)), _const_spec((D_MODEL, LANES))],
        out_specs=(row_spec(D_MODEL), row_spec(D_MODEL), row_spec(LANES)),
        out_shape=(jax.ShapeDtypeStruct((rows, D_MODEL), F32), jax.ShapeDtypeStruct((rows, D_MODEL), F32),
                   jax.ShapeDtypeStruct((rows, LANES), F32)),
        compiler_params=pltpu.CompilerParams(dimension_semantics=("arbitrary",),
                                             vmem_limit_bytes=VMEM_LIMIT),
        name="odd_out_route",
    )(h, o, w_out.astype(BF16), gain[None, :].astype(F32), wr)


MOE_TILE = 512
MOE_FF_STEPS = 4


def _moe_plan(route, tile):
    n_tok = route.shape[0]
    flat_e = route[:, :2].astype(jnp.int32).reshape(-1)
    na = flat_e.shape[0]
    onehot = (flat_e[:, None] == jnp.arange(N_EXPERTS, dtype=jnp.int32)[None, :]).astype(jnp.int32)
    csum = jnp.cumsum(onehot, axis=0)
    rank = jnp.take_along_axis(csum, flat_e[:, None], axis=1)[:, 0] - 1
    counts = csum[-1]
    padded = (counts + tile - 1) // tile * tile
    pend = jnp.cumsum(padded)
    dest = (pend - padded)[flat_e] + rank
    n_tiles = -(-(na + N_EXPERTS * (tile - 1)) // tile) + 2
    n_rows = n_tiles * tile
    pair = jnp.full((n_rows,), -1, jnp.int32).at[dest].set(jnp.arange(na, dtype=jnp.int32))
    row_tok = jnp.where(pair >= 0, pair // 2, 0)
    row_dst = jnp.where(pair >= 0, (pair % 2) * n_tok + pair // 2, 0)
    tile_cnt = jnp.sum((pair >= 0).reshape(n_tiles, tile), axis=1).astype(jnp.int32)
    tile_e = jnp.minimum(jnp.searchsorted(pend, jnp.arange(n_tiles, dtype=jnp.int32) * tile, side='right'),
                         N_EXPERTS - 1).astype(jnp.int32)
    n_used = (pend[-1] // tile).astype(jnp.int32).reshape(1)
    return row_tok, row_dst, tile_e, n_used, tile_cnt, n_tiles


def _moe_ffn_kernel(te_ref, nused_ref, cnt_ref, tok0_ref, tokn_ref, dst_ref, x_hbm, w1_ref, w3_ref, w2_ref,
                    y_hbm, xbuf, xb16, ybuf, gsem, ssem):
    i, f = pl.program_id(0), pl.program_id(1)
    steps = pl.num_programs(1)
    tile = xb16.shape[0]
    per_step = tile // MOE_FF_STEPS
    slot = i % 2
    n_used = nused_ref[0]

    @pl.when((i == 0) & (f == 0))
    def _():
        def issue(r, carry):
            pltpu.make_async_copy(x_hbm.at[pl.ds(tok0_ref[0, 0, r], 1), :], xbuf.at[0, pl.ds(r, 1), :],
                                  gsem.at[0]).start()
            return carry
        lax.fori_loop(0, tile, issue, 0)

    @pl.when((f == 0) & (i >= 2) & (i - 2 < n_used))
    def _():
        def wait_rows(rows):
            def body(_, carry):
                pltpu.make_async_copy(ybuf.at[slot, pl.ds(0, rows), :], y_hbm.at[pl.ds(0, rows), :],
                                      ssem.at[slot]).wait()
                return carry
            return body
        done = cnt_ref[jnp.maximum(i - 2, 0)]
        lax.fori_loop(0, done // CHUNK, wait_rows(CHUNK), 0)
        lax.fori_loop(0, done % CHUNK, wait_rows(1), 0)

    @pl.when((f == 0) & (i <= n_used))
    def _():
        pltpu.make_async_copy(x_hbm.at[pl.ds(0, tile), :], xbuf.at[slot], gsem.at[slot]).wait()
        xb16[...] = xbuf[slot].astype(BF16)
        ybuf[slot] = jnp.zeros((tile, D_MODEL), F32)

    @pl.when(i < n_used)
    def _():
        for r in range(per_step):
            row = f * per_step + r
            pltpu.make_async_copy(x_hbm.at[pl.ds(tokn_ref[0, 0, row], 1), :],
                                  xbuf.at[1 - slot, pl.ds(row, 1), :], gsem.at[1 - slot]).start()
        x = xb16[...]
        a = jnp.dot(x, w1_ref[0], preferred_element_type=F32)
        b = jnp.dot(x, w3_ref[0], preferred_element_type=F32)
        ybuf[slot] += jnp.dot((_silu(a) * b).astype(BF16), w2_ref[0], preferred_element_type=F32)

    @pl.when((i < n_used) & (f == steps - 1))
    def _():
        def issue(r, carry):
            pltpu.make_async_copy(ybuf.at[slot, pl.ds(r, 1), :], y_hbm.at[pl.ds(dst_ref[0, 0, r], 1), :],
                                  ssem.at[slot]).start()
            return carry
        lax.fori_loop(0, cnt_ref[i], issue, 0)


def _moe_ffn(xn, w1, w3, w2, row_tok, row_dst, tile_e, n_used, tile_cnt, n_tiles, tile):
    n_tok = xn.shape[0]
    steps = MOE_FF_STEPS
    ff = D_FF_EXPERT // steps
    fi = lambda i, f, nu: jnp.where(i < nu[0], f, steps - 1)
    tok_spec = lambda imap: pl.BlockSpec((1, 1, tile), imap, memory_space=pltpu.SMEM)
    grid_spec = pltpu.PrefetchScalarGridSpec(
        num_scalar_prefetch=3,
        grid=(n_tiles, steps),
        in_specs=[tok_spec(lambda i, f, te, nu, cnt: (0, 0, 0)),
                  tok_spec(lambda i, f, te, nu, cnt: (jnp.minimum(i + 1, n_tiles - 1), 0, 0)),
                  tok_spec(lambda i, f, te, nu, cnt: (i, 0, 0)),
                  pl.BlockSpec(memory_space=pl.ANY),
                  pl.BlockSpec((1, D_MODEL, ff), lambda i, f, te, nu, cnt: (te[i], 0, fi(i, f, nu))),
                  pl.BlockSpec((1, D_MODEL, ff), lambda i, f, te, nu, cnt: (te[i], 0, fi(i, f, nu))),
                  pl.BlockSpec((1, ff, D_MODEL), lambda i, f, te, nu, cnt: (te[i], fi(i, f, nu), 0))],
        out_specs=pl.BlockSpec(memory_space=pl.ANY),
        scratch_shapes=[pltpu.VMEM((2, tile, D_MODEL), F32), pltpu.VMEM((tile, D_MODEL), BF16),
                        pltpu.VMEM((2, tile, D_MODEL), F32),
                        pltpu.SemaphoreType.DMA((2,)), pltpu.SemaphoreType.DMA((2,))],
    )
    tok3 = row_tok.reshape(n_tiles, 1, tile)
    return pl.pallas_call(
        _moe_ffn_kernel,
        grid_spec=grid_spec,
        out_shape=jax.ShapeDtypeStruct((2 * n_tok, D_MODEL), F32),
        compiler_params=pltpu.CompilerParams(dimension_semantics=("arbitrary", "arbitrary"),
                                             vmem_limit_bytes=VMEM_LIMIT),
        name="moe_ffn",
    )(tile_e, n_used, tile_cnt, tok3, tok3, row_dst.reshape(n_tiles, 1, tile), xn, w1, w3, w2)


def _moe_combine_kernel(h_ref, y0_ref, y1_ref, route_ref, o_ref):
    g = route_ref[...]
    o_ref[0] = h_ref[...] + g[:, 2:3] * y0_ref[...] + g[:, 3:4] * y1_ref[...]


def _moe_combine(h3, y, route, batch, seq, lp, tc):
    n_tok = h3.shape[0]
    frame_rows = lambda width, off: pl.BlockSpec(
        (pl.Element(tc), pl.Element(width)),
        lambda b, j: (pl.multiple_of(off + b * lp + CHUNK + j * tc, CHUNK), 0))
    return pl.pallas_call(
        _moe_combine_kernel,
        grid=(batch, seq // tc),
        in_specs=[frame_rows(D_MODEL, 0), frame_rows(D_MODEL, 0), frame_rows(D_MODEL, n_tok),
                  frame_rows(LANES, 0)],
        out_specs=pl.BlockSpec((1, tc, D_MODEL), lambda b, j: (b, j, 0)),
        out_shape=jax.ShapeDtypeStruct((batch, seq, D_MODEL), F32),
        compiler_params=pltpu.CompilerParams(dimension_semantics=("arbitrary", "arbitrary"),
                                             vmem_limit_bytes=VMEM_LIMIT),
        name="moe_combine",
    )(h3, y, y, route)


def _layout_rows(seq):
    return -(-(CHUNK + seq) // ROW_TILE) * ROW_TILE


def _proj_tile(lp):
    return next(t for t in (640, 512, 384, ROW_TILE) if lp % t == 0)


def kernel(x, meta_tokens, lb_logits, mix_norm, ffn_norm, e_w_in, a_q_gain, a_k_gain, a_lam_q1, a_lam_k1, a_lam_q2, a_lam_k2, a_sub_gain, b_conv, b_a_log, b_dt_bias, b_out_gain, e_w_out, ffn_w1, ffn_w3, ffn_w2, o_w_in, c_out_gain, o_w_out, router, moe_w1, moe_w3, moe_w2):
    batch, seq, _ = x.shape
    lp = _layout_rows(seq)
    rows = batch * lp
    tm = _proj_tile(lp)
    meta = jnp.broadcast_to(meta_tokens[None].astype(x.dtype), (batch, N_META, D_MODEL))
    h = jnp.concatenate([jnp.zeros((batch, FRONT, D_MODEL), x.dtype), meta, x,
                         jnp.zeros((batch, lp - CHUNK - seq, D_MODEL), x.dtype)], axis=1).reshape(rows, D_MODEL)

    q, k, v, bqkv, bz, gp = _even_in(h, mix_norm[0], e_w_in[0], a_q_gain[0], a_k_gain[0], lp, tm)
    lam_init = 0.8 - 0.6 * math.exp(-0.3 * 0)
    lam = (jnp.exp(jnp.sum(a_lam_q1[0].astype(F32) * a_lam_k1[0].astype(F32)))
           - jnp.exp(jnp.sum(a_lam_q2[0].astype(F32) * a_lam_k2[0].astype(F32))) + lam_init)
    o_a = _diff_attention(q, k, v, a_sub_gain[0], lam, lam_init, batch, lp, ROW_TILE)
    o_b = _gated_deltanet(bqkv, bz, gp, b_conv[0], b_a_log[0], b_dt_bias[0], b_out_gain[0], lp)
    h = _even_out(h, o_a, o_b, e_w_out[0], ffn_norm[0], ffn_w1[0], ffn_w3[0], ffn_w2[0], lp, seq, tm)

    lb_all = jnp.cumsum(jax.nn.softmax(lb_logits.astype(F32), axis=0), axis=0)
    lb = (lb_all - lb_all[0])[1]
    cq, ck, clf, ci, cz = _odd_in(h, mix_norm[1], o_w_in[0], lb, lp, seq, tm)
    o_c = _hgrn2(cq, ck, clf, ci, cz, c_out_gain[0], lp)
    h3, xn, route = _odd_out(h, o_c, o_w_out[0], ffn_norm[1], router[0], lp, seq, tm)
    row_tok, row_dst, tile_e, n_used, tile_cnt, n_tiles = _moe_plan(route, MOE_TILE)
    y = _moe_ffn(xn, moe_w1[0].astype(BF16), moe_w3[0].astype(BF16), moe_w2[0].astype(BF16),
                 row_tok, row_dst, tile_e, n_used, tile_cnt, n_tiles, MOE_TILE)
    return _moe_combine(h3, y, route, batch, seq, lp, ROW_TILE)
```

```python
import functools
import math

import jax
import jax.numpy as jnp
from jax import lax
from jax.experimental import pallas as pl
from jax.experimental.pallas import tpu as pltpu

D_MODEL = 1024
CHUNK = 64
N_META = 16
FRONT = CHUNK - N_META
EPS = 1e-6
ROPE_THETA = 500000.0
LOG2E = math.log2(math.e)

A_HEADS = 4
A_DK = 64
A_DV = 128
A_ROT = A_DK // 4
A_W = A_HEADS * A_DV

B_HEADS = 4
B_DK = 128
B_W = B_HEADS * B_DK
CONV_K = 4

C_HEADS = 8
C_DK = 128

D_FF = 2816
N_EXPERTS = 8
D_FF_EXPERT = 3584

LANES = 128
ROW_TILE = 256
VMEM_LIMIT = 56 * 1024 * 1024

BF16 = jnp.bfloat16
F32 = jnp.float32
HIGHEST = lax.Precision.HIGHEST


def _const_spec(shape):
    return pl.BlockSpec(shape, lambda *_: (0,) * len(shape), pipeline_mode=pl.Buffered(1))


def _rms(x, gain):
    return x * lax.rsqrt(jnp.mean(x * x, axis=-1, keepdims=True) + EPS) * gain


def _silu(x):
    return x * (1.0 / (1.0 + jnp.exp(-x)))


def _sigmoid(x):
    return 1.0 / (1.0 + jnp.exp(-x))


def _row_valid(tile_rows, lp, seq):
    r = pl.program_id(0) * tile_rows + lax.broadcasted_iota(jnp.int32, (tile_rows, 1), 0)
    p = r % lp
    return (p >= FRONT) & (p < CHUNK + seq)


def _head_norm_rope(a, ones_bd, gain, cos, sin_lo, sin_hi):
    ssq = jnp.dot((a * a).astype(BF16), ones_bd, preferred_element_type=F32)
    y = a * lax.rsqrt(ssq * (1.0 / A_DK) + EPS) * gain
    up = pltpu.roll(y, A_W - A_ROT // 2, axis=1)
    dn = pltpu.roll(y, A_ROT // 2, axis=1)
    return y * cos + up * sin_lo + dn * sin_hi


def _even_in_kernel(h_ref, g_ref, wq_ref, wk_ref, wv_ref, wb_ref, wz_ref, wg_ref, bd_ref,
                    qg_ref, kg_ref, cos_ref, slo_ref, shi_ref,
                    q_ref, k_ref, v_ref, b_ref, z_ref, gp_ref):
    xn = _rms(h_ref[...], g_ref[...]).astype(BF16)
    rep = lambda t: jnp.concatenate([t] * (A_W // LANES), axis=1)
    cos, slo, shi = rep(cos_ref[...]), rep(slo_ref[...]), rep(shi_ref[...])
    bd = bd_ref[...]
    aq = jnp.dot(xn, wq_ref[...], preferred_element_type=F32)
    q_ref[...] = (_head_norm_rope(aq, bd, qg_ref[...], cos, slo, shi) * (A_DK ** -0.5 * LOG2E)).astype(BF16)
    ak = jnp.dot(xn, wk_ref[...], preferred_element_type=F32)
    k_ref[...] = _head_norm_rope(ak, bd, kg_ref[...], cos, slo, shi).astype(BF16)
    v_ref[...] = jnp.dot(xn, wv_ref[...], preferred_element_type=F32).astype(BF16)
    b_ref[...] = jnp.dot(xn, wb_ref[...], preferred_element_type=F32).astype(BF16)
    z_ref[...] = jnp.dot(xn, wz_ref[...], preferred_element_type=F32).astype(BF16)
    gp_ref[...] = jnp.dot(xn, wg_ref[...], preferred_element_type=F32)


def _even_in(h, gain, w_in, q_gain, k_gain, lp, tm):
    rows = h.shape[0]
    wb16 = w_in.astype(BF16)
    c0, c1, c2, c3, c4 = A_W, 2 * A_W, 3 * A_W, 3 * A_W + 3 * B_W, 3 * A_W + 4 * B_W
    wq, wk, wv, wb, wz = wb16[:, :c0], wb16[:, c0:c1], wb16[:, c1:c2], wb16[:, c2:c3], wb16[:, c3:c4]
    wg = jnp.pad(wb16[:, c4:], ((0, 0), (0, LANES - 2 * B_HEADS)))
    d = jnp.arange(A_W)
    ones_bd = (d[:, None] // A_DK == d[None, :] // A_DK).astype(BF16)
    half = A_ROT // 2
    inv_freq = ROPE_THETA ** (-jnp.arange(half, dtype=F32) * 2.0 / A_ROT)
    pos = (jnp.arange(lp) - FRONT).astype(F32)
    ang = pos[:, None] * inv_freq[None, :]
    cosv, sinv = jnp.cos(ang), jnp.sin(ang)
    zeros = jnp.zeros((lp, A_DK - A_ROT), F32)
    one = jnp.ones((lp, A_DK - A_ROT), F32)
    zh = jnp.zeros((lp, half), F32)
    cos64 = jnp.concatenate([cosv, cosv, one], axis=1)
    slo64 = jnp.concatenate([-sinv, zh, zeros], axis=1)
    shi64 = jnp.concatenate([zh, sinv, zeros], axis=1)
    two = lambda t: jnp.concatenate([t, t], axis=1)
    cos_t, slo_t, shi_t = two(cos64), two(slo64), two(shi64)
    tile8 = lambda g: jnp.tile(g.astype(F32), 2 * A_HEADS)[None, :]
    n_pos = lp // tm
    row_spec = lambda w: pl.BlockSpec((tm, w), lambda i: (i, 0))
    pos_spec = pl.BlockSpec((tm, LANES), lambda i: (i % n_pos, 0))
    outs = pl.pallas_call(
        _even_in_kernel,
        grid=(rows // tm,),
        in_specs=[row_spec(D_MODEL), _const_spec((1, D_MODEL)),
                  _const_spec(wq.shape), _const_spec(wk.shape), _const_spec(wv.shape),
                  _const_spec(wb.shape), _const_spec(wz.shape), _const_spec(wg.shape),
                  _const_spec(ones_bd.shape), _const_spec((1, A_W)), _const_spec((1, A_W)),
                  pos_spec, pos_spec, pos_spec],
        out_specs=(row_spec(A_W), row_spec(A_W), row_spec(A_W), row_spec(3 * B_W), row_spec(B_W),
                   row_spec(LANES)),
        out_shape=(jax.ShapeDtypeStruct((rows, A_W), BF16), jax.ShapeDtypeStruct((rows, A_W), BF16),
                   jax.ShapeDtypeStruct((rows, A_W), BF16), jax.ShapeDtypeStruct((rows, 3 * B_W), BF16),
                   jax.ShapeDtypeStruct((rows, B_W), BF16), jax.ShapeDtypeStruct((rows, LANES), F32)),
        compiler_params=pltpu.CompilerParams(dimension_semantics=("arbitrary",),
                                             vmem_limit_bytes=VMEM_LIMIT),
        name="even_in_proj",
    )(h, gain[None, :].astype(F32), wq, wk, wv, wb, wz, wg, ones_bd, tile8(q_gain), tile8(k_gain),
      cos_t, slo_t, shi_t)
    return outs


ATTN_WIDE = 1024


def _attn_kernel(q_ref, k_ref, v_ref, sg_ref, lam_ref, o_ref, qs_ref, vx_ref, m_ref, acc_ref,
                 sa_ref, sb_ref, *, tq, width, lam_init):
    qi = pl.program_id(2)

    @pl.when(qi == 0)
    def _():
        vx_ref[:, :A_DV] = v_ref[...]
        vx_ref[:, A_DV:] = jnp.ones((vx_ref.shape[0], A_DV), BF16)

    q = q_ref[...]
    lane = lax.broadcasted_iota(jnp.int32, q.shape, 1)
    zero = jnp.zeros_like(q)
    qs_ref[:tq, :] = jnp.where(lane < A_DK, q, zero)
    qs_ref[tq:, :] = jnp.where(lane >= A_DK, q, zero)
    m_ref[...] = jnp.full(m_ref.shape, -jnp.inf, F32)
    acc_ref[...] = jnp.zeros(acc_ref.shape, F32)

    lp = k_ref.shape[0]
    qchunk = (qi * tq + lax.broadcasted_iota(jnp.int32, (2 * tq, 1), 0) % tq) // CHUNK

    def window(t):
        return pl.multiple_of(jnp.minimum(t * width, lp - width), tq)

    def scores(s_ref, t):
        s_ref[...] = lax.dot_general(qs_ref[...], k_ref[pl.ds(window(t), width), :],
                                     (((1,), (1,)), ((), ())), preferred_element_type=F32)

    def softmax_pv(s_ref, t, mask):
        start = window(t)
        kpos = start + lax.broadcasted_iota(jnp.int32, (1, width), 1)
        kchunk = jnp.where((kpos >= t * width) & (kpos >= FRONT), kpos // CHUNK, lp)
        nblk = width // LANES

        def block(c):
            cs = slice(c * LANES, (c + 1) * LANES)
            if mask == "full":
                return jnp.where(kchunk[:, cs] <= qchunk, s_ref[:, cs], -jnp.inf)
            if mask == "pad" and c == 0:
                return jnp.where(kpos[:, cs] >= FRONT, s_ref[:, cs], -jnp.inf)
            return s_ref[:, cs]

        mx = functools.reduce(jnp.maximum, [block(c) for c in range(nblk)])
        m_old = m_ref[...]
        m_new = jnp.maximum(m_old, jnp.max(mx, axis=-1, keepdims=True))
        alpha = jnp.exp2(m_old - m_new)
        p = jnp.concatenate([jnp.exp2((block(c) - m_new).astype(BF16)) for c in range(nblk)], axis=1)
        pv = jnp.dot(p, vx_ref[pl.ds(start, width), :], preferred_element_type=F32)
        acc_ref[:, :A_DV] = alpha * acc_ref[:, :A_DV] + pv[:, :A_DV]
        acc_ref[:, A_DV:] = alpha * acc_ref[:, A_DV:] + pv[:, A_DV:]
        m_ref[...] = m_new

    n_pairs = ((qi * tq + tq + width - 1) // width + 1) // 2

    def pair(j, masks):
        scores(sb_ref, 2 * j + 1)
        softmax_pv(sa_ref, 2 * j, masks[0])
        scores(sa_ref, 2 * j + 2)
        softmax_pv(sb_ref, 2 * j + 1, masks[1])

    scores(sa_ref, 0)

    @pl.when(n_pairs > 1)
    def _():
        pair(0, ("pad", "none"))

    def inner(j, carry):
        pair(j, ("none", "none"))
        return carry
    lax.fori_loop(1, n_pairs - 1, inner, 0)
    scores(sb_ref, 2 * n_pairs - 1)
    softmax_pv(sa_ref, 2 * n_pairs - 2, "full")
    softmax_pv(sb_ref, 2 * n_pairs - 1, "full")

    o = acc_ref[:, :A_DV] / acc_ref[:, A_DV:]
    o = o[:tq] - lam_ref[...] * o[tq:]
    o_ref[...] = (_rms(o, sg_ref[...]) * (1.0 - lam_init)).astype(o_ref.dtype)


def _diff_attention(q, k, v, sub_gain, lam, lam_init, batch, lp, tq):
    nq = lp // tq
    kv_spec = pl.BlockSpec((lp, LANES), lambda b, h, i: (b, h))
    q_spec = pl.BlockSpec((tq, LANES), lambda b, h, i: (b * nq + i, h))
    vec = pl.BlockSpec((1, LANES), lambda b, h, i: (0, 0))
    width = min(ATTN_WIDE, lp)
    return pl.pallas_call(
        functools.partial(_attn_kernel, tq=tq, width=width, lam_init=lam_init),
        grid=(batch, A_HEADS, nq),
        in_specs=[q_spec, kv_spec, kv_spec, vec, vec],
        out_specs=q_spec,
        out_shape=jax.ShapeDtypeStruct(q.shape, BF16),
        scratch_shapes=[pltpu.VMEM((2 * tq, LANES), BF16), pltpu.VMEM((lp, 2 * A_DV), BF16),
                        pltpu.VMEM((2 * tq, LANES), F32), pltpu.VMEM((2 * tq, 2 * A_DV), F32),
                        pltpu.VMEM((2 * tq, width), F32), pltpu.VMEM((2 * tq, width), F32)],
        compiler_params=pltpu.CompilerParams(dimension_semantics=("arbitrary",) * 3,
                                             vmem_limit_bytes=VMEM_LIMIT),
        name="diff_attention",
    )(q, k, v, sub_gain[None, :].astype(F32), jnp.full((1, LANES), lam, F32))


def _chunk_masks(n):
    r = lax.broadcasted_iota(jnp.int32, (n, n), 0)
    c = lax.broadcasted_iota(jnp.int32, (n, n), 1)
    same = (r // CHUNK) == (c // CHUNK)
    return r, c, same


def _bdot(a, b):
    return jnp.dot(a.astype(BF16), b.astype(BF16), preferred_element_type=F32)


def _bdot_nt(a, b):
    return lax.dot_general(a.astype(BF16), b.astype(BF16), (((1,), (1,)), ((), ())),
                           preferred_element_type=F32)


def _bdot_tn(a, b):
    return lax.dot_general(a.astype(BF16), b.astype(BF16), (((0,), (0,)), ((), ())),
                           preferred_element_type=F32)


def _gdn_kernel(x_ref, z_ref, gp_ref, cw_ref, alog_ref, dtb_ref, og_ref, o_ref,
                xbuf_ref, s_ref, *, steps_per_batch):
    n = ROW_TILE
    step = pl.program_id(0) % steps_per_batch

    @pl.when(step == 0)
    def _():
        xbuf_ref[:8, :] = jnp.zeros((8, 3 * B_W), F32)
        s_ref[...] = jnp.zeros_like(s_ref)

    xbuf_ref[8:, :] = x_ref[...].astype(F32)
    conv = cw_ref[CONV_K - 1:CONV_K, :] * xbuf_ref[8:, :]
    for j in range(CONV_K - 1):
        conv = conv + cw_ref[j:j + 1, :] * xbuf_ref[5 + j:5 + j + n, :]
    xbuf_ref[:8, :] = xbuf_ref[n:n + 8, :]
    conv = _silu(conv)

    r, c, same = _chunk_masks(n)
    incl = same & (r >= c)
    strict = same & (r > c)
    eye = (r == c).astype(F32)
    gp = gp_ref[...]
    beta_all = _sigmoid(gp)
    gpre = gp + dtb_ref[...]
    softplus = jnp.maximum(gpre, 0.0) + jnp.log(1.0 + jnp.exp(-jnp.abs(gpre)))
    row = step * n + lax.broadcasted_iota(jnp.int32, (n, 1), 0)
    g_all = jnp.where(row >= FRONT, -jnp.exp(alog_ref[...]) * softplus, 0.0)
    gc_all = jnp.dot(incl.astype(F32), g_all, precision=HIGHEST, preferred_element_type=F32)
    last = (same & (c % CHUNK == CHUNK - 1)).astype(F32)
    gl_all = jnp.dot(last, gc_all, precision=HIGHEST, preferred_element_type=F32)
    gc_rows = gc_all.T

    heads = range(B_HEADS)
    chunks = [slice(ci * CHUNK, (ci + 1) * CHUNK) for ci in range(n // CHUNK)]
    p, t, rhs, qk, qg, kdec, decay = [], [], [], [], [], [], []
    for h in heads:
        qh = conv[:, h * B_DK:(h + 1) * B_DK]
        kh = conv[:, B_W + h * B_DK:B_W + (h + 1) * B_DK]
        vh = conv[:, 2 * B_W + h * B_DK:2 * B_W + (h + 1) * B_DK]
        qh = qh * lax.rsqrt(jnp.sum(qh * qh, axis=-1, keepdims=True) + EPS) * (B_DK ** -0.5)
        kh = kh * lax.rsqrt(jnp.sum(kh * kh, axis=-1, keepdims=True) + EPS)
        beta = beta_all[:, h:h + 1]
        gcol = gc_all[:, B_HEADS + h:B_HEADS + h + 1]
        glcol = gl_all[:, B_HEADS + h:B_HEADS + h + 1]
        grow = gc_rows[B_HEADS + h:B_HEADS + h + 1, :]
        dec = jnp.exp(jnp.where(incl, gcol - grow, -jnp.inf))
        kb = kh * beta
        egc = jnp.exp(gcol)
        p0 = -(_bdot_nt(kb, kh) * jnp.where(strict, dec, 0.0))
        p.append(p0)
        t.append(eye + p0)
        rhs.append(jnp.concatenate([vh * beta, kb * egc], axis=1))
        qk.append(_bdot_nt(qh, kh) * dec)
        qg.append(qh * egc)
        kdec.append(kh * jnp.exp(glcol - gcol))
        decay.append([jnp.exp(glcol[rs.stop - 1:rs.stop, :]) for rs in chunks])
    for _ in range(5):
        for h in heads:
            p[h] = _bdot(p[h], p[h])
            t[h] = t[h] + _bdot(t[h], p[h])
    sol = [_bdot(t[h], rhs[h]) for h in heads]
    u = [s[:, :B_DK] for s in sol]
    w = [s[:, B_DK:] for s in sol]
    kw = [[_bdot_tn(kdec[h][rs], w[h][rs]) for rs in chunks] for h in heads]
    ku = [[_bdot_tn(kdec[h][rs], u[h][rs]) for rs in chunks] for h in heads]
    state = [s_ref[h] for h in heads]
    before = [[] for _ in heads]
    for ci in range(len(chunks)):
        for h in heads:
            before[h].append(state[h])
            state[h] = state[h] * decay[h][ci] - _bdot(kw[h][ci], state[h]) + ku[h][ci]
    for h in heads:
        s_ref[h] = state[h]
        vnew, inter = [], []
        for ci, rs in enumerate(chunks):
            ws = _bdot(jnp.concatenate([w[h][rs], qg[h][rs]], axis=0), before[h][ci])
            vnew.append(u[h][rs] - ws[:CHUNK])
            inter.append(ws[CHUNK:])
        o = jnp.concatenate(inter, axis=0) + _bdot(qk[h], jnp.concatenate(vnew, axis=0))
        sl = slice(h * B_DK, (h + 1) * B_DK)
        o_ref[:, sl] = (_rms(o, og_ref[...]) * _silu(z_ref[:, sl].astype(F32))).astype(o_ref.dtype)


def _gated_deltanet(bqkv, bz, gp, conv_w, a_log, dt_bias, out_gain, lp):
    rows = bqkv.shape[0]
    n = ROW_TILE
    head_vec = lambda p: jnp.zeros((1, LANES), F32).at[0, B_HEADS:2 * B_HEADS].set(p.astype(F32))
    row_spec = lambda w: pl.BlockSpec((n, w), lambda i: (i, 0))
    return pl.pallas_call(
        functools.partial(_gdn_kernel, steps_per_batch=lp // n),
        grid=(rows // n,),
        in_specs=[row_spec(3 * B_W), row_spec(B_W), row_spec(LANES), _const_spec((CONV_K, 3 * B_W)),
                  _const_spec((1, LANES)), _const_spec((1, LANES)), _const_spec((1, B_DK))],
        out_specs=row_spec(B_W),
        out_shape=jax.ShapeDtypeStruct((rows, B_W), BF16),
        scratch_shapes=[pltpu.VMEM((n + 8, 3 * B_W), F32), pltpu.VMEM((B_HEADS, B_DK, B_DK), F32)],
        compiler_params=pltpu.CompilerParams(dimension_semantics=("arbitrary",),
                                             vmem_limit_bytes=VMEM_LIMIT),
        name="gated_deltanet",
    )(bqkv, bz, gp, conv_w.astype(F32), head_vec(a_log), head_vec(dt_bias), out_gain[None, :].astype(F32))


FF_COLS = 256


def _even_out_kernel(h_ref, oa_ref, ob_ref, woa_ref, wob_ref, g_ref, w1_ref, w3_ref, w2_ref, o_ref,
                     act_ref, *, tm, lp, seq):
    h1 = (h_ref[...] + jnp.dot(oa_ref[...], woa_ref[...], preferred_element_type=F32)
          + jnp.dot(ob_ref[...], wob_ref[...], preferred_element_type=F32))
    xn = _rms(h1, g_ref[...]).astype(BF16)
    for c in range(D_FF // FF_COLS):
        cs = slice(c * FF_COLS, (c + 1) * FF_COLS)
        a = jnp.dot(xn, w1_ref[:, cs], preferred_element_type=F32)
        b = jnp.dot(xn, w3_ref[:, cs], preferred_element_type=F32)
        act_ref[:, cs] = (_silu(a) * b).astype(BF16)
    out = h1 + jnp.dot(act_ref[...], w2_ref[...], preferred_element_type=F32)
    o_ref[...] = jnp.where(_row_valid(tm, lp, seq), out, 0.0)


def _even_out(h, oa, ob, w_out, gain, w1, w3, w2, lp, seq, tm):
    rows = h.shape[0]
    wo = w_out.astype(BF16)
    row_spec = lambda w: pl.BlockSpec((tm, w), lambda i: (i, 0))
    return pl.pallas_call(
        functools.partial(_even_out_kernel, tm=tm, lp=lp, seq=seq),
        grid=(rows // tm,),
        in_specs=[row_spec(D_MODEL), row_spec(A_W), row_spec(B_W),
                  _const_spec((A_W, D_MODEL)), _const_spec((B_W, D_MODEL)), _const_spec((1, D_MODEL)),
                  _const_spec(w1.shape), _const_spec(w3.shape), _const_spec(w2.shape)],
        out_specs=row_spec(D_MODEL),
        out_shape=jax.ShapeDtypeStruct((rows, D_MODEL), F32),
        scratch_shapes=[pltpu.VMEM((tm, D_FF), BF16)],
        compiler_params=pltpu.CompilerParams(dimension_semantics=("arbitrary",),
                                             vmem_limit_bytes=VMEM_LIMIT),
        name="even_out_ffn",
    )(h, oa, ob, wo[:A_W], wo[A_W:], gain[None, :].astype(F32),
      w1.astype(BF16), w3.astype(BF16), w2.astype(BF16))


def _odd_in_kernel(h_ref, g_ref, wq_ref, wf_ref, wi_ref, wz_ref, lb_ref,
                   q_ref, k_ref, lf_ref, i_ref, z_ref, *, tm, lp, seq):
    xn = _rms(h_ref[...], g_ref[...]).astype(BF16)
    valid = _row_valid(tm, lp, seq)
    q_ref[...] = jnp.dot(xn, wq_ref[...], preferred_element_type=F32).astype(BF16)
    f = jnp.dot(xn, wf_ref[...], preferred_element_type=F32)
    lb = lb_ref[...]
    fg = lb + (1.0 - lb) * _sigmoid(f)
    k_ref[...] = jnp.where(valid, 1.0 - fg, 0.0).astype(BF16)
    lf_ref[...] = jnp.where(valid, jnp.log(fg), 0.0)
    i_ref[...] = jnp.dot(xn, wi_ref[...], preferred_element_type=F32).astype(BF16)
    z_ref[...] = jnp.dot(xn, wz_ref[...], preferred_element_type=F32).astype(BF16)


def _odd_in(h, gain, w_in, lb, lp, seq, tm):
    rows = h.shape[0]
    wb16 = w_in.astype(BF16)
    ws = [wb16[:, j * D_MODEL:(j + 1) * D_MODEL] for j in range(4)]
    row_spec = pl.BlockSpec((tm, D_MODEL), lambda i: (i, 0))
    wspec = _const_spec((D_MODEL, D_MODEL))
    vec = _const_spec((1, D_MODEL))
    sd = lambda dt: jax.ShapeDtypeStruct((rows, D_MODEL), dt)
    return pl.pallas_call(
        functools.partial(_odd_in_kernel, tm=tm, lp=lp, seq=seq),
        grid=(rows // tm,),
        in_specs=[row_spec, vec, wspec, wspec, wspec, wspec, vec],
        out_specs=(row_spec,) * 5,
        out_shape=(sd(BF16), sd(BF16), sd(F32), sd(BF16), sd(BF16)),
        compiler_params=pltpu.CompilerParams(dimension_semantics=("arbitrary",),
                                             vmem_limit_bytes=VMEM_LIMIT),
        name="odd_in_proj",
    )(h, gain[None, :].astype(F32), *ws, lb[None, :].astype(F32))


SUB = 8
HALVES = (32, 16, 8)


def _hgrn_kernel(q_ref, k_ref, lf_ref, i_ref, z_ref, og_ref, sel_ref, o_ref, s_ref, p_ref,
                 *, steps_per_batch):
    n = ROW_TILE
    step = pl.program_id(0) % steps_per_batch

    @pl.when(step == 0)
    def _():
        s_ref[...] = jnp.zeros_like(s_ref)

    r, c, same = _chunk_masks(n)
    incl = same & (r >= c)
    gc_all = jnp.dot(incl.astype(F32), lf_ref[...], precision=HIGHEST, preferred_element_type=F32) * LOG2E
    rrow = lax.broadcasted_iota(jnp.int32, (n, 1), 0)
    inner = rrow % SUB
    level_mask = [(r // (2 * half) == c // (2 * half)).astype(F32) for half in HALVES]
    diag_mask = (r // SUB == c // SUB).astype(F32)

    def block_rows(a, size, j):
        a3 = a.reshape(n // size, size, a.shape[1])
        return jnp.broadcast_to(a3[:, j:j + 1, :], a3.shape).reshape(a.shape)

    for h in range(C_HEADS):
        sl = slice(h * C_DK, (h + 1) * C_DK)
        q = q_ref[:, sl].astype(F32)
        k = k_ref[:, sl].astype(F32)
        iv = i_ref[:, sl]
        gc = gc_all[:, sl]
        gl = block_rows(gc, CHUNK, CHUNK - 1)
        qg = q * jnp.exp2(gc)
        kdec = k * jnp.exp2(gl - gc)
        att = jnp.zeros((n, n), F32)
        for half, mask in zip(HALVES, level_mask):
            gref = block_rows(gc, 2 * half, half - 1)
            upper = (rrow % (2 * half)) >= half
            qt = q * jnp.exp2(jnp.where(upper, gc - gref, -jnp.inf))
            kt = k * jnp.exp2(jnp.where(upper, -jnp.inf, gref - gc))
            att = att + mask * _bdot_nt(qt, kt)
        for j in range(SUB):
            kj = block_rows(k, SUB, j)
            gj = block_rows(gc, SUB, j)
            p_ref[:, j * C_DK:(j + 1) * C_DK] = (
                q * kj * jnp.exp2(jnp.where(inner >= j, gc - gj, -jnp.inf))).astype(BF16)
        diag = jnp.dot(p_ref[...], sel_ref[...], preferred_element_type=F32)
        att = att + diag_mask * diag
        st = s_ref[h]
        inter = []
        for ci in range(n // CHUNK):
            rs = slice(ci * CHUNK, (ci + 1) * CHUNK)
            inter.append(_bdot_nt(qg[rs], st))
            st = st * jnp.exp2(gl[ci * CHUNK:ci * CHUNK + 1, :]) + _bdot_tn(iv[rs], kdec[rs])
        s_ref[h] = st
        o = jnp.concatenate(inter, axis=0) + _bdot(att, iv)
        o_ref[:, sl] = (_rms(o, og_ref[...]) * _silu(z_ref[:, sl].astype(F32))).astype(o_ref.dtype)


def _hgrn2(q, k, lf, iv, z, out_gain, lp):
    rows = q.shape[0]
    n = ROW_TILE
    row_spec = pl.BlockSpec((n, D_MODEL), lambda i: (i, 0))
    sel = ((jnp.arange(SUB * C_DK)[:, None] // C_DK) == (jnp.arange(n)[None, :] % SUB)).astype(BF16)
    return pl.pallas_call(
        functools.partial(_hgrn_kernel, steps_per_batch=lp // n),
        grid=(rows // n,),
        in_specs=[row_spec] * 5 + [_const_spec((1, C_DK)), _const_spec(sel.shape)],
        out_specs=row_spec,
        out_shape=jax.ShapeDtypeStruct((rows, D_MODEL), BF16),
        scratch_shapes=[pltpu.VMEM((C_HEADS, C_DK, C_DK), F32), pltpu.VMEM((n, SUB * C_DK), BF16)],
        compiler_params=pltpu.CompilerParams(dimension_semantics=("arbitrary",),
                                             vmem_limit_bytes=VMEM_LIMIT),
        name="hgrn2",
    )(q, k, lf, iv, z, out_gain[None, :].astype(F32), sel)


def _odd_out_kernel(h_ref, o_ref, wo_ref, g_ref, wr_ref, h3_ref, xn_ref, route_ref, *, tm, lp, seq):
    h3 = h_ref[...] + jnp.dot(o_ref[...], wo_ref[...], preferred_element_type=F32)
    h3 = jnp.where(_row_valid(tm, lp, seq), h3, 0.0)
    h3_ref[...] = h3
    xn = _rms(h3, g_ref[...])
    xn_ref[...] = xn
    logits = jnp.dot(xn.astype(BF16), wr_ref[...], preferred_element_type=F32)
    lane = lax.broadcasted_iota(jnp.int32, logits.shape, 1)
    logits = jnp.where(lane < N_EXPERTS, logits, -jnp.inf)
    v1 = jnp.max(logits, axis=-1, keepdims=True)
    e1 = jnp.min(jnp.where(logits == v1, lane, LANES), axis=-1, keepdims=True)
    rest = jnp.where(lane == e1, -jnp.inf, logits)
    v2 = jnp.max(rest, axis=-1, keepdims=True)
    e2 = jnp.min(jnp.where(rest == v2, lane, LANES), axis=-1, keepdims=True)
    t = jnp.exp(v2 - v1)
    g1 = 1.0 / (1.0 + t)
    g2 = t * g1
    route_ref[...] = jnp.where(lane == 0, e1.astype(F32),
                               jnp.where(lane == 1, e2.astype(F32),
                                         jnp.where(lane == 2, g1, jnp.where(lane == 3, g2, 0.0))))


def _odd_out(h, o, w_out, gain, router, lp, seq, tm):
    rows = h.shape[0]
    wr = jnp.pad(router.astype(BF16), ((0, 0), (0, LANES - N_EXPERTS)))
    row_spec = lambda w: pl.BlockSpec((tm, w), lambda i: (i, 0))
    return pl.pallas_call(
        functools.partial(_odd_out_kernel, tm=tm, lp=lp, seq=seq),
        grid=(rows // tm,),
        in_specs=[row_spec(D_MODEL), row_spec(D_MODEL), _const_spec((D_MODEL, D_MODEL)),
                  _const_spec((1, D_MODEL)), _const_spec((D_MODEL, LANES))],
        out_specs=(row_spec(D_MODEL), row_spec(D_MODEL), row_spec(LANES)),
        out_shape=(jax.ShapeDtypeStruct((rows, D_MODEL), F32), jax.ShapeDtypeStruct((rows, D_MODEL), F32),
                   jax.ShapeDtypeStruct((rows, LANES), F32)),
        compiler_params=pltpu.CompilerParams(dimension_semantics=("arbitrary",),
                                             vmem_limit_bytes=VMEM_LIMIT),
        name="odd_out_route",
    )(h, o, w_out.astype(BF16), gain[None, :].astype(F32), wr)


MOE_TILE = 512
MOE_FF_STEPS = 4


def _moe_plan(route, tile):
    n_tok = route.shape[0]
    flat_e = route[:, :2].astype(jnp.int32).reshape(-1)
    na = flat_e.shape[0]
    onehot = (flat_e[:, None] == jnp.arange(N_EXPERTS, dtype=jnp.int32)[None, :]).astype(jnp.int32)
    csum = jnp.cumsum(onehot, axis=0)
    rank = jnp.take_along_axis(csum, flat_e[:, None], axis=1)[:, 0] - 1
    counts = csum[-1]
    padded = (counts + tile - 1) // tile * tile
    pend = jnp.cumsum(padded)
    dest = (pend - padded)[flat_e] + rank
    n_tiles = -(-(na + N_EXPERTS * (tile - 1)) // tile) + 2
    n_rows = n_tiles * tile
    pair = jnp.full((n_rows,), -1, jnp.int32).at[dest].set(jnp.arange(na, dtype=jnp.int32))
    row_tok = jnp.where(pair >= 0, pair // 2, 0)
    row = jnp.arange(n_rows, dtype=jnp.int32)
    dump = na + ((row // tile) % 2) * tile + row % tile
    row_dst = jnp.where(pair >= 0, (pair % 2) * n_tok + pair // 2, dump)
    tile_e = jnp.minimum(jnp.searchsorted(pend, jnp.arange(n_tiles, dtype=jnp.int32) * tile, side='right'),
                         N_EXPERTS - 1).astype(jnp.int32)
    n_used = (pend[-1] // tile).astype(jnp.int32).reshape(1)
    return row_tok, row_dst, tile_e, n_used, n_tiles


def _moe_ffn_kernel(te_ref, nused_ref, tok0_ref, tokn_ref, dstp_ref, x_hbm, w1_ref, w3_ref, w2_ref,
                    y_hbm, xbuf, xb16, ybuf, gsem, ssem):
    i, f = pl.program_id(0), pl.program_id(1)
    tile = xb16.shape[0]
    per_step = tile // MOE_FF_STEPS
    slot = i % 2
    n_used = nused_ref[0]
    n_out = y_hbm.shape[0] - 2 * tile

    def gather_start(tok_ref, row, dst_slot):
        pltpu.make_async_copy(x_hbm.at[pl.ds(tok_ref[0, 0, row], 1), :],
                              xbuf.at[dst_slot, pl.ds(row, 1), :], gsem.at[dst_slot]).start()

    def scatter_start(row, src_slot):
        pltpu.make_async_copy(ybuf.at[src_slot, pl.ds(row, 1), :],
                              y_hbm.at[pl.ds(dstp_ref[0, 0, row], 1), :], ssem.at[src_slot]).start()

    @pl.when((i == 0) & (f == 0))
    def _():
        ybuf[...] = jnp.zeros(ybuf.shape, F32)
        for s in range(2):
            dump = pltpu.make_async_copy(ybuf.at[s], y_hbm.at[pl.ds(n_out + s * tile, tile), :], ssem.at[s])
            dump.start()
            dump.wait()

        def issue(r, carry):
            gather_start(tok0_ref, r, 0)
            return carry
        lax.fori_loop(0, tile, issue, 0)

    @pl.when((f == 0) & (i >= 2) & (i - 2 < n_used))
    def _():
        pltpu.make_async_copy(ybuf.at[slot], y_hbm.at[pl.ds(0, tile), :], ssem.at[slot]).wait()

    @pl.when((f == 0) & (i <= n_used))
    def _():
        pltpu.make_async_copy(x_hbm.at[pl.ds(0, tile), :], xbuf.at[slot], gsem.at[slot]).wait()
        xb16[...] = xbuf[slot].astype(BF16)
        ybuf[slot] = jnp.zeros((tile, D_MODEL), F32)

    def compute(scatter_previous):
        for r in range(per_step):
            gather_start(tokn_ref, f * per_step + r, 1 - slot)
            if scatter_previous:
                scatter_start(f * per_step + r, 1 - slot)
        x = xb16[...]
        a = jnp.dot(x, w1_ref[0], preferred_element_type=F32)
        b = jnp.dot(x, w3_ref[0], preferred_element_type=F32)
        ybuf[slot] += jnp.dot((_silu(a) * b).astype(BF16), w2_ref[0], preferred_element_type=F32)

    @pl.when(i == 0)
    def _():
        compute(False)

    @pl.when((i > 0) & (i < n_used))
    def _():
        compute(True)

    @pl.when((i == n_used) & (f == 0))
    def _():
        def issue(r, carry):
            scatter_start(r, 1 - slot)
            return carry
        lax.fori_loop(0, tile, issue, 0)


def _moe_ffn(xn, w1, w3, w2, row_tok, row_dst, tile_e, n_used, n_tiles, tile):
    n_tok = xn.shape[0]
    steps = MOE_FF_STEPS
    ff = D_FF_EXPERT // steps
    fi = lambda i, f, nu: jnp.where(i < nu[0], f, steps - 1)
    tok_spec = lambda imap: pl.BlockSpec((1, 1, tile), imap, memory_space=pltpu.SMEM)
    grid_spec = pltpu.PrefetchScalarGridSpec(
        num_scalar_prefetch=2,
        grid=(n_tiles, steps),
        in_specs=[tok_spec(lambda i, f, te, nu: (0, 0, 0)),
                  tok_spec(lambda i, f, te, nu: (jnp.minimum(i + 1, n_tiles - 1), 0, 0)),
                  tok_spec(lambda i, f, te, nu: (jnp.maximum(i - 1, 0), 0, 0)),
                  pl.BlockSpec(memory_space=pl.ANY),
                  pl.BlockSpec((1, D_MODEL, ff), lambda i, f, te, nu: (te[i], 0, fi(i, f, nu))),
                  pl.BlockSpec((1, D_MODEL, ff), lambda i, f, te, nu: (te[i], 0, fi(i, f, nu))),
                  pl.BlockSpec((1, ff, D_MODEL), lambda i, f, te, nu: (te[i], fi(i, f, nu), 0))],
        out_specs=pl.BlockSpec(memory_space=pl.ANY),
        scratch_shapes=[pltpu.VMEM((2, tile, D_MODEL), F32), pltpu.VMEM((tile, D_MODEL), BF16),
                        pltpu.VMEM((2, tile, D_MODEL), F32),
                        pltpu.SemaphoreType.DMA((2,)), pltpu.SemaphoreType.DMA((2,))],
    )
    tok3 = row_tok.reshape(n_tiles, 1, tile)
    return pl.pallas_call(
        _moe_ffn_kernel,
        grid_spec=grid_spec,
        out_shape=jax.ShapeDtypeStruct((2 * n_tok + 2 * tile, D_MODEL), F32),
        compiler_params=pltpu.CompilerParams(dimension_semantics=("arbitrary", "arbitrary"),
                                             vmem_limit_bytes=VMEM_LIMIT),
        name="moe_ffn",
    )(tile_e, n_used, tok3, tok3, row_dst.reshape(n_tiles, 1, tile), xn, w1, w3, w2)


def _moe_combine_kernel(h_ref, y0_ref, y1_ref, route_ref, o_ref):
    g = route_ref[...]
    o_ref[0] = h_ref[...] + g[:, 2:3] * y0_ref[...] + g[:, 3:4] * y1_ref[...]


def _moe_combine(h3, y, route, batch, seq, lp, tc):
    n_tok = h3.shape[0]
    frame_rows = lambda width, off: pl.BlockSpec(
        (pl.Element(tc), pl.Element(width)),
        lambda b, j: (pl.multiple_of(off + b * lp + CHUNK + j * tc, CHUNK), 0))
    return pl.pallas_call(
        _moe_combine_kernel,
        grid=(batch, seq // tc),
        in_specs=[frame_rows(D_MODEL, 0), frame_rows(D_MODEL, 0), frame_rows(D_MODEL, n_tok),
                  frame_rows(LANES, 0)],
        out_specs=pl.BlockSpec((1, tc, D_MODEL), lambda b, j: (b, j, 0)),
        out_shape=jax.ShapeDtypeStruct((batch, seq, D_MODEL), F32),
        compiler_params=pltpu.CompilerParams(dimension_semantics=("arbitrary", "arbitrary"),
                                             vmem_limit_bytes=VMEM_LIMIT),
        name="moe_combine",
    )(h3, y, y, route)


def _layout_rows(seq):
    return -(-(CHUNK + seq) // ROW_TILE) * ROW_TILE


def _proj_tile(lp):
    return next(t for t in (640, 512, 384, ROW_TILE) if lp % t == 0)


def kernel(x, meta_tokens, lb_logits, mix_norm, ffn_norm, e_w_in, a_q_gain, a_k_gain, a_lam_q1, a_lam_k1, a_lam_q2, a_lam_k2, a_sub_gain, b_conv, b_a_log, b_dt_bias, b_out_gain, e_w_out, ffn_w1, ffn_w3, ffn_w2, o_w_in, c_out_gain, o_w_out, router, moe_w1, moe_w3, moe_w2):
    batch, seq, _ = x.shape
    lp = _layout_rows(seq)
    rows = batch * lp
    tm = _proj_tile(lp)
    meta = jnp.broadcast_to(meta_tokens[None].astype(x.dtype), (batch, N_META, D_MODEL))
    h = jnp.concatenate([jnp.zeros((batch, FRONT, D_MODEL), x.dtype), meta, x,
                         jnp.zeros((batch, lp - CHUNK - seq, D_MODEL), x.dtype)], axis=1).reshape(rows, D_MODEL)

    q, k, v, bqkv, bz, gp = _even_in(h, mix_norm[0], e_w_in[0], a_q_gain[0], a_k_gain[0], lp, tm)
    lam_init = 0.8 - 0.6 * math.exp(-0.3 * 0)
    lam = (jnp.exp(jnp.sum(a_lam_q1[0].astype(F32) * a_lam_k1[0].astype(F32)))
           - jnp.exp(jnp.sum(a_lam_q2[0].astype(F32) * a_lam_k2[0].astype(F32))) + lam_init)
    o_a = _diff_attention(q, k, v, a_sub_gain[0], lam, lam_init, batch, lp, ROW_TILE)
    o_b = _gated_deltanet(bqkv, bz, gp, b_conv[0], b_a_log[0], b_dt_bias[0], b_out_gain[0], lp)
    h = _even_out(h, o_a, o_b, e_w_out[0], ffn_norm[0], ffn_w1[0], ffn_w3[0], ffn_w2[0], lp, seq, tm)

    lb_all = jnp.cumsum(jax.nn.softmax(lb_logits.astype(F32), axis=0), axis=0)
    lb = (lb_all - lb_all[0])[1]
    cq, ck, clf, ci, cz = _odd_in(h, mix_norm[1], o_w_in[0], lb, lp, seq, tm)
    o_c = _hgrn2(cq, ck, clf, ci, cz, c_out_gain[0], lp)
    h3, xn, route = _odd_out(h, o_c, o_w_out[0], ffn_norm[1], router[0], lp, seq, tm)
    row_tok, row_dst, tile_e, n_used, n_tiles = _moe_plan(route, MOE_TILE)
    y = _moe_ffn(xn, moe_w1[0].astype(BF16), moe_w3[0].astype(BF16), moe_w2[0].astype(BF16),
                 row_tok, row_dst, tile_e, n_used, n_tiles, MOE_TILE)
    return _moe_combine(h3, y, route, batch, seq, lp, ROW_TILE)
```

```python
import functools
import math

import jax
import jax.numpy as jnp
from jax import lax
from jax.experimental import pallas as pl
from jax.experimental.pallas import tpu as pltpu

D_MODEL = 1024
CHUNK = 64
N_META = 16
FRONT = CHUNK - N_META
EPS = 1e-6
ROPE_THETA = 500000.0
LOG2E = math.log2(math.e)

A_HEADS = 4
A_DK = 64
A_DV = 128
A_ROT = A_DK // 4
A_W = A_HEADS * A_DV

B_HEADS = 4
B_DK = 128
B_W = B_HEADS * B_DK
CONV_K = 4

C_HEADS = 8
C_DK = 128

D_FF = 2816
N_EXPERTS = 8
D_FF_EXPERT = 3584

LANES = 128
ROW_TILE = 256
VMEM_LIMIT = 56 * 1024 * 1024

BF16 = jnp.bfloat16
F32 = jnp.float32
HIGHEST = lax.Precision.HIGHEST


def _const_spec(shape):
    return pl.BlockSpec(shape, lambda *_: (0,) * len(shape), pipeline_mode=pl.Buffered(1))


def _rms(x, gain):
    return x * lax.rsqrt(jnp.mean(x * x, axis=-1, keepdims=True) + EPS) * gain


def _silu(x):
    return x * (1.0 / (1.0 + jnp.exp(-x)))


def _sigmoid(x):
    return 1.0 / (1.0 + jnp.exp(-x))


def _row_valid(tile_rows, lp, seq):
    r = pl.program_id(0) * tile_rows + lax.broadcasted_iota(jnp.int32, (tile_rows, 1), 0)
    p = r % lp
    return (p >= FRONT) & (p < CHUNK + seq)


def _head_norm_rope(a, ones_bd, gain, cos, sin_lo, sin_hi):
    ssq = jnp.dot((a * a).astype(BF16), ones_bd, preferred_element_type=F32)
    y = a * lax.rsqrt(ssq * (1.0 / A_DK) + EPS) * gain
    up = pltpu.roll(y, A_W - A_ROT // 2, axis=1)
    dn = pltpu.roll(y, A_ROT // 2, axis=1)
    return y * cos + up * sin_lo + dn * sin_hi


def _even_in_kernel(h_ref, g_ref, wq_ref, wk_ref, wv_ref, wb_ref, wz_ref, wg_ref, bd_ref,
                    qg_ref, kg_ref, cos_ref, slo_ref, shi_ref,
                    q_ref, k_ref, v_ref, b_ref, z_ref, gp_ref):
    xn = _rms(h_ref[...], g_ref[...]).astype(BF16)
    rep = lambda t: jnp.concatenate([t] * (A_W // LANES), axis=1)
    cos, slo, shi = rep(cos_ref[...]), rep(slo_ref[...]), rep(shi_ref[...])
    bd = bd_ref[...]
    aq = jnp.dot(xn, wq_ref[...], preferred_element_type=F32)
    q_ref[...] = (_head_norm_rope(aq, bd, qg_ref[...], cos, slo, shi) * (A_DK ** -0.5 * LOG2E)).astype(BF16)
    ak = jnp.dot(xn, wk_ref[...], preferred_element_type=F32)
    k_ref[...] = _head_norm_rope(ak, bd, kg_ref[...], cos, slo, shi).astype(BF16)
    v_ref[...] = jnp.dot(xn, wv_ref[...], preferred_element_type=F32).astype(BF16)
    b_ref[...] = jnp.dot(xn, wb_ref[...], preferred_element_type=F32).astype(BF16)
    z_ref[...] = jnp.dot(xn, wz_ref[...], preferred_element_type=F32).astype(BF16)
    gp_ref[...] = jnp.dot(xn, wg_ref[...], preferred_element_type=F32)


def _even_in(h, gain, w_in, q_gain, k_gain, lp, tm):
    rows = h.shape[0]
    wb16 = w_in.astype(BF16)
    c0, c1, c2, c3, c4 = A_W, 2 * A_W, 3 * A_W, 3 * A_W + 3 * B_W, 3 * A_W + 4 * B_W
    wq, wk, wv, wb, wz = wb16[:, :c0], wb16[:, c0:c1], wb16[:, c1:c2], wb16[:, c2:c3], wb16[:, c3:c4]
    wg = jnp.pad(wb16[:, c4:], ((0, 0), (0, LANES - 2 * B_HEADS)))
    d = jnp.arange(A_W)
    ones_bd = (d[:, None] // A_DK == d[None, :] // A_DK).astype(BF16)
    half = A_ROT // 2
    inv_freq = ROPE_THETA ** (-jnp.arange(half, dtype=F32) * 2.0 / A_ROT)
    pos = (jnp.arange(lp) - FRONT).astype(F32)
    ang = pos[:, None] * inv_freq[None, :]
    cosv, sinv = jnp.cos(ang), jnp.sin(ang)
    zeros = jnp.zeros((lp, A_DK - A_ROT), F32)
    one = jnp.ones((lp, A_DK - A_ROT), F32)
    zh = jnp.zeros((lp, half), F32)
    cos64 = jnp.concatenate([cosv, cosv, one], axis=1)
    slo64 = jnp.concatenate([-sinv, zh, zeros], axis=1)
    shi64 = jnp.concatenate([zh, sinv, zeros], axis=1)
    two = lambda t: jnp.concatenate([t, t], axis=1)
    cos_t, slo_t, shi_t = two(cos64), two(slo64), two(shi64)
    tile8 = lambda g: jnp.tile(g.astype(F32), 2 * A_HEADS)[None, :]
    n_pos = lp // tm
    row_spec = lambda w: pl.BlockSpec((tm, w), lambda i: (i, 0))
    pos_spec = pl.BlockSpec((tm, LANES), lambda i: (i % n_pos, 0))
    outs = pl.pallas_call(
        _even_in_kernel,
        grid=(rows // tm,),
        in_specs=[row_spec(D_MODEL), _const_spec((1, D_MODEL)),
                  _const_spec(wq.shape), _const_spec(wk.shape), _const_spec(wv.shape),
                  _const_spec(wb.shape), _const_spec(wz.shape), _const_spec(wg.shape),
                  _const_spec(ones_bd.shape), _const_spec((1, A_W)), _const_spec((1, A_W)),
                  pos_spec, pos_spec, pos_spec],
        out_specs=(row_spec(A_W), row_spec(A_W), row_spec(A_W), row_spec(3 * B_W), row_spec(B_W),
                   row_spec(LANES)),
        out_shape=(jax.ShapeDtypeStruct((rows, A_W), BF16), jax.ShapeDtypeStruct((rows, A_W), BF16),
                   jax.ShapeDtypeStruct((rows, A_W), BF16), jax.ShapeDtypeStruct((rows, 3 * B_W), BF16),
                   jax.ShapeDtypeStruct((rows, B_W), BF16), jax.ShapeDtypeStruct((rows, LANES), F32)),
        compiler_params=pltpu.CompilerParams(dimension_semantics=("arbitrary",),
                                             vmem_limit_bytes=VMEM_LIMIT),
        name="even_in_proj",
    )(h, gain[None, :].astype(F32), wq, wk, wv, wb, wz, wg, ones_bd, tile8(q_gain), tile8(k_gain),
      cos_t, slo_t, shi_t)
    return outs


ATTN_WIDE = 1024


ATTN_HEADS = 2


def _attn_kernel(q_ref, k_ref, v_ref, sg_ref, lam_ref, o_ref, qs_ref, vx_ref, m_ref, acc_ref,
                 sa_ref, sb_ref, *, tq, width, lam_init):
    qi = pl.program_id(2)
    heads = range(ATTN_HEADS)
    lanes = lambda s: slice(s * LANES, (s + 1) * LANES)

    @pl.when(qi == 0)
    def _():
        for s in heads:
            vx_ref[s, :, :A_DV] = v_ref[:, lanes(s)]
            vx_ref[s, :, A_DV:] = jnp.ones((vx_ref.shape[1], A_DV), BF16)

    lane = lax.broadcasted_iota(jnp.int32, (tq, LANES), 1)
    zero = jnp.zeros((tq, LANES), BF16)
    for s in heads:
        q = q_ref[:, lanes(s)]
        qs_ref[s, :tq, :] = jnp.where(lane < A_DK, q, zero)
        qs_ref[s, tq:, :] = jnp.where(lane >= A_DK, q, zero)
    m_ref[...] = jnp.full(m_ref.shape, -jnp.inf, F32)
    acc_ref[...] = jnp.zeros(acc_ref.shape, F32)

    lp = k_ref.shape[0]
    qchunk = (qi * tq + lax.broadcasted_iota(jnp.int32, (2 * tq, 1), 0) % tq) // CHUNK

    def window(t):
        return pl.multiple_of(jnp.minimum(t * width, lp - width), tq)

    def scores(s_ref, t):
        for s in heads:
            s_ref[s] = lax.dot_general(qs_ref[s], k_ref[pl.ds(window(t), width), lanes(s)],
                                       (((1,), (1,)), ((), ())), preferred_element_type=F32)

    def softmax_pv(s_ref, t, mask):
        start = window(t)
        kpos = start + lax.broadcasted_iota(jnp.int32, (1, width), 1)
        kchunk = jnp.where((kpos >= t * width) & (kpos >= FRONT), kpos // CHUNK, lp)
        nblk = width // LANES

        def block(s, c):
            cs = slice(c * LANES, (c + 1) * LANES)
            if mask == "full":
                return jnp.where(kchunk[:, cs] <= qchunk, s_ref[s, :, cs], -jnp.inf)
            if mask == "pad" and c == 0:
                return jnp.where(kpos[:, cs] >= FRONT, s_ref[s, :, cs], -jnp.inf)
            return s_ref[s, :, cs]

        for s in heads:
            mx = functools.reduce(jnp.maximum, [block(s, c) for c in range(nblk)])
            m_old = m_ref[s]
            m_new = jnp.maximum(m_old, jnp.max(mx, axis=-1, keepdims=True))
            alpha = jnp.exp2(m_old - m_new)
            p = jnp.concatenate([jnp.exp2((block(s, c) - m_new).astype(BF16)) for c in range(nblk)], axis=1)
            pv = jnp.dot(p, vx_ref[s, pl.ds(start, width), :], preferred_element_type=F32)
            acc_ref[s, :, :A_DV] = alpha * acc_ref[s, :, :A_DV] + pv[:, :A_DV]
            acc_ref[s, :, A_DV:] = alpha * acc_ref[s, :, A_DV:] + pv[:, A_DV:]
            m_ref[s] = m_new

    n_pairs = ((qi * tq + tq + width - 1) // width + 1) // 2

    def pair(j, masks):
        scores(sb_ref, 2 * j + 1)
        softmax_pv(sa_ref, 2 * j, masks[0])
        scores(sa_ref, 2 * j + 2)
        softmax_pv(sb_ref, 2 * j + 1, masks[1])

    scores(sa_ref, 0)

    @pl.when(n_pairs > 1)
    def _():
        pair(0, ("pad", "none"))

    def inner(j, carry):
        pair(j, ("none", "none"))
        return carry
    lax.fori_loop(1, n_pairs - 1, inner, 0)
    scores(sb_ref, 2 * n_pairs - 1)
    softmax_pv(sa_ref, 2 * n_pairs - 2, "full")
    softmax_pv(sb_ref, 2 * n_pairs - 1, "full")

    for s in heads:
        o = acc_ref[s, :, :A_DV] / acc_ref[s, :, A_DV:]
        o = o[:tq] - lam_ref[...] * o[tq:]
        o_ref[:, lanes(s)] = (_rms(o, sg_ref[...]) * (1.0 - lam_init)).astype(o_ref.dtype)


def _diff_attention(q, k, v, sub_gain, lam, lam_init, batch, lp, tq):
    nq = lp // tq
    hw = ATTN_HEADS * LANES
    kv_spec = pl.BlockSpec((lp, hw), lambda b, h, i: (b, h), pipeline_mode=pl.Buffered(1))
    q_spec = pl.BlockSpec((tq, hw), lambda b, h, i: (b * nq + i, h))
    vec = pl.BlockSpec((1, LANES), lambda b, h, i: (0, 0))
    width = min(ATTN_WIDE, lp)
    return pl.pallas_call(
        functools.partial(_attn_kernel, tq=tq, width=width, lam_init=lam_init),
        grid=(batch, A_HEADS // ATTN_HEADS, nq),
        in_specs=[q_spec, kv_spec, kv_spec, vec, vec],
        out_specs=q_spec,
        out_shape=jax.ShapeDtypeStruct(q.shape, BF16),
        scratch_shapes=[pltpu.VMEM((ATTN_HEADS, 2 * tq, LANES), BF16),
                        pltpu.VMEM((ATTN_HEADS, lp, 2 * A_DV), BF16),
                        pltpu.VMEM((ATTN_HEADS, 2 * tq, LANES), F32),
                        pltpu.VMEM((ATTN_HEADS, 2 * tq, 2 * A_DV), F32),
                        pltpu.VMEM((ATTN_HEADS, 2 * tq, width), F32),
                        pltpu.VMEM((ATTN_HEADS, 2 * tq, width), F32)],
        compiler_params=pltpu.CompilerParams(dimension_semantics=("arbitrary",) * 3,
                                             vmem_limit_bytes=VMEM_LIMIT),
        name="diff_attention",
    )(q, k, v, sub_gain[None, :].astype(F32), jnp.full((1, LANES), lam, F32))


def _chunk_masks(n):
    r = lax.broadcasted_iota(jnp.int32, (n, n), 0)
    c = lax.broadcasted_iota(jnp.int32, (n, n), 1)
    same = (r // CHUNK) == (c // CHUNK)
    return r, c, same


def _bdot(a, b):
    return jnp.dot(a.astype(BF16), b.astype(BF16), preferred_element_type=F32)


def _bdot_nt(a, b):
    return lax.dot_general(a.astype(BF16), b.astype(BF16), (((1,), (1,)), ((), ())),
                           preferred_element_type=F32)


def _bdot_tn(a, b):
    return lax.dot_general(a.astype(BF16), b.astype(BF16), (((0,), (0,)), ((), ())),
                           preferred_element_type=F32)


def _gdn_kernel(x_ref, z_ref, gp_ref, cw_ref, alog_ref, dtb_ref, og_ref, o_ref,
                xbuf_ref, s_ref, *, steps_per_batch):
    n = ROW_TILE
    step = pl.program_id(0) % steps_per_batch

    @pl.when(step == 0)
    def _():
        xbuf_ref[:8, :] = jnp.zeros((8, 3 * B_W), F32)
        s_ref[...] = jnp.zeros_like(s_ref)

    xbuf_ref[8:, :] = x_ref[...].astype(F32)
    conv = cw_ref[CONV_K - 1:CONV_K, :] * xbuf_ref[8:, :]
    for j in range(CONV_K - 1):
        conv = conv + cw_ref[j:j + 1, :] * xbuf_ref[5 + j:5 + j + n, :]
    xbuf_ref[:8, :] = xbuf_ref[n:n + 8, :]
    conv = _silu(conv)

    r, c, same = _chunk_masks(n)
    incl = same & (r >= c)
    strict = same & (r > c)
    eye = (r == c).astype(F32)
    gp = gp_ref[...]
    beta_all = _sigmoid(gp)
    gpre = gp + dtb_ref[...]
    softplus = jnp.maximum(gpre, 0.0) + jnp.log(1.0 + jnp.exp(-jnp.abs(gpre)))
    row = step * n + lax.broadcasted_iota(jnp.int32, (n, 1), 0)
    g_all = jnp.where(row >= FRONT, -jnp.exp(alog_ref[...]) * softplus, 0.0)
    gc_all = jnp.dot(incl.astype(F32), g_all, precision=HIGHEST, preferred_element_type=F32)
    last = (same & (c % CHUNK == CHUNK - 1)).astype(F32)
    gl_all = jnp.dot(last, gc_all, precision=HIGHEST, preferred_element_type=F32)
    gc_rows = gc_all.T

    heads = range(B_HEADS)
    chunks = [slice(ci * CHUNK, (ci + 1) * CHUNK) for ci in range(n // CHUNK)]
    p, t, rhs, qk, qg, kdec, decay = [], [], [], [], [], [], []
    for h in heads:
        qh = conv[:, h * B_DK:(h + 1) * B_DK]
        kh = conv[:, B_W + h * B_DK:B_W + (h + 1) * B_DK]
        vh = conv[:, 2 * B_W + h * B_DK:2 * B_W + (h + 1) * B_DK]
        qh = qh * lax.rsqrt(jnp.sum(qh * qh, axis=-1, keepdims=True) + EPS) * (B_DK ** -0.5)
        kh = kh * lax.rsqrt(jnp.sum(kh * kh, axis=-1, keepdims=True) + EPS)
        beta = beta_all[:, h:h + 1]
        gcol = gc_all[:, B_HEADS + h:B_HEADS + h + 1]
        glcol = gl_all[:, B_HEADS + h:B_HEADS + h + 1]
        grow = gc_rows[B_HEADS + h:B_HEADS + h + 1, :]
        dec = jnp.exp(jnp.where(incl, gcol - grow, -jnp.inf))
        kb = kh * beta
        egc = jnp.exp(gcol)
        p0 = -(_bdot_nt(kb, kh) * jnp.where(strict, dec, 0.0))
        p.append(p0)
        t.append(eye + p0)
        rhs.append(jnp.concatenate([vh * beta, kb * egc], axis=1))
        qk.append(_bdot_nt(qh, kh) * dec)
        qg.append(qh * egc)
        kdec.append(kh * jnp.exp(glcol - gcol))
        decay.append([jnp.exp(glcol[rs.stop - 1:rs.stop, :]) for rs in chunks])
    for _ in range(5):
        for h in heads:
            p[h] = _bdot(p[h], p[h])
            t[h] = t[h] + _bdot(t[h], p[h])
    sol = [_bdot(t[h], rhs[h]) for h in heads]
    u = [s[:, :B_DK] for s in sol]
    w = [s[:, B_DK:] for s in sol]
    kw = [[_bdot_tn(kdec[h][rs], w[h][rs]) for rs in chunks] for h in heads]
    ku = [[_bdot_tn(kdec[h][rs], u[h][rs]) for rs in chunks] for h in heads]
    state = [s_ref[h] for h in heads]
    before = [[] for _ in heads]
    for ci in range(len(chunks)):
        for h in heads:
            before[h].append(state[h])
            state[h] = state[h] * decay[h][ci] - _bdot(kw[h][ci], state[h]) + ku[h][ci]
    for h in heads:
        s_ref[h] = state[h]
        vnew, inter = [], []
        for ci, rs in enumerate(chunks):
            ws = _bdot(jnp.concatenate([w[h][rs], qg[h][rs]], axis=0), before[h][ci])
            vnew.append(u[h][rs] - ws[:CHUNK])
            inter.append(ws[CHUNK:])
        o = jnp.concatenate(inter, axis=0) + _bdot(qk[h], jnp.concatenate(vnew, axis=0))
        sl = slice(h * B_DK, (h + 1) * B_DK)
        o_ref[:, sl] = (_rms(o, og_ref[...]) * _silu(z_ref[:, sl].astype(F32))).astype(o_ref.dtype)


def _gated_deltanet(bqkv, bz, gp, conv_w, a_log, dt_bias, out_gain, lp):
    rows = bqkv.shape[0]
    n = ROW_TILE
    head_vec = lambda p: jnp.zeros((1, LANES), F32).at[0, B_HEADS:2 * B_HEADS].set(p.astype(F32))
    row_spec = lambda w: pl.BlockSpec((n, w), lambda i: (i, 0))
    return pl.pallas_call(
        functools.partial(_gdn_kernel, steps_per_batch=lp // n),
        grid=(rows // n,),
        in_specs=[row_spec(3 * B_W), row_spec(B_W), row_spec(LANES), _const_spec((CONV_K, 3 * B_W)),
                  _const_spec((1, LANES)), _const_spec((1, LANES)), _const_spec((1, B_DK))],
        out_specs=row_spec(B_W),
        out_shape=jax.ShapeDtypeStruct((rows, B_W), BF16),
        scratch_shapes=[pltpu.VMEM((n + 8, 3 * B_W), F32), pltpu.VMEM((B_HEADS, B_DK, B_DK), F32)],
        compiler_params=pltpu.CompilerParams(dimension_semantics=("arbitrary",),
                                             vmem_limit_bytes=VMEM_LIMIT),
        name="gated_deltanet",
    )(bqkv, bz, gp, conv_w.astype(F32), head_vec(a_log), head_vec(dt_bias), out_gain[None, :].astype(F32))


FF_COLS = 256


def _even_out_kernel(h_ref, oa_ref, ob_ref, woa_ref, wob_ref, g_ref, w1_ref, w3_ref, w2_ref, o_ref,
                     act_ref, *, tm, lp, seq):
    h1 = (h_ref[...] + jnp.dot(oa_ref[...], woa_ref[...], preferred_element_type=F32)
          + jnp.dot(ob_ref[...], wob_ref[...], preferred_element_type=F32))
    xn = _rms(h1, g_ref[...]).astype(BF16)
    for c in range(D_FF // FF_COLS):
        cs = slice(c * FF_COLS, (c + 1) * FF_COLS)
        a = jnp.dot(xn, w1_ref[:, cs], preferred_element_type=F32)
        b = jnp.dot(xn, w3_ref[:, cs], preferred_element_type=F32)
        act_ref[:, cs] = (_silu(a) * b).astype(BF16)
    out = h1 + jnp.dot(act_ref[...], w2_ref[...], preferred_element_type=F32)
    o_ref[...] = jnp.where(_row_valid(tm, lp, seq), out, 0.0)


def _even_out(h, oa, ob, w_out, gain, w1, w3, w2, lp, seq, tm):
    rows = h.shape[0]
    wo = w_out.astype(BF16)
    row_spec = lambda w: pl.BlockSpec((tm, w), lambda i: (i, 0))
    return pl.pallas_call(
        functools.partial(_even_out_kernel, tm=tm, lp=lp, seq=seq),
        grid=(rows // tm,),
        in_specs=[row_spec(D_MODEL), row_spec(A_W), row_spec(B_W),
                  _const_spec((A_W, D_MODEL)), _const_spec((B_W, D_MODEL)), _const_spec((1, D_MODEL)),
                  _const_spec(w1.shape), _const_spec(w3.shape), _const_spec(w2.shape)],
        out_specs=row_spec(D_MODEL),
        out_shape=jax.ShapeDtypeStruct((rows, D_MODEL), F32),
        scratch_shapes=[pltpu.VMEM((tm, D_FF), BF16)],
        compiler_params=pltpu.CompilerParams(dimension_semantics=("arbitrary",),
                                             vmem_limit_bytes=VMEM_LIMIT),
        name="even_out_ffn",
    )(h, oa, ob, wo[:A_W], wo[A_W:], gain[None, :].astype(F32),
      w1.astype(BF16), w3.astype(BF16), w2.astype(BF16))


def _odd_in_kernel(h_ref, g_ref, wq_ref, wf_ref, wi_ref, wz_ref, lb_ref,
                   q_ref, k_ref, lf_ref, i_ref, z_ref, *, tm, lp, seq):
    xn = _rms(h_ref[...], g_ref[...]).astype(BF16)
    valid = _row_valid(tm, lp, seq)
    q_ref[...] = jnp.dot(xn, wq_ref[...], preferred_element_type=F32).astype(BF16)
    f = jnp.dot(xn, wf_ref[...], preferred_element_type=F32)
    lb = lb_ref[...]
    fg = lb + (1.0 - lb) * _sigmoid(f)
    k_ref[...] = jnp.where(valid, 1.0 - fg, 0.0).astype(BF16)
    lf_ref[...] = jnp.where(valid, jnp.log(fg), 0.0)
    i_ref[...] = jnp.dot(xn, wi_ref[...], preferred_element_type=F32).astype(BF16)
    z_ref[...] = jnp.dot(xn, wz_ref[...], preferred_element_type=F32).astype(BF16)


def _odd_in(h, gain, w_in, lb, lp, seq, tm):
    rows = h.shape[0]
    wb16 = w_in.astype(BF16)
    ws = [wb16[:, j * D_MODEL:(j + 1) * D_MODEL] for j in range(4)]
    row_spec = pl.BlockSpec((tm, D_MODEL), lambda i: (i, 0))
    wspec = _const_spec((D_MODEL, D_MODEL))
    vec = _const_spec((1, D_MODEL))
    sd = lambda dt: jax.ShapeDtypeStruct((rows, D_MODEL), dt)
    return pl.pallas_call(
        functools.partial(_odd_in_kernel, tm=tm, lp=lp, seq=seq),
        grid=(rows // tm,),
        in_specs=[row_spec, vec, wspec, wspec, wspec, wspec, vec],
        out_specs=(row_spec,) * 5,
        out_shape=(sd(BF16), sd(BF16), sd(F32), sd(BF16), sd(BF16)),
        compiler_params=pltpu.CompilerParams(dimension_semantics=("arbitrary",),
                                             vmem_limit_bytes=VMEM_LIMIT),
        name="odd_in_proj",
    )(h, gain[None, :].astype(F32), *ws, lb[None, :].astype(F32))


SUB = 8
HALVES = (32, 16, 8)


def _hgrn_kernel(q_ref, k_ref, lf_ref, i_ref, z_ref, og_ref, sel_ref, o_ref, s_ref, p_ref,
                 *, steps_per_batch):
    n = ROW_TILE
    step = pl.program_id(0) % steps_per_batch

    @pl.when(step == 0)
    def _():
        s_ref[...] = jnp.zeros_like(s_ref)

    r, c, same = _chunk_masks(n)
    incl = same & (r >= c)
    gc_all = jnp.dot(incl.astype(F32), lf_ref[...], precision=HIGHEST, preferred_element_type=F32) * LOG2E
    rrow = lax.broadcasted_iota(jnp.int32, (n, 1), 0)
    inner = rrow % SUB
    level_mask = [(r // (2 * half) == c // (2 * half)).astype(F32) for half in HALVES]
    diag_mask = (r // SUB == c // SUB).astype(F32)

    def block_rows(a, size, j):
        a3 = a.reshape(n // size, size, a.shape[1])
        return jnp.broadcast_to(a3[:, j:j + 1, :], a3.shape).reshape(a.shape)

    for h in range(C_HEADS):
        sl = slice(h * C_DK, (h + 1) * C_DK)
        q = q_ref[:, sl].astype(F32)
        k = k_ref[:, sl].astype(F32)
        iv = i_ref[:, sl]
        gc = gc_all[:, sl]
        gl = block_rows(gc, CHUNK, CHUNK - 1)
        qg = q * jnp.exp2(gc)
        kdec = k * jnp.exp2(gl - gc)
        att = jnp.zeros((n, n), F32)
        for half, mask in zip(HALVES, level_mask):
            gref = block_rows(gc, 2 * half, half - 1)
            upper = (rrow % (2 * half)) >= half
            qt = q * jnp.exp2(jnp.where(upper, gc - gref, -jnp.inf))
            kt = k * jnp.exp2(jnp.where(upper, -jnp.inf, gref - gc))
            att = att + mask * _bdot_nt(qt, kt)
        for j in range(SUB):
            kj = block_rows(k, SUB, j)
            gj = block_rows(gc, SUB, j)
            p_ref[:, j * C_DK:(j + 1) * C_DK] = (
                q * kj * jnp.exp2(jnp.where(inner >= j, gc - gj, -jnp.inf))).astype(BF16)
        diag = jnp.dot(p_ref[...], sel_ref[...], preferred_element_type=F32)
        att = att + diag_mask * diag
        st = s_ref[h]
        inter = []
        for ci in range(n // CHUNK):
            rs = slice(ci * CHUNK, (ci + 1) * CHUNK)
            inter.append(_bdot_nt(qg[rs], st))
            st = st * jnp.exp2(gl[ci * CHUNK:ci * CHUNK + 1, :]) + _bdot_tn(iv[rs], kdec[rs])
        s_ref[h] = st
        o = jnp.concatenate(inter, axis=0) + _bdot(att, iv)
        o_ref[:, sl] = (_rms(o, og_ref[...]) * _silu(z_ref[:, sl].astype(F32))).astype(o_ref.dtype)


def _hgrn2(q, k, lf, iv, z, out_gain, lp):
    rows = q.shape[0]
    n = ROW_TILE
    row_spec = pl.BlockSpec((n, D_MODEL), lambda i: (i, 0))
    sel = ((jnp.arange(SUB * C_DK)[:, None] // C_DK) == (jnp.arange(n)[None, :] % SUB)).astype(BF16)
    return pl.pallas_call(
        functools.partial(_hgrn_kernel, steps_per_batch=lp // n),
        grid=(rows // n,),
        in_specs=[row_spec] * 5 + [_const_spec((1, C_DK)), _const_spec(sel.shape)],
        out_specs=row_spec,
        out_shape=jax.ShapeDtypeStruct((rows, D_MODEL), BF16),
        scratch_shapes=[pltpu.VMEM((C_HEADS, C_DK, C_DK), F32), pltpu.VMEM((n, SUB * C_DK), BF16)],
        compiler_params=pltpu.CompilerParams(dimension_semantics=("arbitrary",),
                                             vmem_limit_bytes=VMEM_LIMIT),
        name="hgrn2",
    )(q, k, lf, iv, z, out_gain[None, :].astype(F32), sel)


def _odd_out_kernel(h_ref, o_ref, wo_ref, g_ref, wr_ref, h3_ref, xn_ref, route_ref, *, tm, lp, seq):
    h3 = h_ref[...] + jnp.dot(o_ref[...], wo_ref[...], preferred_element_type=F32)
    h3 = jnp.where(_row_valid(tm, lp, seq), h3, 0.0)
    h3_ref[...] = h3
    xn = _rms(h3, g_ref[...])
    xn_ref[...] = xn
    logits = jnp.dot(xn.astype(BF16), wr_ref[...], preferred_element_type=F32)
    lane = lax.broadcasted_iota(jnp.int32, logits.shape, 1)
    logits = jnp.where(lane < N_EXPERTS, logits, -jnp.inf)
    v1 = jnp.max(logits, axis=-1, keepdims=True)
    e1 = jnp.min(jnp.where(logits == v1, lane, LANES), axis=-1, keepdims=True)
    rest = jnp.where(lane == e1, -jnp.inf, logits)
    v2 = jnp.max(rest, axis=-1, keepdims=True)
    e2 = jnp.min(jnp.where(rest == v2, lane, LANES), axis=-1, keepdims=True)
    t = jnp.exp(v2 - v1)
    g1 = 1.0 / (1.0 + t)
    g2 = t * g1
    route_ref[...] = jnp.where(lane == 0, e1.astype(F32),
                               jnp.where(lane == 1, e2.astype(F32),
                                         jnp.where(lane == 2, g1, jnp.where(lane == 3, g2, 0.0))))


def _odd_out(h, o, w_out, gain, router, lp, seq, tm):
    rows = h.shape[0]
    wr = jnp.pad(router.astype(BF16), ((0, 0), (0, LANES - N_EXPERTS)))
    row_spec = lambda w: pl.BlockSpec((tm, w), lambda i: (i, 0))
    return pl.pallas_call(
        functools.partial(_odd_out_kernel, tm=tm, lp=lp, seq=seq),
        grid=(rows // tm,),
        in_specs=[row_spec(D_MODEL), row_spec(D_MODEL), _const_spec((D_MODEL, D_MODEL)),
                  _const_spec((1, D_MODEL)), _const_spec((D_MODEL, LANES))],
        out_specs=(row_spec(D_MODEL), row_spec(D_MODEL), row_spec(LANES)),
        out_shape=(jax.ShapeDtypeStruct((rows, D_MODEL), F32), jax.ShapeDtypeStruct((rows, D_MODEL), F32),
                   jax.ShapeDtypeStruct((rows, LANES), F32)),
        compiler_params=pltpu.CompilerParams(dimension_semantics=("arbitrary",),
                                             vmem_limit_bytes=VMEM_LIMIT),
        name="odd_out_route",
    )(h, o, w_out.astype(BF16), gain[None, :].astype(F32), wr)


MOE_TILE = 512
MOE_FF_STEPS = 2


def _moe_plan(route, tile):
    n_tok = route.shape[0]
    flat_e = route[:, :2].astype(jnp.int32).reshape(-1)
    na = flat_e.shape[0]
    onehot = (flat_e[:, None] == jnp.arange(N_EXPERTS, dtype=jnp.int32)[None, :]).astype(jnp.int32)
    csum = jnp.cumsum(onehot, axis=0)
    rank = jnp.take_along_axis(csum, flat_e[:, None], axis=1)[:, 0] - 1
    counts = csum[-1]
    padded = (counts + tile - 1) // tile * tile
    pend = jnp.cumsum(padded)
    dest = (pend - padded)[flat_e] + rank
    n_tiles = -(-(na + N_EXPERTS * (tile - 1)) // tile) + 2
    n_rows = n_tiles * tile
    pair = jnp.full((n_rows,), -1, jnp.int32).at[dest].set(jnp.arange(na, dtype=jnp.int32),
                                                            unique_indices=True)
    row_tok = jnp.where(pair >= 0, pair // 2, 0)
    row = jnp.arange(n_rows, dtype=jnp.int32)
    dump = na + ((row // tile) % 2) * tile + row % tile
    row_dst = jnp.where(pair >= 0, (pair % 2) * n_tok + pair // 2, dump)
    tile_e = jnp.minimum(jnp.searchsorted(pend, jnp.arange(n_tiles, dtype=jnp.int32) * tile, side='right'),
                         N_EXPERTS - 1).astype(jnp.int32)
    n_used = (pend[-1] // tile).astype(jnp.int32).reshape(1)
    return row_tok, row_dst, tile_e, n_used, n_tiles


def _moe_ffn_kernel(te_ref, nused_ref, tok0_ref, tokn_ref, dstp_ref, x_hbm, w1_ref, w3_ref, w2_ref,
                    y_hbm, xbuf, xb16, ybuf, gsem, ssem):
    i, f = pl.program_id(0), pl.program_id(1)
    tile = xb16.shape[0]
    per_step = tile // MOE_FF_STEPS
    slot = i % 2
    n_used = nused_ref[0]
    n_out = y_hbm.shape[0] - 2 * tile

    def gather_start(tok_ref, row, dst_slot):
        pltpu.make_async_copy(x_hbm.at[pl.ds(tok_ref[0, 0, row], 1), :],
                              xbuf.at[dst_slot, pl.ds(row, 1), :], gsem.at[dst_slot]).start()

    def scatter_start(row, src_slot):
        pltpu.make_async_copy(ybuf.at[src_slot, pl.ds(row, 1), :],
                              y_hbm.at[pl.ds(dstp_ref[0, 0, row], 1), :], ssem.at[src_slot]).start()

    @pl.when((i == 0) & (f == 0))
    def _():
        ybuf[...] = jnp.zeros(ybuf.shape, F32)
        for s in range(2):
            dump = pltpu.make_async_copy(ybuf.at[s], y_hbm.at[pl.ds(n_out + s * tile, tile), :], ssem.at[s])
            dump.start()
            dump.wait()

        def issue(r, carry):
            gather_start(tok0_ref, r, 0)
            return carry
        lax.fori_loop(0, tile, issue, 0)

    @pl.when((f == 0) & (i >= 2) & (i - 2 < n_used))
    def _():
        pltpu.make_async_copy(ybuf.at[slot], y_hbm.at[pl.ds(0, tile), :], ssem.at[slot]).wait()

    @pl.when((f == 0) & (i <= n_used))
    def _():
        pltpu.make_async_copy(x_hbm.at[pl.ds(0, tile), :], xbuf.at[slot], gsem.at[slot]).wait()
        xb16[...] = xbuf[slot].astype(BF16)
        ybuf[slot] = jnp.zeros((tile, D_MODEL), F32)

    def compute(scatter_previous):
        for r in range(per_step):
            gather_start(tokn_ref, f * per_step + r, 1 - slot)
            if scatter_previous:
                scatter_start(f * per_step + r, 1 - slot)
        x = xb16[...]
        a = jnp.dot(x, w1_ref[0], preferred_element_type=F32)
        b = jnp.dot(x, w3_ref[0], preferred_element_type=F32)
        ybuf[slot] += jnp.dot((_silu(a) * b).astype(BF16), w2_ref[0], preferred_element_type=F32)

    @pl.when(i == 0)
    def _():
        compute(False)

    @pl.when((i > 0) & (i < n_used))
    def _():
        compute(True)

    @pl.when((i == n_used) & (f == 0))
    def _():
        def issue(r, carry):
            scatter_start(r, 1 - slot)
            return carry
        lax.fori_loop(0, tile, issue, 0)


def _moe_ffn(xn, w1, w3, w2, row_tok, row_dst, tile_e, n_used, n_tiles, tile):
    n_tok = xn.shape[0]
    steps = MOE_FF_STEPS
    ff = D_FF_EXPERT // steps
    fi = lambda i, f, nu: jnp.where(i < nu[0], f, steps - 1)
    tok_spec = lambda imap: pl.BlockSpec((1, 1, tile), imap, memory_space=pltpu.SMEM)
    grid_spec = pltpu.PrefetchScalarGridSpec(
        num_scalar_prefetch=2,
        grid=(n_tiles, steps),
        in_specs=[tok_spec(lambda i, f, te, nu: (0, 0, 0)),
                  tok_spec(lambda i, f, te, nu: (jnp.minimum(i + 1, n_tiles - 1), 0, 0)),
                  tok_spec(lambda i, f, te, nu: (jnp.maximum(i - 1, 0), 0, 0)),
                  pl.BlockSpec(memory_space=pl.ANY),
                  pl.BlockSpec((1, D_MODEL, ff), lambda i, f, te, nu: (te[i], 0, fi(i, f, nu))),
                  pl.BlockSpec((1, D_MODEL, ff), lambda i, f, te, nu: (te[i], 0, fi(i, f, nu))),
                  pl.BlockSpec((1, ff, D_MODEL), lambda i, f, te, nu: (te[i], fi(i, f, nu), 0))],
        out_specs=pl.BlockSpec(memory_space=pl.ANY),
        scratch_shapes=[pltpu.VMEM((2, tile, D_MODEL), F32), pltpu.VMEM((tile, D_MODEL), BF16),
                        pltpu.VMEM((2, tile, D_MODEL), F32),
                        pltpu.SemaphoreType.DMA((2,)), pltpu.SemaphoreType.DMA((2,))],
    )
    tok3 = row_tok.reshape(n_tiles, 1, tile)
    return pl.pallas_call(
        _moe_ffn_kernel,
        grid_spec=grid_spec,
        out_shape=jax.ShapeDtypeStruct((2 * n_tok + 2 * tile, D_MODEL), F32),
        compiler_params=pltpu.CompilerParams(dimension_semantics=("arbitrary", "arbitrary"),
                                             vmem_limit_bytes=VMEM_LIMIT),
        name="moe_ffn",
    )(tile_e, n_used, tok3, tok3, row_dst.reshape(n_tiles, 1, tile), xn, w1, w3, w2)


def _moe_combine_kernel(h_ref, y0_ref, y1_ref, route_ref, o_ref):
    g = route_ref[...]
    o_ref[0] = h_ref[...] + g[:, 2:3] * y0_ref[...] + g[:, 3:4] * y1_ref[...]


def _moe_combine(h3, y, route, batch, seq, lp, tc):
    n_tok = h3.shape[0]
    frame_rows = lambda width, off: pl.BlockSpec(
        (pl.Element(tc), pl.Element(width)),
        lambda b, j: (pl.multiple_of(off + b * lp + CHUNK + j * tc, CHUNK), 0))
    return pl.pallas_call(
        _moe_combine_kernel,
        grid=(batch, seq // tc),
        in_specs=[frame_rows(D_MODEL, 0), frame_rows(D_MODEL, 0), frame_rows(D_MODEL, n_tok),
                  frame_rows(LANES, 0)],
        out_specs=pl.BlockSpec((1, tc, D_MODEL), lambda b, j: (b, j, 0)),
        out_shape=jax.ShapeDtypeStruct((batch, seq, D_MODEL), F32),
        compiler_params=pltpu.CompilerParams(dimension_semantics=("arbitrary", "arbitrary"),
                                             vmem_limit_bytes=VMEM_LIMIT),
        name="moe_combine",
    )(h3, y, y, route)


def _layout_rows(seq):
    return -(-(CHUNK + seq) // ROW_TILE) * ROW_TILE


def _proj_tile(lp):
    return next(t for t in (640, 512, 384, ROW_TILE) if lp % t == 0)


def kernel(x, meta_tokens, lb_logits, mix_norm, ffn_norm, e_w_in, a_q_gain, a_k_gain, a_lam_q1, a_lam_k1, a_lam_q2, a_lam_k2, a_sub_gain, b_conv, b_a_log, b_dt_bias, b_out_gain, e_w_out, ffn_w1, ffn_w3, ffn_w2, o_w_in, c_out_gain, o_w_out, router, moe_w1, moe_w3, moe_w2):
    batch, seq, _ = x.shape
    lp = _layout_rows(seq)
    rows = batch * lp
    tm = _proj_tile(lp)
    meta = jnp.broadcast_to(meta_tokens[None].astype(x.dtype), (batch, N_META, D_MODEL))
    h = jnp.concatenate([jnp.zeros((batch, FRONT, D_MODEL), x.dtype), meta, x,
                         jnp.zeros((batch, lp - CHUNK - seq, D_MODEL), x.dtype)], axis=1).reshape(rows, D_MODEL)

    q, k, v, bqkv, bz, gp = _even_in(h, mix_norm[0], e_w_in[0], a_q_gain[0], a_k_gain[0], lp, tm)
    lam_init = 0.8 - 0.6 * math.exp(-0.3 * 0)
    lam = (jnp.exp(jnp.sum(a_lam_q1[0].astype(F32) * a_lam_k1[0].astype(F32)))
           - jnp.exp(jnp.sum(a_lam_q2[0].astype(F32) * a_lam_k2[0].astype(F32))) + lam_init)
    o_a = _diff_attention(q, k, v, a_sub_gain[0], lam, lam_init, batch, lp, ROW_TILE)
    o_b = _gated_deltanet(bqkv, bz, gp, b_conv[0], b_a_log[0], b_dt_bias[0], b_out_gain[0], lp)
    h = _even_out(h, o_a, o_b, e_w_out[0], ffn_norm[0], ffn_w1[0], ffn_w3[0], ffn_w2[0], lp, seq, tm)

    lb_all = jnp.cumsum(jax.nn.softmax(lb_logits.astype(F32), axis=0), axis=0)
    lb = (lb_all - lb_all[0])[1]
    cq, ck, clf, ci, cz = _odd_in(h, mix_norm[1], o_w_in[0], lb, lp, seq, tm)
    o_c = _hgrn2(cq, ck, clf, ci, cz, c_out_gain[0], lp)
    h3, xn, route = _odd_out(h, o_c, o_w_out[0], ffn_norm[1], router[0], lp, seq, tm)
    row_tok, row_dst, tile_e, n_used, n_tiles = _moe_plan(route, MOE_TILE)
    y = _moe_ffn(xn, moe_w1[0].astype(BF16), moe_w3[0].astype(BF16), moe_w2[0].astype(BF16),
                 row_tok, row_dst, tile_e, n_used, n_tiles, MOE_TILE)
    return _moe_combine(h3, y, route, batch, seq, lp, ROW_TILE)
```

```python
import functools
import math

import jax
import jax.numpy as jnp
from jax import lax
from jax.experimental import pallas as pl
from jax.experimental.pallas import tpu as pltpu

D_MODEL = 1024
CHUNK = 64
N_META = 16
FRONT = CHUNK - N_META
EPS = 1e-6
ROPE_THETA = 500000.0
LOG2E = math.log2(math.e)

A_HEADS = 4
A_DK = 64
A_DV = 128
A_ROT = A_DK // 4
A_W = A_HEADS * A_DV

B_HEADS = 4
B_DK = 128
B_W = B_HEADS * B_DK
CONV_K = 4

C_HEADS = 8
C_DK = 128

D_FF = 2816
N_EXPERTS = 8
D_FF_EXPERT = 3584

LANES = 128
ROW_TILE = 256
VMEM_LIMIT = 56 * 1024 * 1024

BF16 = jnp.bfloat16
F32 = jnp.float32


def _const_spec(shape):
    return pl.BlockSpec(shape, lambda *_: (0,) * len(shape), pipeline_mode=pl.Buffered(1))


def _rms(x, gain):
    return x * lax.rsqrt(jnp.mean(x * x, axis=-1, keepdims=True) + EPS) * gain


def _silu(x):
    return x * (1.0 / (1.0 + jnp.exp(-x)))


def _sigmoid(x):
    return 1.0 / (1.0 + jnp.exp(-x))


def _row_valid(tile_rows, lp, seq):
    r = pl.program_id(0) * tile_rows + lax.broadcasted_iota(jnp.int32, (tile_rows, 1), 0)
    p = r % lp
    return (p >= FRONT) & (p < CHUNK + seq)


def _head_norm_rope(a, ones_bd, gain, cos, sin_lo, sin_hi):
    ssq = jnp.dot((a * a).astype(BF16), ones_bd, preferred_element_type=F32)
    y = a * lax.rsqrt(ssq * (1.0 / A_DK) + EPS) * gain
    up = pltpu.roll(y, A_W - A_ROT // 2, axis=1)
    dn = pltpu.roll(y, A_ROT // 2, axis=1)
    return y * cos + up * sin_lo + dn * sin_hi


def _even_in_kernel(h_ref, g_ref, wq_ref, wk_ref, wv_ref, wb_ref, wz_ref, wg_ref, bd_ref,
                    qg_ref, kg_ref, cos_ref, slo_ref, shi_ref,
                    q_ref, k_ref, v_ref, b_ref, z_ref, gp_ref):
    xn = _rms(h_ref[...], g_ref[...]).astype(BF16)
    rep = lambda t: jnp.concatenate([t] * (A_W // LANES), axis=1)
    cos, slo, shi = rep(cos_ref[...]), rep(slo_ref[...]), rep(shi_ref[...])
    bd = bd_ref[...]
    aq = jnp.dot(xn, wq_ref[...], preferred_element_type=F32)
    q_ref[...] = (_head_norm_rope(aq, bd, qg_ref[...], cos, slo, shi) * (A_DK ** -0.5 * LOG2E)).astype(BF16)
    ak = jnp.dot(xn, wk_ref[...], preferred_element_type=F32)
    k_ref[...] = _head_norm_rope(ak, bd, kg_ref[...], cos, slo, shi).astype(BF16)
    v_ref[...] = jnp.dot(xn, wv_ref[...], preferred_element_type=F32).astype(BF16)
    b_ref[...] = jnp.dot(xn, wb_ref[...], preferred_element_type=F32).astype(BF16)
    z_ref[...] = jnp.dot(xn, wz_ref[...], preferred_element_type=F32).astype(BF16)
    gp_ref[...] = jnp.dot(xn, wg_ref[...], preferred_element_type=F32)


def _even_in(h, gain, w_in, q_gain, k_gain, lp, tm):
    rows = h.shape[0]
    wb16 = w_in.astype(BF16)
    c0, c1, c2, c3, c4 = A_W, 2 * A_W, 3 * A_W, 3 * A_W + 3 * B_W, 3 * A_W + 4 * B_W
    wq, wk, wv, wb, wz = wb16[:, :c0], wb16[:, c0:c1], wb16[:, c1:c2], wb16[:, c2:c3], wb16[:, c3:c4]
    wg = jnp.pad(wb16[:, c4:], ((0, 0), (0, LANES - 2 * B_HEADS)))
    d = jnp.arange(A_W)
    ones_bd = (d[:, None] // A_DK == d[None, :] // A_DK).astype(BF16)
    half = A_ROT // 2
    inv_freq = ROPE_THETA ** (-jnp.arange(half, dtype=F32) * 2.0 / A_ROT)
    pos = (jnp.arange(lp) - FRONT).astype(F32)
    ang = pos[:, None] * inv_freq[None, :]
    cosv, sinv = jnp.cos(ang), jnp.sin(ang)
    zeros = jnp.zeros((lp, A_DK - A_ROT), F32)
    one = jnp.ones((lp, A_DK - A_ROT), F32)
    zh = jnp.zeros((lp, half), F32)
    cos64 = jnp.concatenate([cosv, cosv, one], axis=1)
    slo64 = jnp.concatenate([-sinv, zh, zeros], axis=1)
    shi64 = jnp.concatenate([zh, sinv, zeros], axis=1)
    two = lambda t: jnp.concatenate([t, t], axis=1)
    cos_t, slo_t, shi_t = two(cos64), two(slo64), two(shi64)
    tile8 = lambda g: jnp.tile(g.astype(F32), 2 * A_HEADS)[None, :]
    n_pos = lp // tm
    row_spec = lambda w: pl.BlockSpec((tm, w), lambda i: (i, 0))
    pos_spec = pl.BlockSpec((tm, LANES), lambda i: (i % n_pos, 0))
    outs = pl.pallas_call(
        _even_in_kernel,
        grid=(rows // tm,),
        in_specs=[row_spec(D_MODEL), _const_spec((1, D_MODEL)),
                  _const_spec(wq.shape), _const_spec(wk.shape), _const_spec(wv.shape),
                  _const_spec(wb.shape), _const_spec(wz.shape), _const_spec(wg.shape),
                  _const_spec(ones_bd.shape), _const_spec((1, A_W)), _const_spec((1, A_W)),
                  pos_spec, pos_spec, pos_spec],
        out_specs=(row_spec(A_W), row_spec(A_W), row_spec(A_W), row_spec(3 * B_W), row_spec(B_W),
                   row_spec(LANES)),
        out_shape=(jax.ShapeDtypeStruct((rows, A_W), BF16), jax.ShapeDtypeStruct((rows, A_W), BF16),
                   jax.ShapeDtypeStruct((rows, A_W), BF16), jax.ShapeDtypeStruct((rows, 3 * B_W), BF16),
                   jax.ShapeDtypeStruct((rows, B_W), BF16), jax.ShapeDtypeStruct((rows, LANES), F32)),
        compiler_params=pltpu.CompilerParams(dimension_semantics=("arbitrary",),
                                             vmem_limit_bytes=VMEM_LIMIT),
        name="even_in_proj",
    )(h, gain[None, :].astype(F32), wq, wk, wv, wb, wz, wg, ones_bd, tile8(q_gain), tile8(k_gain),
      cos_t, slo_t, shi_t)
    return outs


ATTN_WIDE = 1024


ATTN_HEADS = 2


def _attn_kernel(q_ref, k_ref, v_ref, sg_ref, lam_ref, o_ref, qs_ref, vx_ref, m_ref, acc_ref,
                 sa_ref, sb_ref, *, tq, width, lam_init):
    qi = pl.program_id(2)
    heads = range(ATTN_HEADS)
    lanes = lambda s: slice(s * LANES, (s + 1) * LANES)

    @pl.when(qi == 0)
    def _():
        for s in heads:
            vx_ref[s, :, :A_DV] = v_ref[:, lanes(s)]
            vx_ref[s, :, A_DV:] = jnp.ones((vx_ref.shape[1], A_DV), BF16)

    lane = lax.broadcasted_iota(jnp.int32, (tq, LANES), 1)
    zero = jnp.zeros((tq, LANES), BF16)
    for s in heads:
        q = q_ref[:, lanes(s)]
        qs_ref[s, :tq, :] = jnp.where(lane < A_DK, q, zero)
        qs_ref[s, tq:, :] = jnp.where(lane >= A_DK, q, zero)
    m_ref[...] = jnp.full(m_ref.shape, -jnp.inf, F32)
    acc_ref[...] = jnp.zeros(acc_ref.shape, F32)

    lp = k_ref.shape[0]
    qchunk = (qi * tq + lax.broadcasted_iota(jnp.int32, (2 * tq, 1), 0) % tq) // CHUNK

    def window(t):
        return pl.multiple_of(jnp.minimum(t * width, lp - width), tq)

    def scores(s_ref, t):
        for s in heads:
            s_ref[s] = lax.dot_general(qs_ref[s], k_ref[pl.ds(window(t), width), lanes(s)],
                                       (((1,), (1,)), ((), ())), preferred_element_type=F32)

    def softmax_pv(s_ref, t, mask):
        start = window(t)
        kpos = start + lax.broadcasted_iota(jnp.int32, (1, width), 1)
        kchunk = jnp.where((kpos >= t * width) & (kpos >= FRONT), kpos // CHUNK, lp)
        nblk = width // LANES

        def block(s, c):
            cs = slice(c * LANES, (c + 1) * LANES)
            if mask == "full":
                return jnp.where(kchunk[:, cs] <= qchunk, s_ref[s, :, cs], -jnp.inf)
            if mask == "pad" and c == 0:
                return jnp.where(kpos[:, cs] >= FRONT, s_ref[s, :, cs], -jnp.inf)
            return s_ref[s, :, cs]

        for s in heads:
            mx = functools.reduce(jnp.maximum, [block(s, c) for c in range(nblk)])
            m_old = m_ref[s]
            m_new = jnp.maximum(m_old, jnp.max(mx, axis=-1, keepdims=True))
            alpha = jnp.exp2(m_old - m_new)
            p = jnp.concatenate([jnp.exp2((block(s, c) - m_new).astype(BF16)) for c in range(nblk)], axis=1)
            pv = jnp.dot(p, vx_ref[s, pl.ds(start, width), :], preferred_element_type=F32)
            acc_ref[s, :, :A_DV] = alpha * acc_ref[s, :, :A_DV] + pv[:, :A_DV]
            acc_ref[s, :, A_DV:] = alpha * acc_ref[s, :, A_DV:] + pv[:, A_DV:]
            m_ref[s] = m_new

    n_tiles = (qi * tq + tq + width - 1) // width
    n_pairs = (n_tiles + 1) // 2

    def pair(j, masks):
        scores(sb_ref, 2 * j + 1)
        softmax_pv(sa_ref, 2 * j, masks[0])
        scores(sa_ref, 2 * j + 2)
        softmax_pv(sb_ref, 2 * j + 1, masks[1])

    scores(sa_ref, 0)

    @pl.when(n_pairs > 1)
    def _():
        pair(0, ("pad", "none"))

    def inner(j, carry):
        pair(j, ("none", "none"))
        return carry
    lax.fori_loop(1, n_pairs - 1, inner, 0)

    @pl.when(n_tiles == 2 * n_pairs)
    def _():
        scores(sb_ref, 2 * n_pairs - 1)
        softmax_pv(sa_ref, 2 * n_pairs - 2, "full")
        softmax_pv(sb_ref, 2 * n_pairs - 1, "full")

    @pl.when(n_tiles < 2 * n_pairs)
    def _():
        softmax_pv(sa_ref, 2 * n_pairs - 2, "full")

    for s in heads:
        o = acc_ref[s, :, :A_DV] / acc_ref[s, :, A_DV:]
        o = o[:tq] - lam_ref[...] * o[tq:]
        o_ref[:, lanes(s)] = (_rms(o, sg_ref[...]) * (1.0 - lam_init)).astype(o_ref.dtype)


def _diff_attention(q, k, v, sub_gain, lam, lam_init, batch, lp, tq):
    nq = lp // tq
    hw = ATTN_HEADS * LANES
    kv_spec = pl.BlockSpec((lp, hw), lambda b, h, i: (b, h), pipeline_mode=pl.Buffered(1))
    q_spec = pl.BlockSpec((tq, hw), lambda b, h, i: (b * nq + i, h))
    vec = pl.BlockSpec((1, LANES), lambda b, h, i: (0, 0))
    width = min(ATTN_WIDE, lp)
    return pl.pallas_call(
        functools.partial(_attn_kernel, tq=tq, width=width, lam_init=lam_init),
        grid=(batch, A_HEADS // ATTN_HEADS, nq),
        in_specs=[q_spec, kv_spec, kv_spec, vec, vec],
        out_specs=q_spec,
        out_shape=jax.ShapeDtypeStruct(q.shape, BF16),
        scratch_shapes=[pltpu.VMEM((ATTN_HEADS, 2 * tq, LANES), BF16),
                        pltpu.VMEM((ATTN_HEADS, lp, 2 * A_DV), BF16),
                        pltpu.VMEM((ATTN_HEADS, 2 * tq, LANES), F32),
                        pltpu.VMEM((ATTN_HEADS, 2 * tq, 2 * A_DV), F32),
                        pltpu.VMEM((ATTN_HEADS, 2 * tq, width), F32),
                        pltpu.VMEM((ATTN_HEADS, 2 * tq, width), F32)],
        compiler_params=pltpu.CompilerParams(dimension_semantics=("arbitrary",) * 3,
                                             vmem_limit_bytes=VMEM_LIMIT),
        name="diff_attention",
    )(q, k, v, sub_gain[None, :].astype(F32), jnp.full((1, LANES), lam, F32))


def _chunk_masks(n):
    r = lax.broadcasted_iota(jnp.int32, (n, n), 0)
    c = lax.broadcasted_iota(jnp.int32, (n, n), 1)
    same = (r // CHUNK) == (c // CHUNK)
    return r, c, same


def _mask_dot(mask, x):
    hi = x.astype(BF16)
    rest = x - hi.astype(F32)
    mid = rest.astype(BF16)
    lo = (rest - mid.astype(F32)).astype(BF16)
    m = mask.astype(BF16)
    return jnp.dot(jnp.concatenate([m, m, m], axis=1), jnp.concatenate([hi, mid, lo], axis=0),
                   preferred_element_type=F32)


def _bdot(a, b):
    return jnp.dot(a.astype(BF16), b.astype(BF16), preferred_element_type=F32)


def _bdot_nt(a, b):
    return lax.dot_general(a.astype(BF16), b.astype(BF16), (((1,), (1,)), ((), ())),
                           preferred_element_type=F32)


def _bdot_tn(a, b):
    return lax.dot_general(a.astype(BF16), b.astype(BF16), (((0,), (0,)), ((), ())),
                           preferred_element_type=F32)


def _gdn_kernel(x_ref, z_ref, gp_ref, cw_ref, alog_ref, dtb_ref, og_ref, o_ref,
                xbuf_ref, s_ref, *, steps_per_batch):
    n = ROW_TILE
    step = pl.program_id(0) % steps_per_batch

    @pl.when(step == 0)
    def _():
        xbuf_ref[:8, :] = jnp.zeros((8, 3 * B_W), F32)
        s_ref[...] = jnp.zeros_like(s_ref)

    xbuf_ref[8:, :] = x_ref[...].astype(F32)
    conv = cw_ref[CONV_K - 1:CONV_K, :] * xbuf_ref[8:, :]
    for j in range(CONV_K - 1):
        conv = conv + cw_ref[j:j + 1, :] * xbuf_ref[5 + j:5 + j + n, :]
    xbuf_ref[:8, :] = xbuf_ref[n:n + 8, :]
    conv = _silu(conv)

    r, c, same = _chunk_masks(n)
    incl = same & (r >= c)
    strict = same & (r > c)
    eye = (r == c).astype(F32)
    gp = gp_ref[...]
    beta_all = _sigmoid(gp)
    gpre = gp + dtb_ref[...]
    softplus = jnp.maximum(gpre, 0.0) + jnp.log(1.0 + jnp.exp(-jnp.abs(gpre)))
    row = step * n + lax.broadcasted_iota(jnp.int32, (n, 1), 0)
    g_all = jnp.where(row >= FRONT, -jnp.exp(alog_ref[...]) * softplus, 0.0)
    sums = _mask_dot(jnp.concatenate([incl, same], axis=0), g_all)
    gc_all, gl_all = sums[:n], sums[n:]
    gc_rows = gc_all.T

    heads = range(B_HEADS)
    chunks = [slice(ci * CHUNK, (ci + 1) * CHUNK) for ci in range(n // CHUNK)]
    p, t, rhs, qk, qg, kdec, decay = [], [], [], [], [], [], []
    for h in heads:
        qh = conv[:, h * B_DK:(h + 1) * B_DK]
        kh = conv[:, B_W + h * B_DK:B_W + (h + 1) * B_DK]
        vh = conv[:, 2 * B_W + h * B_DK:2 * B_W + (h + 1) * B_DK]
        qh = qh * lax.rsqrt(jnp.sum(qh * qh, axis=-1, keepdims=True) + EPS) * (B_DK ** -0.5)
        kh = kh * lax.rsqrt(jnp.sum(kh * kh, axis=-1, keepdims=True) + EPS)
        beta = beta_all[:, h:h + 1]
        gcol = gc_all[:, B_HEADS + h:B_HEADS + h + 1]
        glcol = gl_all[:, B_HEADS + h:B_HEADS + h + 1]
        grow = gc_rows[B_HEADS + h:B_HEADS + h + 1, :]
        dec = jnp.exp(jnp.where(incl, gcol - grow, -jnp.inf))
        kb = kh * beta
        egc = jnp.exp(gcol)
        p0 = -(_bdot_nt(kb, kh) * jnp.where(strict, dec, 0.0))
        p.append(p0)
        t.append(eye + p0)
        rhs.append(jnp.concatenate([vh * beta, kb * egc], axis=1))
        qk.append(_bdot_nt(qh, kh) * dec)
        qg.append(qh * egc)
        kdec.append(kh * jnp.exp(glcol - gcol))
        decay.append([jnp.exp(glcol[rs.stop - 1:rs.stop, :]) for rs in chunks])
    for _ in range(5):
        for h in heads:
            p[h] = _bdot(p[h], p[h])
            t[h] = t[h] + _bdot(t[h], p[h])
    sol = [_bdot(t[h], rhs[h]) for h in heads]
    u = [s[:, :B_DK] for s in sol]
    w = [s[:, B_DK:] for s in sol]
    kw = [[_bdot_tn(kdec[h][rs], w[h][rs]) for rs in chunks] for h in heads]
    ku = [[_bdot_tn(kdec[h][rs], u[h][rs]) for rs in chunks] for h in heads]
    state = [s_ref[h] for h in heads]
    before = [[] for _ in heads]
    for ci in range(len(chunks)):
        for h in heads:
            before[h].append(state[h])
            state[h] = state[h] * decay[h][ci] - _bdot(kw[h][ci], state[h]) + ku[h][ci]
    for h in heads:
        s_ref[h] = state[h]
        vnew, inter = [], []
        for ci, rs in enumerate(chunks):
            ws = _bdot(jnp.concatenate([w[h][rs], qg[h][rs]], axis=0), before[h][ci])
            vnew.append(u[h][rs] - ws[:CHUNK])
            inter.append(ws[CHUNK:])
        o = jnp.concatenate(inter, axis=0) + _bdot(qk[h], jnp.concatenate(vnew, axis=0))
        sl = slice(h * B_DK, (h + 1) * B_DK)
        o_ref[:, sl] = (_rms(o, og_ref[...]) * _silu(z_ref[:, sl].astype(F32))).astype(o_ref.dtype)


def _gated_deltanet(bqkv, bz, gp, conv_w, a_log, dt_bias, out_gain, lp):
    rows = bqkv.shape[0]
    n = ROW_TILE
    head_vec = lambda p: jnp.zeros((1, LANES), F32).at[0, B_HEADS:2 * B_HEADS].set(p.astype(F32))
    row_spec = lambda w: pl.BlockSpec((n, w), lambda i: (i, 0))
    return pl.pallas_call(
        functools.partial(_gdn_kernel, steps_per_batch=lp // n),
        grid=(rows // n,),
        in_specs=[row_spec(3 * B_W), row_spec(B_W), row_spec(LANES), _const_spec((CONV_K, 3 * B_W)),
                  _const_spec((1, LANES)), _const_spec((1, LANES)), _const_spec((1, B_DK))],
        out_specs=row_spec(B_W),
        out_shape=jax.ShapeDtypeStruct((rows, B_W), BF16),
        scratch_shapes=[pltpu.VMEM((n + 8, 3 * B_W), F32), pltpu.VMEM((B_HEADS, B_DK, B_DK), F32)],
        compiler_params=pltpu.CompilerParams(dimension_semantics=("arbitrary",),
                                             vmem_limit_bytes=VMEM_LIMIT),
        name="gated_deltanet",
    )(bqkv, bz, gp, conv_w.astype(F32), head_vec(a_log), head_vec(dt_bias), out_gain[None, :].astype(F32))


FF_COLS = 256


def _even_out_kernel(h_ref, oa_ref, ob_ref, woa_ref, wob_ref, g_ref, w1_ref, w3_ref, w2_ref, o_ref,
                     act_ref, *, tm, lp, seq):
    h1 = (h_ref[...] + jnp.dot(oa_ref[...], woa_ref[...], preferred_element_type=F32)
          + jnp.dot(ob_ref[...], wob_ref[...], preferred_element_type=F32))
    xn = _rms(h1, g_ref[...]).astype(BF16)
    for c in range(D_FF // FF_COLS):
        cs = slice(c * FF_COLS, (c + 1) * FF_COLS)
        a = jnp.dot(xn, w1_ref[:, cs], preferred_element_type=F32)
        b = jnp.dot(xn, w3_ref[:, cs], preferred_element_type=F32)
        act_ref[:, cs] = (_silu(a) * b).astype(BF16)
    out = h1 + jnp.dot(act_ref[...], w2_ref[...], preferred_element_type=F32)
    o_ref[...] = jnp.where(_row_valid(tm, lp, seq), out, 0.0)


def _even_out(h, oa, ob, w_out, gain, w1, w3, w2, lp, seq, tm):
    rows = h.shape[0]
    wo = w_out.astype(BF16)
    row_spec = lambda w: pl.BlockSpec((tm, w), lambda i: (i, 0))
    return pl.pallas_call(
        functools.partial(_even_out_kernel, tm=tm, lp=lp, seq=seq),
        grid=(rows // tm,),
        in_specs=[row_spec(D_MODEL), row_spec(A_W), row_spec(B_W),
                  _const_spec((A_W, D_MODEL)), _const_spec((B_W, D_MODEL)), _const_spec((1, D_MODEL)),
                  _const_spec(w1.shape), _const_spec(w3.shape), _const_spec(w2.shape)],
        out_specs=row_spec(D_MODEL),
        out_shape=jax.ShapeDtypeStruct((rows, D_MODEL), F32),
        scratch_shapes=[pltpu.VMEM((tm, D_FF), BF16)],
        compiler_params=pltpu.CompilerParams(dimension_semantics=("arbitrary",),
                                             vmem_limit_bytes=VMEM_LIMIT),
        name="even_out_ffn",
    )(h, oa, ob, wo[:A_W], wo[A_W:], gain[None, :].astype(F32),
      w1.astype(BF16), w3.astype(BF16), w2.astype(BF16))


def _odd_in_kernel(h_ref, g_ref, wq_ref, wf_ref, wi_ref, wz_ref, lb_ref,
                   q_ref, k_ref, lf_ref, i_ref, z_ref, *, tm, lp, seq):
    xn = _rms(h_ref[...], g_ref[...]).astype(BF16)
    valid = _row_valid(tm, lp, seq)
    q_ref[...] = jnp.dot(xn, wq_ref[...], preferred_element_type=F32).astype(BF16)
    f = jnp.dot(xn, wf_ref[...], preferred_element_type=F32)
    lb = lb_ref[...]
    fg = lb + (1.0 - lb) * _sigmoid(f)
    k_ref[...] = jnp.where(valid, 1.0 - fg, 0.0).astype(BF16)
    lf_ref[...] = jnp.where(valid, jnp.log(fg), 0.0)
    i_ref[...] = jnp.dot(xn, wi_ref[...], preferred_element_type=F32).astype(BF16)
    z_ref[...] = jnp.dot(xn, wz_ref[...], preferred_element_type=F32).astype(BF16)


def _odd_in(h, gain, w_in, lb, lp, seq, tm):
    rows = h.shape[0]
    wb16 = w_in.astype(BF16)
    ws = [wb16[:, j * D_MODEL:(j + 1) * D_MODEL] for j in range(4)]
    row_spec = pl.BlockSpec((tm, D_MODEL), lambda i: (i, 0))
    wspec = _const_spec((D_MODEL, D_MODEL))
    vec = _const_spec((1, D_MODEL))
    sd = lambda dt: jax.ShapeDtypeStruct((rows, D_MODEL), dt)
    return pl.pallas_call(
        functools.partial(_odd_in_kernel, tm=tm, lp=lp, seq=seq),
        grid=(rows // tm,),
        in_specs=[row_spec, vec, wspec, wspec, wspec, wspec, vec],
        out_specs=(row_spec,) * 5,
        out_shape=(sd(BF16), sd(BF16), sd(F32), sd(BF16), sd(BF16)),
        compiler_params=pltpu.CompilerParams(dimension_semantics=("arbitrary",),
                                             vmem_limit_bytes=VMEM_LIMIT),
        name="odd_in_proj",
    )(h, gain[None, :].astype(F32), *ws, lb[None, :].astype(F32))


SUB = 8
HALVES = (32, 16, 8)


def _hgrn_kernel(q_ref, k_ref, lf_ref, i_ref, z_ref, og_ref, sel_ref, o_ref, s_ref, p_ref, kf_ref, gf_ref,
                 *, steps_per_batch):
    n = ROW_TILE
    step = pl.program_id(0) % steps_per_batch

    @pl.when(step == 0)
    def _():
        s_ref[...] = jnp.zeros_like(s_ref)

    r, c, same = _chunk_masks(n)
    incl = same & (r >= c)
    gc_all = _mask_dot(incl, lf_ref[...]) * LOG2E
    rrow = lax.broadcasted_iota(jnp.int32, (n, 1), 0)
    inner = rrow % SUB
    level_mask = [(r // (2 * half) == c // (2 * half)).astype(F32) for half in HALVES]
    diag_mask = (r // SUB == c // SUB).astype(F32)

    kf_ref[...] = k_ref[...].astype(F32)
    gf_ref[...] = gc_all

    def block_rows(ref, sl, size, j):
        return jnp.concatenate([jnp.broadcast_to(ref[pl.ds(b * size + j, 1), sl], (size, C_DK))
                                for b in range(n // size)], axis=0)

    def products(h):
        sl = slice(h * C_DK, (h + 1) * C_DK)
        q = q_ref[:, sl].astype(F32)
        k = kf_ref[:, sl]
        gc = gc_all[:, sl]
        gl = block_rows(gf_ref, sl, CHUNK, CHUNK - 1)
        qg = q * jnp.exp2(gc)
        kdec = k * jnp.exp2(gl - gc)
        parts = []
        for half in HALVES:
            gref = block_rows(gf_ref, sl, 2 * half, half - 1)
            upper = (rrow % (2 * half)) >= half
            qt = q * jnp.exp2(jnp.where(upper, gc - gref, -jnp.inf))
            kt = k * jnp.exp2(jnp.where(upper, -jnp.inf, gref - gc))
            parts.append(_bdot_nt(qt, kt))
        pbuf = p_ref.at[h % 2]
        for j in range(SUB):
            kj = block_rows(kf_ref, sl, SUB, j)
            gj = block_rows(gf_ref, sl, SUB, j)
            pbuf[:, j * C_DK:(j + 1) * C_DK] = (
                q * kj * jnp.exp2(jnp.where(inner >= j, gc - gj, -jnp.inf))).astype(BF16)
        diag = jnp.dot(pbuf[...], sel_ref[...], preferred_element_type=F32)
        return parts, diag, qg, kdec, gl

    def finish(h, parts, diag, qg, kdec, gl):
        sl = slice(h * C_DK, (h + 1) * C_DK)
        iv = i_ref[:, sl]
        att = diag_mask * diag
        for mask, part in zip(level_mask, parts):
            att = att + mask * part
        st = s_ref[h]
        inter = []
        for ci in range(n // CHUNK):
            rs = slice(ci * CHUNK, (ci + 1) * CHUNK)
            inter.append(_bdot_nt(qg[rs], st))
            st = st * jnp.exp2(gl[ci * CHUNK:ci * CHUNK + 1, :]) + _bdot_tn(iv[rs], kdec[rs])
        s_ref[h] = st
        o = jnp.concatenate(inter, axis=0) + _bdot(att, iv)
        o_ref[:, sl] = (_rms(o, og_ref[...]) * _silu(z_ref[:, sl].astype(F32))).astype(o_ref.dtype)

    for h in range(C_HEADS):
        finish(h, *products(h))


def _hgrn2(q, k, lf, iv, z, out_gain, lp):
    rows = q.shape[0]
    n = ROW_TILE
    row_spec = pl.BlockSpec((n, D_MODEL), lambda i: (i, 0))
    sel = ((jnp.arange(SUB * C_DK)[:, None] // C_DK) == (jnp.arange(n)[None, :] % SUB)).astype(BF16)
    return pl.pallas_call(
        functools.partial(_hgrn_kernel, steps_per_batch=lp // n),
        grid=(rows // n,),
        in_specs=[row_spec] * 5 + [_const_spec((1, C_DK)), _const_spec(sel.shape)],
        out_specs=row_spec,
        out_shape=jax.ShapeDtypeStruct((rows, D_MODEL), BF16),
        scratch_shapes=[pltpu.VMEM((C_HEADS, C_DK, C_DK), F32), pltpu.VMEM((2, n, SUB * C_DK), BF16),
                        pltpu.VMEM((n, D_MODEL), F32), pltpu.VMEM((n, D_MODEL), F32)],
        compiler_params=pltpu.CompilerParams(dimension_semantics=("arbitrary",),
                                             vmem_limit_bytes=VMEM_LIMIT),
        name="hgrn2",
    )(q, k, lf, iv, z, out_gain[None, :].astype(F32), sel)


def _odd_out_kernel(h_ref, o_ref, wo_ref, g_ref, wr_ref, h3_ref, xn_ref, route_ref, *, tm, lp, seq):
    h3 = h_ref[...] + jnp.dot(o_ref[...], wo_ref[...], preferred_element_type=F32)
    h3 = jnp.where(_row_valid(tm, lp, seq), h3, 0.0)
    h3_ref[...] = h3
    xn = _rms(h3, g_ref[...])
    xn_ref[...] = xn
    logits = jnp.dot(xn.astype(BF16), wr_ref[...], preferred_element_type=F32)
    lane = lax.broadcasted_iota(jnp.int32, logits.shape, 1)
    logits = jnp.where(lane < N_EXPERTS, logits, -jnp.inf)
    v1 = jnp.max(logits, axis=-1, keepdims=True)
    e1 = jnp.min(jnp.where(logits == v1, lane, LANES), axis=-1, keepdims=True)
    rest = jnp.where(lane == e1, -jnp.inf, logits)
    v2 = jnp.max(rest, axis=-1, keepdims=True)
    e2 = jnp.min(jnp.where(rest == v2, lane, LANES), axis=-1, keepdims=True)
    t = jnp.exp(v2 - v1)
    g1 = 1.0 / (1.0 + t)
    g2 = t * g1
    route_ref[...] = jnp.where(lane == 0, e1.astype(F32),
                               jnp.where(lane == 1, e2.astype(F32),
                                         jnp.where(lane == 2, g1, jnp.where(lane == 3, g2, 0.0))))


def _odd_out(h, o, w_out, gain, router, lp, seq, tm):
    rows = h.shape[0]
    wr = jnp.pad(router.astype(BF16), ((0, 0), (0, LANES - N_EXPERTS)))
    row_spec = lambda w: pl.BlockSpec((tm, w), lambda i: (i, 0))
    return pl.pallas_call(
        functools.partial(_odd_out_kernel, tm=tm, lp=lp, seq=seq),
        grid=(rows // tm,),
        in_specs=[row_spec(D_MODEL), row_spec(D_MODEL), _const_spec((D_MODEL, D_MODEL)),
                  _const_spec((1, D_MODEL)), _const_spec((D_MODEL, LANES))],
        out_specs=(row_spec(D_MODEL), row_spec(D_MODEL), row_spec(LANES)),
        out_shape=(jax.ShapeDtypeStruct((rows, D_MODEL), F32), jax.ShapeDtypeStruct((rows, D_MODEL), F32),
                   jax.ShapeDtypeStruct((rows, LANES), F32)),
        compiler_params=pltpu.CompilerParams(dimension_semantics=("arbitrary",),
                                             vmem_limit_bytes=VMEM_LIMIT),
        name="odd_out_route",
    )(h, o, w_out.astype(BF16), gain[None, :].astype(F32), wr)


MOE_TILE = 512
MOE_FF_STEPS = 2


def _moe_plan(route, tile):
    n_tok = route.shape[0]
    flat_e = route[:, :2].astype(jnp.int32).reshape(-1)
    na = flat_e.shape[0]
    counts = jnp.sum(flat_e[:, None] == jnp.arange(N_EXPERTS, dtype=jnp.int32)[None, :], axis=0,
                     dtype=jnp.int32)
    order = jnp.argsort(flat_e, stable=True).astype(jnp.int32)
    first = jnp.cumsum(counts) - counts
    padded = (counts + tile - 1) // tile * tile
    pend = jnp.cumsum(padded)
    n_tiles = -(-(na + N_EXPERTS * (tile - 1)) // tile) + 2
    n_rows = n_tiles * tile
    tile_e = jnp.minimum(jnp.searchsorted(pend, jnp.arange(n_tiles, dtype=jnp.int32) * tile, side='right'),
                         N_EXPERTS - 1).astype(jnp.int32)
    row = jnp.arange(n_rows, dtype=jnp.int32)
    row_e = jnp.repeat(tile_e, tile)
    local = row - (pend - padded)[row_e]
    real = local < counts[row_e]
    pair = order[jnp.minimum(first[row_e] + local, na - 1)]
    row_tok = jnp.where(real, pair // 2, 0)
    dump = na + ((row // tile) % 2) * tile + row % tile
    row_dst = jnp.where(real, (pair % 2) * n_tok + pair // 2, dump)
    n_used = (pend[-1] // tile).astype(jnp.int32).reshape(1)
    return row_tok, row_dst, tile_e, n_used, n_tiles


def _moe_ffn_kernel(te_ref, nused_ref, tok0_ref, tokn_ref, dstp_ref, x_hbm, w1_ref, w3_ref, w2_ref,
                    y_hbm, xbuf, xb16, ybuf, gsem, ssem):
    i, f = pl.program_id(0), pl.program_id(1)
    tile = xb16.shape[0]
    per_step = tile // MOE_FF_STEPS
    slot = i % 2
    n_used = nused_ref[0]
    n_out = y_hbm.shape[0] - 2 * tile

    def gather_start(tok_ref, row, dst_slot):
        pltpu.make_async_copy(x_hbm.at[pl.ds(tok_ref[0, 0, row], 1), :],
                              xbuf.at[dst_slot, pl.ds(row, 1), :], gsem.at[dst_slot]).start()

    def scatter_start(row, src_slot):
        pltpu.make_async_copy(ybuf.at[src_slot, pl.ds(row, 1), :],
                              y_hbm.at[pl.ds(dstp_ref[0, 0, row], 1), :], ssem.at[src_slot]).start()

    @pl.when((i == 0) & (f == 0))
    def _():
        ybuf[...] = jnp.zeros(ybuf.shape, F32)
        for s in range(2):
            dump = pltpu.make_async_copy(ybuf.at[s], y_hbm.at[pl.ds(n_out + s * tile, tile), :], ssem.at[s])
            dump.start()
            dump.wait()

        def issue(r, carry):
            gather_start(tok0_ref, r, 0)
            return carry
        lax.fori_loop(0, tile, issue, 0)

    @pl.when((f == 0) & (i >= 2) & (i - 2 < n_used))
    def _():
        pltpu.make_async_copy(ybuf.at[slot], y_hbm.at[pl.ds(0, tile), :], ssem.at[slot]).wait()

    @pl.when((f == 0) & (i <= n_used))
    def _():
        pltpu.make_async_copy(x_hbm.at[pl.ds(0, tile), :], xbuf.at[slot], gsem.at[slot]).wait()
        xb16[...] = xbuf[slot].astype(BF16)
        ybuf[slot] = jnp.zeros((tile, D_MODEL), F32)

    def compute(scatter_previous):
        for r in range(per_step):
            gather_start(tokn_ref, f * per_step + r, 1 - slot)
            if scatter_previous:
                scatter_start(f * per_step + r, 1 - slot)
        x = xb16[...]
        a = jnp.dot(x, w1_ref[0], preferred_element_type=F32)
        b = jnp.dot(x, w3_ref[0], preferred_element_type=F32)
        ybuf[slot] += jnp.dot((_silu(a) * b).astype(BF16), w2_ref[0], preferred_element_type=F32)

    @pl.when(i == 0)
    def _():
        compute(False)

    @pl.when((i > 0) & (i < n_used))
    def _():
        compute(True)

    @pl.when((i == n_used) & (f == 0))
    def _():
        def issue(r, carry):
            scatter_start(r, 1 - slot)
            return carry
        lax.fori_loop(0, tile, issue, 0)


def _moe_ffn(xn, w1, w3, w2, row_tok, row_dst, tile_e, n_used, n_tiles, tile):
    n_tok = xn.shape[0]
    steps = MOE_FF_STEPS
    ff = D_FF_EXPERT // steps
    fi = lambda i, f, nu: jnp.where(i < nu[0], f, steps - 1)
    tok_spec = lambda imap: pl.BlockSpec((1, 1, tile), imap, memory_space=pltpu.SMEM)
    grid_spec = pltpu.PrefetchScalarGridSpec(
        num_scalar_prefetch=2,
        grid=(n_tiles, steps),
        in_specs=[tok_spec(lambda i, f, te, nu: (0, 0, 0)),
                  tok_spec(lambda i, f, te, nu: (jnp.minimum(i + 1, n_tiles - 1), 0, 0)),
                  tok_spec(lambda i, f, te, nu: (jnp.maximum(i - 1, 0), 0, 0)),
                  pl.BlockSpec(memory_space=pl.ANY),
                  pl.BlockSpec((1, D_MODEL, ff), lambda i, f, te, nu: (te[i], 0, fi(i, f, nu))),
                  pl.BlockSpec((1, D_MODEL, ff), lambda i, f, te, nu: (te[i], 0, fi(i, f, nu))),
                  pl.BlockSpec((1, ff, D_MODEL), lambda i, f, te, nu: (te[i], fi(i, f, nu), 0))],
        out_specs=pl.BlockSpec(memory_space=pl.ANY),
        scratch_shapes=[pltpu.VMEM((2, tile, D_MODEL), F32), pltpu.VMEM((tile, D_MODEL), BF16),
                        pltpu.VMEM((2, tile, D_MODEL), F32),
                        pltpu.SemaphoreType.DMA((2,)), pltpu.SemaphoreType.DMA((2,))],
    )
    tok3 = row_tok.reshape(n_tiles, 1, tile)
    return pl.pallas_call(
        _moe_ffn_kernel,
        grid_spec=grid_spec,
        out_shape=jax.ShapeDtypeStruct((2 * n_tok + 2 * tile, D_MODEL), F32),
        compiler_params=pltpu.CompilerParams(dimension_semantics=("arbitrary", "arbitrary"),
                                             vmem_limit_bytes=VMEM_LIMIT),
        name="moe_ffn",
    )(tile_e, n_used, tok3, tok3, row_dst.reshape(n_tiles, 1, tile), xn, w1, w3, w2)


def _moe_combine_kernel(h_ref, y0_ref, y1_ref, route_ref, o_ref):
    g = route_ref[...]
    o_ref[0] = h_ref[...] + g[:, 2:3] * y0_ref[...] + g[:, 3:4] * y1_ref[...]


def _moe_combine(h3, y, route, batch, seq, lp, tc):
    n_tok = h3.shape[0]
    frame_rows = lambda width, off: pl.BlockSpec(
        (pl.Element(tc), pl.Element(width)),
        lambda b, j: (pl.multiple_of(off + b * lp + CHUNK + j * tc, CHUNK), 0))
    return pl.pallas_call(
        _moe_combine_kernel,
        grid=(batch, seq // tc),
        in_specs=[frame_rows(D_MODEL, 0), frame_rows(D_MODEL, 0), frame_rows(D_MODEL, n_tok),
                  frame_rows(LANES, 0)],
        out_specs=pl.BlockSpec((1, tc, D_MODEL), lambda b, j: (b, j, 0)),
        out_shape=jax.ShapeDtypeStruct((batch, seq, D_MODEL), F32),
        compiler_params=pltpu.CompilerParams(dimension_semantics=("arbitrary", "arbitrary"),
                                             vmem_limit_bytes=VMEM_LIMIT),
        name="moe_combine",
    )(h3, y, y, route)


def _layout_rows(seq):
    return -(-(CHUNK + seq) // ROW_TILE) * ROW_TILE


def _proj_tile(lp):
    return next(t for t in (640, 512, 384, ROW_TILE) if lp % t == 0)


def kernel(x, meta_tokens, lb_logits, mix_norm, ffn_norm, e_w_in, a_q_gain, a_k_gain, a_lam_q1, a_lam_k1, a_lam_q2, a_lam_k2, a_sub_gain, b_conv, b_a_log, b_dt_bias, b_out_gain, e_w_out, ffn_w1, ffn_w3, ffn_w2, o_w_in, c_out_gain, o_w_out, router, moe_w1, moe_w3, moe_w2):
    batch, seq, _ = x.shape
    lp = _layout_rows(seq)
    rows = batch * lp
    tm = _proj_tile(lp)
    meta = jnp.broadcast_to(meta_tokens[None].astype(x.dtype), (batch, N_META, D_MODEL))
    h = jnp.concatenate([jnp.zeros((batch, FRONT, D_MODEL), x.dtype), meta, x,
                         jnp.zeros((batch, lp - CHUNK - seq, D_MODEL), x.dtype)], axis=1).reshape(rows, D_MODEL)

    q, k, v, bqkv, bz, gp = _even_in(h, mix_norm[0], e_w_in[0], a_q_gain[0], a_k_gain[0], lp, tm)
    lam_init = 0.8 - 0.6 * math.exp(-0.3 * 0)
    lam = (jnp.exp(jnp.sum(a_lam_q1[0].astype(F32) * a_lam_k1[0].astype(F32)))
           - jnp.exp(jnp.sum(a_lam_q2[0].astype(F32) * a_lam_k2[0].astype(F32))) + lam_init)
    o_a = _diff_attention(q, k, v, a_sub_gain[0], lam, lam_init, batch, lp, ROW_TILE)
    o_b = _gated_deltanet(bqkv, bz, gp, b_conv[0], b_a_log[0], b_dt_bias[0], b_out_gain[0], lp)
    h = _even_out(h, o_a, o_b, e_w_out[0], ffn_norm[0], ffn_w1[0], ffn_w3[0], ffn_w2[0], lp, seq, tm)

    lb_all = jnp.cumsum(jax.nn.softmax(lb_logits.astype(F32), axis=0), axis=0)
    lb = (lb_all - lb_all[0])[1]
    cq, ck, clf, ci, cz = _odd_in(h, mix_norm[1], o_w_in[0], lb, lp, seq, tm)
    o_c = _hgrn2(cq, ck, clf, ci, cz, c_out_gain[0], lp)
    h3, xn, route = _odd_out(h, o_c, o_w_out[0], ffn_norm[1], router[0], lp, seq, tm)
    row_tok, row_dst, tile_e, n_used, n_tiles = _moe_plan(route, MOE_TILE)
    y = _moe_ffn(xn, moe_w1[0].astype(BF16), moe_w3[0].astype(BF16), moe_w2[0].astype(BF16),
                 row_tok, row_dst, tile_e, n_used, n_tiles, MOE_TILE)
    return _moe_combine(h3, y, route, batch, seq, lp, ROW_TILE)
```

```python
import functools
import math

import jax
import jax.numpy as jnp
from jax import lax
from jax.experimental import pallas as pl
from jax.experimental.pallas import tpu as pltpu

D_MODEL = 1024
CHUNK = 64
N_META = 16
FRONT = CHUNK - N_META
EPS = 1e-6
ROPE_THETA = 500000.0
LOG2E = math.log2(math.e)

A_HEADS = 4
A_DK = 64
A_DV = 128
A_ROT = A_DK // 4
A_W = A_HEADS * A_DV

B_HEADS = 4
B_DK = 128
B_W = B_HEADS * B_DK
CONV_K = 4

C_HEADS = 8
C_DK = 128

D_FF = 2816
N_EXPERTS = 8
D_FF_EXPERT = 3584

LANES = 128
ROW_TILE = 256
VMEM_LIMIT = 56 * 1024 * 1024

BF16 = jnp.bfloat16
F32 = jnp.float32


def _const_spec(shape):
    return pl.BlockSpec(shape, lambda *_: (0,) * len(shape), pipeline_mode=pl.Buffered(1))


def _rms(x, gain):
    return x * lax.rsqrt(jnp.mean(x * x, axis=-1, keepdims=True) + EPS) * gain


def _silu(x):
    return x * (1.0 / (1.0 + jnp.exp(-x)))


def _sigmoid(x):
    return 1.0 / (1.0 + jnp.exp(-x))


def _row_valid(tile_rows, lp, seq):
    r = pl.program_id(0) * tile_rows + lax.broadcasted_iota(jnp.int32, (tile_rows, 1), 0)
    p = r % lp
    return (p >= FRONT) & (p < CHUNK + seq)


def _head_norm_rope(a, ones_bd, gain, cos, sin_lo, sin_hi):
    ssq = jnp.dot((a * a).astype(BF16), ones_bd, preferred_element_type=F32)
    y = a * lax.rsqrt(ssq * (1.0 / A_DK) + EPS) * gain
    up = pltpu.roll(y, A_W - A_ROT // 2, axis=1)
    dn = pltpu.roll(y, A_ROT // 2, axis=1)
    return y * cos + up * sin_lo + dn * sin_hi


def _even_in_kernel(h_ref, g_ref, wq_ref, wk_ref, wv_ref, wb_ref, wz_ref, wg_ref, bd_ref,
                    qg_ref, kg_ref, cos_ref, slo_ref, shi_ref,
                    q_ref, k_ref, v_ref, b_ref, z_ref, gp_ref):
    xn = _rms(h_ref[...], g_ref[...]).astype(BF16)
    rep = lambda t: jnp.concatenate([t] * (A_W // LANES), axis=1)
    cos, slo, shi = rep(cos_ref[...]), rep(slo_ref[...]), rep(shi_ref[...])
    bd = bd_ref[...]
    aq = jnp.dot(xn, wq_ref[...], preferred_element_type=F32)
    q_ref[...] = (_head_norm_rope(aq, bd, qg_ref[...], cos, slo, shi) * (A_DK ** -0.5 * LOG2E)).astype(BF16)
    ak = jnp.dot(xn, wk_ref[...], preferred_element_type=F32)
    k_ref[...] = _head_norm_rope(ak, bd, kg_ref[...], cos, slo, shi).astype(BF16)
    v_ref[...] = jnp.dot(xn, wv_ref[...], preferred_element_type=F32).astype(BF16)
    b_ref[...] = jnp.dot(xn, wb_ref[...], preferred_element_type=F32).astype(BF16)
    z_ref[...] = jnp.dot(xn, wz_ref[...], preferred_element_type=F32).astype(BF16)
    gp_ref[...] = jnp.dot(xn, wg_ref[...], preferred_element_type=F32)


def _even_in(h, gain, w_in, q_gain, k_gain, lp, tm):
    rows = h.shape[0]
    wb16 = w_in.astype(BF16)
    c0, c1, c2, c3, c4 = A_W, 2 * A_W, 3 * A_W, 3 * A_W + 3 * B_W, 3 * A_W + 4 * B_W
    wq, wk, wv, wb, wz = wb16[:, :c0], wb16[:, c0:c1], wb16[:, c1:c2], wb16[:, c2:c3], wb16[:, c3:c4]
    wg = jnp.pad(wb16[:, c4:], ((0, 0), (0, LANES - 2 * B_HEADS)))
    d = jnp.arange(A_W)
    ones_bd = (d[:, None] // A_DK == d[None, :] // A_DK).astype(BF16)
    half = A_ROT // 2
    inv_freq = ROPE_THETA ** (-jnp.arange(half, dtype=F32) * 2.0 / A_ROT)
    pos = (jnp.arange(lp) - FRONT).astype(F32)
    dim = jnp.arange(LANES) % A_DK
    ang = pos[:, None] * inv_freq[dim % half][None, :]
    cos_t = jnp.where(dim < A_ROT, jnp.cos(ang), 1.0)
    slo_t = jnp.where(dim < half, -jnp.sin(ang), 0.0)
    shi_t = jnp.where((dim >= half) & (dim < A_ROT), jnp.sin(ang), 0.0)
    tile8 = lambda g: jnp.tile(g.astype(F32), 2 * A_HEADS)[None, :]
    n_pos = lp // tm
    row_spec = lambda w: pl.BlockSpec((tm, w), lambda i: (i, 0))
    pos_spec = pl.BlockSpec((tm, LANES), lambda i: (i % n_pos, 0))
    outs = pl.pallas_call(
        _even_in_kernel,
        grid=(rows // tm,),
        in_specs=[row_spec(D_MODEL), _const_spec((1, D_MODEL)),
                  _const_spec(wq.shape), _const_spec(wk.shape), _const_spec(wv.shape),
                  _const_spec(wb.shape), _const_spec(wz.shape), _const_spec(wg.shape),
                  _const_spec(ones_bd.shape), _const_spec((1, A_W)), _const_spec((1, A_W)),
                  pos_spec, pos_spec, pos_spec],
        out_specs=(row_spec(A_W), row_spec(A_W), row_spec(A_W), row_spec(3 * B_W), row_spec(B_W),
                   row_spec(LANES)),
        out_shape=(jax.ShapeDtypeStruct((rows, A_W), BF16), jax.ShapeDtypeStruct((rows, A_W), BF16),
                   jax.ShapeDtypeStruct((rows, A_W), BF16), jax.ShapeDtypeStruct((rows, 3 * B_W), BF16),
                   jax.ShapeDtypeStruct((rows, B_W), BF16), jax.ShapeDtypeStruct((rows, LANES), F32)),
        compiler_params=pltpu.CompilerParams(dimension_semantics=("arbitrary",),
                                             vmem_limit_bytes=VMEM_LIMIT),
        name="even_in_proj",
    )(h, gain[None, :].astype(F32), wq, wk, wv, wb, wz, wg, ones_bd, tile8(q_gain), tile8(k_gain),
      cos_t, slo_t, shi_t)
    return outs


ATTN_WIDE = 1024


ATTN_HEADS = 2


def _attn_kernel(q_ref, k_ref, v_ref, sg_ref, lam_ref, o_ref, qs_ref, vx_ref, m_ref, acc_ref,
                 sa_ref, sb_ref, *, tq, width, lam_init):
    qi = pl.program_id(2)
    heads = range(ATTN_HEADS)
    lanes = lambda s: slice(s * LANES, (s + 1) * LANES)

    @pl.when(qi == 0)
    def _():
        for s in heads:
            vx_ref[s, :, :A_DV] = v_ref[:, lanes(s)]
            vx_ref[s, :, A_DV:] = jnp.ones((vx_ref.shape[1], A_DV), BF16)

    lane = lax.broadcasted_iota(jnp.int32, (tq, LANES), 1)
    zero = jnp.zeros((tq, LANES), BF16)
    for s in heads:
        q = q_ref[:, lanes(s)]
        qs_ref[s, :tq, :] = jnp.where(lane < A_DK, q, zero)
        qs_ref[s, tq:, :] = jnp.where(lane >= A_DK, q, zero)
    m_ref[...] = jnp.full(m_ref.shape, -jnp.inf, F32)
    acc_ref[...] = jnp.zeros(acc_ref.shape, F32)

    lp = k_ref.shape[0]
    qchunk = (qi * tq + lax.broadcasted_iota(jnp.int32, (2 * tq, 1), 0) % tq) // CHUNK

    def window(t):
        return pl.multiple_of(jnp.minimum(t * width, lp - width), tq)

    def scores(s_ref, t):
        for s in heads:
            s_ref[s] = lax.dot_general(qs_ref[s], k_ref[pl.ds(window(t), width), lanes(s)],
                                       (((1,), (1,)), ((), ())), preferred_element_type=F32)

    def softmax_pv(s_ref, t, mask):
        start = window(t)
        kpos = start + lax.broadcasted_iota(jnp.int32, (1, width), 1)
        kchunk = jnp.where((kpos >= t * width) & (kpos >= FRONT), kpos // CHUNK, lp)
        nblk = width // LANES

        def block(s, c):
            cs = slice(c * LANES, (c + 1) * LANES)
            if mask == "full":
                return jnp.where(kchunk[:, cs] <= qchunk, s_ref[s, :, cs], -jnp.inf)
            if mask == "pad" and c == 0:
                return jnp.where(kpos[:, cs] >= FRONT, s_ref[s, :, cs], -jnp.inf)
            return s_ref[s, :, cs]

        for s in heads:
            mx = functools.reduce(jnp.maximum, [block(s, c) for c in range(nblk)])
            m_old = m_ref[s]
            m_new = jnp.maximum(m_old, jnp.max(mx, axis=-1, keepdims=True))
            alpha = jnp.exp2(m_old - m_new)
            p = jnp.concatenate([jnp.exp2((block(s, c) - m_new).astype(BF16)) for c in range(nblk)], axis=1)
            pv = jnp.dot(p, vx_ref[s, pl.ds(start, width), :], preferred_element_type=F32)
            acc_ref[s, :, :A_DV] = alpha * acc_ref[s, :, :A_DV] + pv[:, :A_DV]
            acc_ref[s, :, A_DV:] = alpha * acc_ref[s, :, A_DV:] + pv[:, A_DV:]
            m_ref[s] = m_new

    n_tiles = (qi * tq + tq + width - 1) // width
    n_pairs = (n_tiles + 1) // 2

    def pair(j, masks):
        scores(sb_ref, 2 * j + 1)
        softmax_pv(sa_ref, 2 * j, masks[0])
        scores(sa_ref, 2 * j + 2)
        softmax_pv(sb_ref, 2 * j + 1, masks[1])

    scores(sa_ref, 0)

    @pl.when(n_pairs > 1)
    def _():
        pair(0, ("pad", "none"))

    def inner(j, carry):
        pair(j, ("none", "none"))
        return carry
    lax.fori_loop(1, n_pairs - 1, inner, 0)

    @pl.when(n_tiles == 2 * n_pairs)
    def _():
        scores(sb_ref, 2 * n_pairs - 1)
        softmax_pv(sa_ref, 2 * n_pairs - 2, "full")
        softmax_pv(sb_ref, 2 * n_pairs - 1, "full")

    @pl.when(n_tiles < 2 * n_pairs)
    def _():
        softmax_pv(sa_ref, 2 * n_pairs - 2, "full")

    for s in heads:
        o = acc_ref[s, :, :A_DV] / acc_ref[s, :, A_DV:]
        o = o[:tq] - lam_ref[...] * o[tq:]
        o_ref[:, lanes(s)] = (_rms(o, sg_ref[...]) * (1.0 - lam_init)).astype(o_ref.dtype)


def _diff_attention(q, k, v, sub_gain, lam, lam_init, batch, lp, tq):
    nq = lp // tq
    hw = ATTN_HEADS * LANES
    kv_spec = pl.BlockSpec((lp, hw), lambda b, h, i: (b, h), pipeline_mode=pl.Buffered(1))
    q_spec = pl.BlockSpec((tq, hw), lambda b, h, i: (b * nq + i, h))
    vec = pl.BlockSpec((1, LANES), lambda b, h, i: (0, 0))
    width = min(ATTN_WIDE, lp)
    return pl.pallas_call(
        functools.partial(_attn_kernel, tq=tq, width=width, lam_init=lam_init),
        grid=(batch, A_HEADS // ATTN_HEADS, nq),
        in_specs=[q_spec, kv_spec, kv_spec, vec, vec],
        out_specs=q_spec,
        out_shape=jax.ShapeDtypeStruct(q.shape, BF16),
        scratch_shapes=[pltpu.VMEM((ATTN_HEADS, 2 * tq, LANES), BF16),
                        pltpu.VMEM((ATTN_HEADS, lp, 2 * A_DV), BF16),
                        pltpu.VMEM((ATTN_HEADS, 2 * tq, LANES), F32),
                        pltpu.VMEM((ATTN_HEADS, 2 * tq, 2 * A_DV), F32),
                        pltpu.VMEM((ATTN_HEADS, 2 * tq, width), F32),
                        pltpu.VMEM((ATTN_HEADS, 2 * tq, width), F32)],
        compiler_params=pltpu.CompilerParams(dimension_semantics=("arbitrary",) * 3,
                                             vmem_limit_bytes=VMEM_LIMIT),
        name="diff_attention",
    )(q, k, v, sub_gain[None, :].astype(F32), jnp.full((1, LANES), lam, F32))


def _chunk_masks(n):
    r = lax.broadcasted_iota(jnp.int32, (n, n), 0)
    c = lax.broadcasted_iota(jnp.int32, (n, n), 1)
    same = (r // CHUNK) == (c // CHUNK)
    return r, c, same


def _mask_dot(mask, x):
    hi = x.astype(BF16)
    rest = x - hi.astype(F32)
    mid = rest.astype(BF16)
    lo = (rest - mid.astype(F32)).astype(BF16)
    m = mask.astype(BF16)
    return jnp.dot(jnp.concatenate([m, m, m], axis=1), jnp.concatenate([hi, mid, lo], axis=0),
                   preferred_element_type=F32)


def _bdot(a, b):
    return jnp.dot(a.astype(BF16), b.astype(BF16), preferred_element_type=F32)


def _bdot_nt(a, b):
    return lax.dot_general(a.astype(BF16), b.astype(BF16), (((1,), (1,)), ((), ())),
                           preferred_element_type=F32)


def _bdot_tn(a, b):
    return lax.dot_general(a.astype(BF16), b.astype(BF16), (((0,), (0,)), ((), ())),
                           preferred_element_type=F32)


def _gdn_kernel(x_ref, z_ref, gp_ref, cw_ref, alog_ref, dtb_ref, og_ref, o_ref,
                xbuf_ref, s_ref, *, steps_per_batch):
    n = ROW_TILE
    step = pl.program_id(0) % steps_per_batch

    @pl.when(step == 0)
    def _():
        xbuf_ref[:8, :] = jnp.zeros((8, 3 * B_W), F32)
        s_ref[...] = jnp.zeros_like(s_ref)

    xbuf_ref[8:, :] = x_ref[...].astype(F32)
    conv = cw_ref[CONV_K - 1:CONV_K, :] * xbuf_ref[8:, :]
    for j in range(CONV_K - 1):
        conv = conv + cw_ref[j:j + 1, :] * xbuf_ref[5 + j:5 + j + n, :]
    xbuf_ref[:8, :] = xbuf_ref[n:n + 8, :]
    conv = _silu(conv)

    r, c, same = _chunk_masks(n)
    incl = same & (r >= c)
    strict = same & (r > c)
    eye = (r == c).astype(F32)
    gp = gp_ref[...]
    beta_all = _sigmoid(gp)
    gpre = gp + dtb_ref[...]
    softplus = jnp.maximum(gpre, 0.0) + jnp.log(1.0 + jnp.exp(-jnp.abs(gpre)))
    row = step * n + lax.broadcasted_iota(jnp.int32, (n, 1), 0)
    g_all = jnp.where(row >= FRONT, -jnp.exp(alog_ref[...]) * softplus, 0.0)
    sums = _mask_dot(jnp.concatenate([incl, same], axis=0), g_all)
    gc_all, gl_all = sums[:n], sums[n:]
    gc_rows = gc_all.T

    heads = range(B_HEADS)
    chunks = [slice(ci * CHUNK, (ci + 1) * CHUNK) for ci in range(n // CHUNK)]
    p, t, rhs, qk, qg, kdec, decay = [], [], [], [], [], [], []
    for h in heads:
        qh = conv[:, h * B_DK:(h + 1) * B_DK]
        kh = conv[:, B_W + h * B_DK:B_W + (h + 1) * B_DK]
        vh = conv[:, 2 * B_W + h * B_DK:2 * B_W + (h + 1) * B_DK]
        qh = qh * lax.rsqrt(jnp.sum(qh * qh, axis=-1, keepdims=True) + EPS) * (B_DK ** -0.5)
        kh = kh * lax.rsqrt(jnp.sum(kh * kh, axis=-1, keepdims=True) + EPS)
        beta = beta_all[:, h:h + 1]
        gcol = gc_all[:, B_HEADS + h:B_HEADS + h + 1]
        glcol = gl_all[:, B_HEADS + h:B_HEADS + h + 1]
        grow = gc_rows[B_HEADS + h:B_HEADS + h + 1, :]
        dec = jnp.exp(jnp.where(incl, gcol - grow, -jnp.inf))
        kb = kh * beta
        egc = jnp.exp(gcol)
        p0 = -(_bdot_nt(kb, kh) * jnp.where(strict, dec, 0.0))
        p.append(p0)
        t.append(eye + p0)
        rhs.append(jnp.concatenate([vh * beta, kb * egc], axis=1))
        qk.append(_bdot_nt(qh, kh) * dec)
        qg.append(qh * egc)
        kdec.append(kh * jnp.exp(glcol - gcol))
        decay.append([jnp.exp(glcol[rs.stop - 1:rs.stop, :]) for rs in chunks])
    for _ in range(5):
        for h in heads:
            p[h] = _bdot(p[h], p[h])
            t[h] = t[h] + _bdot(t[h], p[h])
    sol = [_bdot(t[h], rhs[h]) for h in heads]
    u = [s[:, :B_DK] for s in sol]
    w = [s[:, B_DK:] for s in sol]
    kw = [[_bdot_tn(kdec[h][rs], w[h][rs]) for rs in chunks] for h in heads]
    ku = [[_bdot_tn(kdec[h][rs], u[h][rs]) for rs in chunks] for h in heads]
    state = [s_ref[h] for h in heads]
    before = [[] for _ in heads]
    for ci in range(len(chunks)):
        for h in heads:
            before[h].append(state[h])
            state[h] = state[h] * decay[h][ci] - _bdot(kw[h][ci], state[h]) + ku[h][ci]
    for h in heads:
        s_ref[h] = state[h]
        vnew, inter = [], []
        for ci, rs in enumerate(chunks):
            ws = _bdot(jnp.concatenate([w[h][rs], qg[h][rs]], axis=0), before[h][ci])
            vnew.append(u[h][rs] - ws[:CHUNK])
            inter.append(ws[CHUNK:])
        o = jnp.concatenate(inter, axis=0) + _bdot(qk[h], jnp.concatenate(vnew, axis=0))
        sl = slice(h * B_DK, (h + 1) * B_DK)
        o_ref[:, sl] = (_rms(o, og_ref[...]) * _silu(z_ref[:, sl].astype(F32))).astype(o_ref.dtype)


def _gated_deltanet(bqkv, bz, gp, conv_w, a_log, dt_bias, out_gain, lp):
    rows = bqkv.shape[0]
    n = ROW_TILE
    head_vec = lambda p: jnp.zeros((1, LANES), F32).at[0, B_HEADS:2 * B_HEADS].set(p.astype(F32))
    row_spec = lambda w: pl.BlockSpec((n, w), lambda i: (i, 0))
    return pl.pallas_call(
        functools.partial(_gdn_kernel, steps_per_batch=lp // n),
        grid=(rows // n,),
        in_specs=[row_spec(3 * B_W), row_spec(B_W), row_spec(LANES), _const_spec((CONV_K, 3 * B_W)),
                  _const_spec((1, LANES)), _const_spec((1, LANES)), _const_spec((1, B_DK))],
        out_specs=row_spec(B_W),
        out_shape=jax.ShapeDtypeStruct((rows, B_W), BF16),
        scratch_shapes=[pltpu.VMEM((n + 8, 3 * B_W), F32), pltpu.VMEM((B_HEADS, B_DK, B_DK), F32)],
        compiler_params=pltpu.CompilerParams(dimension_semantics=("arbitrary",),
                                             vmem_limit_bytes=VMEM_LIMIT),
        name="gated_deltanet",
    )(bqkv, bz, gp, conv_w.astype(F32), head_vec(a_log), head_vec(dt_bias), out_gain[None, :].astype(F32))


FF_COLS = 256


def _even_out_kernel(h_ref, oa_ref, ob_ref, woa_ref, wob_ref, g_ref, w1_ref, w3_ref, w2_ref, o_ref,
                     act_ref, *, tm, lp, seq):
    h1 = (h_ref[...] + jnp.dot(oa_ref[...], woa_ref[...], preferred_element_type=F32)
          + jnp.dot(ob_ref[...], wob_ref[...], preferred_element_type=F32))
    xn = _rms(h1, g_ref[...]).astype(BF16)
    for c in range(D_FF // FF_COLS):
        cs = slice(c * FF_COLS, (c + 1) * FF_COLS)
        a = jnp.dot(xn, w1_ref[:, cs], preferred_element_type=F32)
        b = jnp.dot(xn, w3_ref[:, cs], preferred_element_type=F32)
        act_ref[:, cs] = (_silu(a) * b).astype(BF16)
    out = h1 + jnp.dot(act_ref[...], w2_ref[...], preferred_element_type=F32)
    o_ref[...] = jnp.where(_row_valid(tm, lp, seq), out, 0.0)


def _even_out(h, oa, ob, w_out, gain, w1, w3, w2, lp, seq, tm):
    rows = h.shape[0]
    wo = w_out.astype(BF16)
    row_spec = lambda w: pl.BlockSpec((tm, w), lambda i: (i, 0))
    return pl.pallas_call(
        functools.partial(_even_out_kernel, tm=tm, lp=lp, seq=seq),
        grid=(rows // tm,),
        in_specs=[row_spec(D_MODEL), row_spec(A_W), row_spec(B_W),
                  _const_spec((A_W, D_MODEL)), _const_spec((B_W, D_MODEL)), _const_spec((1, D_MODEL)),
                  _const_spec(w1.shape), _const_spec(w3.shape), _const_spec(w2.shape)],
        out_specs=row_spec(D_MODEL),
        out_shape=jax.ShapeDtypeStruct((rows, D_MODEL), F32),
        scratch_shapes=[pltpu.VMEM((tm, D_FF), BF16)],
        compiler_params=pltpu.CompilerParams(dimension_semantics=("arbitrary",),
                                             vmem_limit_bytes=VMEM_LIMIT),
        name="even_out_ffn",
    )(h, oa, ob, wo[:A_W], wo[A_W:], gain[None, :].astype(F32),
      w1.astype(BF16), w3.astype(BF16), w2.astype(BF16))


def _odd_in_kernel(h_ref, g_ref, wq_ref, wf_ref, wi_ref, wz_ref, lb_ref,
                   q_ref, k_ref, lf_ref, i_ref, z_ref, *, tm, lp, seq):
    xn = _rms(h_ref[...], g_ref[...]).astype(BF16)
    valid = _row_valid(tm, lp, seq)
    q_ref[...] = jnp.dot(xn, wq_ref[...], preferred_element_type=F32).astype(BF16)
    f = jnp.dot(xn, wf_ref[...], preferred_element_type=F32)
    lb = lb_ref[...]
    fg = lb + (1.0 - lb) * _sigmoid(f)
    k_ref[...] = jnp.where(valid, 1.0 - fg, 0.0).astype(BF16)
    lf_ref[...] = jnp.where(valid, jnp.log(fg), 0.0)
    i_ref[...] = jnp.dot(xn, wi_ref[...], preferred_element_type=F32).astype(BF16)
    z_ref[...] = jnp.dot(xn, wz_ref[...], preferred_element_type=F32).astype(BF16)


def _odd_in(h, gain, w_in, lb, lp, seq, tm):
    rows = h.shape[0]
    wb16 = w_in.astype(BF16)
    ws = [wb16[:, j * D_MODEL:(j + 1) * D_MODEL] for j in range(4)]
    row_spec = pl.BlockSpec((tm, D_MODEL), lambda i: (i, 0))
    wspec = _const_spec((D_MODEL, D_MODEL))
    vec = _const_spec((1, D_MODEL))
    sd = lambda dt: jax.ShapeDtypeStruct((rows, D_MODEL), dt)
    return pl.pallas_call(
        functools.partial(_odd_in_kernel, tm=tm, lp=lp, seq=seq),
        grid=(rows // tm,),
        in_specs=[row_spec, vec, wspec, wspec, wspec, wspec, vec],
        out_specs=(row_spec,) * 5,
        out_shape=(sd(BF16), sd(BF16), sd(F32), sd(BF16), sd(BF16)),
        compiler_params=pltpu.CompilerParams(dimension_semantics=("arbitrary",),
                                             vmem_limit_bytes=VMEM_LIMIT),
        name="odd_in_proj",
    )(h, gain[None, :].astype(F32), *ws, lb[None, :].astype(F32))


SUB = 8
HALVES = (32, 16, 8)


def _hgrn_kernel(q_ref, k_ref, lf_ref, i_ref, z_ref, og_ref, sel_ref, o_ref, s_ref, p_ref, kf_ref, gf_ref,
                 *, steps_per_batch):
    n = ROW_TILE
    step = pl.program_id(0) % steps_per_batch

    @pl.when(step == 0)
    def _():
        s_ref[...] = jnp.zeros_like(s_ref)

    r, c, same = _chunk_masks(n)
    incl = same & (r >= c)
    gc_all = _mask_dot(incl, lf_ref[...]) * LOG2E
    rrow = lax.broadcasted_iota(jnp.int32, (n, 1), 0)
    inner = rrow % SUB
    level_mask = [(r // (2 * half) == c // (2 * half)).astype(F32) for half in HALVES]
    diag_mask = (r // SUB == c // SUB).astype(F32)

    kf_ref[...] = k_ref[...].astype(F32)
    gf_ref[...] = gc_all

    def block_rows(ref, sl, size, j):
        return jnp.concatenate([jnp.broadcast_to(ref[pl.ds(b * size + j, 1), sl], (size, C_DK))
                                for b in range(n // size)], axis=0)

    def products(h):
        sl = slice(h * C_DK, (h + 1) * C_DK)
        q = q_ref[:, sl].astype(F32)
        k = kf_ref[:, sl]
        gc = gc_all[:, sl]
        gl = block_rows(gf_ref, sl, CHUNK, CHUNK - 1)
        qg = q * jnp.exp2(gc)
        kdec = k * jnp.exp2(gl - gc)
        parts = []
        for half in HALVES:
            gref = block_rows(gf_ref, sl, 2 * half, half - 1)
            upper = (rrow % (2 * half)) >= half
            qt = q * jnp.exp2(jnp.where(upper, gc - gref, -jnp.inf))
            kt = k * jnp.exp2(jnp.where(upper, -jnp.inf, gref - gc))
            parts.append(_bdot_nt(qt, kt))
        pbuf = p_ref.at[h % 2]
        for j in range(SUB):
            kj = block_rows(kf_ref, sl, SUB, j)
            gj = block_rows(gf_ref, sl, SUB, j)
            pbuf[:, j * C_DK:(j + 1) * C_DK] = (
                q * kj * jnp.exp2(jnp.where(inner >= j, gc - gj, -jnp.inf))).astype(BF16)
        diag = jnp.dot(pbuf[...], sel_ref[...], preferred_element_type=F32)
        return parts, diag, qg, kdec, gl

    def finish(h, parts, diag, qg, kdec, gl):
        sl = slice(h * C_DK, (h + 1) * C_DK)
        iv = i_ref[:, sl]
        att = diag_mask * diag
        for mask, part in zip(level_mask, parts):
            att = att + mask * part
        st = s_ref[h]
        inter = []
        for ci in range(n // CHUNK):
            rs = slice(ci * CHUNK, (ci + 1) * CHUNK)
            inter.append(_bdot_nt(qg[rs], st))
            st = st * jnp.exp2(gl[ci * CHUNK:ci * CHUNK + 1, :]) + _bdot_tn(iv[rs], kdec[rs])
        s_ref[h] = st
        o = jnp.concatenate(inter, axis=0) + _bdot(att, iv)
        o_ref[:, sl] = (_rms(o, og_ref[...]) * _silu(z_ref[:, sl].astype(F32))).astype(o_ref.dtype)

    for h in range(C_HEADS):
        finish(h, *products(h))


def _hgrn2(q, k, lf, iv, z, out_gain, lp):
    rows = q.shape[0]
    n = ROW_TILE
    row_spec = pl.BlockSpec((n, D_MODEL), lambda i: (i, 0))
    sel = ((jnp.arange(SUB * C_DK)[:, None] // C_DK) == (jnp.arange(n)[None, :] % SUB)).astype(BF16)
    return pl.pallas_call(
        functools.partial(_hgrn_kernel, steps_per_batch=lp // n),
        grid=(rows // n,),
        in_specs=[row_spec] * 5 + [_const_spec((1, C_DK)), _const_spec(sel.shape)],
        out_specs=row_spec,
        out_shape=jax.ShapeDtypeStruct((rows, D_MODEL), BF16),
        scratch_shapes=[pltpu.VMEM((C_HEADS, C_DK, C_DK), F32), pltpu.VMEM((2, n, SUB * C_DK), BF16),
                        pltpu.VMEM((n, D_MODEL), F32), pltpu.VMEM((n, D_MODEL), F32)],
        compiler_params=pltpu.CompilerParams(dimension_semantics=("arbitrary",),
                                             vmem_limit_bytes=VMEM_LIMIT),
        name="hgrn2",
    )(q, k, lf, iv, z, out_gain[None, :].astype(F32), sel)


def _odd_out_kernel(h_ref, o_ref, wo_ref, g_ref, wr_ref, h3_ref, xn_ref, route_ref, *, tm, lp, seq):
    h3 = h_ref[...] + jnp.dot(o_ref[...], wo_ref[...], preferred_element_type=F32)
    h3 = jnp.where(_row_valid(tm, lp, seq), h3, 0.0)
    h3_ref[...] = h3
    xn = _rms(h3, g_ref[...])
    xn_ref[...] = xn
    logits = jnp.dot(xn.astype(BF16), wr_ref[...], preferred_element_type=F32)
    lane = lax.broadcasted_iota(jnp.int32, logits.shape, 1)
    logits = jnp.where(lane < N_EXPERTS, logits, -jnp.inf)
    v1 = jnp.max(logits, axis=-1, keepdims=True)
    e1 = jnp.min(jnp.where(logits == v1, lane, LANES), axis=-1, keepdims=True)
    rest = jnp.where(lane == e1, -jnp.inf, logits)
    v2 = jnp.max(rest, axis=-1, keepdims=True)
    e2 = jnp.min(jnp.where(rest == v2, lane, LANES), axis=-1, keepdims=True)
    t = jnp.exp(v2 - v1)
    g1 = 1.0 / (1.0 + t)
    g2 = t * g1
    route_ref[...] = jnp.where(lane == 0, e1.astype(F32),
                               jnp.where(lane == 1, e2.astype(F32),
                                         jnp.where(lane == 2, g1, jnp.where(lane == 3, g2, 0.0))))


def _odd_out(h, o, w_out, gain, router, lp, seq, tm):
    rows = h.shape[0]
    wr = jnp.pad(router.astype(BF16), ((0, 0), (0, LANES - N_EXPERTS)))
    row_spec = lambda w: pl.BlockSpec((tm, w), lambda i: (i, 0))
    return pl.pallas_call(
        functools.partial(_odd_out_kernel, tm=tm, lp=lp, seq=seq),
        grid=(rows // tm,),
        in_specs=[row_spec(D_MODEL), row_spec(D_MODEL), _const_spec((D_MODEL, D_MODEL)),
                  _const_spec((1, D_MODEL)), _const_spec((D_MODEL, LANES))],
        out_specs=(row_spec(D_MODEL), row_spec(D_MODEL), row_spec(LANES)),
        out_shape=(jax.ShapeDtypeStruct((rows, D_MODEL), F32), jax.ShapeDtypeStruct((rows, D_MODEL), F32),
                   jax.ShapeDtypeStruct((rows, LANES), F32)),
        compiler_params=pltpu.CompilerParams(dimension_semantics=("arbitrary",),
                                             vmem_limit_bytes=VMEM_LIMIT),
        name="odd_out_route",
    )(h, o, w_out.astype(BF16), gain[None, :].astype(F32), wr)


MOE_TILE = 512
MOE_FF_STEPS = 2


def _moe_plan(route, tile):
    n_tok = route.shape[0]
    flat_e = route[:, :2].astype(jnp.int32).reshape(-1)
    na = flat_e.shape[0]
    counts = jnp.sum(flat_e[:, None] == jnp.arange(N_EXPERTS, dtype=jnp.int32)[None, :], axis=0,
                     dtype=jnp.int32)
    order = jnp.argsort(flat_e, stable=True).astype(jnp.int32)
    first = jnp.cumsum(counts) - counts
    padded = (counts + tile - 1) // tile * tile
    pend = jnp.cumsum(padded)
    n_tiles = -(-(na + N_EXPERTS * (tile - 1)) // tile) + 2
    n_rows = n_tiles * tile
    tile_e = jnp.minimum(jnp.searchsorted(pend, jnp.arange(n_tiles, dtype=jnp.int32) * tile, side='right'),
                         N_EXPERTS - 1).astype(jnp.int32)
    row = jnp.arange(n_rows, dtype=jnp.int32)
    row_e = jnp.repeat(tile_e, tile)
    local = row - (pend - padded)[row_e]
    real = local < counts[row_e]
    pair = order[jnp.minimum(first[row_e] + local, na - 1)]
    row_tok = jnp.where(real, pair // 2, 0)
    dump = na + ((row // tile) % 2) * tile + row % tile
    row_dst = jnp.where(real, (pair % 2) * n_tok + pair // 2, dump)
    n_used = (pend[-1] // tile).astype(jnp.int32).reshape(1)
    return row_tok, row_dst, tile_e, n_used, n_tiles


def _moe_ffn_kernel(te_ref, nused_ref, tok0_ref, tokn_ref, dstp_ref, x_hbm, w1_ref, w3_ref, w2_ref,
                    y_hbm, xbuf, xb16, ybuf, gsem, ssem):
    i, f = pl.program_id(0), pl.program_id(1)
    tile = xb16.shape[0]
    per_step = tile // MOE_FF_STEPS
    slot = i % 2
    n_used = nused_ref[0]
    n_out = y_hbm.shape[0] - 2 * tile

    def gather_start(tok_ref, row, dst_slot):
        pltpu.make_async_copy(x_hbm.at[pl.ds(tok_ref[0, 0, row], 1), :],
                              xbuf.at[dst_slot, pl.ds(row, 1), :], gsem.at[dst_slot]).start()

    def scatter_start(row, src_slot):
        pltpu.make_async_copy(ybuf.at[src_slot, pl.ds(row, 1), :],
                              y_hbm.at[pl.ds(dstp_ref[0, 0, row], 1), :], ssem.at[src_slot]).start()

    @pl.when((i == 0) & (f == 0))
    def _():
        ybuf[...] = jnp.zeros(ybuf.shape, F32)
        for s in range(2):
            dump = pltpu.make_async_copy(ybuf.at[s], y_hbm.at[pl.ds(n_out + s * tile, tile), :], ssem.at[s])
            dump.start()
            dump.wait()

        def issue(r, carry):
            gather_start(tok0_ref, r, 0)
            return carry
        lax.fori_loop(0, tile, issue, 0)

    @pl.when((f == 0) & (i >= 2) & (i - 2 < n_used))
    def _():
        pltpu.make_async_copy(ybuf.at[slot], y_hbm.at[pl.ds(0, tile), :], ssem.at[slot]).wait()

    @pl.when((f == 0) & (i <= n_used))
    def _():
        pltpu.make_async_copy(x_hbm.at[pl.ds(0, tile), :], xbuf.at[slot], gsem.at[slot]).wait()
        xb16[...] = xbuf[slot].astype(BF16)
        ybuf[slot] = jnp.zeros((tile, D_MODEL), F32)

    def compute(scatter_previous):
        for r in range(per_step):
            gather_start(tokn_ref, f * per_step + r, 1 - slot)
            if scatter_previous:
                scatter_start(f * per_step + r, 1 - slot)
        x = xb16[...]
        a = jnp.dot(x, w1_ref[0], preferred_element_type=F32)
        b = jnp.dot(x, w3_ref[0], preferred_element_type=F32)
        ybuf[slot] += jnp.dot((_silu(a) * b).astype(BF16), w2_ref[0], preferred_element_type=F32)

    @pl.when(i == 0)
    def _():
        compute(False)

    @pl.when((i > 0) & (i < n_used))
    def _():
        compute(True)

    @pl.when((i == n_used) & (f == 0))
    def _():
        def issue(r, carry):
            scatter_start(r, 1 - slot)
            return carry
        lax.fori_loop(0, tile, issue, 0)


def _moe_ffn(xn, w1, w3, w2, row_tok, row_dst, tile_e, n_used, n_tiles, tile):
    n_tok = xn.shape[0]
    steps = MOE_FF_STEPS
    ff = D_FF_EXPERT // steps
    fi = lambda i, f, nu: jnp.where(i < nu[0], f, steps - 1)
    tok_spec = lambda imap: pl.BlockSpec((1, 1, tile), imap, memory_space=pltpu.SMEM)
    grid_spec = pltpu.PrefetchScalarGridSpec(
        num_scalar_prefetch=2,
        grid=(n_tiles, steps),
        in_specs=[tok_spec(lambda i, f, te, nu: (0, 0, 0)),
                  tok_spec(lambda i, f, te, nu: (jnp.minimum(i + 1, n_tiles - 1), 0, 0)),
                  tok_spec(lambda i, f, te, nu: (jnp.maximum(i - 1, 0), 0, 0)),
                  pl.BlockSpec(memory_space=pl.ANY),
                  pl.BlockSpec((1, D_MODEL, ff), lambda i, f, te, nu: (te[i], 0, fi(i, f, nu))),
                  pl.BlockSpec((1, D_MODEL, ff), lambda i, f, te, nu: (te[i], 0, fi(i, f, nu))),
                  pl.BlockSpec((1, ff, D_MODEL), lambda i, f, te, nu: (te[i], fi(i, f, nu), 0))],
        out_specs=pl.BlockSpec(memory_space=pl.ANY),
        scratch_shapes=[pltpu.VMEM((2, tile, D_MODEL), F32), pltpu.VMEM((tile, D_MODEL), BF16),
                        pltpu.VMEM((2, tile, D_MODEL), F32),
                        pltpu.SemaphoreType.DMA((2,)), pltpu.SemaphoreType.DMA((2,))],
    )
    tok3 = row_tok.reshape(n_tiles, 1, tile)
    return pl.pallas_call(
        _moe_ffn_kernel,
        grid_spec=grid_spec,
        out_shape=jax.ShapeDtypeStruct((2 * n_tok + 2 * tile, D_MODEL), F32),
        compiler_params=pltpu.CompilerParams(dimension_semantics=("arbitrary", "arbitrary"),
                                             vmem_limit_bytes=VMEM_LIMIT),
        name="moe_ffn",
    )(tile_e, n_used, tok3, tok3, row_dst.reshape(n_tiles, 1, tile), xn, w1, w3, w2)


def _moe_combine_kernel(h_ref, y0_ref, y1_ref, route_ref, o_ref):
    g = route_ref[...]
    o_ref[0] = h_ref[...] + g[:, 2:3] * y0_ref[...] + g[:, 3:4] * y1_ref[...]


def _moe_combine(h3, y, route, batch, seq, lp, tc):
    n_tok = h3.shape[0]
    frame_rows = lambda width, off: pl.BlockSpec(
        (pl.Element(tc), pl.Element(width)),
        lambda b, j: (pl.multiple_of(off + b * lp + CHUNK + j * tc, CHUNK), 0))
    return pl.pallas_call(
        _moe_combine_kernel,
        grid=(batch, seq // tc),
        in_specs=[frame_rows(D_MODEL, 0), frame_rows(D_MODEL, 0), frame_rows(D_MODEL, n_tok),
                  frame_rows(LANES, 0)],
        out_specs=pl.BlockSpec((1, tc, D_MODEL), lambda b, j: (b, j, 0)),
        out_shape=jax.ShapeDtypeStruct((batch, seq, D_MODEL), F32),
        compiler_params=pltpu.CompilerParams(dimension_semantics=("arbitrary", "arbitrary"),
                                             vmem_limit_bytes=VMEM_LIMIT),
        name="moe_combine",
    )(h3, y, y, route)


def _layout_rows(seq):
    return -(-(CHUNK + seq) // ROW_TILE) * ROW_TILE


def _proj_tile(lp):
    return next(t for t in (640, 512, 384, ROW_TILE) if lp % t == 0)


def kernel(x, meta_tokens, lb_logits, mix_norm, ffn_norm, e_w_in, a_q_gain, a_k_gain, a_lam_q1, a_lam_k1, a_lam_q2, a_lam_k2, a_sub_gain, b_conv, b_a_log, b_dt_bias, b_out_gain, e_w_out, ffn_w1, ffn_w3, ffn_w2, o_w_in, c_out_gain, o_w_out, router, moe_w1, moe_w3, moe_w2):
    batch, seq, _ = x.shape
    lp = _layout_rows(seq)
    rows = batch * lp
    tm = _proj_tile(lp)
    meta = jnp.broadcast_to(meta_tokens[None].astype(x.dtype), (batch, N_META, D_MODEL))
    h = jnp.concatenate([jnp.zeros((batch, FRONT, D_MODEL), x.dtype), meta, x,
                         jnp.zeros((batch, lp - CHUNK - seq, D_MODEL), x.dtype)], axis=1).reshape(rows, D_MODEL)

    q, k, v, bqkv, bz, gp = _even_in(h, mix_norm[0], e_w_in[0], a_q_gain[0], a_k_gain[0], lp, tm)
    lam_init = 0.8 - 0.6 * math.exp(-0.3 * 0)
    lam = (jnp.exp(jnp.sum(a_lam_q1[0].astype(F32) * a_lam_k1[0].astype(F32)))
           - jnp.exp(jnp.sum(a_lam_q2[0].astype(F32) * a_lam_k2[0].astype(F32))) + lam_init)
    o_a = _diff_attention(q, k, v, a_sub_gain[0], lam, lam_init, batch, lp, ROW_TILE)
    o_b = _gated_deltanet(bqkv, bz, gp, b_conv[0], b_a_log[0], b_dt_bias[0], b_out_gain[0], lp)
    h = _even_out(h, o_a, o_b, e_w_out[0], ffn_norm[0], ffn_w1[0], ffn_w3[0], ffn_w2[0], lp, seq, tm)

    lb_all = jnp.cumsum(jax.nn.softmax(lb_logits.astype(F32), axis=0), axis=0)
    lb = (lb_all - lb_all[0])[1]
    cq, ck, clf, ci, cz = _odd_in(h, mix_norm[1], o_w_in[0], lb, lp, seq, tm)
    o_c = _hgrn2(cq, ck, clf, ci, cz, c_out_gain[0], lp)
    h3, xn, route = _odd_out(h, o_c, o_w_out[0], ffn_norm[1], router[0], lp, seq, tm)
    row_tok, row_dst, tile_e, n_used, n_tiles = _moe_plan(route, MOE_TILE)
    y = _moe_ffn(xn, moe_w1[0].astype(BF16), moe_w3[0].astype(BF16), moe_w2[0].astype(BF16),
                 row_tok, row_dst, tile_e, n_used, n_tiles, MOE_TILE)
    return _moe_combine(h3, y, route, batch, seq, lp, ROW_TILE)
```

```python
import functools
import math

import jax
import jax.numpy as jnp
from jax import lax
from jax.experimental import pallas as pl
from jax.experimental.pallas import tpu as pltpu

D_MODEL = 1024
CHUNK = 64
N_META = 16
FRONT = CHUNK - N_META
EPS = 1e-6
ROPE_THETA = 500000.0
LOG2E = math.log2(math.e)

A_HEADS = 4
A_DK = 64
A_DV = 128
A_ROT = A_DK // 4
A_W = A_HEADS * A_DV

B_HEADS = 4
B_DK = 128
B_W = B_HEADS * B_DK
CONV_K = 4

C_HEADS = 8
C_DK = 128

D_FF = 2816
N_EXPERTS = 8
D_FF_EXPERT = 3584

LANES = 128
ROW_TILE = 256
VMEM_LIMIT = 56 * 1024 * 1024

BF16 = jnp.bfloat16
F32 = jnp.float32


def _const_spec(shape):
    return pl.BlockSpec(shape, lambda *_: (0,) * len(shape), pipeline_mode=pl.Buffered(1))


def _rms(x, gain):
    return x * lax.rsqrt(jnp.mean(x * x, axis=-1, keepdims=True) + EPS) * gain


def _silu(x):
    return x * (1.0 / (1.0 + jnp.exp(-x)))


def _sigmoid(x):
    return 1.0 / (1.0 + jnp.exp(-x))


def _row_valid(tile_rows, lp, seq):
    r = pl.program_id(0) * tile_rows + lax.broadcasted_iota(jnp.int32, (tile_rows, 1), 0)
    p = r % lp
    return (p >= FRONT) & (p < CHUNK + seq)


def _head_norm_rope(a, ones_bd, gain, cos, sin_lo, sin_hi):
    ssq = jnp.dot((a * a).astype(BF16), ones_bd, preferred_element_type=F32)
    y = a * lax.rsqrt(ssq * (1.0 / A_DK) + EPS) * gain
    up = pltpu.roll(y, A_W - A_ROT // 2, axis=1)
    dn = pltpu.roll(y, A_ROT // 2, axis=1)
    return y * cos + up * sin_lo + dn * sin_hi


def _even_in_kernel(h_ref, g_ref, wq_ref, wk_ref, wv_ref, wb_ref, wz_ref, wg_ref, bd_ref,
                    qg_ref, kg_ref, cos_ref, slo_ref, shi_ref,
                    q_ref, k_ref, v_ref, b_ref, z_ref, gp_ref):
    xn = _rms(h_ref[...], g_ref[...]).astype(BF16)
    rep = lambda t: jnp.concatenate([t] * (A_W // LANES), axis=1)
    cos, slo, shi = rep(cos_ref[...]), rep(slo_ref[...]), rep(shi_ref[...])
    bd = bd_ref[...]
    aq = jnp.dot(xn, wq_ref[...], preferred_element_type=F32)
    q_ref[...] = (_head_norm_rope(aq, bd, qg_ref[...], cos, slo, shi) * (A_DK ** -0.5 * LOG2E)).astype(BF16)
    ak = jnp.dot(xn, wk_ref[...], preferred_element_type=F32)
    k_ref[...] = _head_norm_rope(ak, bd, kg_ref[...], cos, slo, shi).astype(BF16)
    v_ref[...] = jnp.dot(xn, wv_ref[...], preferred_element_type=F32).astype(BF16)
    b_ref[...] = jnp.dot(xn, wb_ref[...], preferred_element_type=F32).astype(BF16)
    z_ref[...] = jnp.dot(xn, wz_ref[...], preferred_element_type=F32).astype(BF16)
    gp_ref[...] = jnp.dot(xn, wg_ref[...], preferred_element_type=F32)


def _even_in(h, gain, w_in, q_gain, k_gain, lp, tm):
    rows = h.shape[0]
    wb16 = w_in.astype(BF16)
    c0, c1, c2, c3, c4 = A_W, 2 * A_W, 3 * A_W, 3 * A_W + 3 * B_W, 3 * A_W + 4 * B_W
    wq, wk, wv, wb, wz = wb16[:, :c0], wb16[:, c0:c1], wb16[:, c1:c2], wb16[:, c2:c3], wb16[:, c3:c4]
    wg = jnp.pad(wb16[:, c4:], ((0, 0), (0, LANES - 2 * B_HEADS)))
    d = jnp.arange(A_W)
    ones_bd = (d[:, None] // A_DK == d[None, :] // A_DK).astype(BF16)
    half = A_ROT // 2
    inv_freq = ROPE_THETA ** (-jnp.arange(half, dtype=F32) * 2.0 / A_ROT)
    pos = (jnp.arange(lp) - FRONT).astype(F32)
    dim = jnp.arange(LANES) % A_DK
    ang = pos[:, None] * inv_freq[dim % half][None, :]
    cos_t = jnp.where(dim < A_ROT, jnp.cos(ang), 1.0)
    slo_t = jnp.where(dim < half, -jnp.sin(ang), 0.0)
    shi_t = jnp.where((dim >= half) & (dim < A_ROT), jnp.sin(ang), 0.0)
    tile8 = lambda g: jnp.tile(g.astype(F32), 2 * A_HEADS)[None, :]
    n_pos = lp // tm
    row_spec = lambda w: pl.BlockSpec((tm, w), lambda i: (i, 0))
    pos_spec = pl.BlockSpec((tm, LANES), lambda i: (i % n_pos, 0))
    outs = pl.pallas_call(
        _even_in_kernel,
        grid=(rows // tm,),
        in_specs=[row_spec(D_MODEL), _const_spec((1, D_MODEL)),
                  _const_spec(wq.shape), _const_spec(wk.shape), _const_spec(wv.shape),
                  _const_spec(wb.shape), _const_spec(wz.shape), _const_spec(wg.shape),
                  _const_spec(ones_bd.shape), _const_spec((1, A_W)), _const_spec((1, A_W)),
                  pos_spec, pos_spec, pos_spec],
        out_specs=(row_spec(A_W), row_spec(A_W), row_spec(A_W), row_spec(3 * B_W), row_spec(B_W),
                   row_spec(LANES)),
        out_shape=(jax.ShapeDtypeStruct((rows, A_W), BF16), jax.ShapeDtypeStruct((rows, A_W), BF16),
                   jax.ShapeDtypeStruct((rows, A_W), BF16), jax.ShapeDtypeStruct((rows, 3 * B_W), BF16),
                   jax.ShapeDtypeStruct((rows, B_W), BF16), jax.ShapeDtypeStruct((rows, LANES), F32)),
        compiler_params=pltpu.CompilerParams(dimension_semantics=("arbitrary",),
                                             vmem_limit_bytes=VMEM_LIMIT),
        name="even_in_proj",
    )(h, gain[None, :].astype(F32), wq, wk, wv, wb, wz, wg, ones_bd, tile8(q_gain), tile8(k_gain),
      cos_t, slo_t, shi_t)
    return outs


ATTN_WIDE = 512


ATTN_HEADS = 2
V_ROWS = A_DV + 16


def _attn_kernel(q_ref, k_ref, v_ref, sg_ref, lam_ref, o_ref, qs_ref, vx_ref, m_ref, acc_ref,
                 sa_ref, sb_ref, *, tq, width, lam_init):
    qi = pl.program_id(2)
    heads = range(ATTN_HEADS)
    lanes = lambda s: slice(s * LANES, (s + 1) * LANES)

    lp = k_ref.shape[0]

    @pl.when(qi == 0)
    def _():
        for s in heads:
            for c0 in range(0, lp, width):
                c1 = min(c0 + width, lp)
                vx_ref[s, :A_DV, c0:c1] = v_ref[c0:c1, lanes(s)].astype(F32).T.astype(BF16)
            row = lax.broadcasted_iota(jnp.int32, (V_ROWS - A_DV, lp), 0)
            vx_ref[s, A_DV:, :] = jnp.where(row == 0, 1.0, 0.0).astype(BF16)

    lane = lax.broadcasted_iota(jnp.int32, (tq, LANES), 1)
    zero = jnp.zeros((tq, LANES), BF16)
    for s in heads:
        q = q_ref[:, lanes(s)]
        both = jnp.concatenate([jnp.where(lane < A_DK, q, zero), jnp.where(lane >= A_DK, q, zero)], axis=0)
        qs_ref[s] = both.astype(F32).T.astype(BF16)
    m_ref[...] = jnp.full(m_ref.shape, -jnp.inf, F32)
    acc_ref[...] = jnp.zeros(acc_ref.shape, F32)

    qchunk = (qi * tq + lax.broadcasted_iota(jnp.int32, (1, 2 * tq), 1) % tq) // CHUNK
    row_block = 128

    def window(t):
        return pl.multiple_of(jnp.minimum(t * width, lp - width), tq)

    def scores(s_ref, t):
        for s in heads:
            s_ref[s] = jnp.dot(k_ref[pl.ds(window(t), width), lanes(s)], qs_ref[s],
                               preferred_element_type=F32)

    def softmax_pv(s_ref, t, mask):
        start = window(t)
        kpos = start + lax.broadcasted_iota(jnp.int32, (width, 1), 0)
        kchunk = jnp.where((kpos >= t * width) & (kpos >= FRONT), kpos // CHUNK, lp)
        nblk = width // row_block

        def block(s, c):
            rs = slice(c * row_block, (c + 1) * row_block)
            if mask == "full":
                return jnp.where(kchunk[rs] <= qchunk, s_ref[s, rs, :], -jnp.inf)
            if mask == "pad" and c == 0:
                return jnp.where(kpos[rs] >= FRONT, s_ref[s, rs, :], -jnp.inf)
            return s_ref[s, rs, :]

        for s in heads:
            mx = functools.reduce(jnp.maximum, [block(s, c) for c in range(nblk)])
            m_old = m_ref[s]
            m_new = jnp.maximum(m_old, jnp.max(mx, axis=0, keepdims=True))
            alpha = jnp.exp2(m_old - m_new)
            p = jnp.concatenate([jnp.exp2((block(s, c) - m_new).astype(BF16)) for c in range(nblk)], axis=0)
            pv = jnp.dot(vx_ref[s, :, pl.ds(start, width)], p, preferred_element_type=F32)
            acc_ref[s] = alpha * acc_ref[s] + pv
            m_ref[s] = m_new

    n_tiles = (qi * tq + tq + width - 1) // width
    n_pairs = (n_tiles + 1) // 2

    def pair(j, masks):
        scores(sb_ref, 2 * j + 1)
        softmax_pv(sa_ref, 2 * j, masks[0])
        scores(sa_ref, 2 * j + 2)
        softmax_pv(sb_ref, 2 * j + 1, masks[1])

    scores(sa_ref, 0)

    @pl.when(n_pairs > 1)
    def _():
        pair(0, ("pad", "none"))

    def inner(j, carry):
        pair(j, ("none", "none"))
        return carry
    lax.fori_loop(1, n_pairs - 1, inner, 0)

    @pl.when(n_tiles == 2 * n_pairs)
    def _():
        scores(sb_ref, 2 * n_pairs - 1)
        softmax_pv(sa_ref, 2 * n_pairs - 2, "full")
        softmax_pv(sb_ref, 2 * n_pairs - 1, "full")

    @pl.when(n_tiles < 2 * n_pairs)
    def _():
        softmax_pv(sa_ref, 2 * n_pairs - 2, "full")

    for s in heads:
        o = (acc_ref[s, :A_DV, :] / acc_ref[s, A_DV:A_DV + 1, :]).T
        o = o[:tq] - lam_ref[...] * o[tq:]
        o_ref[:, lanes(s)] = (_rms(o, sg_ref[...]) * (1.0 - lam_init)).astype(o_ref.dtype)


def _diff_attention(q, k, v, sub_gain, lam, lam_init, batch, lp, tq):
    nq = lp // tq
    hw = ATTN_HEADS * LANES
    kv_spec = pl.BlockSpec((lp, hw), lambda b, h, i: (b, h), pipeline_mode=pl.Buffered(1))
    q_spec = pl.BlockSpec((tq, hw), lambda b, h, i: (b * nq + i, h))
    vec = pl.BlockSpec((1, LANES), lambda b, h, i: (0, 0))
    width = min(ATTN_WIDE, lp)
    return pl.pallas_call(
        functools.partial(_attn_kernel, tq=tq, width=width, lam_init=lam_init),
        grid=(batch, A_HEADS // ATTN_HEADS, nq),
        in_specs=[q_spec, kv_spec, kv_spec, vec, vec],
        out_specs=q_spec,
        out_shape=jax.ShapeDtypeStruct(q.shape, BF16),
        scratch_shapes=[pltpu.VMEM((ATTN_HEADS, LANES, 2 * tq), BF16),
                        pltpu.VMEM((ATTN_HEADS, V_ROWS, lp), BF16),
                        pltpu.VMEM((ATTN_HEADS, 1, 2 * tq), F32),
                        pltpu.VMEM((ATTN_HEADS, V_ROWS, 2 * tq), F32),
                        pltpu.VMEM((ATTN_HEADS, width, 2 * tq), F32),
                        pltpu.VMEM((ATTN_HEADS, width, 2 * tq), F32)],
        compiler_params=pltpu.CompilerParams(dimension_semantics=("arbitrary",) * 3,
                                             vmem_limit_bytes=VMEM_LIMIT),
        name="diff_attention",
    )(q, k, v, sub_gain[None, :].astype(F32), jnp.full((1, LANES), lam, F32))


def _chunk_masks(n):
    r = lax.broadcasted_iota(jnp.int32, (n, n), 0)
    c = lax.broadcasted_iota(jnp.int32, (n, n), 1)
    same = (r // CHUNK) == (c // CHUNK)
    return r, c, same


def _mask_dot(mask, x):
    hi = x.astype(BF16)
    rest = x - hi.astype(F32)
    mid = rest.astype(BF16)
    lo = (rest - mid.astype(F32)).astype(BF16)
    m = mask.astype(BF16)
    return jnp.dot(jnp.concatenate([m, m, m], axis=1), jnp.concatenate([hi, mid, lo], axis=0),
                   preferred_element_type=F32)


def _bdot(a, b):
    return jnp.dot(a.astype(BF16), b.astype(BF16), preferred_element_type=F32)


def _bdot_nt(a, b):
    return lax.dot_general(a.astype(BF16), b.astype(BF16), (((1,), (1,)), ((), ())),
                           preferred_element_type=F32)


def _bdot_tn(a, b):
    return lax.dot_general(a.astype(BF16), b.astype(BF16), (((0,), (0,)), ((), ())),
                           preferred_element_type=F32)


def _gdn_kernel(x_ref, z_ref, gp_ref, cw_ref, alog_ref, dtb_ref, og_ref, o_ref,
                xbuf_ref, s_ref, *, steps_per_batch):
    n = ROW_TILE
    step = pl.program_id(0) % steps_per_batch

    @pl.when(step == 0)
    def _():
        xbuf_ref[:8, :] = jnp.zeros((8, 3 * B_W), F32)
        s_ref[...] = jnp.zeros_like(s_ref)

    xbuf_ref[8:, :] = x_ref[...].astype(F32)
    conv = cw_ref[CONV_K - 1:CONV_K, :] * xbuf_ref[8:, :]
    for j in range(CONV_K - 1):
        conv = conv + cw_ref[j:j + 1, :] * xbuf_ref[5 + j:5 + j + n, :]
    xbuf_ref[:8, :] = xbuf_ref[n:n + 8, :]
    conv = _silu(conv)

    r, c, same = _chunk_masks(n)
    incl = same & (r >= c)
    strict = same & (r > c)
    eye = (r == c).astype(F32)
    gp = gp_ref[...]
    beta_all = _sigmoid(gp)
    gpre = gp + dtb_ref[...]
    softplus = jnp.maximum(gpre, 0.0) + jnp.log(1.0 + jnp.exp(-jnp.abs(gpre)))
    row = step * n + lax.broadcasted_iota(jnp.int32, (n, 1), 0)
    g_all = jnp.where(row >= FRONT, -jnp.exp(alog_ref[...]) * softplus, 0.0)
    sums = _mask_dot(jnp.concatenate([incl, same], axis=0), g_all)
    gc_all, gl_all = sums[:n], sums[n:]
    gc_rows = gc_all.T

    heads = range(B_HEADS)
    chunks = [slice(ci * CHUNK, (ci + 1) * CHUNK) for ci in range(n // CHUNK)]
    p, t, rhs, qk, qg, kdec, decay = [], [], [], [], [], [], []
    for h in heads:
        qh = conv[:, h * B_DK:(h + 1) * B_DK]
        kh = conv[:, B_W + h * B_DK:B_W + (h + 1) * B_DK]
        vh = conv[:, 2 * B_W + h * B_DK:2 * B_W + (h + 1) * B_DK]
        qh = qh * lax.rsqrt(jnp.sum(qh * qh, axis=-1, keepdims=True) + EPS) * (B_DK ** -0.5)
        kh = kh * lax.rsqrt(jnp.sum(kh * kh, axis=-1, keepdims=True) + EPS)
        beta = beta_all[:, h:h + 1]
        gcol = gc_all[:, B_HEADS + h:B_HEADS + h + 1]
        glcol = gl_all[:, B_HEADS + h:B_HEADS + h + 1]
        grow = gc_rows[B_HEADS + h:B_HEADS + h + 1, :]
        dec = jnp.exp(jnp.where(incl, gcol - grow, -jnp.inf))
        kb = kh * beta
        egc = jnp.exp(gcol)
        p0 = -(_bdot_nt(kb, kh) * jnp.where(strict, dec, 0.0))
        p.append(p0)
        t.append(eye + p0)
        rhs.append(jnp.concatenate([vh * beta, kb * egc], axis=1))
        qk.append(_bdot_nt(qh, kh) * dec)
        qg.append(qh * egc)
        kdec.append(kh * jnp.exp(glcol - gcol))
        decay.append([jnp.exp(glcol[rs.stop - 1:rs.stop, :]) for rs in chunks])
    for _ in range(5):
        for h in heads:
            p[h] = _bdot(p[h], p[h])
            t[h] = t[h] + _bdot(t[h], p[h])
    sol = [_bdot(t[h], rhs[h]) for h in heads]
    u = [s[:, :B_DK] for s in sol]
    w = [s[:, B_DK:] for s in sol]
    kw = [[_bdot_tn(kdec[h][rs], w[h][rs]) for rs in chunks] for h in heads]
    ku = [[_bdot_tn(kdec[h][rs], u[h][rs]) for rs in chunks] for h in heads]
    state = [s_ref[h] for h in heads]
    before = [[] for _ in heads]
    for ci in range(len(chunks)):
        for h in heads:
            before[h].append(state[h])
            state[h] = state[h] * decay[h][ci] - _bdot(kw[h][ci], state[h]) + ku[h][ci]
    for h in heads:
        s_ref[h] = state[h]
        vnew, inter = [], []
        for ci, rs in enumerate(chunks):
            ws = _bdot(jnp.concatenate([w[h][rs], qg[h][rs]], axis=0), before[h][ci])
            vnew.append(u[h][rs] - ws[:CHUNK])
            inter.append(ws[CHUNK:])
        o = jnp.concatenate(inter, axis=0) + _bdot(qk[h], jnp.concatenate(vnew, axis=0))
        sl = slice(h * B_DK, (h + 1) * B_DK)
        o_ref[:, sl] = (_rms(o, og_ref[...]) * _silu(z_ref[:, sl].astype(F32))).astype(o_ref.dtype)


def _gated_deltanet(bqkv, bz, gp, conv_w, a_log, dt_bias, out_gain, lp):
    rows = bqkv.shape[0]
    n = ROW_TILE
    head_vec = lambda p: jnp.zeros((1, LANES), F32).at[0, B_HEADS:2 * B_HEADS].set(p.astype(F32))
    row_spec = lambda w: pl.BlockSpec((n, w), lambda i: (i, 0))
    return pl.pallas_call(
        functools.partial(_gdn_kernel, steps_per_batch=lp // n),
        grid=(rows // n,),
        in_specs=[row_spec(3 * B_W), row_spec(B_W), row_spec(LANES), _const_spec((CONV_K, 3 * B_W)),
                  _const_spec((1, LANES)), _const_spec((1, LANES)), _const_spec((1, B_DK))],
        out_specs=row_spec(B_W),
        out_shape=jax.ShapeDtypeStruct((rows, B_W), BF16),
        scratch_shapes=[pltpu.VMEM((n + 8, 3 * B_W), F32), pltpu.VMEM((B_HEADS, B_DK, B_DK), F32)],
        compiler_params=pltpu.CompilerParams(dimension_semantics=("arbitrary",),
                                             vmem_limit_bytes=VMEM_LIMIT),
        name="gated_deltanet",
    )(bqkv, bz, gp, conv_w.astype(F32), head_vec(a_log), head_vec(dt_bias), out_gain[None, :].astype(F32))


FF_COLS = 256


def _even_out_kernel(h_ref, oa_ref, ob_ref, woa_ref, wob_ref, g_ref, w1_ref, w3_ref, w2_ref, o_ref,
                     act_ref, *, tm, lp, seq):
    h1 = (h_ref[...] + jnp.dot(oa_ref[...], woa_ref[...], preferred_element_type=F32)
          + jnp.dot(ob_ref[...], wob_ref[...], preferred_element_type=F32))
    xn = _rms(h1, g_ref[...]).astype(BF16)
    for c in range(D_FF // FF_COLS):
        cs = slice(c * FF_COLS, (c + 1) * FF_COLS)
        a = jnp.dot(xn, w1_ref[:, cs], preferred_element_type=F32)
        b = jnp.dot(xn, w3_ref[:, cs], preferred_element_type=F32)
        act_ref[:, cs] = (_silu(a) * b).astype(BF16)
    out = h1 + jnp.dot(act_ref[...], w2_ref[...], preferred_element_type=F32)
    o_ref[...] = jnp.where(_row_valid(tm, lp, seq), out, 0.0)


def _even_out(h, oa, ob, w_out, gain, w1, w3, w2, lp, seq, tm):
    rows = h.shape[0]
    wo = w_out.astype(BF16)
    row_spec = lambda w: pl.BlockSpec((tm, w), lambda i: (i, 0))
    return pl.pallas_call(
        functools.partial(_even_out_kernel, tm=tm, lp=lp, seq=seq),
        grid=(rows // tm,),
        in_specs=[row_spec(D_MODEL), row_spec(A_W), row_spec(B_W),
                  _const_spec((A_W, D_MODEL)), _const_spec((B_W, D_MODEL)), _const_spec((1, D_MODEL)),
                  _const_spec(w1.shape), _const_spec(w3.shape), _const_spec(w2.shape)],
        out_specs=row_spec(D_MODEL),
        out_shape=jax.ShapeDtypeStruct((rows, D_MODEL), F32),
        scratch_shapes=[pltpu.VMEM((tm, D_FF), BF16)],
        compiler_params=pltpu.CompilerParams(dimension_semantics=("arbitrary",),
                                             vmem_limit_bytes=VMEM_LIMIT),
        name="even_out_ffn",
    )(h, oa, ob, wo[:A_W], wo[A_W:], gain[None, :].astype(F32),
      w1.astype(BF16), w3.astype(BF16), w2.astype(BF16))


def _odd_in_kernel(h_ref, g_ref, wq_ref, wf_ref, wi_ref, wz_ref, lb_ref,
                   q_ref, k_ref, lf_ref, i_ref, z_ref, *, tm, lp, seq):
    xn = _rms(h_ref[...], g_ref[...]).astype(BF16)
    valid = _row_valid(tm, lp, seq)
    q_ref[...] = jnp.dot(xn, wq_ref[...], preferred_element_type=F32).astype(BF16)
    f = jnp.dot(xn, wf_ref[...], preferred_element_type=F32)
    lb = lb_ref[...]
    fg = lb + (1.0 - lb) * _sigmoid(f)
    k_ref[...] = jnp.where(valid, 1.0 - fg, 0.0).astype(BF16)
    lf_ref[...] = jnp.where(valid, jnp.log(fg), 0.0)
    i_ref[...] = jnp.dot(xn, wi_ref[...], preferred_element_type=F32).astype(BF16)
    z_ref[...] = jnp.dot(xn, wz_ref[...], preferred_element_type=F32).astype(BF16)


def _odd_in(h, gain, w_in, lb, lp, seq, tm):
    rows = h.shape[0]
    wb16 = w_in.astype(BF16)
    ws = [wb16[:, j * D_MODEL:(j + 1) * D_MODEL] for j in range(4)]
    row_spec = pl.BlockSpec((tm, D_MODEL), lambda i: (i, 0))
    wspec = _const_spec((D_MODEL, D_MODEL))
    vec = _const_spec((1, D_MODEL))
    sd = lambda dt: jax.ShapeDtypeStruct((rows, D_MODEL), dt)
    return pl.pallas_call(
        functools.partial(_odd_in_kernel, tm=tm, lp=lp, seq=seq),
        grid=(rows // tm,),
        in_specs=[row_spec, vec, wspec, wspec, wspec, wspec, vec],
        out_specs=(row_spec,) * 5,
        out_shape=(sd(BF16), sd(BF16), sd(F32), sd(BF16), sd(BF16)),
        compiler_params=pltpu.CompilerParams(dimension_semantics=("arbitrary",),
                                             vmem_limit_bytes=VMEM_LIMIT),
        name="odd_in_proj",
    )(h, gain[None, :].astype(F32), *ws, lb[None, :].astype(F32))


SUB = 8
HALVES = (32, 16, 8)


def _hgrn_kernel(q_ref, k_ref, lf_ref, i_ref, z_ref, og_ref, sel_ref, o_ref, s_ref, p_ref, kf_ref, gf_ref,
                 *, steps_per_batch):
    n = ROW_TILE
    step = pl.program_id(0) % steps_per_batch

    @pl.when(step == 0)
    def _():
        s_ref[...] = jnp.zeros_like(s_ref)

    r, c, same = _chunk_masks(n)
    incl = same & (r >= c)
    gc_all = _mask_dot(incl, lf_ref[...]) * LOG2E
    rrow = lax.broadcasted_iota(jnp.int32, (n, 1), 0)
    inner = rrow % SUB
    level_mask = [(r // (2 * half) == c // (2 * half)).astype(F32) for half in HALVES]
    diag_mask = (r // SUB == c // SUB).astype(F32)

    kf_ref[...] = k_ref[...].astype(F32)
    gf_ref[...] = gc_all

    def block_rows(ref, sl, size, j):
        return jnp.concatenate([jnp.broadcast_to(ref[pl.ds(b * size + j, 1), sl], (size, C_DK))
                                for b in range(n // size)], axis=0)

    def products(h):
        sl = slice(h * C_DK, (h + 1) * C_DK)
        q = q_ref[:, sl].astype(F32)
        k = kf_ref[:, sl]
        gc = gc_all[:, sl]
        gl = block_rows(gf_ref, sl, CHUNK, CHUNK - 1)
        qg = q * jnp.exp2(gc)
        kdec = k * jnp.exp2(gl - gc)
        parts = []
        for half in HALVES:
            gref = block_rows(gf_ref, sl, 2 * half, half - 1)
            upper = (rrow % (2 * half)) >= half
            qt = q * jnp.exp2(jnp.where(upper, gc - gref, -jnp.inf))
            kt = k * jnp.exp2(jnp.where(upper, -jnp.inf, gref - gc))
            parts.append(_bdot_nt(qt, kt))
        pbuf = p_ref.at[h % 2]
        for j in range(SUB):
            kj = block_rows(kf_ref, sl, SUB, j)
            gj = block_rows(gf_ref, sl, SUB, j)
            pbuf[:, j * C_DK:(j + 1) * C_DK] = (
                q * kj * jnp.exp2(jnp.where(inner >= j, gc - gj, -jnp.inf))).astype(BF16)
        diag = jnp.dot(pbuf[...], sel_ref[...], preferred_element_type=F32)
        return parts, diag, qg, kdec, gl

    def finish(h, parts, diag, qg, kdec, gl):
        sl = slice(h * C_DK, (h + 1) * C_DK)
        iv = i_ref[:, sl]
        att = diag_mask * diag
        for mask, part in zip(level_mask, parts):
            att = att + mask * part
        st = s_ref[h]
        inter = []
        for ci in range(n // CHUNK):
            rs = slice(ci * CHUNK, (ci + 1) * CHUNK)
            inter.append(_bdot_nt(qg[rs], st))
            st = st * jnp.exp2(gl[ci * CHUNK:ci * CHUNK + 1, :]) + _bdot_tn(iv[rs], kdec[rs])
        s_ref[h] = st
        o = jnp.concatenate(inter, axis=0) + _bdot(att, iv)
        o_ref[:, sl] = (_rms(o, og_ref[...]) * _silu(z_ref[:, sl].astype(F32))).astype(o_ref.dtype)

    for h in range(C_HEADS):
        finish(h, *products(h))


def _hgrn2(q, k, lf, iv, z, out_gain, lp):
    rows = q.shape[0]
    n = ROW_TILE
    row_spec = pl.BlockSpec((n, D_MODEL), lambda i: (i, 0))
    sel = ((jnp.arange(SUB * C_DK)[:, None] // C_DK) == (jnp.arange(n)[None, :] % SUB)).astype(BF16)
    return pl.pallas_call(
        functools.partial(_hgrn_kernel, steps_per_batch=lp // n),
        grid=(rows // n,),
        in_specs=[row_spec] * 5 + [_const_spec((1, C_DK)), _const_spec(sel.shape)],
        out_specs=row_spec,
        out_shape=jax.ShapeDtypeStruct((rows, D_MODEL), BF16),
        scratch_shapes=[pltpu.VMEM((C_HEADS, C_DK, C_DK), F32), pltpu.VMEM((2, n, SUB * C_DK), BF16),
                        pltpu.VMEM((n, D_MODEL), F32), pltpu.VMEM((n, D_MODEL), F32)],
        compiler_params=pltpu.CompilerParams(dimension_semantics=("arbitrary",),
                                             vmem_limit_bytes=VMEM_LIMIT),
        name="hgrn2",
    )(q, k, lf, iv, z, out_gain[None, :].astype(F32), sel)


def _odd_out_kernel(h_ref, o_ref, wo_ref, g_ref, wr_ref, h3_ref, xn_ref, route_ref, *, tm, lp, seq):
    h3 = h_ref[...] + jnp.dot(o_ref[...], wo_ref[...], preferred_element_type=F32)
    h3 = jnp.where(_row_valid(tm, lp, seq), h3, 0.0)
    h3_ref[...] = h3
    xn = _rms(h3, g_ref[...])
    xn_ref[...] = xn
    logits = jnp.dot(xn.astype(BF16), wr_ref[...], preferred_element_type=F32)
    lane = lax.broadcasted_iota(jnp.int32, logits.shape, 1)
    logits = jnp.where(lane < N_EXPERTS, logits, -jnp.inf)
    v1 = jnp.max(logits, axis=-1, keepdims=True)
    e1 = jnp.min(jnp.where(logits == v1, lane, LANES), axis=-1, keepdims=True)
    rest = jnp.where(lane == e1, -jnp.inf, logits)
    v2 = jnp.max(rest, axis=-1, keepdims=True)
    e2 = jnp.min(jnp.where(rest == v2, lane, LANES), axis=-1, keepdims=True)
    t = jnp.exp(v2 - v1)
    g1 = 1.0 / (1.0 + t)
    g2 = t * g1
    route_ref[...] = jnp.where(lane == 0, e1.astype(F32),
                               jnp.where(lane == 1, e2.astype(F32),
                                         jnp.where(lane == 2, g1, jnp.where(lane == 3, g2, 0.0))))


def _odd_out(h, o, w_out, gain, router, lp, seq, tm):
    rows = h.shape[0]
    wr = jnp.pad(router.astype(BF16), ((0, 0), (0, LANES - N_EXPERTS)))
    row_spec = lambda w: pl.BlockSpec((tm, w), lambda i: (i, 0))
    return pl.pallas_call(
        functools.partial(_odd_out_kernel, tm=tm, lp=lp, seq=seq),
        grid=(rows // tm,),
        in_specs=[row_spec(D_MODEL), row_spec(D_MODEL), _const_spec((D_MODEL, D_MODEL)),
                  _const_spec((1, D_MODEL)), _const_spec((D_MODEL, LANES))],
        out_specs=(row_spec(D_MODEL), row_spec(D_MODEL), row_spec(LANES)),
        out_shape=(jax.ShapeDtypeStruct((rows, D_MODEL), F32), jax.ShapeDtypeStruct((rows, D_MODEL), F32),
                   jax.ShapeDtypeStruct((rows, LANES), F32)),
        compiler_params=pltpu.CompilerParams(dimension_semantics=("arbitrary",),
                                             vmem_limit_bytes=VMEM_LIMIT),
        name="odd_out_route",
    )(h, o, w_out.astype(BF16), gain[None, :].astype(F32), wr)


MOE_TILE = 512
MOE_FF_STEPS = 2


def _moe_plan(route, tile):
    n_tok = route.shape[0]
    flat_e = route[:, :2].astype(jnp.int32).reshape(-1)
    na = flat_e.shape[0]
    counts = jnp.sum(flat_e[:, None] == jnp.arange(N_EXPERTS, dtype=jnp.int32)[None, :], axis=0,
                     dtype=jnp.int32)
    order = jnp.argsort(flat_e, stable=True).astype(jnp.int32)
    first = jnp.cumsum(counts) - counts
    padded = (counts + tile - 1) // tile * tile
    pend = jnp.cumsum(padded)
    n_tiles = -(-(na + N_EXPERTS * (tile - 1)) // tile) + 2
    n_rows = n_tiles * tile
    tile_e = jnp.minimum(jnp.searchsorted(pend, jnp.arange(n_tiles, dtype=jnp.int32) * tile, side='right'),
                         N_EXPERTS - 1).astype(jnp.int32)
    row = jnp.arange(n_rows, dtype=jnp.int32)
    row_e = jnp.repeat(tile_e, tile)
    local = row - (pend - padded)[row_e]
    real = local < counts[row_e]
    pair = order[jnp.minimum(first[row_e] + local, na - 1)]
    row_tok = jnp.where(real, pair // 2, 0)
    dump = na + ((row // tile) % 2) * tile + row % tile
    row_dst = jnp.where(real, (pair % 2) * n_tok + pair // 2, dump)
    n_used = (pend[-1] // tile).astype(jnp.int32).reshape(1)
    return row_tok, row_dst, tile_e, n_used, n_tiles


def _moe_ffn_kernel(te_ref, nused_ref, tok0_ref, tokn_ref, dstp_ref, x_hbm, w1_ref, w3_ref, w2_ref,
                    y_hbm, xbuf, xb16, ybuf, gsem, ssem):
    i, f = pl.program_id(0), pl.program_id(1)
    tile = xb16.shape[0]
    per_step = tile // MOE_FF_STEPS
    slot = i % 2
    n_used = nused_ref[0]
    n_out = y_hbm.shape[0] - 2 * tile

    def gather_start(tok_ref, row, dst_slot):
        pltpu.make_async_copy(x_hbm.at[pl.ds(tok_ref[0, 0, row], 1), :],
                              xbuf.at[dst_slot, pl.ds(row, 1), :], gsem.at[dst_slot]).start()

    def scatter_start(row, src_slot):
        pltpu.make_async_copy(ybuf.at[src_slot, pl.ds(row, 1), :],
                              y_hbm.at[pl.ds(dstp_ref[0, 0, row], 1), :], ssem.at[src_slot]).start()

    @pl.when((i == 0) & (f == 0))
    def _():
        ybuf[...] = jnp.zeros(ybuf.shape, F32)
        for s in range(2):
            dump = pltpu.make_async_copy(ybuf.at[s], y_hbm.at[pl.ds(n_out + s * tile, tile), :], ssem.at[s])
            dump.start()
            dump.wait()

        def issue(r, carry):
            gather_start(tok0_ref, r, 0)
            return carry
        lax.fori_loop(0, tile, issue, 0)

    @pl.when((f == 0) & (i >= 2) & (i - 2 < n_used))
    def _():
        pltpu.make_async_copy(ybuf.at[slot], y_hbm.at[pl.ds(0, tile), :], ssem.at[slot]).wait()

    @pl.when((f == 0) & (i <= n_used))
    def _():
        pltpu.make_async_copy(x_hbm.at[pl.ds(0, tile), :], xbuf.at[slot], gsem.at[slot]).wait()
        xb16[...] = xbuf[slot].astype(BF16)
        ybuf[slot] = jnp.zeros((tile, D_MODEL), F32)

    def compute(scatter_previous):
        for r in range(per_step):
            gather_start(tokn_ref, f * per_step + r, 1 - slot)
            if scatter_previous:
                scatter_start(f * per_step + r, 1 - slot)
        x = xb16[...]
        a = jnp.dot(x, w1_ref[0], preferred_element_type=F32)
        b = jnp.dot(x, w3_ref[0], preferred_element_type=F32)
        ybuf[slot] += jnp.dot((_silu(a) * b).astype(BF16), w2_ref[0], preferred_element_type=F32)

    @pl.when(i == 0)
    def _():
        compute(False)

    @pl.when((i > 0) & (i < n_used))
    def _():
        compute(True)

    @pl.when((i == n_used) & (f == 0))
    def _():
        def issue(r, carry):
            scatter_start(r, 1 - slot)
            return carry
        lax.fori_loop(0, tile, issue, 0)


def _moe_ffn(xn, w1, w3, w2, row_tok, row_dst, tile_e, n_used, n_tiles, tile):
    n_tok = xn.shape[0]
    steps = MOE_FF_STEPS
    ff = D_FF_EXPERT // steps
    fi = lambda i, f, nu: jnp.where(i < nu[0], f, steps - 1)
    tok_spec = lambda imap: pl.BlockSpec((1, 1, tile), imap, memory_space=pltpu.SMEM)
    grid_spec = pltpu.PrefetchScalarGridSpec(
        num_scalar_prefetch=2,
        grid=(n_tiles, steps),
        in_specs=[tok_spec(lambda i, f, te, nu: (0, 0, 0)),
                  tok_spec(lambda i, f, te, nu: (jnp.minimum(i + 1, n_tiles - 1), 0, 0)),
                  tok_spec(lambda i, f, te, nu: (jnp.maximum(i - 1, 0), 0, 0)),
                  pl.BlockSpec(memory_space=pl.ANY),
                  pl.BlockSpec((1, D_MODEL, ff), lambda i, f, te, nu: (te[i], 0, fi(i, f, nu))),
                  pl.BlockSpec((1, D_MODEL, ff), lambda i, f, te, nu: (te[i], 0, fi(i, f, nu))),
                  pl.BlockSpec((1, ff, D_MODEL), lambda i, f, te, nu: (te[i], fi(i, f, nu), 0))],
        out_specs=pl.BlockSpec(memory_space=pl.ANY),
        scratch_shapes=[pltpu.VMEM((2, tile, D_MODEL), F32), pltpu.VMEM((tile, D_MODEL), BF16),
                        pltpu.VMEM((2, tile, D_MODEL), F32),
                        pltpu.SemaphoreType.DMA((2,)), pltpu.SemaphoreType.DMA((2,))],
    )
    tok3 = row_tok.reshape(n_tiles, 1, tile)
    return pl.pallas_call(
        _moe_ffn_kernel,
        grid_spec=grid_spec,
        out_shape=jax.ShapeDtypeStruct((2 * n_tok + 2 * tile, D_MODEL), F32),
        compiler_params=pltpu.CompilerParams(dimension_semantics=("arbitrary", "arbitrary"),
                                             vmem_limit_bytes=VMEM_LIMIT),
        name="moe_ffn",
    )(tile_e, n_used, tok3, tok3, row_dst.reshape(n_tiles, 1, tile), xn, w1, w3, w2)


def _moe_combine_kernel(h_ref, y0_ref, y1_ref, route_ref, o_ref):
    g = route_ref[...]
    o_ref[0] = h_ref[...] + g[:, 2:3] * y0_ref[...] + g[:, 3:4] * y1_ref[...]


def _moe_combine(h3, y, route, batch, seq, lp, tc):
    n_tok = h3.shape[0]
    frame_rows = lambda width, off: pl.BlockSpec(
        (pl.Element(tc), pl.Element(width)),
        lambda b, j: (pl.multiple_of(off + b * lp + CHUNK + j * tc, CHUNK), 0))
    return pl.pallas_call(
        _moe_combine_kernel,
        grid=(batch, seq // tc),
        in_specs=[frame_rows(D_MODEL, 0), frame_rows(D_MODEL, 0), frame_rows(D_MODEL, n_tok),
                  frame_rows(LANES, 0)],
        out_specs=pl.BlockSpec((1, tc, D_MODEL), lambda b, j: (b, j, 0)),
        out_shape=jax.ShapeDtypeStruct((batch, seq, D_MODEL), F32),
        compiler_params=pltpu.CompilerParams(dimension_semantics=("arbitrary", "arbitrary"),
                                             vmem_limit_bytes=VMEM_LIMIT),
        name="moe_combine",
    )(h3, y, y, route)


def _layout_rows(seq):
    return -(-(CHUNK + seq) // ROW_TILE) * ROW_TILE


def _proj_tile(lp):
    return next(t for t in (640, 512, 384, ROW_TILE) if lp % t == 0)


def kernel(x, meta_tokens, lb_logits, mix_norm, ffn_norm, e_w_in, a_q_gain, a_k_gain, a_lam_q1, a_lam_k1, a_lam_q2, a_lam_k2, a_sub_gain, b_conv, b_a_log, b_dt_bias, b_out_gain, e_w_out, ffn_w1, ffn_w3, ffn_w2, o_w_in, c_out_gain, o_w_out, router, moe_w1, moe_w3, moe_w2):
    batch, seq, _ = x.shape
    lp = _layout_rows(seq)
    rows = batch * lp
    tm = _proj_tile(lp)
    meta = jnp.broadcast_to(meta_tokens[None].astype(x.dtype), (batch, N_META, D_MODEL))
    h = jnp.concatenate([jnp.zeros((batch, FRONT, D_MODEL), x.dtype), meta, x,
                         jnp.zeros((batch, lp - CHUNK - seq, D_MODEL), x.dtype)], axis=1).reshape(rows, D_MODEL)

    q, k, v, bqkv, bz, gp = _even_in(h, mix_norm[0], e_w_in[0], a_q_gain[0], a_k_gain[0], lp, tm)
    lam_init = 0.8 - 0.6 * math.exp(-0.3 * 0)
    lam = (jnp.exp(jnp.sum(a_lam_q1[0].astype(F32) * a_lam_k1[0].astype(F32)))
           - jnp.exp(jnp.sum(a_lam_q2[0].astype(F32) * a_lam_k2[0].astype(F32))) + lam_init)
    o_a = _diff_attention(q, k, v, a_sub_gain[0], lam, lam_init, batch, lp, ROW_TILE)
    o_b = _gated_deltanet(bqkv, bz, gp, b_conv[0], b_a_log[0], b_dt_bias[0], b_out_gain[0], lp)
    h = _even_out(h, o_a, o_b, e_w_out[0], ffn_norm[0], ffn_w1[0], ffn_w3[0], ffn_w2[0], lp, seq, tm)

    lb_all = jnp.cumsum(jax.nn.softmax(lb_logits.astype(F32), axis=0), axis=0)
    lb = (lb_all - lb_all[0])[1]
    cq, ck, clf, ci, cz = _odd_in(h, mix_norm[1], o_w_in[0], lb, lp, seq, tm)
    o_c = _hgrn2(cq, ck, clf, ci, cz, c_out_gain[0], lp)
    h3, xn, route = _odd_out(h, o_c, o_w_out[0], ffn_norm[1], router[0], lp, seq, tm)
    row_tok, row_dst, tile_e, n_used, n_tiles = _moe_plan(route, MOE_TILE)
    y = _moe_ffn(xn, moe_w1[0].astype(BF16), moe_w3[0].astype(BF16), moe_w2[0].astype(BF16),
                 row_tok, row_dst, tile_e, n_used, n_tiles, MOE_TILE)
    return _moe_combine(h3, y, route, batch, seq, lp, ROW_TILE)
```

```python
import functools
import math

import jax
import jax.numpy as jnp
from jax import lax
from jax.experimental import pallas as pl
from jax.experimental.pallas import tpu as pltpu

D_MODEL = 1024
CHUNK = 64
N_META = 16
FRONT = CHUNK - N_META
EPS = 1e-6
ROPE_THETA = 500000.0
LOG2E = math.log2(math.e)

A_HEADS = 4
A_DK = 64
A_DV = 128
A_ROT = A_DK // 4
A_W = A_HEADS * A_DV

B_HEADS = 4
B_DK = 128
B_W = B_HEADS * B_DK
CONV_K = 4

C_HEADS = 8
C_DK = 128

D_FF = 2816
N_EXPERTS = 8
D_FF_EXPERT = 3584

LANES = 128
ROW_TILE = 256
VMEM_LIMIT = 56 * 1024 * 1024

BF16 = jnp.bfloat16
F32 = jnp.float32


def _const_spec(shape):
    return pl.BlockSpec(shape, lambda *_: (0,) * len(shape), pipeline_mode=pl.Buffered(1))


def _rms(x, gain):
    return x * lax.rsqrt(jnp.mean(x * x, axis=-1, keepdims=True) + EPS) * gain


def _silu(x):
    return x * (1.0 / (1.0 + jnp.exp(-x)))


def _sigmoid(x):
    return 1.0 / (1.0 + jnp.exp(-x))


def _row_valid(tile_rows, lp, seq):
    r = pl.program_id(0) * tile_rows + lax.broadcasted_iota(jnp.int32, (tile_rows, 1), 0)
    p = r % lp
    return (p >= FRONT) & (p < CHUNK + seq)


def _head_norm_rope(a, ones_bd, gain, cos, sin_lo, sin_hi):
    ssq = jnp.dot((a * a).astype(BF16), ones_bd, preferred_element_type=F32)
    y = a * lax.rsqrt(ssq * (1.0 / A_DK) + EPS) * gain
    up = pltpu.roll(y, A_W - A_ROT // 2, axis=1)
    dn = pltpu.roll(y, A_ROT // 2, axis=1)
    return y * cos + up * sin_lo + dn * sin_hi


def _even_in_kernel(h_ref, g_ref, wq_ref, wk_ref, wv_ref, wb_ref, wz_ref, wg_ref, bd_ref,
                    qg_ref, kg_ref, cos_ref, slo_ref, shi_ref,
                    q_ref, k_ref, v_ref, b_ref, z_ref, gp_ref):
    xn = _rms(h_ref[...], g_ref[...]).astype(BF16)
    rep = lambda t: jnp.concatenate([t] * (A_W // LANES), axis=1)
    cos, slo, shi = rep(cos_ref[...]), rep(slo_ref[...]), rep(shi_ref[...])
    bd = bd_ref[...]
    aq = jnp.dot(xn, wq_ref[...], preferred_element_type=F32)
    q_ref[...] = (_head_norm_rope(aq, bd, qg_ref[...], cos, slo, shi) * (A_DK ** -0.5 * LOG2E)).astype(BF16)
    ak = jnp.dot(xn, wk_ref[...], preferred_element_type=F32)
    k_ref[...] = _head_norm_rope(ak, bd, kg_ref[...], cos, slo, shi).astype(BF16)
    v_ref[...] = jnp.dot(xn, wv_ref[...], preferred_element_type=F32).astype(BF16)
    b_ref[...] = jnp.dot(xn, wb_ref[...], preferred_element_type=F32).astype(BF16)
    z_ref[...] = jnp.dot(xn, wz_ref[...], preferred_element_type=F32).astype(BF16)
    gp_ref[...] = jnp.dot(xn, wg_ref[...], preferred_element_type=F32)


def _even_in(h, gain, w_in, q_gain, k_gain, lp, tm):
    rows = h.shape[0]
    wb16 = w_in.astype(BF16)
    c0, c1, c2, c3, c4 = A_W, 2 * A_W, 3 * A_W, 3 * A_W + 3 * B_W, 3 * A_W + 4 * B_W
    wq, wk, wv, wb, wz = wb16[:, :c0], wb16[:, c0:c1], wb16[:, c1:c2], wb16[:, c2:c3], wb16[:, c3:c4]
    wg = jnp.pad(wb16[:, c4:], ((0, 0), (0, LANES - 2 * B_HEADS)))
    d = jnp.arange(A_W)
    ones_bd = (d[:, None] // A_DK == d[None, :] // A_DK).astype(BF16)
    half = A_ROT // 2
    inv_freq = ROPE_THETA ** (-jnp.arange(half, dtype=F32) * 2.0 / A_ROT)
    pos = (jnp.arange(lp) - FRONT).astype(F32)
    dim = jnp.arange(LANES) % A_DK
    ang = pos[:, None] * inv_freq[dim % half][None, :]
    cos_t = jnp.where(dim < A_ROT, jnp.cos(ang), 1.0)
    slo_t = jnp.where(dim < half, -jnp.sin(ang), 0.0)
    shi_t = jnp.where((dim >= half) & (dim < A_ROT), jnp.sin(ang), 0.0)
    tile8 = lambda g: jnp.tile(g.astype(F32), 2 * A_HEADS)[None, :]
    n_pos = lp // tm
    row_spec = lambda w: pl.BlockSpec((tm, w), lambda i: (i, 0))
    pos_spec = pl.BlockSpec((tm, LANES), lambda i: (i % n_pos, 0))
    outs = pl.pallas_call(
        _even_in_kernel,
        grid=(rows // tm,),
        in_specs=[row_spec(D_MODEL), _const_spec((1, D_MODEL)),
                  _const_spec(wq.shape), _const_spec(wk.shape), _const_spec(wv.shape),
                  _const_spec(wb.shape), _const_spec(wz.shape), _const_spec(wg.shape),
                  _const_spec(ones_bd.shape), _const_spec((1, A_W)), _const_spec((1, A_W)),
                  pos_spec, pos_spec, pos_spec],
        out_specs=(row_spec(A_W), row_spec(A_W), row_spec(A_W), row_spec(3 * B_W), row_spec(B_W),
                   row_spec(LANES)),
        out_shape=(jax.ShapeDtypeStruct((rows, A_W), BF16), jax.ShapeDtypeStruct((rows, A_W), BF16),
                   jax.ShapeDtypeStruct((rows, A_W), BF16), jax.ShapeDtypeStruct((rows, 3 * B_W), BF16),
                   jax.ShapeDtypeStruct((rows, B_W), BF16), jax.ShapeDtypeStruct((rows, LANES), F32)),
        compiler_params=pltpu.CompilerParams(dimension_semantics=("arbitrary",),
                                             vmem_limit_bytes=VMEM_LIMIT),
        name="even_in_proj",
    )(h, gain[None, :].astype(F32), wq, wk, wv, wb, wz, wg, ones_bd, tile8(q_gain), tile8(k_gain),
      cos_t, slo_t, shi_t)
    return outs


ATTN_WIDE = 1024


ATTN_HEADS = 2


def _attn_kernel(q_ref, k_ref, v_ref, sg_ref, lam_ref, o_ref, qs_ref, vx_ref, m_ref, acc_ref,
                 sa_ref, sb_ref, *, tq, width, lam_init):
    qi = pl.program_id(2)
    heads = range(ATTN_HEADS)
    lanes = lambda s: slice(s * LANES, (s + 1) * LANES)

    @pl.when(qi == 0)
    def _():
        for s in heads:
            vx_ref[s, :, :A_DV] = v_ref[:, lanes(s)]
            vx_ref[s, :, A_DV:] = jnp.ones((vx_ref.shape[1], A_DV), BF16)

    lane = lax.broadcasted_iota(jnp.int32, (tq, LANES), 1)
    zero = jnp.zeros((tq, LANES), BF16)
    for s in heads:
        q = q_ref[:, lanes(s)]
        qs_ref[s, :tq, :] = jnp.where(lane < A_DK, q, zero)
        qs_ref[s, tq:, :] = jnp.where(lane >= A_DK, q, zero)
    m_ref[...] = jnp.full(m_ref.shape, -jnp.inf, F32)
    acc_ref[...] = jnp.zeros(acc_ref.shape, F32)

    lp = k_ref.shape[0]
    qchunk = (qi * tq + lax.broadcasted_iota(jnp.int32, (2 * tq, 1), 0) % tq) // CHUNK

    def window(t):
        return pl.multiple_of(jnp.minimum(t * width, lp - width), tq)

    def scores(s_ref, t):
        for s in heads:
            s_ref[s] = lax.dot_general(qs_ref[s], k_ref[pl.ds(window(t), width), lanes(s)],
                                       (((1,), (1,)), ((), ())), preferred_element_type=F32)

    def softmax_pv(s_ref, t, mask):
        start = window(t)
        kpos = start + lax.broadcasted_iota(jnp.int32, (1, width), 1)
        kchunk = jnp.where((kpos >= t * width) & (kpos >= FRONT), kpos // CHUNK, lp)
        nblk = width // LANES

        def block(s, c):
            cs = slice(c * LANES, (c + 1) * LANES)
            if mask == "full":
                return jnp.where(kchunk[:, cs] <= qchunk, s_ref[s, :, cs], -jnp.inf)
            if mask == "pad" and c == 0:
                return jnp.where(kpos[:, cs] >= FRONT, s_ref[s, :, cs], -jnp.inf)
            return s_ref[s, :, cs]

        for s in heads:
            mx = functools.reduce(jnp.maximum, [block(s, c) for c in range(nblk)])
            m_old = m_ref[s]
            m_new = jnp.maximum(m_old, jnp.max(mx, axis=-1, keepdims=True))
            alpha = jnp.exp2(m_old - m_new)
            p = jnp.concatenate([jnp.exp2((block(s, c) - m_new).astype(BF16)) for c in range(nblk)], axis=1)
            pv = jnp.dot(p, vx_ref[s, pl.ds(start, width), :], preferred_element_type=F32)
            acc_ref[s, :, :A_DV] = alpha * acc_ref[s, :, :A_DV] + pv[:, :A_DV]
            acc_ref[s, :, A_DV:] = alpha * acc_ref[s, :, A_DV:] + pv[:, A_DV:]
            m_ref[s] = m_new

    n_tiles = (qi * tq + tq + width - 1) // width
    n_pairs = (n_tiles + 1) // 2

    def pair(j, masks):
        scores(sb_ref, 2 * j + 1)
        softmax_pv(sa_ref, 2 * j, masks[0])
        scores(sa_ref, 2 * j + 2)
        softmax_pv(sb_ref, 2 * j + 1, masks[1])

    scores(sa_ref, 0)

    @pl.when(n_pairs > 1)
    def _():
        pair(0, ("pad", "none"))

    def inner(j, carry):
        pair(j, ("none", "none"))
        return carry
    lax.fori_loop(1, n_pairs - 1, inner, 0)

    @pl.when(n_tiles == 2 * n_pairs)
    def _():
        scores(sb_ref, 2 * n_pairs - 1)
        softmax_pv(sa_ref, 2 * n_pairs - 2, "full")
        softmax_pv(sb_ref, 2 * n_pairs - 1, "full")

    @pl.when(n_tiles < 2 * n_pairs)
    def _():
        softmax_pv(sa_ref, 2 * n_pairs - 2, "full")

    for s in heads:
        o = acc_ref[s, :, :A_DV] / acc_ref[s, :, A_DV:]
        o = o[:tq] - lam_ref[...] * o[tq:]
        o_ref[:, lanes(s)] = (_rms(o, sg_ref[...]) * (1.0 - lam_init)).astype(o_ref.dtype)


def _diff_attention(q, k, v, sub_gain, lam, lam_init, batch, lp, tq):
    nq = lp // tq
    hw = ATTN_HEADS * LANES
    kv_spec = pl.BlockSpec((lp, hw), lambda b, h, i: (b, h), pipeline_mode=pl.Buffered(1))
    q_spec = pl.BlockSpec((tq, hw), lambda b, h, i: (b * nq + i, h))
    vec = pl.BlockSpec((1, LANES), lambda b, h, i: (0, 0))
    width = min(ATTN_WIDE, lp)
    return pl.pallas_call(
        functools.partial(_attn_kernel, tq=tq, width=width, lam_init=lam_init),
        grid=(batch, A_HEADS // ATTN_HEADS, nq),
        in_specs=[q_spec, kv_spec, kv_spec, vec, vec],
        out_specs=q_spec,
        out_shape=jax.ShapeDtypeStruct(q.shape, BF16),
        scratch_shapes=[pltpu.VMEM((ATTN_HEADS, 2 * tq, LANES), BF16),
                        pltpu.VMEM((ATTN_HEADS, lp, 2 * A_DV), BF16),
                        pltpu.VMEM((ATTN_HEADS, 2 * tq, LANES), F32),
                        pltpu.VMEM((ATTN_HEADS, 2 * tq, 2 * A_DV), F32),
                        pltpu.VMEM((ATTN_HEADS, 2 * tq, width), F32),
                        pltpu.VMEM((ATTN_HEADS, 2 * tq, width), F32)],
        compiler_params=pltpu.CompilerParams(dimension_semantics=("arbitrary",) * 3,
                                             vmem_limit_bytes=VMEM_LIMIT),
        name="diff_attention",
    )(q, k, v, sub_gain[None, :].astype(F32), jnp.full((1, LANES), lam, F32))


def _chunk_masks(n):
    r = lax.broadcasted_iota(jnp.int32, (n, n), 0)
    c = lax.broadcasted_iota(jnp.int32, (n, n), 1)
    same = (r // CHUNK) == (c // CHUNK)
    return r, c, same


def _mask_dot(mask, x):
    hi = x.astype(BF16)
    rest = x - hi.astype(F32)
    mid = rest.astype(BF16)
    lo = (rest - mid.astype(F32)).astype(BF16)
    m = mask.astype(BF16)
    return jnp.dot(jnp.concatenate([m, m, m], axis=1), jnp.concatenate([hi, mid, lo], axis=0),
                   preferred_element_type=F32)


def _bdot(a, b):
    return jnp.dot(a.astype(BF16), b.astype(BF16), preferred_element_type=F32)


def _bdot_nt(a, b):
    return lax.dot_general(a.astype(BF16), b.astype(BF16), (((1,), (1,)), ((), ())),
                           preferred_element_type=F32)


def _bdot_tn(a, b):
    return lax.dot_general(a.astype(BF16), b.astype(BF16), (((0,), (0,)), ((), ())),
                           preferred_element_type=F32)


def _gdn_kernel(x_ref, z_ref, gp_ref, cw_ref, alog_ref, dtb_ref, og_ref, o_ref,
                xbuf_ref, s_ref, *, steps_per_batch):
    n = ROW_TILE
    step = pl.program_id(0) % steps_per_batch

    @pl.when(step == 0)
    def _():
        xbuf_ref[:8, :] = jnp.zeros((8, 3 * B_W), F32)
        s_ref[...] = jnp.zeros_like(s_ref)

    xbuf_ref[8:, :] = x_ref[...].astype(F32)
    conv = cw_ref[CONV_K - 1:CONV_K, :] * xbuf_ref[8:, :]
    for j in range(CONV_K - 1):
        conv = conv + cw_ref[j:j + 1, :] * xbuf_ref[5 + j:5 + j + n, :]
    xbuf_ref[:8, :] = xbuf_ref[n:n + 8, :]
    conv = _silu(conv)

    r, c, same = _chunk_masks(n)
    incl = same & (r >= c)
    strict = same & (r > c)
    eye = (r == c).astype(F32)
    gp = gp_ref[...]
    beta_all = _sigmoid(gp)
    gpre = gp + dtb_ref[...]
    softplus = jnp.maximum(gpre, 0.0) + jnp.log(1.0 + jnp.exp(-jnp.abs(gpre)))
    row = step * n + lax.broadcasted_iota(jnp.int32, (n, 1), 0)
    g_all = jnp.where(row >= FRONT, -jnp.exp(alog_ref[...]) * softplus, 0.0)
    sums = _mask_dot(jnp.concatenate([incl, same], axis=0), g_all)
    gc_all, gl_all = sums[:n], sums[n:]
    gc_rows = gc_all.T

    heads = range(B_HEADS)
    chunks = [slice(ci * CHUNK, (ci + 1) * CHUNK) for ci in range(n // CHUNK)]
    p, t, rhs, qk, qg, kdec, decay = [], [], [], [], [], [], []
    for h in heads:
        qh = conv[:, h * B_DK:(h + 1) * B_DK]
        kh = conv[:, B_W + h * B_DK:B_W + (h + 1) * B_DK]
        vh = conv[:, 2 * B_W + h * B_DK:2 * B_W + (h + 1) * B_DK]
        qh = qh * lax.rsqrt(jnp.sum(qh * qh, axis=-1, keepdims=True) + EPS) * (B_DK ** -0.5)
        kh = kh * lax.rsqrt(jnp.sum(kh * kh, axis=-1, keepdims=True) + EPS)
        beta = beta_all[:, h:h + 1]
        gcol = gc_all[:, B_HEADS + h:B_HEADS + h + 1]
        glcol = gl_all[:, B_HEADS + h:B_HEADS + h + 1]
        grow = gc_rows[B_HEADS + h:B_HEADS + h + 1, :]
        dec = jnp.exp(jnp.where(incl, gcol - grow, -jnp.inf))
        kb = kh * beta
        egc = jnp.exp(gcol)
        p0 = -(_bdot_nt(kb, kh) * jnp.where(strict, dec, 0.0))
        p.append(p0)
        t.append(eye + p0)
        rhs.append(jnp.concatenate([vh * beta, kb * egc], axis=1))
        qk.append(_bdot_nt(qh, kh) * dec)
        qg.append(qh * egc)
        kdec.append(kh * jnp.exp(glcol - gcol))
        decay.append([jnp.exp(glcol[rs.stop - 1:rs.stop, :]) for rs in chunks])
    for _ in range(5):
        for h in heads:
            p[h] = _bdot(p[h], p[h])
            t[h] = t[h] + _bdot(t[h], p[h])
    sol = [_bdot(t[h], rhs[h]) for h in heads]
    u = [s[:, :B_DK] for s in sol]
    w = [s[:, B_DK:] for s in sol]
    kw = [[_bdot_tn(kdec[h][rs], w[h][rs]) for rs in chunks] for h in heads]
    ku = [[_bdot_tn(kdec[h][rs], u[h][rs]) for rs in chunks] for h in heads]
    state = [s_ref[h] for h in heads]
    before = [[] for _ in heads]
    for ci in range(len(chunks)):
        for h in heads:
            before[h].append(state[h])
            state[h] = state[h] * decay[h][ci] - _bdot(kw[h][ci], state[h]) + ku[h][ci]
    for h in heads:
        s_ref[h] = state[h]
        vnew, inter = [], []
        for ci, rs in enumerate(chunks):
            ws = _bdot(jnp.concatenate([w[h][rs], qg[h][rs]], axis=0), before[h][ci])
            vnew.append(u[h][rs] - ws[:CHUNK])
            inter.append(ws[CHUNK:])
        o = jnp.concatenate(inter, axis=0) + _bdot(qk[h], jnp.concatenate(vnew, axis=0))
        sl = slice(h * B_DK, (h + 1) * B_DK)
        o_ref[:, sl] = (_rms(o, og_ref[...]) * _silu(z_ref[:, sl].astype(F32))).astype(o_ref.dtype)


def _gated_deltanet(bqkv, bz, gp, conv_w, a_log, dt_bias, out_gain, lp):
    rows = bqkv.shape[0]
    n = ROW_TILE
    head_vec = lambda p: jnp.zeros((1, LANES), F32).at[0, B_HEADS:2 * B_HEADS].set(p.astype(F32))
    row_spec = lambda w: pl.BlockSpec((n, w), lambda i: (i, 0))
    return pl.pallas_call(
        functools.partial(_gdn_kernel, steps_per_batch=lp // n),
        grid=(rows // n,),
        in_specs=[row_spec(3 * B_W), row_spec(B_W), row_spec(LANES), _const_spec((CONV_K, 3 * B_W)),
                  _const_spec((1, LANES)), _const_spec((1, LANES)), _const_spec((1, B_DK))],
        out_specs=row_spec(B_W),
        out_shape=jax.ShapeDtypeStruct((rows, B_W), BF16),
        scratch_shapes=[pltpu.VMEM((n + 8, 3 * B_W), F32), pltpu.VMEM((B_HEADS, B_DK, B_DK), F32)],
        compiler_params=pltpu.CompilerParams(dimension_semantics=("arbitrary",),
                                             vmem_limit_bytes=VMEM_LIMIT),
        name="gated_deltanet",
    )(bqkv, bz, gp, conv_w.astype(F32), head_vec(a_log), head_vec(dt_bias), out_gain[None, :].astype(F32))


FF_COLS = 256


def _even_out_kernel(h_ref, oa_ref, ob_ref, woa_ref, wob_ref, g_ref, w1_ref, w3_ref, w2_ref, o_ref,
                     act_ref, *, tm, lp, seq):
    h1 = (h_ref[...] + jnp.dot(oa_ref[...], woa_ref[...], preferred_element_type=F32)
          + jnp.dot(ob_ref[...], wob_ref[...], preferred_element_type=F32))
    xn = _rms(h1, g_ref[...]).astype(BF16)
    for c in range(D_FF // FF_COLS):
        cs = slice(c * FF_COLS, (c + 1) * FF_COLS)
        a = jnp.dot(xn, w1_ref[:, cs], preferred_element_type=F32)
        b = jnp.dot(xn, w3_ref[:, cs], preferred_element_type=F32)
        act_ref[:, cs] = (_silu(a) * b).astype(BF16)
    out = h1 + jnp.dot(act_ref[...], w2_ref[...], preferred_element_type=F32)
    o_ref[...] = jnp.where(_row_valid(tm, lp, seq), out, 0.0)


def _even_out(h, oa, ob, w_out, gain, w1, w3, w2, lp, seq, tm):
    rows = h.shape[0]
    wo = w_out.astype(BF16)
    row_spec = lambda w: pl.BlockSpec((tm, w), lambda i: (i, 0))
    return pl.pallas_call(
        functools.partial(_even_out_kernel, tm=tm, lp=lp, seq=seq),
        grid=(rows // tm,),
        in_specs=[row_spec(D_MODEL), row_spec(A_W), row_spec(B_W),
                  _const_spec((A_W, D_MODEL)), _const_spec((B_W, D_MODEL)), _const_spec((1, D_MODEL)),
                  _const_spec(w1.shape), _const_spec(w3.shape), _const_spec(w2.shape)],
        out_specs=row_spec(D_MODEL),
        out_shape=jax.ShapeDtypeStruct((rows, D_MODEL), F32),
        scratch_shapes=[pltpu.VMEM((tm, D_FF), BF16)],
        compiler_params=pltpu.CompilerParams(dimension_semantics=("arbitrary",),
                                             vmem_limit_bytes=VMEM_LIMIT),
        name="even_out_ffn",
    )(h, oa, ob, wo[:A_W], wo[A_W:], gain[None, :].astype(F32),
      w1.astype(BF16), w3.astype(BF16), w2.astype(BF16))


def _odd_in_kernel(h_ref, g_ref, wq_ref, wf_ref, wi_ref, wz_ref, lb_ref,
                   q_ref, k_ref, lf_ref, i_ref, z_ref, *, tm, lp, seq):
    xn = _rms(h_ref[...], g_ref[...]).astype(BF16)
    valid = _row_valid(tm, lp, seq)
    q_ref[...] = jnp.dot(xn, wq_ref[...], preferred_element_type=F32).astype(BF16)
    f = jnp.dot(xn, wf_ref[...], preferred_element_type=F32)
    lb = lb_ref[...]
    fg = lb + (1.0 - lb) * _sigmoid(f)
    k_ref[...] = jnp.where(valid, 1.0 - fg, 0.0).astype(BF16)
    lf_ref[...] = jnp.where(valid, jnp.log(fg), 0.0)
    i_ref[...] = jnp.dot(xn, wi_ref[...], preferred_element_type=F32).astype(BF16)
    z_ref[...] = jnp.dot(xn, wz_ref[...], preferred_element_type=F32).astype(BF16)


def _odd_in(h, gain, w_in, lb, lp, seq, tm):
    rows = h.shape[0]
    wb16 = w_in.astype(BF16)
    ws = [wb16[:, j * D_MODEL:(j + 1) * D_MODEL] for j in range(4)]
    row_spec = pl.BlockSpec((tm, D_MODEL), lambda i: (i, 0))
    wspec = _const_spec((D_MODEL, D_MODEL))
    vec = _const_spec((1, D_MODEL))
    sd = lambda dt: jax.ShapeDtypeStruct((rows, D_MODEL), dt)
    return pl.pallas_call(
        functools.partial(_odd_in_kernel, tm=tm, lp=lp, seq=seq),
        grid=(rows // tm,),
        in_specs=[row_spec, vec, wspec, wspec, wspec, wspec, vec],
        out_specs=(row_spec,) * 5,
        out_shape=(sd(BF16), sd(BF16), sd(F32), sd(BF16), sd(BF16)),
        compiler_params=pltpu.CompilerParams(dimension_semantics=("arbitrary",),
                                             vmem_limit_bytes=VMEM_LIMIT),
        name="odd_in_proj",
    )(h, gain[None, :].astype(F32), *ws, lb[None, :].astype(F32))


SUB = 8
HALVES = (32, 16, 8)


def _hgrn_kernel(q_ref, k_ref, lf_ref, i_ref, z_ref, og_ref, sel_ref, o_ref, s_ref, p_ref, kf_ref, gf_ref,
                 *, steps_per_batch):
    n = ROW_TILE
    step = pl.program_id(0) % steps_per_batch

    @pl.when(step == 0)
    def _():
        s_ref[...] = jnp.zeros_like(s_ref)

    r, c, same = _chunk_masks(n)
    incl = same & (r >= c)
    gc_all = _mask_dot(incl, lf_ref[...]) * LOG2E
    rrow = lax.broadcasted_iota(jnp.int32, (n, 1), 0)
    inner = rrow % SUB
    level_mask = [(r // (2 * half) == c // (2 * half)).astype(F32) for half in HALVES]
    diag_mask = (r // SUB == c // SUB).astype(F32)

    kf_ref[...] = k_ref[...].astype(F32)
    gf_ref[...] = gc_all

    def block_rows(ref, sl, size, j):
        return jnp.concatenate([jnp.broadcast_to(ref[pl.ds(b * size + j, 1), sl], (size, C_DK))
                                for b in range(n // size)], axis=0)

    def products(h):
        sl = slice(h * C_DK, (h + 1) * C_DK)
        q = q_ref[:, sl].astype(F32)
        k = kf_ref[:, sl]
        gc = gc_all[:, sl]
        gl = block_rows(gf_ref, sl, CHUNK, CHUNK - 1)
        qg = q * jnp.exp2(gc)
        kdec = k * jnp.exp2(gl - gc)
        parts = []
        for half in HALVES:
            gref = block_rows(gf_ref, sl, 2 * half, half - 1)
            upper = (rrow % (2 * half)) >= half
            qt = q * jnp.exp2(jnp.where(upper, gc - gref, -jnp.inf))
            kt = k * jnp.exp2(jnp.where(upper, -jnp.inf, gref - gc))
            parts.append(_bdot_nt(qt, kt))
        pbuf = p_ref.at[h % 2]
        for j in range(SUB):
            kj = block_rows(kf_ref, sl, SUB, j)
            gj = block_rows(gf_ref, sl, SUB, j)
            pbuf[:, j * C_DK:(j + 1) * C_DK] = (
                q * kj * jnp.exp2(jnp.where(inner >= j, gc - gj, -jnp.inf))).astype(BF16)
        diag = jnp.dot(pbuf[...], sel_ref[...], preferred_element_type=F32)
        return parts, diag, qg, kdec, gl

    def finish(h, parts, diag, qg, kdec, gl):
        sl = slice(h * C_DK, (h + 1) * C_DK)
        iv = i_ref[:, sl]
        att = diag_mask * diag
        for mask, part in zip(level_mask, parts):
            att = att + mask * part
        st = s_ref[h]
        inter = []
        for ci in range(n // CHUNK):
            rs = slice(ci * CHUNK, (ci + 1) * CHUNK)
            inter.append(_bdot_nt(qg[rs], st))
            st = st * jnp.exp2(gl[ci * CHUNK:ci * CHUNK + 1, :]) + _bdot_tn(iv[rs], kdec[rs])
        s_ref[h] = st
        o = jnp.concatenate(inter, axis=0) + _bdot(att, iv)
        o_ref[:, sl] = (_rms(o, og_ref[...]) * _silu(z_ref[:, sl].astype(F32))).astype(o_ref.dtype)

    for h in range(C_HEADS):
        finish(h, *products(h))


def _hgrn2(q, k, lf, iv, z, out_gain, lp):
    rows = q.shape[0]
    n = ROW_TILE
    row_spec = pl.BlockSpec((n, D_MODEL), lambda i: (i, 0))
    sel = ((jnp.arange(SUB * C_DK)[:, None] // C_DK) == (jnp.arange(n)[None, :] % SUB)).astype(BF16)
    return pl.pallas_call(
        functools.partial(_hgrn_kernel, steps_per_batch=lp // n),
        grid=(rows // n,),
        in_specs=[row_spec] * 5 + [_const_spec((1, C_DK)), _const_spec(sel.shape)],
        out_specs=row_spec,
        out_shape=jax.ShapeDtypeStruct((rows, D_MODEL), BF16),
        scratch_shapes=[pltpu.VMEM((C_HEADS, C_DK, C_DK), F32), pltpu.VMEM((2, n, SUB * C_DK), BF16),
                        pltpu.VMEM((n, D_MODEL), F32), pltpu.VMEM((n, D_MODEL), F32)],
        compiler_params=pltpu.CompilerParams(dimension_semantics=("arbitrary",),
                                             vmem_limit_bytes=VMEM_LIMIT),
        name="hgrn2",
    )(q, k, lf, iv, z, out_gain[None, :].astype(F32), sel)


def _odd_out_kernel(h_ref, o_ref, wo_ref, g_ref, wr_ref, h3_ref, xn_ref, route_ref, *, tm, lp, seq):
    h3 = h_ref[...] + jnp.dot(o_ref[...], wo_ref[...], preferred_element_type=F32)
    h3 = jnp.where(_row_valid(tm, lp, seq), h3, 0.0)
    h3_ref[...] = h3
    xn = _rms(h3, g_ref[...])
    xn_ref[...] = xn
    logits = jnp.dot(xn.astype(BF16), wr_ref[...], preferred_element_type=F32)
    lane = lax.broadcasted_iota(jnp.int32, logits.shape, 1)
    logits = jnp.where(lane < N_EXPERTS, logits, -jnp.inf)
    v1 = jnp.max(logits, axis=-1, keepdims=True)
    e1 = jnp.min(jnp.where(logits == v1, lane, LANES), axis=-1, keepdims=True)
    rest = jnp.where(lane == e1, -jnp.inf, logits)
    v2 = jnp.max(rest, axis=-1, keepdims=True)
    e2 = jnp.min(jnp.where(rest == v2, lane, LANES), axis=-1, keepdims=True)
    t = jnp.exp(v2 - v1)
    g1 = 1.0 / (1.0 + t)
    g2 = t * g1
    route_ref[...] = jnp.where(lane == 0, e1.astype(F32),
                               jnp.where(lane == 1, e2.astype(F32),
                                         jnp.where(lane == 2, g1, jnp.where(lane == 3, g2, 0.0))))


def _odd_out(h, o, w_out, gain, router, lp, seq, tm):
    rows = h.shape[0]
    wr = jnp.pad(router.astype(BF16), ((0, 0), (0, LANES - N_EXPERTS)))
    row_spec = lambda w: pl.BlockSpec((tm, w), lambda i: (i, 0))
    return pl.pallas_call(
        functools.partial(_odd_out_kernel, tm=tm, lp=lp, seq=seq),
        grid=(rows // tm,),
        in_specs=[row_spec(D_MODEL), row_spec(D_MODEL), _const_spec((D_MODEL, D_MODEL)),
                  _const_spec((1, D_MODEL)), _const_spec((D_MODEL, LANES))],
        out_specs=(row_spec(D_MODEL), row_spec(D_MODEL), row_spec(LANES)),
        out_shape=(jax.ShapeDtypeStruct((rows, D_MODEL), F32), jax.ShapeDtypeStruct((rows, D_MODEL), F32),
                   jax.ShapeDtypeStruct((rows, LANES), F32)),
        compiler_params=pltpu.CompilerParams(dimension_semantics=("arbitrary",),
                                             vmem_limit_bytes=VMEM_LIMIT),
        name="odd_out_route",
    )(h, o, w_out.astype(BF16), gain[None, :].astype(F32), wr)


MOE_TILE = 512
MOE_FF_STEPS = 2


def _moe_plan(route, tile):
    n_tok = route.shape[0]
    flat_e = route[:, :2].astype(jnp.int32).reshape(-1)
    na = flat_e.shape[0]
    counts = jnp.sum(flat_e[:, None] == jnp.arange(N_EXPERTS, dtype=jnp.int32)[None, :], axis=0,
                     dtype=jnp.int32)
    order = jnp.argsort(flat_e, stable=True).astype(jnp.int32)
    first = jnp.cumsum(counts) - counts
    padded = (counts + tile - 1) // tile * tile
    pend = jnp.cumsum(padded)
    n_tiles = -(-(na + N_EXPERTS * (tile - 1)) // tile) + 2
    n_rows = n_tiles * tile
    tile_e = jnp.minimum(jnp.searchsorted(pend, jnp.arange(n_tiles, dtype=jnp.int32) * tile, side='right'),
                         N_EXPERTS - 1).astype(jnp.int32)
    row = jnp.arange(n_rows, dtype=jnp.int32)
    row_e = jnp.repeat(tile_e, tile)
    local = row - (pend - padded)[row_e]
    real = local < counts[row_e]
    pair = order[jnp.minimum(first[row_e] + local, na - 1)]
    row_tok = jnp.where(real, pair // 2, 0)
    dump = na + ((row // tile) % 2) * tile + row % tile
    row_dst = jnp.where(real, (pair % 2) * n_tok + pair // 2, dump)
    n_used = (pend[-1] // tile).astype(jnp.int32).reshape(1)
    return row_tok, row_dst, tile_e, n_used, n_tiles


def _moe_ffn_kernel(te_ref, nused_ref, tok0_ref, tokn_ref, dstp_ref, x_hbm, w1_ref, w3_ref, w2_ref,
                    y_hbm, xbuf, xb16, ybuf, gsem, ssem):
    i, f = pl.program_id(0), pl.program_id(1)
    tile = xb16.shape[0]
    per_step = tile // MOE_FF_STEPS
    slot = i % 2
    n_used = nused_ref[0]
    n_out = y_hbm.shape[0] - 2 * tile

    def gather_start(tok_ref, row, dst_slot):
        pltpu.make_async_copy(x_hbm.at[pl.ds(tok_ref[0, 0, row], 1), :],
                              xbuf.at[dst_slot, pl.ds(row, 1), :], gsem.at[dst_slot]).start()

    def scatter_start(row, src_slot):
        pltpu.make_async_copy(ybuf.at[src_slot, pl.ds(row, 1), :],
                              y_hbm.at[pl.ds(dstp_ref[0, 0, row], 1), :], ssem.at[src_slot]).start()

    @pl.when((i == 0) & (f == 0))
    def _():
        ybuf[...] = jnp.zeros(ybuf.shape, F32)
        for s in range(2):
            dump = pltpu.make_async_copy(ybuf.at[s], y_hbm.at[pl.ds(n_out + s * tile, tile), :], ssem.at[s])
            dump.start()
            dump.wait()

        def issue(r, carry):
            gather_start(tok0_ref, r, 0)
            return carry
        lax.fori_loop(0, tile, issue, 0)

    @pl.when((f == 0) & (i >= 2) & (i - 2 < n_used))
    def _():
        pltpu.make_async_copy(ybuf.at[slot], y_hbm.at[pl.ds(0, tile), :], ssem.at[slot]).wait()

    @pl.when((f == 0) & (i <= n_used))
    def _():
        pltpu.make_async_copy(x_hbm.at[pl.ds(0, tile), :], xbuf.at[slot], gsem.at[slot]).wait()
        xb16[...] = xbuf[slot].astype(BF16)
        ybuf[slot] = jnp.zeros((tile, D_MODEL), F32)

    def compute(scatter_previous, step, par):
        for r in range(per_step):
            gather_start(tokn_ref, step * per_step + r, 1 - par)
            if scatter_previous:
                scatter_start(step * per_step + r, 1 - par)
        x = xb16[...]
        a = jnp.dot(x, w1_ref[0], preferred_element_type=F32)
        b = jnp.dot(x, w3_ref[0], preferred_element_type=F32)
        ybuf[par] += jnp.dot((_silu(a) * b).astype(BF16), w2_ref[0], preferred_element_type=F32)

    for step in range(MOE_FF_STEPS):
        @pl.when((i == 0) & (f == step))
        def _():
            compute(False, step, 0)

        for par in range(2):
            @pl.when((i > 0) & (i < n_used) & (f == step) & (slot == par))
            def _():
                compute(True, step, par)

    @pl.when((i == n_used) & (f == 0))
    def _():
        def issue(r, carry):
            scatter_start(r, 1 - slot)
            return carry
        lax.fori_loop(0, tile, issue, 0)


def _moe_ffn(xn, w1, w3, w2, row_tok, row_dst, tile_e, n_used, n_tiles, tile):
    n_tok = xn.shape[0]
    steps = MOE_FF_STEPS
    ff = D_FF_EXPERT // steps
    fi = lambda i, f, nu: jnp.where(i < nu[0], f, steps - 1)
    tok_spec = lambda imap: pl.BlockSpec((1, 1, tile), imap, memory_space=pltpu.SMEM)
    grid_spec = pltpu.PrefetchScalarGridSpec(
        num_scalar_prefetch=2,
        grid=(n_tiles, steps),
        in_specs=[tok_spec(lambda i, f, te, nu: (0, 0, 0)),
                  tok_spec(lambda i, f, te, nu: (jnp.minimum(i + 1, n_tiles - 1), 0, 0)),
                  tok_spec(lambda i, f, te, nu: (jnp.maximum(i - 1, 0), 0, 0)),
                  pl.BlockSpec(memory_space=pl.ANY),
                  pl.BlockSpec((1, D_MODEL, ff), lambda i, f, te, nu: (te[i], 0, fi(i, f, nu))),
                  pl.BlockSpec((1, D_MODEL, ff), lambda i, f, te, nu: (te[i], 0, fi(i, f, nu))),
                  pl.BlockSpec((1, ff, D_MODEL), lambda i, f, te, nu: (te[i], fi(i, f, nu), 0))],
        out_specs=pl.BlockSpec(memory_space=pl.ANY),
        scratch_shapes=[pltpu.VMEM((2, tile, D_MODEL), F32), pltpu.VMEM((tile, D_MODEL), BF16),
                        pltpu.VMEM((2, tile, D_MODEL), F32),
                        pltpu.SemaphoreType.DMA((2,)), pltpu.SemaphoreType.DMA((2,))],
    )
    tok3 = row_tok.reshape(n_tiles, 1, tile)
    return pl.pallas_call(
        _moe_ffn_kernel,
        grid_spec=grid_spec,
        out_shape=jax.ShapeDtypeStruct((2 * n_tok + 2 * tile, D_MODEL), F32),
        compiler_params=pltpu.CompilerParams(dimension_semantics=("arbitrary", "arbitrary"),
                                             vmem_limit_bytes=VMEM_LIMIT),
        name="moe_ffn",
    )(tile_e, n_used, tok3, tok3, row_dst.reshape(n_tiles, 1, tile), xn, w1, w3, w2)


def _moe_combine_kernel(h_ref, y0_ref, y1_ref, route_ref, o_ref):
    g = route_ref[...]
    o_ref[0] = h_ref[...] + g[:, 2:3] * y0_ref[...] + g[:, 3:4] * y1_ref[...]


def _moe_combine(h3, y, route, batch, seq, lp, tc):
    n_tok = h3.shape[0]
    frame_rows = lambda width, off: pl.BlockSpec(
        (pl.Element(tc), pl.Element(width)),
        lambda b, j: (pl.multiple_of(off + b * lp + CHUNK + j * tc, CHUNK), 0))
    return pl.pallas_call(
        _moe_combine_kernel,
        grid=(batch, seq // tc),
        in_specs=[frame_rows(D_MODEL, 0), frame_rows(D_MODEL, 0), frame_rows(D_MODEL, n_tok),
                  frame_rows(LANES, 0)],
        out_specs=pl.BlockSpec((1, tc, D_MODEL), lambda b, j: (b, j, 0)),
        out_shape=jax.ShapeDtypeStruct((batch, seq, D_MODEL), F32),
        compiler_params=pltpu.CompilerParams(dimension_semantics=("arbitrary", "arbitrary"),
                                             vmem_limit_bytes=VMEM_LIMIT),
        name="moe_combine",
    )(h3, y, y, route)


def _layout_rows(seq):
    return -(-(CHUNK + seq) // ROW_TILE) * ROW_TILE


def _proj_tile(lp):
    return next(t for t in (640, 512, 384, ROW_TILE) if lp % t == 0)


def kernel(x, meta_tokens, lb_logits, mix_norm, ffn_norm, e_w_in, a_q_gain, a_k_gain, a_lam_q1, a_lam_k1, a_lam_q2, a_lam_k2, a_sub_gain, b_conv, b_a_log, b_dt_bias, b_out_gain, e_w_out, ffn_w1, ffn_w3, ffn_w2, o_w_in, c_out_gain, o_w_out, router, moe_w1, moe_w3, moe_w2):
    batch, seq, _ = x.shape
    lp = _layout_rows(seq)
    rows = batch * lp
    tm = _proj_tile(lp)
    meta = jnp.broadcast_to(meta_tokens[None].astype(x.dtype), (batch, N_META, D_MODEL))
    h = jnp.concatenate([jnp.zeros((batch, FRONT, D_MODEL), x.dtype), meta, x,
                         jnp.zeros((batch, lp - CHUNK - seq, D_MODEL), x.dtype)], axis=1).reshape(rows, D_MODEL)

    q, k, v, bqkv, bz, gp = _even_in(h, mix_norm[0], e_w_in[0], a_q_gain[0], a_k_gain[0], lp, tm)
    lam_init = 0.8 - 0.6 * math.exp(-0.3 * 0)
    lam = (jnp.exp(jnp.sum(a_lam_q1[0].astype(F32) * a_lam_k1[0].astype(F32)))
           - jnp.exp(jnp.sum(a_lam_q2[0].astype(F32) * a_lam_k2[0].astype(F32))) + lam_init)
    o_a = _diff_attention(q, k, v, a_sub_gain[0], lam, lam_init, batch, lp, ROW_TILE)
    o_b = _gated_deltanet(bqkv, bz, gp, b_conv[0], b_a_log[0], b_dt_bias[0], b_out_gain[0], lp)
    h = _even_out(h, o_a, o_b, e_w_out[0], ffn_norm[0], ffn_w1[0], ffn_w3[0], ffn_w2[0], lp, seq, tm)

    lb_all = jnp.cumsum(jax.nn.softmax(lb_logits.astype(F32), axis=0), axis=0)
    lb = (lb_all - lb_all[0])[1]
    cq, ck, clf, ci, cz = _odd_in(h, mix_norm[1], o_w_in[0], lb, lp, seq, tm)
    o_c = _hgrn2(cq, ck, clf, ci, cz, c_out_gain[0], lp)
    h3, xn, route = _odd_out(h, o_c, o_w_out[0], ffn_norm[1], router[0], lp, seq, tm)
    row_tok, row_dst, tile_e, n_used, n_tiles = _moe_plan(route, MOE_TILE)
    y = _moe_ffn(xn, moe_w1[0].astype(BF16), moe_w3[0].astype(BF16), moe_w2[0].astype(BF16),
                 row_tok, row_dst, tile_e, n_used, n_tiles, MOE_TILE)
    return _moe_combine(h3, y, route, batch, seq, lp, ROW_TILE)
```

```python
import functools
import math

import jax
import jax.numpy as jnp
from jax import lax
from jax.experimental import pallas as pl
from jax.experimental.pallas import tpu as pltpu

D_MODEL = 1024
CHUNK = 64
N_META = 16
FRONT = CHUNK - N_META
EPS = 1e-6
ROPE_THETA = 500000.0
LOG2E = math.log2(math.e)

A_HEADS = 4
A_DK = 64
A_DV = 128
A_ROT = A_DK // 4
A_W = A_HEADS * A_DV

B_HEADS = 4
B_DK = 128
B_W = B_HEADS * B_DK
CONV_K = 4

C_HEADS = 8
C_DK = 128

D_FF = 2816
N_EXPERTS = 8
D_FF_EXPERT = 3584

LANES = 128
ROW_TILE = 256
VMEM_LIMIT = 56 * 1024 * 1024

BF16 = jnp.bfloat16
F32 = jnp.float32


def _const_spec(shape):
    return pl.BlockSpec(shape, lambda *_: (0,) * len(shape), pipeline_mode=pl.Buffered(1))


def _rms(x, gain):
    return x * lax.rsqrt(jnp.mean(x * x, axis=-1, keepdims=True) + EPS) * gain


def _silu(x):
    return x * (1.0 / (1.0 + jnp.exp(-x)))


def _sigmoid(x):
    return 1.0 / (1.0 + jnp.exp(-x))


def _row_valid(tile_rows, lp, seq):
    r = pl.program_id(0) * tile_rows + lax.broadcasted_iota(jnp.int32, (tile_rows, 1), 0)
    p = r % lp
    return (p >= FRONT) & (p < CHUNK + seq)


def _head_norm_rope(a, ones_bd, gain, cos, sin_lo, sin_hi):
    ssq = jnp.dot((a * a).astype(BF16), ones_bd, preferred_element_type=F32)
    y = a * lax.rsqrt(ssq * (1.0 / A_DK) + EPS) * gain
    up = pltpu.roll(y, A_W - A_ROT // 2, axis=1)
    dn = pltpu.roll(y, A_ROT // 2, axis=1)
    return y * cos + up * sin_lo + dn * sin_hi


def _even_in_kernel(h_ref, g_ref, wq_ref, wk_ref, wv_ref, wb_ref, wz_ref, wg_ref, bd_ref,
                    qg_ref, kg_ref, cos_ref, slo_ref, shi_ref,
                    q_ref, k_ref, v_ref, b_ref, z_ref, gp_ref):
    xn = _rms(h_ref[...], g_ref[...]).astype(BF16)
    rep = lambda t: jnp.concatenate([t] * (A_W // LANES), axis=1)
    cos, slo, shi = rep(cos_ref[...]), rep(slo_ref[...]), rep(shi_ref[...])
    bd = bd_ref[...]
    aq = jnp.dot(xn, wq_ref[...], preferred_element_type=F32)
    q_ref[...] = (_head_norm_rope(aq, bd, qg_ref[...], cos, slo, shi) * (A_DK ** -0.5 * LOG2E)).astype(BF16)
    ak = jnp.dot(xn, wk_ref[...], preferred_element_type=F32)
    k_ref[...] = _head_norm_rope(ak, bd, kg_ref[...], cos, slo, shi).astype(BF16)
    v_ref[...] = jnp.dot(xn, wv_ref[...], preferred_element_type=F32).astype(BF16)
    b_ref[...] = jnp.dot(xn, wb_ref[...], preferred_element_type=F32).astype(BF16)
    z_ref[...] = jnp.dot(xn, wz_ref[...], preferred_element_type=F32).astype(BF16)
    gp_ref[...] = jnp.dot(xn, wg_ref[...], preferred_element_type=F32)


def _even_in(h, gain, w_in, q_gain, k_gain, lp, tm):
    rows = h.shape[0]
    wb16 = w_in.astype(BF16)
    c0, c1, c2, c3, c4 = A_W, 2 * A_W, 3 * A_W, 3 * A_W + 3 * B_W, 3 * A_W + 4 * B_W
    wq, wk, wv, wb, wz = wb16[:, :c0], wb16[:, c0:c1], wb16[:, c1:c2], wb16[:, c2:c3], wb16[:, c3:c4]
    wg = jnp.pad(wb16[:, c4:], ((0, 0), (0, LANES - 2 * B_HEADS)))
    d = jnp.arange(A_W)
    ones_bd = (d[:, None] // A_DK == d[None, :] // A_DK).astype(BF16)
    half = A_ROT // 2
    inv_freq = ROPE_THETA ** (-jnp.arange(half, dtype=F32) * 2.0 / A_ROT)
    pos = (jnp.arange(lp) - FRONT).astype(F32)
    dim = jnp.arange(LANES) % A_DK
    ang = pos[:, None] * inv_freq[dim % half][None, :]
    cos_t = jnp.where(dim < A_ROT, jnp.cos(ang), 1.0)
    slo_t = jnp.where(dim < half, -jnp.sin(ang), 0.0)
    shi_t = jnp.where((dim >= half) & (dim < A_ROT), jnp.sin(ang), 0.0)
    tile8 = lambda g: jnp.tile(g.astype(F32), 2 * A_HEADS)[None, :]
    n_pos = lp // tm
    row_spec = lambda w: pl.BlockSpec((tm, w), lambda i: (i, 0))
    pos_spec = pl.BlockSpec((tm, LANES), lambda i: (i % n_pos, 0))
    outs = pl.pallas_call(
        _even_in_kernel,
        grid=(rows // tm,),
        in_specs=[row_spec(D_MODEL), _const_spec((1, D_MODEL)),
                  _const_spec(wq.shape), _const_spec(wk.shape), _const_spec(wv.shape),
                  _const_spec(wb.shape), _const_spec(wz.shape), _const_spec(wg.shape),
                  _const_spec(ones_bd.shape), _const_spec((1, A_W)), _const_spec((1, A_W)),
                  pos_spec, pos_spec, pos_spec],
        out_specs=(row_spec(A_W), row_spec(A_W), row_spec(A_W), row_spec(3 * B_W), row_spec(B_W),
                   row_spec(LANES)),
        out_shape=(jax.ShapeDtypeStruct((rows, A_W), BF16), jax.ShapeDtypeStruct((rows, A_W), BF16),
                   jax.ShapeDtypeStruct((rows, A_W), BF16), jax.ShapeDtypeStruct((rows, 3 * B_W), BF16),
                   jax.ShapeDtypeStruct((rows, B_W), BF16), jax.ShapeDtypeStruct((rows, LANES), F32)),
        compiler_params=pltpu.CompilerParams(dimension_semantics=("arbitrary",),
                                             vmem_limit_bytes=VMEM_LIMIT),
        name="even_in_proj",
    )(h, gain[None, :].astype(F32), wq, wk, wv, wb, wz, wg, ones_bd, tile8(q_gain), tile8(k_gain),
      cos_t, slo_t, shi_t)
    return outs


ATTN_WIDE = 1024


ATTN_HEADS = 2


def _attn_kernel(q_ref, k_ref, v_ref, sg_ref, lam_ref, o_ref, qs_ref, vx_ref, m_ref, acc_ref,
                 sa_ref, sb_ref, *, tq, width, lam_init):
    qi = pl.program_id(2)
    heads = range(ATTN_HEADS)
    lanes = lambda s: slice(s * LANES, (s + 1) * LANES)

    @pl.when(qi == 0)
    def _():
        for s in heads:
            vx_ref[s, :, :A_DV] = v_ref[:, lanes(s)]
            vx_ref[s, :, A_DV:] = jnp.ones((vx_ref.shape[1], A_DV), BF16)

    lane = lax.broadcasted_iota(jnp.int32, (tq, LANES), 1)
    zero = jnp.zeros((tq, LANES), BF16)
    for s in heads:
        q = q_ref[:, lanes(s)]
        qs_ref[s, :tq, :] = jnp.where(lane < A_DK, q, zero)
        qs_ref[s, tq:, :] = jnp.where(lane >= A_DK, q, zero)
    m_ref[...] = jnp.full(m_ref.shape, -jnp.inf, F32)
    acc_ref[...] = jnp.zeros(acc_ref.shape, F32)

    lp = k_ref.shape[0]
    qchunk = (qi * tq + lax.broadcasted_iota(jnp.int32, (2 * tq, 1), 0) % tq) // CHUNK

    def window(t):
        return pl.multiple_of(jnp.minimum(t * width, lp - width), tq)

    def scores(s_ref, t):
        for s in heads:
            s_ref[s] = lax.dot_general(qs_ref[s], k_ref[pl.ds(window(t), width), lanes(s)],
                                       (((1,), (1,)), ((), ())), preferred_element_type=F32)

    def softmax_pv(s_ref, t, mask):
        start = window(t)
        kpos = start + lax.broadcasted_iota(jnp.int32, (1, width), 1)
        kchunk = jnp.where((kpos >= t * width) & (kpos >= FRONT), kpos // CHUNK, lp)
        nblk = width // LANES

        def block(s, c):
            cs = slice(c * LANES, (c + 1) * LANES)
            if mask == "full":
                return jnp.where(kchunk[:, cs] <= qchunk, s_ref[s, :, cs], -jnp.inf)
            if mask == "pad" and c == 0:
                return jnp.where(kpos[:, cs] >= FRONT, s_ref[s, :, cs], -jnp.inf)
            return s_ref[s, :, cs]

        for s in heads:
            mx = functools.reduce(jnp.maximum, [block(s, c) for c in range(nblk)])
            m_old = m_ref[s]
            m_new = jnp.maximum(m_old, jnp.max(mx, axis=-1, keepdims=True))
            alpha = jnp.exp2(m_old - m_new)
            p = jnp.concatenate([jnp.exp2((block(s, c) - m_new).astype(BF16)) for c in range(nblk)], axis=1)
            pv = jnp.dot(p, vx_ref[s, pl.ds(start, width), :], preferred_element_type=F32)
            acc_ref[s, :, :A_DV] = alpha * acc_ref[s, :, :A_DV] + pv[:, :A_DV]
            acc_ref[s, :, A_DV:] = alpha * acc_ref[s, :, A_DV:] + pv[:, A_DV:]
            m_ref[s] = m_new

    n_tiles = (qi * tq + tq + width - 1) // width
    n_pairs = (n_tiles + 1) // 2

    def pair(j, masks):
        scores(sb_ref, 2 * j + 1)
        softmax_pv(sa_ref, 2 * j, masks[0])
        scores(sa_ref, 2 * j + 2)
        softmax_pv(sb_ref, 2 * j + 1, masks[1])

    scores(sa_ref, 0)

    @pl.when(n_pairs > 1)
    def _():
        pair(0, ("pad", "none"))

    def inner(j, carry):
        pair(j, ("none", "none"))
        return carry
    lax.fori_loop(1, n_pairs - 1, inner, 0)

    @pl.when(n_tiles == 2 * n_pairs)
    def _():
        scores(sb_ref, 2 * n_pairs - 1)
        softmax_pv(sa_ref, 2 * n_pairs - 2, "full")
        softmax_pv(sb_ref, 2 * n_pairs - 1, "full")

    @pl.when(n_tiles < 2 * n_pairs)
    def _():
        softmax_pv(sa_ref, 2 * n_pairs - 2, "full")

    for s in heads:
        o = acc_ref[s, :, :A_DV] / acc_ref[s, :, A_DV:]
        o = o[:tq] - lam_ref[...] * o[tq:]
        o_ref[:, lanes(s)] = (_rms(o, sg_ref[...]) * (1.0 - lam_init)).astype(o_ref.dtype)


def _diff_attention(q, k, v, sub_gain, lam, lam_init, batch, lp, tq):
    nq = lp // tq
    hw = ATTN_HEADS * LANES
    kv_spec = pl.BlockSpec((lp, hw), lambda b, h, i: (b, h), pipeline_mode=pl.Buffered(1))
    q_spec = pl.BlockSpec((tq, hw), lambda b, h, i: (b * nq + i, h))
    vec = pl.BlockSpec((1, LANES), lambda b, h, i: (0, 0))
    width = min(ATTN_WIDE, lp)
    return pl.pallas_call(
        functools.partial(_attn_kernel, tq=tq, width=width, lam_init=lam_init),
        grid=(batch, A_HEADS // ATTN_HEADS, nq),
        in_specs=[q_spec, kv_spec, kv_spec, vec, vec],
        out_specs=q_spec,
        out_shape=jax.ShapeDtypeStruct(q.shape, BF16),
        scratch_shapes=[pltpu.VMEM((ATTN_HEADS, 2 * tq, LANES), BF16),
                        pltpu.VMEM((ATTN_HEADS, lp, 2 * A_DV), BF16),
                        pltpu.VMEM((ATTN_HEADS, 2 * tq, LANES), F32),
                        pltpu.VMEM((ATTN_HEADS, 2 * tq, 2 * A_DV), F32),
                        pltpu.VMEM((ATTN_HEADS, 2 * tq, width), F32),
                        pltpu.VMEM((ATTN_HEADS, 2 * tq, width), F32)],
        compiler_params=pltpu.CompilerParams(dimension_semantics=("arbitrary",) * 3,
                                             vmem_limit_bytes=VMEM_LIMIT),
        name="diff_attention",
    )(q, k, v, sub_gain[None, :].astype(F32), jnp.full((1, LANES), lam, F32))


def _chunk_masks(n):
    r = lax.broadcasted_iota(jnp.int32, (n, n), 0)
    c = lax.broadcasted_iota(jnp.int32, (n, n), 1)
    same = (r // CHUNK) == (c // CHUNK)
    return r, c, same


def _mask_dot(mask, x):
    hi = x.astype(BF16)
    rest = x - hi.astype(F32)
    mid = rest.astype(BF16)
    lo = (rest - mid.astype(F32)).astype(BF16)
    m = mask.astype(BF16)
    return jnp.dot(jnp.concatenate([m, m, m], axis=1), jnp.concatenate([hi, mid, lo], axis=0),
                   preferred_element_type=F32)


def _bdot(a, b):
    return jnp.dot(a.astype(BF16), b.astype(BF16), preferred_element_type=F32)


def _bdot_nt(a, b):
    return lax.dot_general(a.astype(BF16), b.astype(BF16), (((1,), (1,)), ((), ())),
                           preferred_element_type=F32)


def _bdot_tn(a, b):
    return lax.dot_general(a.astype(BF16), b.astype(BF16), (((0,), (0,)), ((), ())),
                           preferred_element_type=F32)


def _gdn_kernel(x_ref, z_ref, gp_ref, cw_ref, alog_ref, dtb_ref, og_ref, o_ref,
                xbuf_ref, s_ref, *, steps_per_batch):
    n = ROW_TILE
    step = pl.program_id(0) % steps_per_batch

    @pl.when(step == 0)
    def _():
        xbuf_ref[:8, :] = jnp.zeros((8, 3 * B_W), F32)
        s_ref[...] = jnp.zeros_like(s_ref)

    xbuf_ref[8:, :] = x_ref[...].astype(F32)
    conv = cw_ref[CONV_K - 1:CONV_K, :] * xbuf_ref[8:, :]
    for j in range(CONV_K - 1):
        conv = conv + cw_ref[j:j + 1, :] * xbuf_ref[5 + j:5 + j + n, :]
    xbuf_ref[:8, :] = xbuf_ref[n:n + 8, :]
    conv = _silu(conv)

    r, c, same = _chunk_masks(n)
    incl = same & (r >= c)
    strict = same & (r > c)
    eye = (r == c).astype(F32)
    gp = gp_ref[...]
    beta_all = _sigmoid(gp)
    gpre = gp + dtb_ref[...]
    softplus = jnp.maximum(gpre, 0.0) + jnp.log(1.0 + jnp.exp(-jnp.abs(gpre)))
    row = step * n + lax.broadcasted_iota(jnp.int32, (n, 1), 0)
    g_all = jnp.where(row >= FRONT, -jnp.exp(alog_ref[...]) * softplus, 0.0)
    sums = _mask_dot(jnp.concatenate([incl, same], axis=0), g_all)
    gc_all, gl_all = sums[:n], sums[n:]
    gc_rows = gc_all.T

    heads = range(B_HEADS)
    chunks = [slice(ci * CHUNK, (ci + 1) * CHUNK) for ci in range(n // CHUNK)]
    p, t, rhs, qk, qg, kdec, decay = [], [], [], [], [], [], []
    for h in heads:
        qh = conv[:, h * B_DK:(h + 1) * B_DK]
        kh = conv[:, B_W + h * B_DK:B_W + (h + 1) * B_DK]
        vh = conv[:, 2 * B_W + h * B_DK:2 * B_W + (h + 1) * B_DK]
        qh = qh * lax.rsqrt(jnp.sum(qh * qh, axis=-1, keepdims=True) + EPS) * (B_DK ** -0.5)
        kh = kh * lax.rsqrt(jnp.sum(kh * kh, axis=-1, keepdims=True) + EPS)
        beta = beta_all[:, h:h + 1]
        gcol = gc_all[:, B_HEADS + h:B_HEADS + h + 1]
        glcol = gl_all[:, B_HEADS + h:B_HEADS + h + 1]
        grow = gc_rows[B_HEADS + h:B_HEADS + h + 1, :]
        dec = jnp.exp(jnp.where(incl, gcol - grow, -jnp.inf))
        kb = kh * beta
        egc = jnp.exp(gcol)
        p0 = -(_bdot_nt(kb, kh) * jnp.where(strict, dec, 0.0))
        p.append(p0)
        t.append(eye + p0)
        rhs.append(jnp.concatenate([vh * beta, kb * egc], axis=1))
        qk.append(_bdot_nt(qh, kh) * dec)
        qg.append(qh * egc)
        kdec.append(kh * jnp.exp(glcol - gcol))
        decay.append([jnp.exp(glcol[rs.stop - 1:rs.stop, :]) for rs in chunks])
    for _ in range(5):
        for h in heads:
            p[h] = _bdot(p[h], p[h])
            t[h] = t[h] + _bdot(t[h], p[h])
    sol = [_bdot(t[h], rhs[h]) for h in heads]
    u = [s[:, :B_DK] for s in sol]
    w = [s[:, B_DK:] for s in sol]
    kw = [[_bdot_tn(kdec[h][rs], w[h][rs]) for rs in chunks] for h in heads]
    ku = [[_bdot_tn(kdec[h][rs], u[h][rs]) for rs in chunks] for h in heads]
    state = [s_ref[h] for h in heads]
    before = [[] for _ in heads]
    for ci in range(len(chunks)):
        for h in heads:
            before[h].append(state[h])
            state[h] = state[h] * decay[h][ci] - _bdot(kw[h][ci], state[h]) + ku[h][ci]
    for h in heads:
        s_ref[h] = state[h]
        vnew, inter = [], []
        for ci, rs in enumerate(chunks):
            ws = _bdot(jnp.concatenate([w[h][rs], qg[h][rs]], axis=0), before[h][ci])
            vnew.append(u[h][rs] - ws[:CHUNK])
            inter.append(ws[CHUNK:])
        o = jnp.concatenate(inter, axis=0) + _bdot(qk[h], jnp.concatenate(vnew, axis=0))
        sl = slice(h * B_DK, (h + 1) * B_DK)
        o_ref[:, sl] = (_rms(o, og_ref[...]) * _silu(z_ref[:, sl].astype(F32))).astype(o_ref.dtype)


def _gated_deltanet(bqkv, bz, gp, conv_w, a_log, dt_bias, out_gain, lp):
    rows = bqkv.shape[0]
    n = ROW_TILE
    head_vec = lambda p: jnp.zeros((1, LANES), F32).at[0, B_HEADS:2 * B_HEADS].set(p.astype(F32))
    row_spec = lambda w: pl.BlockSpec((n, w), lambda i: (i, 0))
    return pl.pallas_call(
        functools.partial(_gdn_kernel, steps_per_batch=lp // n),
        grid=(rows // n,),
        in_specs=[row_spec(3 * B_W), row_spec(B_W), row_spec(LANES), _const_spec((CONV_K, 3 * B_W)),
                  _const_spec((1, LANES)), _const_spec((1, LANES)), _const_spec((1, B_DK))],
        out_specs=row_spec(B_W),
        out_shape=jax.ShapeDtypeStruct((rows, B_W), BF16),
        scratch_shapes=[pltpu.VMEM((n + 8, 3 * B_W), F32), pltpu.VMEM((B_HEADS, B_DK, B_DK), F32)],
        compiler_params=pltpu.CompilerParams(dimension_semantics=("arbitrary",),
                                             vmem_limit_bytes=VMEM_LIMIT),
        name="gated_deltanet",
    )(bqkv, bz, gp, conv_w.astype(F32), head_vec(a_log), head_vec(dt_bias), out_gain[None, :].astype(F32))


FF_COLS = 256


def _even_out_kernel(h_ref, oa_ref, ob_ref, woa_ref, wob_ref, g_ref, w1_ref, w3_ref, w2_ref, o_ref,
                     act_ref, *, tm, lp, seq):
    h1 = (h_ref[...] + jnp.dot(oa_ref[...], woa_ref[...], preferred_element_type=F32)
          + jnp.dot(ob_ref[...], wob_ref[...], preferred_element_type=F32))
    xn = _rms(h1, g_ref[...]).astype(BF16)
    for c in range(D_FF // FF_COLS):
        cs = slice(c * FF_COLS, (c + 1) * FF_COLS)
        a = jnp.dot(xn, w1_ref[:, cs], preferred_element_type=F32)
        b = jnp.dot(xn, w3_ref[:, cs], preferred_element_type=F32)
        act_ref[:, cs] = (_silu(a) * b).astype(BF16)
    out = h1 + jnp.dot(act_ref[...], w2_ref[...], preferred_element_type=F32)
    o_ref[...] = jnp.where(_row_valid(tm, lp, seq), out, 0.0)


def _even_out(h, oa, ob, w_out, gain, w1, w3, w2, lp, seq, tm):
    rows = h.shape[0]
    wo = w_out.astype(BF16)
    row_spec = lambda w: pl.BlockSpec((tm, w), lambda i: (i, 0))
    return pl.pallas_call(
        functools.partial(_even_out_kernel, tm=tm, lp=lp, seq=seq),
        grid=(rows // tm,),
        in_specs=[row_spec(D_MODEL), row_spec(A_W), row_spec(B_W),
                  _const_spec((A_W, D_MODEL)), _const_spec((B_W, D_MODEL)), _const_spec((1, D_MODEL)),
                  _const_spec(w1.shape), _const_spec(w3.shape), _const_spec(w2.shape)],
        out_specs=row_spec(D_MODEL),
        out_shape=jax.ShapeDtypeStruct((rows, D_MODEL), F32),
        scratch_shapes=[pltpu.VMEM((tm, D_FF), BF16)],
        compiler_params=pltpu.CompilerParams(dimension_semantics=("arbitrary",),
                                             vmem_limit_bytes=VMEM_LIMIT),
        name="even_out_ffn",
    )(h, oa, ob, wo[:A_W], wo[A_W:], gain[None, :].astype(F32),
      w1.astype(BF16), w3.astype(BF16), w2.astype(BF16))


def _odd_in_kernel(h_ref, g_ref, wq_ref, wf_ref, wi_ref, wz_ref, lb_ref,
                   q_ref, k_ref, lf_ref, i_ref, z_ref, *, tm, lp, seq):
    xn = _rms(h_ref[...], g_ref[...]).astype(BF16)
    valid = _row_valid(tm, lp, seq)
    q_ref[...] = jnp.dot(xn, wq_ref[...], preferred_element_type=F32).astype(BF16)
    f = jnp.dot(xn, wf_ref[...], preferred_element_type=F32)
    lb = lb_ref[...]
    fg = lb + (1.0 - lb) * _sigmoid(f)
    k_ref[...] = jnp.where(valid, 1.0 - fg, 0.0).astype(BF16)
    lf_ref[...] = jnp.where(valid, jnp.log(fg), 0.0)
    i_ref[...] = jnp.dot(xn, wi_ref[...], preferred_element_type=F32).astype(BF16)
    z_ref[...] = jnp.dot(xn, wz_ref[...], preferred_element_type=F32).astype(BF16)


def _odd_in(h, gain, w_in, lb, lp, seq, tm):
    rows = h.shape[0]
    wb16 = w_in.astype(BF16)
    ws = [wb16[:, j * D_MODEL:(j + 1) * D_MODEL] for j in range(4)]
    row_spec = pl.BlockSpec((tm, D_MODEL), lambda i: (i, 0))
    wspec = _const_spec((D_MODEL, D_MODEL))
    vec = _const_spec((1, D_MODEL))
    sd = lambda dt: jax.ShapeDtypeStruct((rows, D_MODEL), dt)
    return pl.pallas_call(
        functools.partial(_odd_in_kernel, tm=tm, lp=lp, seq=seq),
        grid=(rows // tm,),
        in_specs=[row_spec, vec, wspec, wspec, wspec, wspec, vec],
        out_specs=(row_spec,) * 5,
        out_shape=(sd(BF16), sd(BF16), sd(F32), sd(BF16), sd(BF16)),
        compiler_params=pltpu.CompilerParams(dimension_semantics=("arbitrary",),
                                             vmem_limit_bytes=VMEM_LIMIT),
        name="odd_in_proj",
    )(h, gain[None, :].astype(F32), *ws, lb[None, :].astype(F32))


SUB = 8
HALVES = (32, 16, 8)


def _hgrn_kernel(q_ref, k_ref, lf_ref, i_ref, z_ref, og_ref, sel_ref, o_ref, s_ref, p_ref, kf_ref, gf_ref,
                 *, steps_per_batch):
    n = ROW_TILE
    step = pl.program_id(0) % steps_per_batch

    @pl.when(step == 0)
    def _():
        s_ref[...] = jnp.zeros_like(s_ref)

    r, c, same = _chunk_masks(n)
    incl = same & (r >= c)
    gc_all = _mask_dot(incl, lf_ref[...]) * LOG2E
    rrow = lax.broadcasted_iota(jnp.int32, (n, 1), 0)
    inner = rrow % SUB
    level_mask = [(r // (2 * half) == c // (2 * half)).astype(F32) for half in HALVES]
    diag_mask = (r // SUB == c // SUB).astype(F32)

    kf_ref[...] = k_ref[...].astype(F32)
    gf_ref[...] = gc_all

    def block_rows(ref, sl, size, j):
        return jnp.concatenate([jnp.broadcast_to(ref[pl.ds(b * size + j, 1), sl], (size, C_DK))
                                for b in range(n // size)], axis=0)

    def products(h):
        sl = slice(h * C_DK, (h + 1) * C_DK)
        q = q_ref[:, sl].astype(F32)
        k = kf_ref[:, sl]
        gc = gc_all[:, sl]
        gl = block_rows(gf_ref, sl, CHUNK, CHUNK - 1)
        qg = q * jnp.exp2(gc)
        kdec = k * jnp.exp2(gl - gc)
        parts = []
        for half in HALVES:
            gref = block_rows(gf_ref, sl, 2 * half, half - 1)
            upper = (rrow % (2 * half)) >= half
            qt = q * jnp.exp2(jnp.where(upper, gc - gref, -jnp.inf))
            kt = k * jnp.exp2(jnp.where(upper, -jnp.inf, gref - gc))
            parts.append(_bdot_nt(qt, kt))
        pbuf = p_ref.at[h % 2]
        for j in range(SUB):
            kj = block_rows(kf_ref, sl, SUB, j)
            gj = block_rows(gf_ref, sl, SUB, j)
            pbuf[:, j * C_DK:(j + 1) * C_DK] = (
                q * kj * jnp.exp2(jnp.where(inner >= j, gc - gj, -jnp.inf))).astype(BF16)
        diag = jnp.dot(pbuf[...], sel_ref[...], preferred_element_type=F32)
        return parts, diag, qg, kdec, gl

    def finish(h, parts, diag, qg, kdec, gl):
        sl = slice(h * C_DK, (h + 1) * C_DK)
        iv = i_ref[:, sl]
        att = diag_mask * diag
        for mask, part in zip(level_mask, parts):
            att = att + mask * part
        st = s_ref[h]
        inter = []
        for ci in range(n // CHUNK):
            rs = slice(ci * CHUNK, (ci + 1) * CHUNK)
            inter.append(_bdot_nt(qg[rs], st))
            st = st * jnp.exp2(gl[ci * CHUNK:ci * CHUNK + 1, :]) + _bdot_tn(iv[rs], kdec[rs])
        s_ref[h] = st
        o = jnp.concatenate(inter, axis=0) + _bdot(att, iv)
        o_ref[:, sl] = (_rms(o, og_ref[...]) * _silu(z_ref[:, sl].astype(F32))).astype(o_ref.dtype)

    for h in range(C_HEADS):
        finish(h, *products(h))


def _hgrn2(q, k, lf, iv, z, out_gain, lp):
    rows = q.shape[0]
    n = ROW_TILE
    row_spec = pl.BlockSpec((n, D_MODEL), lambda i: (i, 0))
    sel = ((jnp.arange(SUB * C_DK)[:, None] // C_DK) == (jnp.arange(n)[None, :] % SUB)).astype(BF16)
    return pl.pallas_call(
        functools.partial(_hgrn_kernel, steps_per_batch=lp // n),
        grid=(rows // n,),
        in_specs=[row_spec] * 5 + [_const_spec((1, C_DK)), _const_spec(sel.shape)],
        out_specs=row_spec,
        out_shape=jax.ShapeDtypeStruct((rows, D_MODEL), BF16),
        scratch_shapes=[pltpu.VMEM((C_HEADS, C_DK, C_DK), F32), pltpu.VMEM((2, n, SUB * C_DK), BF16),
                        pltpu.VMEM((n, D_MODEL), F32), pltpu.VMEM((n, D_MODEL), F32)],
        compiler_params=pltpu.CompilerParams(dimension_semantics=("arbitrary",),
                                             vmem_limit_bytes=VMEM_LIMIT),
        name="hgrn2",
    )(q, k, lf, iv, z, out_gain[None, :].astype(F32), sel)


def _odd_out_kernel(h_ref, o_ref, wo_ref, g_ref, wr_ref, h3_ref, xn_ref, route_ref, *, tm, lp, seq):
    h3 = h_ref[...] + jnp.dot(o_ref[...], wo_ref[...], preferred_element_type=F32)
    h3 = jnp.where(_row_valid(tm, lp, seq), h3, 0.0)
    h3_ref[...] = h3
    xn = _rms(h3, g_ref[...])
    xn_ref[...] = xn
    logits = jnp.dot(xn.astype(BF16), wr_ref[...], preferred_element_type=F32)
    lane = lax.broadcasted_iota(jnp.int32, logits.shape, 1)
    logits = jnp.where(lane < N_EXPERTS, logits, -jnp.inf)
    v1 = jnp.max(logits, axis=-1, keepdims=True)
    e1 = jnp.min(jnp.where(logits == v1, lane, LANES), axis=-1, keepdims=True)
    rest = jnp.where(lane == e1, -jnp.inf, logits)
    v2 = jnp.max(rest, axis=-1, keepdims=True)
    e2 = jnp.min(jnp.where(rest == v2, lane, LANES), axis=-1, keepdims=True)
    t = jnp.exp(v2 - v1)
    g1 = 1.0 / (1.0 + t)
    g2 = t * g1
    route_ref[...] = jnp.where(lane == 0, e1.astype(F32),
                               jnp.where(lane == 1, e2.astype(F32),
                                         jnp.where(lane == 2, g1, jnp.where(lane == 3, g2, 0.0))))


def _odd_out(h, o, w_out, gain, router, lp, seq, tm):
    rows = h.shape[0]
    wr = jnp.pad(router.astype(BF16), ((0, 0), (0, LANES - N_EXPERTS)))
    row_spec = lambda w: pl.BlockSpec((tm, w), lambda i: (i, 0))
    return pl.pallas_call(
        functools.partial(_odd_out_kernel, tm=tm, lp=lp, seq=seq),
        grid=(rows // tm,),
        in_specs=[row_spec(D_MODEL), row_spec(D_MODEL), _const_spec((D_MODEL, D_MODEL)),
                  _const_spec((1, D_MODEL)), _const_spec((D_MODEL, LANES))],
        out_specs=(row_spec(D_MODEL), row_spec(D_MODEL), row_spec(LANES)),
        out_shape=(jax.ShapeDtypeStruct((rows, D_MODEL), F32), jax.ShapeDtypeStruct((rows, D_MODEL), F32),
                   jax.ShapeDtypeStruct((rows, LANES), F32)),
        compiler_params=pltpu.CompilerParams(dimension_semantics=("arbitrary",),
                                             vmem_limit_bytes=VMEM_LIMIT),
        name="odd_out_route",
    )(h, o, w_out.astype(BF16), gain[None, :].astype(F32), wr)


MOE_TILE = 512
MOE_FF_STEPS = 2


def _moe_plan(route, tile):
    n_tok = route.shape[0]
    flat_e = route[:, :2].astype(jnp.int32).reshape(-1)
    na = flat_e.shape[0]
    counts = jnp.sum(flat_e[:, None] == jnp.arange(N_EXPERTS, dtype=jnp.int32)[None, :], axis=0,
                     dtype=jnp.int32)
    order = jnp.argsort(flat_e, stable=True).astype(jnp.int32)
    first = jnp.cumsum(counts) - counts
    padded = (counts + tile - 1) // tile * tile
    pend = jnp.cumsum(padded)
    n_tiles = -(-(na + N_EXPERTS * (tile - 1)) // tile) + 2
    n_rows = n_tiles * tile
    tile_e = jnp.minimum(jnp.searchsorted(pend, jnp.arange(n_tiles, dtype=jnp.int32) * tile, side='right'),
                         N_EXPERTS - 1).astype(jnp.int32)
    row = jnp.arange(n_rows, dtype=jnp.int32)
    row_e = jnp.repeat(tile_e, tile)
    local = row - (pend - padded)[row_e]
    real = local < counts[row_e]
    pair = order[jnp.minimum(first[row_e] + local, na - 1)]
    row_tok = jnp.where(real, pair // 2, 0)
    dump = na + ((row // tile) % 2) * tile + row % tile
    row_dst = jnp.where(real, (pair % 2) * n_tok + pair // 2, dump)
    n_used = (pend[-1] // tile).astype(jnp.int32).reshape(1)
    return row_tok, row_dst, tile_e, n_used, n_tiles


def _moe_ffn_kernel(te_ref, nused_ref, tok0_ref, tokn_ref, dstp_ref, x_hbm, w1_ref, w3_ref, w2_ref,
                    y_hbm, xbuf, xb16, ybuf, gsem, ssem):
    i, f = pl.program_id(0), pl.program_id(1)
    tile = xb16.shape[0]
    slot = i % 2
    n_used = nused_ref[0]
    n_out = y_hbm.shape[0] - 2 * tile

    def gather_start(tok_ref, row, dst_slot):
        pltpu.make_async_copy(x_hbm.at[pl.ds(tok_ref[0, 0, row], 1), :],
                              xbuf.at[dst_slot, pl.ds(row, 1), :], gsem.at[dst_slot]).start()

    def scatter_start(row, src_slot):
        pltpu.make_async_copy(ybuf.at[src_slot, pl.ds(row, 1), :],
                              y_hbm.at[pl.ds(dstp_ref[0, 0, row], 1), :], ssem.at[src_slot]).start()

    def result_wait(s):
        pltpu.make_async_copy(ybuf.at[s], y_hbm.at[pl.ds(0, tile), :], ssem.at[s]).wait()

    @pl.when((i == 0) & (f == 0))
    def _():
        ybuf[...] = jnp.zeros(ybuf.shape, F32)
        for s in range(2):
            pltpu.make_async_copy(ybuf.at[s], y_hbm.at[pl.ds(n_out + s * tile, tile), :], ssem.at[s]).start()

        def issue(r, carry):
            gather_start(tok0_ref, r, 0)
            return carry
        lax.fori_loop(0, tile, issue, 0)

    @pl.when((f == 0) & (i >= n_used) & (i - 2 < n_used))
    def _():
        result_wait(slot)

    @pl.when((f == 0) & (i <= n_used))
    def _():
        pltpu.make_async_copy(x_hbm.at[pl.ds(0, tile), :], xbuf.at[slot], gsem.at[slot]).wait()
        xb16[...] = xbuf[slot].astype(BF16)

    def compute(step, par, first_tile):
        if step == 0:
            for r in range(tile):
                gather_start(tokn_ref, r, 1 - par)
        if step == MOE_FF_STEPS - 1 and not first_tile:
            for r in range(tile):
                scatter_start(r, 1 - par)
        x = xb16[...]
        a = jnp.dot(x, w1_ref[0], preferred_element_type=F32)
        b = jnp.dot(x, w3_ref[0], preferred_element_type=F32)
        y = jnp.dot((_silu(a) * b).astype(BF16), w2_ref[0], preferred_element_type=F32)
        if step == 0:
            result_wait(par)
            ybuf[par] = y
        else:
            ybuf[par] += y

    for step in range(MOE_FF_STEPS):
        @pl.when((i == 0) & (f == step))
        def _():
            compute(step, 0, True)

        for par in range(2):
            @pl.when((i > 0) & (i < n_used) & (f == step) & (slot == par))
            def _():
                compute(step, par, False)

    @pl.when((i == n_used) & (f == 0))
    def _():
        def issue(r, carry):
            scatter_start(r, 1 - slot)
            return carry
        lax.fori_loop(0, tile, issue, 0)


def _moe_ffn(xn, w1, w3, w2, row_tok, row_dst, tile_e, n_used, n_tiles, tile):
    n_tok = xn.shape[0]
    steps = MOE_FF_STEPS
    ff = D_FF_EXPERT // steps
    fi = lambda i, f, nu: jnp.where(i < nu[0], f, steps - 1)
    tok_spec = lambda imap: pl.BlockSpec((1, 1, tile), imap, memory_space=pltpu.SMEM)
    grid_spec = pltpu.PrefetchScalarGridSpec(
        num_scalar_prefetch=2,
        grid=(n_tiles, steps),
        in_specs=[tok_spec(lambda i, f, te, nu: (0, 0, 0)),
                  tok_spec(lambda i, f, te, nu: (jnp.minimum(i + 1, n_tiles - 1), 0, 0)),
                  tok_spec(lambda i, f, te, nu: (jnp.maximum(i - 1, 0), 0, 0)),
                  pl.BlockSpec(memory_space=pl.ANY),
                  pl.BlockSpec((1, D_MODEL, ff), lambda i, f, te, nu: (te[i], 0, fi(i, f, nu))),
                  pl.BlockSpec((1, D_MODEL, ff), lambda i, f, te, nu: (te[i], 0, fi(i, f, nu))),
                  pl.BlockSpec((1, ff, D_MODEL), lambda i, f, te, nu: (te[i], fi(i, f, nu), 0))],
        out_specs=pl.BlockSpec(memory_space=pl.ANY),
        scratch_shapes=[pltpu.VMEM((2, tile, D_MODEL), F32), pltpu.VMEM((tile, D_MODEL), BF16),
                        pltpu.VMEM((2, tile, D_MODEL), F32),
                        pltpu.SemaphoreType.DMA((2,)), pltpu.SemaphoreType.DMA((2,))],
    )
    tok3 = row_tok.reshape(n_tiles, 1, tile)
    return pl.pallas_call(
        _moe_ffn_kernel,
        grid_spec=grid_spec,
        out_shape=jax.ShapeDtypeStruct((2 * n_tok + 2 * tile, D_MODEL), F32),
        compiler_params=pltpu.CompilerParams(dimension_semantics=("arbitrary", "arbitrary"),
                                             vmem_limit_bytes=VMEM_LIMIT),
        name="moe_ffn",
    )(tile_e, n_used, tok3, tok3, row_dst.reshape(n_tiles, 1, tile), xn, w1, w3, w2)


def _moe_combine_kernel(h_ref, y0_ref, y1_ref, route_ref, o_ref):
    g = route_ref[...]
    o_ref[0] = h_ref[...] + g[:, 2:3] * y0_ref[...] + g[:, 3:4] * y1_ref[...]


def _moe_combine(h3, y, route, batch, seq, lp, tc):
    n_tok = h3.shape[0]
    frame_rows = lambda width, off: pl.BlockSpec(
        (pl.Element(tc), pl.Element(width)),
        lambda b, j: (pl.multiple_of(off + b * lp + CHUNK + j * tc, CHUNK), 0))
    return pl.pallas_call(
        _moe_combine_kernel,
        grid=(batch, seq // tc),
        in_specs=[frame_rows(D_MODEL, 0), frame_rows(D_MODEL, 0), frame_rows(D_MODEL, n_tok),
                  frame_rows(LANES, 0)],
        out_specs=pl.BlockSpec((1, tc, D_MODEL), lambda b, j: (b, j, 0)),
        out_shape=jax.ShapeDtypeStruct((batch, seq, D_MODEL), F32),
        compiler_params=pltpu.CompilerParams(dimension_semantics=("arbitrary", "arbitrary"),
                                             vmem_limit_bytes=VMEM_LIMIT),
        name="moe_combine",
    )(h3, y, y, route)


def _layout_rows(seq):
    return -(-(CHUNK + seq) // ROW_TILE) * ROW_TILE


def _proj_tile(lp):
    return next(t for t in (640, 512, 384, ROW_TILE) if lp % t == 0)


def kernel(x, meta_tokens, lb_logits, mix_norm, ffn_norm, e_w_in, a_q_gain, a_k_gain, a_lam_q1, a_lam_k1, a_lam_q2, a_lam_k2, a_sub_gain, b_conv, b_a_log, b_dt_bias, b_out_gain, e_w_out, ffn_w1, ffn_w3, ffn_w2, o_w_in, c_out_gain, o_w_out, router, moe_w1, moe_w3, moe_w2):
    batch, seq, _ = x.shape
    lp = _layout_rows(seq)
    rows = batch * lp
    tm = _proj_tile(lp)
    meta = jnp.broadcast_to(meta_tokens[None].astype(x.dtype), (batch, N_META, D_MODEL))
    h = jnp.concatenate([jnp.zeros((batch, FRONT, D_MODEL), x.dtype), meta, x,
                         jnp.zeros((batch, lp - CHUNK - seq, D_MODEL), x.dtype)], axis=1).reshape(rows, D_MODEL)

    q, k, v, bqkv, bz, gp = _even_in(h, mix_norm[0], e_w_in[0], a_q_gain[0], a_k_gain[0], lp, tm)
    lam_init = 0.8 - 0.6 * math.exp(-0.3 * 0)
    lam = (jnp.exp(jnp.sum(a_lam_q1[0].astype(F32) * a_lam_k1[0].astype(F32)))
           - jnp.exp(jnp.sum(a_lam_q2[0].astype(F32) * a_lam_k2[0].astype(F32))) + lam_init)
    o_a = _diff_attention(q, k, v, a_sub_gain[0], lam, lam_init, batch, lp, ROW_TILE)
    o_b = _gated_deltanet(bqkv, bz, gp, b_conv[0], b_a_log[0], b_dt_bias[0], b_out_gain[0], lp)
    h = _even_out(h, o_a, o_b, e_w_out[0], ffn_norm[0], ffn_w1[0], ffn_w3[0], ffn_w2[0], lp, seq, tm)

    lb_all = jnp.cumsum(jax.nn.softmax(lb_logits.astype(F32), axis=0), axis=0)
    lb = (lb_all - lb_all[0])[1]
    cq, ck, clf, ci, cz = _odd_in(h, mix_norm[1], o_w_in[0], lb, lp, seq, tm)
    o_c = _hgrn2(cq, ck, clf, ci, cz, c_out_gain[0], lp)
    h3, xn, route = _odd_out(h, o_c, o_w_out[0], ffn_norm[1], router[0], lp, seq, tm)
    row_tok, row_dst, tile_e, n_used, n_tiles = _moe_plan(route, MOE_TILE)
    y = _moe_ffn(xn, moe_w1[0].astype(BF16), moe_w3[0].astype(BF16), moe_w2[0].astype(BF16),
                 row_tok, row_dst, tile_e, n_used, n_tiles, MOE_TILE)
    return _moe_combine(h3, y, route, batch, seq, lp, ROW_TILE)
```

```python
import functools
import math

import jax
import jax.numpy as jnp
from jax import lax
from jax.experimental import pallas as pl
from jax.experimental.pallas import tpu as pltpu

D_MODEL = 1024
CHUNK = 64
N_META = 16
FRONT = CHUNK - N_META
EPS = 1e-6
ROPE_THETA = 500000.0
LOG2E = math.log2(math.e)

A_HEADS = 4
A_DK = 64
A_DV = 128
A_ROT = A_DK // 4
A_W = A_HEADS * A_DV

B_HEADS = 4
B_DK = 128
B_W = B_HEADS * B_DK
CONV_K = 4

C_HEADS = 8
C_DK = 128

D_FF = 2816
N_EXPERTS = 8
D_FF_EXPERT = 3584

LANES = 128
ROW_TILE = 256
VMEM_LIMIT = 56 * 1024 * 1024

BF16 = jnp.bfloat16
F32 = jnp.float32


def _const_spec(shape):
    return pl.BlockSpec(shape, lambda *_: (0,) * len(shape), pipeline_mode=pl.Buffered(1))


def _rms(x, gain):
    return x * lax.rsqrt(jnp.mean(x * x, axis=-1, keepdims=True) + EPS) * gain


def _silu(x):
    return x * (1.0 / (1.0 + jnp.exp(-x)))


def _sigmoid(x):
    return 1.0 / (1.0 + jnp.exp(-x))


def _row_valid(tile_rows, lp, seq):
    r = pl.program_id(0) * tile_rows + lax.broadcasted_iota(jnp.int32, (tile_rows, 1), 0)
    p = r % lp
    return (p >= FRONT) & (p < CHUNK + seq)


def _head_norm_rope(a, ones_bd, gain, cos, sin_lo, sin_hi):
    ssq = jnp.dot((a * a).astype(BF16), ones_bd, preferred_element_type=F32)
    y = a * lax.rsqrt(ssq * (1.0 / A_DK) + EPS) * gain
    up = pltpu.roll(y, A_W - A_ROT // 2, axis=1)
    dn = pltpu.roll(y, A_ROT // 2, axis=1)
    return y * cos + up * sin_lo + dn * sin_hi


def _even_in_kernel(h_ref, g_ref, wq_ref, wk_ref, wv_ref, wb_ref, wz_ref, wg_ref, bd_ref,
                    qg_ref, kg_ref, cos_ref, slo_ref, shi_ref,
                    q_ref, k_ref, v_ref, b_ref, z_ref, gp_ref):
    xn = _rms(h_ref[...], g_ref[...]).astype(BF16)
    rep = lambda t: jnp.concatenate([t] * (A_W // LANES), axis=1)
    cos, slo, shi = rep(cos_ref[...]), rep(slo_ref[...]), rep(shi_ref[...])
    bd = bd_ref[...]
    aq = jnp.dot(xn, wq_ref[...], preferred_element_type=F32)
    q_ref[...] = (_head_norm_rope(aq, bd, qg_ref[...], cos, slo, shi) * (A_DK ** -0.5 * LOG2E)).astype(BF16)
    ak = jnp.dot(xn, wk_ref[...], preferred_element_type=F32)
    k_ref[...] = _head_norm_rope(ak, bd, kg_ref[...], cos, slo, shi).astype(BF16)
    v_ref[...] = jnp.dot(xn, wv_ref[...], preferred_element_type=F32).astype(BF16)
    b_ref[...] = jnp.dot(xn, wb_ref[...], preferred_element_type=F32).astype(BF16)
    z_ref[...] = jnp.dot(xn, wz_ref[...], preferred_element_type=F32).astype(BF16)
    gp_ref[...] = jnp.dot(xn, wg_ref[...], preferred_element_type=F32)


def _even_in(h, gain, w_in, q_gain, k_gain, lp, tm):
    rows = h.shape[0]
    wb16 = w_in.astype(BF16)
    c0, c1, c2, c3, c4 = A_W, 2 * A_W, 3 * A_W, 3 * A_W + 3 * B_W, 3 * A_W + 4 * B_W
    wq, wk, wv, wb, wz = wb16[:, :c0], wb16[:, c0:c1], wb16[:, c1:c2], wb16[:, c2:c3], wb16[:, c3:c4]
    wg = jnp.pad(wb16[:, c4:], ((0, 0), (0, LANES - 2 * B_HEADS)))
    d = jnp.arange(A_W)
    ones_bd = (d[:, None] // A_DK == d[None, :] // A_DK).astype(BF16)
    half = A_ROT // 2
    inv_freq = ROPE_THETA ** (-jnp.arange(half, dtype=F32) * 2.0 / A_ROT)
    pos = (jnp.arange(lp) - FRONT).astype(F32)
    dim = jnp.arange(LANES) % A_DK
    ang = pos[:, None] * inv_freq[dim % half][None, :]
    cos_t = jnp.where(dim < A_ROT, jnp.cos(ang), 1.0)
    slo_t = jnp.where(dim < half, -jnp.sin(ang), 0.0)
    shi_t = jnp.where((dim >= half) & (dim < A_ROT), jnp.sin(ang), 0.0)
    tile8 = lambda g: jnp.tile(g.astype(F32), 2 * A_HEADS)[None, :]
    n_pos = lp // tm
    row_spec = lambda w: pl.BlockSpec((tm, w), lambda i: (i, 0))
    pos_spec = pl.BlockSpec((tm, LANES), lambda i: (i % n_pos, 0))
    outs = pl.pallas_call(
        _even_in_kernel,
        grid=(rows // tm,),
        in_specs=[row_spec(D_MODEL), _const_spec((1, D_MODEL)),
                  _const_spec(wq.shape), _const_spec(wk.shape), _const_spec(wv.shape),
                  _const_spec(wb.shape), _const_spec(wz.shape), _const_spec(wg.shape),
                  _const_spec(ones_bd.shape), _const_spec((1, A_W)), _const_spec((1, A_W)),
                  pos_spec, pos_spec, pos_spec],
        out_specs=(row_spec(A_W), row_spec(A_W), row_spec(A_W), row_spec(3 * B_W), row_spec(B_W),
                   row_spec(LANES)),
        out_shape=(jax.ShapeDtypeStruct((rows, A_W), BF16), jax.ShapeDtypeStruct((rows, A_W), BF16),
                   jax.ShapeDtypeStruct((rows, A_W), BF16), jax.ShapeDtypeStruct((rows, 3 * B_W), BF16),
                   jax.ShapeDtypeStruct((rows, B_W), BF16), jax.ShapeDtypeStruct((rows, LANES), F32)),
        compiler_params=pltpu.CompilerParams(dimension_semantics=("arbitrary",),
                                             vmem_limit_bytes=VMEM_LIMIT),
        name="even_in_proj",
    )(h, gain[None, :].astype(F32), wq, wk, wv, wb, wz, wg, ones_bd, tile8(q_gain), tile8(k_gain),
      cos_t, slo_t, shi_t)
    return outs


ATTN_WIDE = 1024


ATTN_HEADS = 2


def _attn_kernel(q_ref, k_ref, v_ref, sg_ref, lam_ref, o_ref, qs_ref, vx_ref, m_ref, acc_ref,
                 sa_ref, sb_ref, *, tq, width, lam_init):
    qi = pl.program_id(2)
    heads = range(ATTN_HEADS)
    lanes = lambda s: slice(s * LANES, (s + 1) * LANES)

    @pl.when(qi == 0)
    def _():
        for s in heads:
            vx_ref[s, :, :A_DV] = v_ref[:, lanes(s)]
            vx_ref[s, :, A_DV:] = jnp.ones((vx_ref.shape[1], A_DV), BF16)

    lane = lax.broadcasted_iota(jnp.int32, (tq, LANES), 1)
    zero = jnp.zeros((tq, LANES), BF16)
    for s in heads:
        q = q_ref[:, lanes(s)]
        qs_ref[s, :tq, :] = jnp.where(lane < A_DK, q, zero)
        qs_ref[s, tq:, :] = jnp.where(lane >= A_DK, q, zero)
    m_ref[...] = jnp.full(m_ref.shape, -jnp.inf, F32)
    acc_ref[...] = jnp.zeros(acc_ref.shape, F32)

    lp = k_ref.shape[0]
    qchunk = (qi * tq + lax.broadcasted_iota(jnp.int32, (2 * tq, 1), 0) % tq) // CHUNK

    def window(t):
        return pl.multiple_of(jnp.minimum(t * width, lp - width), tq)

    def scores(s_ref, t):
        for s in heads:
            s_ref[s] = lax.dot_general(qs_ref[s], k_ref[pl.ds(window(t), width), lanes(s)],
                                       (((1,), (1,)), ((), ())), preferred_element_type=F32)

    def softmax_pv(s_ref, t, mask):
        start = window(t)
        kpos = start + lax.broadcasted_iota(jnp.int32, (1, width), 1)
        kchunk = jnp.where((kpos >= t * width) & (kpos >= FRONT), kpos // CHUNK, lp)
        nblk = width // LANES

        def block(s, c):
            cs = slice(c * LANES, (c + 1) * LANES)
            if mask == "full":
                return jnp.where(kchunk[:, cs] <= qchunk, s_ref[s, :, cs], -jnp.inf)
            if mask == "pad" and c == 0:
                return jnp.where(kpos[:, cs] >= FRONT, s_ref[s, :, cs], -jnp.inf)
            return s_ref[s, :, cs]

        for s in heads:
            mx = functools.reduce(jnp.maximum, [block(s, c) for c in range(nblk)])
            m_old = m_ref[s]
            m_new = jnp.maximum(m_old, jnp.max(mx, axis=-1, keepdims=True))
            alpha = jnp.exp2(m_old - m_new)
            p = jnp.concatenate([jnp.exp2((block(s, c) - m_new).astype(BF16)) for c in range(nblk)], axis=1)
            pv = jnp.dot(p, vx_ref[s, pl.ds(start, width), :], preferred_element_type=F32)
            acc_ref[s, :, :A_DV] = alpha * acc_ref[s, :, :A_DV] + pv[:, :A_DV]
            acc_ref[s, :, A_DV:] = alpha * acc_ref[s, :, A_DV:] + pv[:, A_DV:]
            m_ref[s] = m_new

    n_tiles = (qi * tq + tq + width - 1) // width
    n_pairs = (n_tiles + 1) // 2

    def pair(j, masks):
        scores(sb_ref, 2 * j + 1)
        softmax_pv(sa_ref, 2 * j, masks[0])
        scores(sa_ref, 2 * j + 2)
        softmax_pv(sb_ref, 2 * j + 1, masks[1])

    scores(sa_ref, 0)

    @pl.when(n_pairs > 1)
    def _():
        pair(0, ("pad", "none"))

    def inner(j, carry):
        pair(j, ("none", "none"))
        return carry
    lax.fori_loop(1, n_pairs - 1, inner, 0)

    @pl.when(n_tiles == 2 * n_pairs)
    def _():
        scores(sb_ref, 2 * n_pairs - 1)
        softmax_pv(sa_ref, 2 * n_pairs - 2, "pad")
        softmax_pv(sb_ref, 2 * n_pairs - 1, "full")

    @pl.when(n_tiles < 2 * n_pairs)
    def _():
        softmax_pv(sa_ref, 2 * n_pairs - 2, "full")

    for s in heads:
        o = acc_ref[s, :, :A_DV] / acc_ref[s, :, A_DV:]
        o = o[:tq] - lam_ref[...] * o[tq:]
        o_ref[:, lanes(s)] = (_rms(o, sg_ref[...]) * (1.0 - lam_init)).astype(o_ref.dtype)


def _diff_attention(q, k, v, sub_gain, lam, lam_init, batch, lp, tq):
    nq = lp // tq
    hw = ATTN_HEADS * LANES
    kv_spec = pl.BlockSpec((lp, hw), lambda b, h, i: (b, h), pipeline_mode=pl.Buffered(1))
    q_spec = pl.BlockSpec((tq, hw), lambda b, h, i: (b * nq + i, h))
    vec = pl.BlockSpec((1, LANES), lambda b, h, i: (0, 0))
    width = min(ATTN_WIDE, lp)
    return pl.pallas_call(
        functools.partial(_attn_kernel, tq=tq, width=width, lam_init=lam_init),
        grid=(batch, A_HEADS // ATTN_HEADS, nq),
        in_specs=[q_spec, kv_spec, kv_spec, vec, vec],
        out_specs=q_spec,
        out_shape=jax.ShapeDtypeStruct(q.shape, BF16),
        scratch_shapes=[pltpu.VMEM((ATTN_HEADS, 2 * tq, LANES), BF16),
                        pltpu.VMEM((ATTN_HEADS, lp, 2 * A_DV), BF16),
                        pltpu.VMEM((ATTN_HEADS, 2 * tq, LANES), F32),
                        pltpu.VMEM((ATTN_HEADS, 2 * tq, 2 * A_DV), F32),
                        pltpu.VMEM((ATTN_HEADS, 2 * tq, width), F32),
                        pltpu.VMEM((ATTN_HEADS, 2 * tq, width), F32)],
        compiler_params=pltpu.CompilerParams(dimension_semantics=("arbitrary",) * 3,
                                             vmem_limit_bytes=VMEM_LIMIT),
        name="diff_attention",
    )(q, k, v, sub_gain[None, :].astype(F32), jnp.full((1, LANES), lam, F32))


def _chunk_masks(n):
    r = lax.broadcasted_iota(jnp.int32, (n, n), 0)
    c = lax.broadcasted_iota(jnp.int32, (n, n), 1)
    same = (r // CHUNK) == (c // CHUNK)
    return r, c, same


def _mask_dot(mask, x):
    hi = x.astype(BF16)
    rest = x - hi.astype(F32)
    mid = rest.astype(BF16)
    lo = (rest - mid.astype(F32)).astype(BF16)
    m = mask.astype(BF16)
    return jnp.dot(jnp.concatenate([m, m, m], axis=1), jnp.concatenate([hi, mid, lo], axis=0),
                   preferred_element_type=F32)


def _bdot(a, b):
    return jnp.dot(a.astype(BF16), b.astype(BF16), preferred_element_type=F32)


def _bdot_nt(a, b):
    return lax.dot_general(a.astype(BF16), b.astype(BF16), (((1,), (1,)), ((), ())),
                           preferred_element_type=F32)


def _bdot_tn(a, b):
    return lax.dot_general(a.astype(BF16), b.astype(BF16), (((0,), (0,)), ((), ())),
                           preferred_element_type=F32)


def _gdn_kernel(x_ref, z_ref, gp_ref, cw_ref, alog_ref, dtb_ref, og_ref, o_ref, xbuf_ref, s_ref):
    nb, n = x_ref.shape[0], ROW_TILE
    step = pl.program_id(0)

    @pl.when(step == 0)
    def _():
        xbuf_ref[:, :8, :] = jnp.zeros((nb, 8, 3 * B_W), F32)
        s_ref[...] = jnp.zeros_like(s_ref)

    r, c, same = _chunk_masks(n)
    incl = same & (r >= c)
    strict = same & (r > c)
    eye = (r == c).astype(F32)
    row = step * n + lax.broadcasted_iota(jnp.int32, (n, 1), 0)
    chunks = [slice(ci * CHUNK, (ci + 1) * CHUNK) for ci in range(n // CHUNK)]

    heads = range(nb * B_HEADS)
    p, t, rhs, qk, qg, kdec, decay = [], [], [], [], [], [], []
    for b in range(nb):
        xbuf_ref[b, 8:, :] = x_ref[b].astype(F32)
        conv = cw_ref[CONV_K - 1:CONV_K, :] * xbuf_ref[b, 8:, :]
        for j in range(CONV_K - 1):
            conv = conv + cw_ref[j:j + 1, :] * xbuf_ref[b, 5 + j:5 + j + n, :]
        xbuf_ref[b, :8, :] = xbuf_ref[b, n:n + 8, :]
        conv = _silu(conv)
        gp = gp_ref[b]
        beta_all = _sigmoid(gp)
        gpre = gp + dtb_ref[...]
        softplus = jnp.maximum(gpre, 0.0) + jnp.log(1.0 + jnp.exp(-jnp.abs(gpre)))
        g_all = jnp.where(row >= FRONT, -jnp.exp(alog_ref[...]) * softplus, 0.0)
        sums = _mask_dot(jnp.concatenate([incl, same], axis=0), g_all)
        gc_all, gl_all = sums[:n], sums[n:]
        gc_rows = gc_all.T
        for h in range(B_HEADS):
            qh = conv[:, h * B_DK:(h + 1) * B_DK]
            kh = conv[:, B_W + h * B_DK:B_W + (h + 1) * B_DK]
            vh = conv[:, 2 * B_W + h * B_DK:2 * B_W + (h + 1) * B_DK]
            qh = qh * lax.rsqrt(jnp.sum(qh * qh, axis=-1, keepdims=True) + EPS) * (B_DK ** -0.5)
            kh = kh * lax.rsqrt(jnp.sum(kh * kh, axis=-1, keepdims=True) + EPS)
            beta = beta_all[:, h:h + 1]
            gcol = gc_all[:, B_HEADS + h:B_HEADS + h + 1]
            glcol = gl_all[:, B_HEADS + h:B_HEADS + h + 1]
            grow = gc_rows[B_HEADS + h:B_HEADS + h + 1, :]
            dec = jnp.exp(jnp.where(incl, gcol - grow, -jnp.inf))
            kb = kh * beta
            egc = jnp.exp(gcol)
            p0 = -(_bdot_nt(kb, kh) * jnp.where(strict, dec, 0.0))
            p.append(p0)
            t.append(eye + p0)
            rhs.append(jnp.concatenate([vh * beta, kb * egc], axis=1))
            qk.append(_bdot_nt(qh, kh) * dec)
            qg.append(qh * egc)
            kdec.append(kh * jnp.exp(glcol - gcol))
            decay.append([jnp.exp(glcol[rs.stop - 1:rs.stop, :]) for rs in chunks])
    for _ in range(5):
        for h in heads:
            p[h] = _bdot(p[h], p[h])
            t[h] = t[h] + _bdot(t[h], p[h])
    sol = [_bdot(t[h], rhs[h]) for h in heads]
    u = [s[:, :B_DK] for s in sol]
    w = [s[:, B_DK:] for s in sol]
    kw = [[_bdot_tn(kdec[h][rs], w[h][rs]) for rs in chunks] for h in heads]
    ku = [[_bdot_tn(kdec[h][rs], u[h][rs]) for rs in chunks] for h in heads]
    state = [s_ref[h] for h in heads]
    before = [[] for _ in heads]
    for ci in range(len(chunks)):
        for h in heads:
            before[h].append(state[h])
            state[h] = state[h] * decay[h][ci] - _bdot(kw[h][ci], state[h]) + ku[h][ci]
    for h in heads:
        s_ref[h] = state[h]
        vnew, inter = [], []
        for ci, rs in enumerate(chunks):
            ws = _bdot(jnp.concatenate([w[h][rs], qg[h][rs]], axis=0), before[h][ci])
            vnew.append(u[h][rs] - ws[:CHUNK])
            inter.append(ws[CHUNK:])
        o = jnp.concatenate(inter, axis=0) + _bdot(qk[h], jnp.concatenate(vnew, axis=0))
        b, sl = h // B_HEADS, slice((h % B_HEADS) * B_DK, (h % B_HEADS + 1) * B_DK)
        o_ref[b, :, sl] = (_rms(o, og_ref[...]) * _silu(z_ref[b, :, sl].astype(F32))).astype(o_ref.dtype)


def _gated_deltanet(bqkv, bz, gp, conv_w, a_log, dt_bias, out_gain, lp):
    rows = bqkv.shape[0]
    n, nb = ROW_TILE, rows // lp
    head_vec = lambda p: jnp.zeros((1, LANES), F32).at[0, B_HEADS:2 * B_HEADS].set(p.astype(F32))
    per_batch = lambda a: a.reshape(nb, lp, a.shape[-1])
    row_spec = lambda w: pl.BlockSpec((nb, n, w), lambda i: (0, i, 0))
    out = pl.pallas_call(
        _gdn_kernel,
        grid=(lp // n,),
        in_specs=[row_spec(3 * B_W), row_spec(B_W), row_spec(LANES), _const_spec((CONV_K, 3 * B_W)),
                  _const_spec((1, LANES)), _const_spec((1, LANES)), _const_spec((1, B_DK))],
        out_specs=row_spec(B_W),
        out_shape=jax.ShapeDtypeStruct((nb, lp, B_W), BF16),
        scratch_shapes=[pltpu.VMEM((nb, n + 8, 3 * B_W), F32), pltpu.VMEM((nb * B_HEADS, B_DK, B_DK), F32)],
        compiler_params=pltpu.CompilerParams(dimension_semantics=("arbitrary",),
                                             vmem_limit_bytes=VMEM_LIMIT),
        name="gated_deltanet",
    )(per_batch(bqkv), per_batch(bz), per_batch(gp), conv_w.astype(F32), head_vec(a_log), head_vec(dt_bias),
      out_gain[None, :].astype(F32))
    return out.reshape(rows, B_W)


FF_COLS = 256


def _even_out_kernel(h_ref, oa_ref, ob_ref, woa_ref, wob_ref, g_ref, w1_ref, w3_ref, w2_ref, o_ref,
                     act_ref, *, tm, lp, seq):
    h1 = (h_ref[...] + jnp.dot(oa_ref[...], woa_ref[...], preferred_element_type=F32)
          + jnp.dot(ob_ref[...], wob_ref[...], preferred_element_type=F32))
    xn = _rms(h1, g_ref[...]).astype(BF16)
    for c in range(D_FF // FF_COLS):
        cs = slice(c * FF_COLS, (c + 1) * FF_COLS)
        a = jnp.dot(xn, w1_ref[:, cs], preferred_element_type=F32)
        b = jnp.dot(xn, w3_ref[:, cs], preferred_element_type=F32)
        act_ref[:, cs] = (_silu(a) * b).astype(BF16)
    out = h1 + jnp.dot(act_ref[...], w2_ref[...], preferred_element_type=F32)
    o_ref[...] = jnp.where(_row_valid(tm, lp, seq), out, 0.0)


def _even_out(h, oa, ob, w_out, gain, w1, w3, w2, lp, seq, tm):
    rows = h.shape[0]
    wo = w_out.astype(BF16)
    row_spec = lambda w: pl.BlockSpec((tm, w), lambda i: (i, 0))
    return pl.pallas_call(
        functools.partial(_even_out_kernel, tm=tm, lp=lp, seq=seq),
        grid=(rows // tm,),
        in_specs=[row_spec(D_MODEL), row_spec(A_W), row_spec(B_W),
                  _const_spec((A_W, D_MODEL)), _const_spec((B_W, D_MODEL)), _const_spec((1, D_MODEL)),
                  _const_spec(w1.shape), _const_spec(w3.shape), _const_spec(w2.shape)],
        out_specs=row_spec(D_MODEL),
        out_shape=jax.ShapeDtypeStruct((rows, D_MODEL), F32),
        scratch_shapes=[pltpu.VMEM((tm, D_FF), BF16)],
        compiler_params=pltpu.CompilerParams(dimension_semantics=("arbitrary",),
                                             vmem_limit_bytes=VMEM_LIMIT),
        name="even_out_ffn",
    )(h, oa, ob, wo[:A_W], wo[A_W:], gain[None, :].astype(F32),
      w1.astype(BF16), w3.astype(BF16), w2.astype(BF16))


def _odd_in_kernel(h_ref, g_ref, wq_ref, wf_ref, wi_ref, wz_ref, lb_ref,
                   q_ref, k_ref, lf_ref, i_ref, z_ref, *, tm, lp, seq):
    xn = _rms(h_ref[...], g_ref[...]).astype(BF16)
    valid = _row_valid(tm, lp, seq)
    q_ref[...] = jnp.dot(xn, wq_ref[...], preferred_element_type=F32).astype(BF16)
    f = jnp.dot(xn, wf_ref[...], preferred_element_type=F32)
    lb = lb_ref[...]
    fg = lb + (1.0 - lb) * _sigmoid(f)
    k_ref[...] = jnp.where(valid, 1.0 - fg, 0.0).astype(BF16)
    lf_ref[...] = jnp.where(valid, jnp.log(fg), 0.0)
    i_ref[...] = jnp.dot(xn, wi_ref[...], preferred_element_type=F32).astype(BF16)
    z_ref[...] = jnp.dot(xn, wz_ref[...], preferred_element_type=F32).astype(BF16)


def _odd_in(h, gain, w_in, lb, lp, seq, tm):
    rows = h.shape[0]
    wb16 = w_in.astype(BF16)
    ws = [wb16[:, j * D_MODEL:(j + 1) * D_MODEL] for j in range(4)]
    row_spec = pl.BlockSpec((tm, D_MODEL), lambda i: (i, 0))
    wspec = _const_spec((D_MODEL, D_MODEL))
    vec = _const_spec((1, D_MODEL))
    sd = lambda dt: jax.ShapeDtypeStruct((rows, D_MODEL), dt)
    return pl.pallas_call(
        functools.partial(_odd_in_kernel, tm=tm, lp=lp, seq=seq),
        grid=(rows // tm,),
        in_specs=[row_spec, vec, wspec, wspec, wspec, wspec, vec],
        out_specs=(row_spec,) * 5,
        out_shape=(sd(BF16), sd(BF16), sd(F32), sd(BF16), sd(BF16)),
        compiler_params=pltpu.CompilerParams(dimension_semantics=("arbitrary",),
                                             vmem_limit_bytes=VMEM_LIMIT),
        name="odd_in_proj",
    )(h, gain[None, :].astype(F32), *ws, lb[None, :].astype(F32))


SUB = 8
HALVES = (32, 16, 8)


def _hgrn_kernel(q_ref, k_ref, lf_ref, i_ref, z_ref, og_ref, sel_ref, o_ref, s_ref, p_ref, kf_ref, gf_ref,
                 *, steps_per_batch):
    n = ROW_TILE
    step = pl.program_id(0) % steps_per_batch

    @pl.when(step == 0)
    def _():
        s_ref[...] = jnp.zeros_like(s_ref)

    r, c, same = _chunk_masks(n)
    incl = same & (r >= c)
    gc_all = _mask_dot(incl, lf_ref[...]) * LOG2E
    rrow = lax.broadcasted_iota(jnp.int32, (n, 1), 0)
    inner = rrow % SUB
    level_mask = [(r // (2 * half) == c // (2 * half)).astype(F32) for half in HALVES]
    diag_mask = (r // SUB == c // SUB).astype(F32)

    kf_ref[...] = k_ref[...].astype(F32)
    gf_ref[...] = gc_all

    def block_rows(ref, sl, size, j):
        return jnp.concatenate([jnp.broadcast_to(ref[pl.ds(b * size + j, 1), sl], (size, C_DK))
                                for b in range(n // size)], axis=0)

    def products(h):
        sl = slice(h * C_DK, (h + 1) * C_DK)
        q = q_ref[:, sl].astype(F32)
        k = kf_ref[:, sl]
        gc = gc_all[:, sl]
        gl = block_rows(gf_ref, sl, CHUNK, CHUNK - 1)
        qg = q * jnp.exp2(gc)
        kdec = k * jnp.exp2(gl - gc)
        parts = []
        for half in HALVES:
            gref = block_rows(gf_ref, sl, 2 * half, half - 1)
            upper = (rrow % (2 * half)) >= half
            qt = q * jnp.exp2(jnp.where(upper, gc - gref, -jnp.inf))
            kt = k * jnp.exp2(jnp.where(upper, -jnp.inf, gref - gc))
            parts.append(_bdot_nt(qt, kt))
        pbuf = p_ref.at[h % 2]
        for j in range(SUB):
            kj = block_rows(kf_ref, sl, SUB, j)
            gj = block_rows(gf_ref, sl, SUB, j)
            pbuf[:, j * C_DK:(j + 1) * C_DK] = (
                q * kj * jnp.exp2(jnp.where(inner >= j, gc - gj, -jnp.inf))).astype(BF16)
        diag = jnp.dot(pbuf[...], sel_ref[...], preferred_element_type=F32)
        return parts, diag, qg, kdec, gl

    def finish(h, parts, diag, qg, kdec, gl):
        sl = slice(h * C_DK, (h + 1) * C_DK)
        iv = i_ref[:, sl]
        att = diag_mask * diag
        for mask, part in zip(level_mask, parts):
            att = att + mask * part
        st = s_ref[h]
        inter = []
        for ci in range(n // CHUNK):
            rs = slice(ci * CHUNK, (ci + 1) * CHUNK)
            inter.append(_bdot_nt(qg[rs], st))
            st = st * jnp.exp2(gl[ci * CHUNK:ci * CHUNK + 1, :]) + _bdot_tn(iv[rs], kdec[rs])
        s_ref[h] = st
        o = jnp.concatenate(inter, axis=0) + _bdot(att, iv)
        o_ref[:, sl] = (_rms(o, og_ref[...]) * _silu(z_ref[:, sl].astype(F32))).astype(o_ref.dtype)

    for h in range(C_HEADS):
        finish(h, *products(h))


def _hgrn2(q, k, lf, iv, z, out_gain, lp):
    rows = q.shape[0]
    n = ROW_TILE
    row_spec = pl.BlockSpec((n, D_MODEL), lambda i: (i, 0))
    sel = ((jnp.arange(SUB * C_DK)[:, None] // C_DK) == (jnp.arange(n)[None, :] % SUB)).astype(BF16)
    return pl.pallas_call(
        functools.partial(_hgrn_kernel, steps_per_batch=lp // n),
        grid=(rows // n,),
        in_specs=[row_spec] * 5 + [_const_spec((1, C_DK)), _const_spec(sel.shape)],
        out_specs=row_spec,
        out_shape=jax.ShapeDtypeStruct((rows, D_MODEL), BF16),
        scratch_shapes=[pltpu.VMEM((C_HEADS, C_DK, C_DK), F32), pltpu.VMEM((2, n, SUB * C_DK), BF16),
                        pltpu.VMEM((n, D_MODEL), F32), pltpu.VMEM((n, D_MODEL), F32)],
        compiler_params=pltpu.CompilerParams(dimension_semantics=("arbitrary",),
                                             vmem_limit_bytes=VMEM_LIMIT),
        name="hgrn2",
    )(q, k, lf, iv, z, out_gain[None, :].astype(F32), sel)


def _odd_out_kernel(h_ref, o_ref, wo_ref, g_ref, wr_ref, h3_ref, xn_ref, route_ref, *, tm, lp, seq):
    h3 = h_ref[...] + jnp.dot(o_ref[...], wo_ref[...], preferred_element_type=F32)
    h3 = jnp.where(_row_valid(tm, lp, seq), h3, 0.0)
    h3_ref[...] = h3
    xn = _rms(h3, g_ref[...])
    xn_ref[...] = xn
    logits = jnp.dot(xn.astype(BF16), wr_ref[...], preferred_element_type=F32)
    lane = lax.broadcasted_iota(jnp.int32, logits.shape, 1)
    logits = jnp.where(lane < N_EXPERTS, logits, -jnp.inf)
    v1 = jnp.max(logits, axis=-1, keepdims=True)
    e1 = jnp.min(jnp.where(logits == v1, lane, LANES), axis=-1, keepdims=True)
    rest = jnp.where(lane == e1, -jnp.inf, logits)
    v2 = jnp.max(rest, axis=-1, keepdims=True)
    e2 = jnp.min(jnp.where(rest == v2, lane, LANES), axis=-1, keepdims=True)
    t = jnp.exp(v2 - v1)
    g1 = 1.0 / (1.0 + t)
    g2 = t * g1
    route_ref[...] = jnp.where(lane == 0, e1.astype(F32),
                               jnp.where(lane == 1, e2.astype(F32),
                                         jnp.where(lane == 2, g1, jnp.where(lane == 3, g2, 0.0))))


def _odd_out(h, o, w_out, gain, router, lp, seq, tm):
    rows = h.shape[0]
    wr = jnp.pad(router.astype(BF16), ((0, 0), (0, LANES - N_EXPERTS)))
    row_spec = lambda w: pl.BlockSpec((tm, w), lambda i: (i, 0))
    return pl.pallas_call(
        functools.partial(_odd_out_kernel, tm=tm, lp=lp, seq=seq),
        grid=(rows // tm,),
        in_specs=[row_spec(D_MODEL), row_spec(D_MODEL), _const_spec((D_MODEL, D_MODEL)),
                  _const_spec((1, D_MODEL)), _const_spec((D_MODEL, LANES))],
        out_specs=(row_spec(D_MODEL), row_spec(D_MODEL), row_spec(LANES)),
        out_shape=(jax.ShapeDtypeStruct((rows, D_MODEL), F32), jax.ShapeDtypeStruct((rows, D_MODEL), F32),
                   jax.ShapeDtypeStruct((rows, LANES), F32)),
        compiler_params=pltpu.CompilerParams(dimension_semantics=("arbitrary",),
                                             vmem_limit_bytes=VMEM_LIMIT),
        name="odd_out_route",
    )(h, o, w_out.astype(BF16), gain[None, :].astype(F32), wr)


MOE_TILE = 512
MOE_FF_STEPS = 2


def _moe_plan(route, tile):
    n_tok = route.shape[0]
    flat_e = route[:, :2].astype(jnp.int32).reshape(-1)
    na = flat_e.shape[0]
    counts = jnp.sum(flat_e[:, None] == jnp.arange(N_EXPERTS, dtype=jnp.int32)[None, :], axis=0,
                     dtype=jnp.int32)
    order = jnp.argsort(flat_e, stable=True).astype(jnp.int32)
    first = jnp.cumsum(counts) - counts
    padded = (counts + tile - 1) // tile * tile
    pend = jnp.cumsum(padded)
    n_tiles = -(-(na + N_EXPERTS * (tile - 1)) // tile) + 2
    n_rows = n_tiles * tile
    tile_e = jnp.minimum(jnp.searchsorted(pend, jnp.arange(n_tiles, dtype=jnp.int32) * tile, side='right'),
                         N_EXPERTS - 1).astype(jnp.int32)
    row = jnp.arange(n_rows, dtype=jnp.int32)
    row_e = jnp.repeat(tile_e, tile)
    local = row - (pend - padded)[row_e]
    real = local < counts[row_e]
    pair = order[jnp.minimum(first[row_e] + local, na - 1)]
    row_tok = jnp.where(real, pair // 2, 0)
    dump = na + ((row // tile) % 2) * tile + row % tile
    row_dst = jnp.where(real, (pair % 2) * n_tok + pair // 2, dump)
    n_used = (pend[-1] // tile).astype(jnp.int32).reshape(1)
    return row_tok, row_dst, tile_e, n_used, n_tiles


def _moe_ffn_kernel(te_ref, nused_ref, tok0_ref, tokn_ref, dstp_ref, x_hbm, w1_ref, w3_ref, w2_ref,
                    y_hbm, xbuf, xb16, ybuf, gsem, ssem):
    i, f = pl.program_id(0), pl.program_id(1)
    tile = xb16.shape[0]
    slot = i % 2
    n_used = nused_ref[0]
    n_out = y_hbm.shape[0] - 2 * tile

    def gather_start(tok_ref, row, dst_slot):
        pltpu.make_async_copy(x_hbm.at[pl.ds(tok_ref[0, 0, row], 1), :],
                              xbuf.at[dst_slot, pl.ds(row, 1), :], gsem.at[dst_slot]).start()

    def scatter_start(row, src_slot):
        pltpu.make_async_copy(ybuf.at[src_slot, pl.ds(row, 1), :],
                              y_hbm.at[pl.ds(dstp_ref[0, 0, row], 1), :], ssem.at[src_slot]).start()

    def result_wait(s):
        pltpu.make_async_copy(ybuf.at[s], y_hbm.at[pl.ds(0, tile), :], ssem.at[s]).wait()

    @pl.when((i == 0) & (f == 0))
    def _():
        ybuf[...] = jnp.zeros(ybuf.shape, F32)
        for s in range(2):
            pltpu.make_async_copy(ybuf.at[s], y_hbm.at[pl.ds(n_out + s * tile, tile), :], ssem.at[s]).start()

        def issue(r, carry):
            gather_start(tok0_ref, r, 0)
            return carry
        lax.fori_loop(0, tile, issue, 0)

    @pl.when((f == 0) & (i >= n_used) & (i - 2 < n_used))
    def _():
        result_wait(slot)

    @pl.when((f == 0) & (i <= n_used))
    def _():
        pltpu.make_async_copy(x_hbm.at[pl.ds(0, tile), :], xbuf.at[slot], gsem.at[slot]).wait()
        xb16[...] = xbuf[slot].astype(BF16)

    def compute(step, par, first_tile):
        if step == 0:
            for r in range(tile):
                gather_start(tokn_ref, r, 1 - par)
        if step == MOE_FF_STEPS - 1 and not first_tile:
            for r in range(tile):
                scatter_start(r, 1 - par)
        x = xb16[...]
        a = jnp.dot(x, w1_ref[0], preferred_element_type=F32)
        b = jnp.dot(x, w3_ref[0], preferred_element_type=F32)
        y = jnp.dot((_silu(a) * b).astype(BF16), w2_ref[0], preferred_element_type=F32)
        if step == 0:
            result_wait(par)
            ybuf[par] = y
        else:
            ybuf[par] += y

    for step in range(MOE_FF_STEPS):
        @pl.when((i == 0) & (f == step))
        def _():
            compute(step, 0, True)

        for par in range(2):
            @pl.when((i > 0) & (i < n_used) & (f == step) & (slot == par))
            def _():
                compute(step, par, False)

    @pl.when((i == n_used) & (f == 0))
    def _():
        def issue(r, carry):
            scatter_start(r, 1 - slot)
            return carry
        lax.fori_loop(0, tile, issue, 0)


def _moe_ffn(xn, w1, w3, w2, row_tok, row_dst, tile_e, n_used, n_tiles, tile):
    n_tok = xn.shape[0]
    steps = MOE_FF_STEPS
    ff = D_FF_EXPERT // steps
    fi = lambda i, f, nu: jnp.where(i < nu[0], f, steps - 1)
    tok_spec = lambda imap: pl.BlockSpec((1, 1, tile), imap, memory_space=pltpu.SMEM)
    grid_spec = pltpu.PrefetchScalarGridSpec(
        num_scalar_prefetch=2,
        grid=(n_tiles, steps),
        in_specs=[tok_spec(lambda i, f, te, nu: (0, 0, 0)),
                  tok_spec(lambda i, f, te, nu: (jnp.minimum(i + 1, n_tiles - 1), 0, 0)),
                  tok_spec(lambda i, f, te, nu: (jnp.maximum(i - 1, 0), 0, 0)),
                  pl.BlockSpec(memory_space=pl.ANY),
                  pl.BlockSpec((1, D_MODEL, ff), lambda i, f, te, nu: (te[i], 0, fi(i, f, nu))),
                  pl.BlockSpec((1, D_MODEL, ff), lambda i, f, te, nu: (te[i], 0, fi(i, f, nu))),
                  pl.BlockSpec((1, ff, D_MODEL), lambda i, f, te, nu: (te[i], fi(i, f, nu), 0))],
        out_specs=pl.BlockSpec(memory_space=pl.ANY),
        scratch_shapes=[pltpu.VMEM((2, tile, D_MODEL), F32), pltpu.VMEM((tile, D_MODEL), BF16),
                        pltpu.VMEM((2, tile, D_MODEL), F32),
                        pltpu.SemaphoreType.DMA((2,)), pltpu.SemaphoreType.DMA((2,))],
    )
    tok3 = row_tok.reshape(n_tiles, 1, tile)
    return pl.pallas_call(
        _moe_ffn_kernel,
        grid_spec=grid_spec,
        out_shape=jax.ShapeDtypeStruct((2 * n_tok + 2 * tile, D_MODEL), F32),
        compiler_params=pltpu.CompilerParams(dimension_semantics=("arbitrary", "arbitrary"),
                                             vmem_limit_bytes=VMEM_LIMIT),
        name="moe_ffn",
    )(tile_e, n_used, tok3, tok3, row_dst.reshape(n_tiles, 1, tile), xn, w1, w3, w2)


def _moe_combine_kernel(h_ref, y0_ref, y1_ref, route_ref, o_ref):
    g = route_ref[...]
    o_ref[0] = h_ref[...] + g[:, 2:3] * y0_ref[...] + g[:, 3:4] * y1_ref[...]


def _moe_combine(h3, y, route, batch, seq, lp, tc):
    n_tok = h3.shape[0]
    frame_rows = lambda width, off: pl.BlockSpec(
        (pl.Element(tc), pl.Element(width)),
        lambda b, j: (pl.multiple_of(off + b * lp + CHUNK + j * tc, CHUNK), 0))
    return pl.pallas_call(
        _moe_combine_kernel,
        grid=(batch, seq // tc),
        in_specs=[frame_rows(D_MODEL, 0), frame_rows(D_MODEL, 0), frame_rows(D_MODEL, n_tok),
                  frame_rows(LANES, 0)],
        out_specs=pl.BlockSpec((1, tc, D_MODEL), lambda b, j: (b, j, 0)),
        out_shape=jax.ShapeDtypeStruct((batch, seq, D_MODEL), F32),
        compiler_params=pltpu.CompilerParams(dimension_semantics=("arbitrary", "arbitrary"),
                                             vmem_limit_bytes=VMEM_LIMIT),
        name="moe_combine",
    )(h3, y, y, route)


def _layout_rows(seq):
    return -(-(CHUNK + seq) // ROW_TILE) * ROW_TILE


def _proj_tile(lp):
    return next(t for t in (640, 512, 384, ROW_TILE) if lp % t == 0)


def kernel(x, meta_tokens, lb_logits, mix_norm, ffn_norm, e_w_in, a_q_gain, a_k_gain, a_lam_q1, a_lam_k1, a_lam_q2, a_lam_k2, a_sub_gain, b_conv, b_a_log, b_dt_bias, b_out_gain, e_w_out, ffn_w1, ffn_w3, ffn_w2, o_w_in, c_out_gain, o_w_out, router, moe_w1, moe_w3, moe_w2):
    batch, seq, _ = x.shape
    lp = _layout_rows(seq)
    rows = batch * lp
    tm = _proj_tile(lp)
    meta = jnp.broadcast_to(meta_tokens[None].astype(x.dtype), (batch, N_META, D_MODEL))
    h = jnp.concatenate([jnp.zeros((batch, FRONT, D_MODEL), x.dtype), meta, x,
                         jnp.zeros((batch, lp - CHUNK - seq, D_MODEL), x.dtype)], axis=1).reshape(rows, D_MODEL)

    q, k, v, bqkv, bz, gp = _even_in(h, mix_norm[0], e_w_in[0], a_q_gain[0], a_k_gain[0], lp, tm)
    lam_init = 0.8 - 0.6 * math.exp(-0.3 * 0)
    lam = (jnp.exp(jnp.sum(a_lam_q1[0].astype(F32) * a_lam_k1[0].astype(F32)))
           - jnp.exp(jnp.sum(a_lam_q2[0].astype(F32) * a_lam_k2[0].astype(F32))) + lam_init)
    o_a = _diff_attention(q, k, v, a_sub_gain[0], lam, lam_init, batch, lp, ROW_TILE)
    o_b = _gated_deltanet(bqkv, bz, gp, b_conv[0], b_a_log[0], b_dt_bias[0], b_out_gain[0], lp)
    h = _even_out(h, o_a, o_b, e_w_out[0], ffn_norm[0], ffn_w1[0], ffn_w3[0], ffn_w2[0], lp, seq, tm)

    lb_all = jnp.cumsum(jax.nn.softmax(lb_logits.astype(F32), axis=0), axis=0)
    lb = (lb_all - lb_all[0])[1]
    cq, ck, clf, ci, cz = _odd_in(h, mix_norm[1], o_w_in[0], lb, lp, seq, tm)
    o_c = _hgrn2(cq, ck, clf, ci, cz, c_out_gain[0], lp)
    h3, xn, route = _odd_out(h, o_c, o_w_out[0], ffn_norm[1], router[0], lp, seq, tm)
    row_tok, row_dst, tile_e, n_used, n_tiles = _moe_plan(route, MOE_TILE)
    y = _moe_ffn(xn, moe_w1[0].astype(BF16), moe_w3[0].astype(BF16), moe_w2[0].astype(BF16),
                 row_tok, row_dst, tile_e, n_used, n_tiles, MOE_TILE)
    return _moe_combine(h3, y, route, batch, seq, lp, ROW_TILE)
```

```python
import functools
import math

import jax
import jax.numpy as jnp
from jax import lax
from jax.experimental import pallas as pl
from jax.experimental.pallas import tpu as pltpu

D_MODEL = 1024
CHUNK = 64
N_META = 16
FRONT = CHUNK - N_META
EPS = 1e-6
ROPE_THETA = 500000.0
LOG2E = math.log2(math.e)

A_HEADS = 4
A_DK = 64
A_DV = 128
A_ROT = A_DK // 4
A_W = A_HEADS * A_DV

B_HEADS = 4
B_DK = 128
B_W = B_HEADS * B_DK
CONV_K = 4

C_HEADS = 8
C_DK = 128

D_FF = 2816
N_EXPERTS = 8
D_FF_EXPERT = 3584

LANES = 128
ROW_TILE = 256
VMEM_LIMIT = 56 * 1024 * 1024

BF16 = jnp.bfloat16
F32 = jnp.float32


def _const_spec(shape):
    return pl.BlockSpec(shape, lambda *_: (0,) * len(shape), pipeline_mode=pl.Buffered(1))


def _rms(x, gain):
    return x * lax.rsqrt(jnp.mean(x * x, axis=-1, keepdims=True) + EPS) * gain


def _silu(x):
    return x * (1.0 / (1.0 + jnp.exp(-x)))


def _sigmoid(x):
    return 1.0 / (1.0 + jnp.exp(-x))


def _row_valid(tile_rows, lp, seq):
    r = pl.program_id(0) * tile_rows + lax.broadcasted_iota(jnp.int32, (tile_rows, 1), 0)
    p = r % lp
    return (p >= FRONT) & (p < CHUNK + seq)


def _head_norm_rope(a, ones_bd, gain, cos, sin_lo, sin_hi):
    ssq = jnp.dot((a * a).astype(BF16), ones_bd, preferred_element_type=F32)
    y = a * lax.rsqrt(ssq * (1.0 / A_DK) + EPS) * gain
    up = pltpu.roll(y, A_W - A_ROT // 2, axis=1)
    dn = pltpu.roll(y, A_ROT // 2, axis=1)
    return y * cos + up * sin_lo + dn * sin_hi


def _even_in_kernel(h_ref, g_ref, wq_ref, wk_ref, wv_ref, wb_ref, wz_ref, wg_ref, bd_ref,
                    qg_ref, kg_ref, cos_ref, slo_ref, shi_ref,
                    q_ref, k_ref, v_ref, b_ref, z_ref, gp_ref):
    xn = _rms(h_ref[...], g_ref[...]).astype(BF16)
    rep = lambda t: jnp.concatenate([t] * (A_W // LANES), axis=1)
    cos, slo, shi = rep(cos_ref[...]), rep(slo_ref[...]), rep(shi_ref[...])
    bd = bd_ref[...]
    aq = jnp.dot(xn, wq_ref[...], preferred_element_type=F32)
    q_ref[...] = (_head_norm_rope(aq, bd, qg_ref[...], cos, slo, shi) * (A_DK ** -0.5 * LOG2E)).astype(BF16)
    ak = jnp.dot(xn, wk_ref[...], preferred_element_type=F32)
    k_ref[...] = _head_norm_rope(ak, bd, kg_ref[...], cos, slo, shi).astype(BF16)
    v_ref[...] = jnp.dot(xn, wv_ref[...], preferred_element_type=F32).astype(BF16)
    b_ref[...] = jnp.dot(xn, wb_ref[...], preferred_element_type=F32).astype(BF16)
    z_ref[...] = jnp.dot(xn, wz_ref[...], preferred_element_type=F32).astype(BF16)
    gp_ref[...] = jnp.dot(xn, wg_ref[...], preferred_element_type=F32)


def _even_in(h, gain, w_in, q_gain, k_gain, lp, tm):
    rows = h.shape[0]
    wb16 = w_in.astype(BF16)
    c0, c1, c2, c3, c4 = A_W, 2 * A_W, 3 * A_W, 3 * A_W + 3 * B_W, 3 * A_W + 4 * B_W
    wq, wk, wv, wb, wz = wb16[:, :c0], wb16[:, c0:c1], wb16[:, c1:c2], wb16[:, c2:c3], wb16[:, c3:c4]
    wg = jnp.pad(wb16[:, c4:], ((0, 0), (0, LANES - 2 * B_HEADS)))
    d = jnp.arange(A_W)
    ones_bd = (d[:, None] // A_DK == d[None, :] // A_DK).astype(BF16)
    half = A_ROT // 2
    inv_freq = ROPE_THETA ** (-jnp.arange(half, dtype=F32) * 2.0 / A_ROT)
    pos = (jnp.arange(lp) - FRONT).astype(F32)
    dim = jnp.arange(LANES) % A_DK
    ang = pos[:, None] * inv_freq[dim % half][None, :]
    cos_t = jnp.where(dim < A_ROT, jnp.cos(ang), 1.0)
    slo_t = jnp.where(dim < half, -jnp.sin(ang), 0.0)
    shi_t = jnp.where((dim >= half) & (dim < A_ROT), jnp.sin(ang), 0.0)
    tile8 = lambda g: jnp.tile(g.astype(F32), 2 * A_HEADS)[None, :]
    n_pos = lp // tm
    row_spec = lambda w: pl.BlockSpec((tm, w), lambda i: (i, 0))
    pos_spec = pl.BlockSpec((tm, LANES), lambda i: (i % n_pos, 0))
    outs = pl.pallas_call(
        _even_in_kernel,
        grid=(rows // tm,),
        in_specs=[row_spec(D_MODEL), _const_spec((1, D_MODEL)),
                  _const_spec(wq.shape), _const_spec(wk.shape), _const_spec(wv.shape),
                  _const_spec(wb.shape), _const_spec(wz.shape), _const_spec(wg.shape),
                  _const_spec(ones_bd.shape), _const_spec((1, A_W)), _const_spec((1, A_W)),
                  pos_spec, pos_spec, pos_spec],
        out_specs=(row_spec(A_W), row_spec(A_W), row_spec(A_W), row_spec(3 * B_W), row_spec(B_W),
                   row_spec(LANES)),
        out_shape=(jax.ShapeDtypeStruct((rows, A_W), BF16), jax.ShapeDtypeStruct((rows, A_W), BF16),
                   jax.ShapeDtypeStruct((rows, A_W), BF16), jax.ShapeDtypeStruct((rows, 3 * B_W), BF16),
                   jax.ShapeDtypeStruct((rows, B_W), BF16), jax.ShapeDtypeStruct((rows, LANES), F32)),
        compiler_params=pltpu.CompilerParams(dimension_semantics=("arbitrary",),
                                             vmem_limit_bytes=VMEM_LIMIT),
        name="even_in_proj",
    )(h, gain[None, :].astype(F32), wq, wk, wv, wb, wz, wg, ones_bd, tile8(q_gain), tile8(k_gain),
      cos_t, slo_t, shi_t)
    return outs


ATTN_WIDE = 1024


ATTN_HEADS = 2


def _attn_kernel(q_ref, k_ref, v_ref, sg_ref, lam_ref, o_ref, qs_ref, vx_ref, m_ref, acc_ref,
                 sa_ref, sb_ref, *, tq, width, lam_init):
    qi = pl.program_id(2)
    heads = range(ATTN_HEADS)
    lanes = lambda s: slice(s * LANES, (s + 1) * LANES)

    @pl.when(qi == 0)
    def _():
        for s in heads:
            vx_ref[s, :, :A_DV] = v_ref[:, lanes(s)]
            vx_ref[s, :, A_DV:] = jnp.ones((vx_ref.shape[1], A_DV), BF16)

    lane = lax.broadcasted_iota(jnp.int32, (tq, LANES), 1)
    zero = jnp.zeros((tq, LANES), BF16)
    for s in heads:
        q = q_ref[:, lanes(s)]
        qs_ref[s, :tq, :] = jnp.where(lane < A_DK, q, zero)
        qs_ref[s, tq:, :] = jnp.where(lane >= A_DK, q, zero)
    m_ref[...] = jnp.full(m_ref.shape, -jnp.inf, F32)
    acc_ref[...] = jnp.zeros(acc_ref.shape, F32)

    lp = k_ref.shape[0]
    qchunk = (qi * tq + lax.broadcasted_iota(jnp.int32, (2 * tq, 1), 0) % tq) // CHUNK

    def window(t):
        return pl.multiple_of(jnp.minimum(t * width, lp - width), tq)

    def scores(s_ref, t):
        for s in heads:
            s_ref[s] = lax.dot_general(qs_ref[s], k_ref[pl.ds(window(t), width), lanes(s)],
                                       (((1,), (1,)), ((), ())), preferred_element_type=F32)

    def softmax_pv(s_ref, t, mask):
        start = window(t)
        kpos = start + lax.broadcasted_iota(jnp.int32, (1, width), 1)
        kchunk = jnp.where((kpos >= t * width) & (kpos >= FRONT), kpos // CHUNK, lp)
        nblk = width // LANES

        def block(s, c):
            cs = slice(c * LANES, (c + 1) * LANES)
            if mask == "full":
                return jnp.where(kchunk[:, cs] <= qchunk, s_ref[s, :, cs], -jnp.inf)
            if mask == "pad" and c == 0:
                return jnp.where(kpos[:, cs] >= FRONT, s_ref[s, :, cs], -jnp.inf)
            return s_ref[s, :, cs]

        for s in heads:
            mx = functools.reduce(jnp.maximum, [block(s, c) for c in range(nblk)])
            m_old = m_ref[s]
            m_new = jnp.maximum(m_old, jnp.max(mx, axis=-1, keepdims=True))
            alpha = jnp.exp2(m_old - m_new)
            p = jnp.concatenate([jnp.exp2((block(s, c) - m_new).astype(BF16)) for c in range(nblk)], axis=1)
            pv = jnp.dot(p, vx_ref[s, pl.ds(start, width), :], preferred_element_type=F32)
            acc_ref[s, :, :A_DV] = alpha * acc_ref[s, :, :A_DV] + pv[:, :A_DV]
            acc_ref[s, :, A_DV:] = alpha * acc_ref[s, :, A_DV:] + pv[:, A_DV:]
            m_ref[s] = m_new

    n_tiles = (qi * tq + tq + width - 1) // width
    n_pairs = (n_tiles + 1) // 2

    def pair(j, masks):
        scores(sb_ref, 2 * j + 1)
        softmax_pv(sa_ref, 2 * j, masks[0])
        scores(sa_ref, 2 * j + 2)
        softmax_pv(sb_ref, 2 * j + 1, masks[1])

    scores(sa_ref, 0)

    @pl.when(n_pairs > 1)
    def _():
        pair(0, ("pad", "none"))

    def inner(j, carry):
        pair(j, ("none", "none"))
        return carry
    lax.fori_loop(1, n_pairs - 1, inner, 0)

    @pl.when(n_tiles == 2 * n_pairs)
    def _():
        scores(sb_ref, 2 * n_pairs - 1)
        softmax_pv(sa_ref, 2 * n_pairs - 2, "pad")
        softmax_pv(sb_ref, 2 * n_pairs - 1, "full")

    @pl.when(n_tiles < 2 * n_pairs)
    def _():
        softmax_pv(sa_ref, 2 * n_pairs - 2, "full")

    for s in heads:
        o = acc_ref[s, :, :A_DV] / acc_ref[s, :, A_DV:]
        o = o[:tq] - lam_ref[...] * o[tq:]
        o_ref[:, lanes(s)] = (_rms(o, sg_ref[...]) * (1.0 - lam_init)).astype(o_ref.dtype)


def _diff_attention(q, k, v, sub_gain, lam, lam_init, batch, lp, tq):
    nq = lp // tq
    hw = ATTN_HEADS * LANES
    kv_spec = pl.BlockSpec((lp, hw), lambda b, h, i: (b, h), pipeline_mode=pl.Buffered(1))
    q_spec = pl.BlockSpec((tq, hw), lambda b, h, i: (b * nq + i, h))
    vec = pl.BlockSpec((1, LANES), lambda b, h, i: (0, 0))
    width = min(ATTN_WIDE, lp)
    return pl.pallas_call(
        functools.partial(_attn_kernel, tq=tq, width=width, lam_init=lam_init),
        grid=(batch, A_HEADS // ATTN_HEADS, nq),
        in_specs=[q_spec, kv_spec, kv_spec, vec, vec],
        out_specs=q_spec,
        out_shape=jax.ShapeDtypeStruct(q.shape, BF16),
        scratch_shapes=[pltpu.VMEM((ATTN_HEADS, 2 * tq, LANES), BF16),
                        pltpu.VMEM((ATTN_HEADS, lp, 2 * A_DV), BF16),
                        pltpu.VMEM((ATTN_HEADS, 2 * tq, LANES), F32),
                        pltpu.VMEM((ATTN_HEADS, 2 * tq, 2 * A_DV), F32),
                        pltpu.VMEM((ATTN_HEADS, 2 * tq, width), F32),
                        pltpu.VMEM((ATTN_HEADS, 2 * tq, width), F32)],
        compiler_params=pltpu.CompilerParams(dimension_semantics=("arbitrary",) * 3,
                                             vmem_limit_bytes=VMEM_LIMIT),
        name="diff_attention",
    )(q, k, v, sub_gain[None, :].astype(F32), jnp.full((1, LANES), lam, F32))


def _chunk_masks(n):
    r = lax.broadcasted_iota(jnp.int32, (n, n), 0)
    c = lax.broadcasted_iota(jnp.int32, (n, n), 1)
    same = (r // CHUNK) == (c // CHUNK)
    return r, c, same


def _mask_dot(mask, x):
    hi = x.astype(BF16)
    rest = x - hi.astype(F32)
    mid = rest.astype(BF16)
    lo = (rest - mid.astype(F32)).astype(BF16)
    m = mask.astype(BF16)
    return jnp.dot(jnp.concatenate([m, m, m], axis=1), jnp.concatenate([hi, mid, lo], axis=0),
                   preferred_element_type=F32)


def _bdot(a, b):
    return jnp.dot(a.astype(BF16), b.astype(BF16), preferred_element_type=F32)


def _bdot_nt(a, b):
    return lax.dot_general(a.astype(BF16), b.astype(BF16), (((1,), (1,)), ((), ())),
                           preferred_element_type=F32)


def _bdot_tn(a, b):
    return lax.dot_general(a.astype(BF16), b.astype(BF16), (((0,), (0,)), ((), ())),
                           preferred_element_type=F32)


def _gdn_kernel(x_ref, z_ref, gp_ref, cw_ref, alog_ref, dtb_ref, og_ref, o_ref, xbuf_ref, s_ref):
    nb, n = x_ref.shape[0], ROW_TILE
    step = pl.program_id(0)

    @pl.when(step == 0)
    def _():
        xbuf_ref[:, :8, :] = jnp.zeros((nb, 8, 3 * B_W), F32)
        s_ref[...] = jnp.zeros_like(s_ref)

    r, c, same = _chunk_masks(n)
    incl = same & (r >= c)
    strict = same & (r > c)
    eye = (r == c).astype(F32)
    row = step * n + lax.broadcasted_iota(jnp.int32, (n, 1), 0)
    chunks = [slice(ci * CHUNK, (ci + 1) * CHUNK) for ci in range(n // CHUNK)]

    heads = range(nb * B_HEADS)
    p, t, rhs, qk, qg, kdec, decay = [], [], [], [], [], [], []
    for b in range(nb):
        xbuf_ref[b, 8:, :] = x_ref[b].astype(F32)
        conv = cw_ref[CONV_K - 1:CONV_K, :] * xbuf_ref[b, 8:, :]
        for j in range(CONV_K - 1):
            conv = conv + cw_ref[j:j + 1, :] * xbuf_ref[b, 5 + j:5 + j + n, :]
        xbuf_ref[b, :8, :] = xbuf_ref[b, n:n + 8, :]
        conv = _silu(conv)
        gp = gp_ref[b]
        beta_all = _sigmoid(gp)
        gpre = gp + dtb_ref[...]
        softplus = jnp.maximum(gpre, 0.0) + jnp.log(1.0 + jnp.exp(-jnp.abs(gpre)))
        g_all = jnp.where(row >= FRONT, -jnp.exp(alog_ref[...]) * softplus, 0.0)
        sums = _mask_dot(jnp.concatenate([incl, same], axis=0), g_all)
        gc_all, gl_all = sums[:n], sums[n:]
        gc_rows = gc_all.T
        for h in range(B_HEADS):
            qh = conv[:, h * B_DK:(h + 1) * B_DK]
            kh = conv[:, B_W + h * B_DK:B_W + (h + 1) * B_DK]
            vh = conv[:, 2 * B_W + h * B_DK:2 * B_W + (h + 1) * B_DK]
            qh = qh * lax.rsqrt(jnp.sum(qh * qh, axis=-1, keepdims=True) + EPS) * (B_DK ** -0.5)
            kh = kh * lax.rsqrt(jnp.sum(kh * kh, axis=-1, keepdims=True) + EPS)
            beta = beta_all[:, h:h + 1]
            gcol = gc_all[:, B_HEADS + h:B_HEADS + h + 1]
            glcol = gl_all[:, B_HEADS + h:B_HEADS + h + 1]
            grow = gc_rows[B_HEADS + h:B_HEADS + h + 1, :]
            dec = jnp.exp(jnp.where(incl, gcol - grow, -jnp.inf))
            kb = kh * beta
            egc = jnp.exp(gcol)
            p0 = -(_bdot_nt(kb, kh) * jnp.where(strict, dec, 0.0))
            p.append(p0)
            t.append(eye + p0)
            rhs.append(jnp.concatenate([vh * beta, kb * egc], axis=1))
            qk.append(_bdot_nt(qh, kh) * dec)
            qg.append(qh * egc)
            kdec.append(kh * jnp.exp(glcol - gcol))
            decay.append([jnp.exp(glcol[rs.stop - 1:rs.stop, :]) for rs in chunks])
    for _ in range(5):
        for h in heads:
            p[h] = _bdot(p[h], p[h])
            t[h] = t[h] + _bdot(t[h], p[h])
    sol = [_bdot(t[h], rhs[h]) for h in heads]
    u = [s[:, :B_DK] for s in sol]
    w = [s[:, B_DK:] for s in sol]
    kw = [[_bdot_tn(kdec[h][rs], w[h][rs]) for rs in chunks] for h in heads]
    ku = [[_bdot_tn(kdec[h][rs], u[h][rs]) for rs in chunks] for h in heads]
    state = [s_ref[h] for h in heads]
    before = [[] for _ in heads]
    for ci in range(len(chunks)):
        for h in heads:
            before[h].append(state[h])
            state[h] = state[h] * decay[h][ci] - _bdot(kw[h][ci], state[h]) + ku[h][ci]
    for h in heads:
        s_ref[h] = state[h]
        vnew, inter = [], []
        for ci, rs in enumerate(chunks):
            ws = _bdot(jnp.concatenate([w[h][rs], qg[h][rs]], axis=0), before[h][ci])
            vnew.append(u[h][rs] - ws[:CHUNK])
            inter.append(ws[CHUNK:])
        o = jnp.concatenate(inter, axis=0) + _bdot(qk[h], jnp.concatenate(vnew, axis=0))
        b, sl = h // B_HEADS, slice((h % B_HEADS) * B_DK, (h % B_HEADS + 1) * B_DK)
        o_ref[b, :, sl] = (_rms(o, og_ref[...]) * _silu(z_ref[b, :, sl].astype(F32))).astype(o_ref.dtype)


def _gated_deltanet(bqkv, bz, gp, conv_w, a_log, dt_bias, out_gain, lp):
    rows = bqkv.shape[0]
    n, nb = ROW_TILE, rows // lp
    head_vec = lambda p: jnp.zeros((1, LANES), F32).at[0, B_HEADS:2 * B_HEADS].set(p.astype(F32))
    per_batch = lambda a: a.reshape(nb, lp, a.shape[-1])
    row_spec = lambda w: pl.BlockSpec((nb, n, w), lambda i: (0, i, 0))
    out = pl.pallas_call(
        _gdn_kernel,
        grid=(lp // n,),
        in_specs=[row_spec(3 * B_W), row_spec(B_W), row_spec(LANES), _const_spec((CONV_K, 3 * B_W)),
                  _const_spec((1, LANES)), _const_spec((1, LANES)), _const_spec((1, B_DK))],
        out_specs=row_spec(B_W),
        out_shape=jax.ShapeDtypeStruct((nb, lp, B_W), BF16),
        scratch_shapes=[pltpu.VMEM((nb, n + 8, 3 * B_W), F32), pltpu.VMEM((nb * B_HEADS, B_DK, B_DK), F32)],
        compiler_params=pltpu.CompilerParams(dimension_semantics=("arbitrary",),
                                             vmem_limit_bytes=VMEM_LIMIT),
        name="gated_deltanet",
    )(per_batch(bqkv), per_batch(bz), per_batch(gp), conv_w.astype(F32), head_vec(a_log), head_vec(dt_bias),
      out_gain[None, :].astype(F32))
    return out.reshape(rows, B_W)


FF_COLS = 256


def _even_out_kernel(h_ref, oa_ref, ob_ref, woa_ref, wob_ref, g_ref, w1_ref, w3_ref, w2_ref, o_ref,
                     act_ref, *, tm, lp, seq):
    h1 = (h_ref[...] + jnp.dot(oa_ref[...], woa_ref[...], preferred_element_type=F32)
          + jnp.dot(ob_ref[...], wob_ref[...], preferred_element_type=F32))
    xn = _rms(h1, g_ref[...]).astype(BF16)
    for c in range(D_FF // FF_COLS):
        cs = slice(c * FF_COLS, (c + 1) * FF_COLS)
        a = jnp.dot(xn, w1_ref[:, cs], preferred_element_type=F32)
        b = jnp.dot(xn, w3_ref[:, cs], preferred_element_type=F32)
        act_ref[:, cs] = (_silu(a) * b).astype(BF16)
    out = h1 + jnp.dot(act_ref[...], w2_ref[...], preferred_element_type=F32)
    o_ref[...] = jnp.where(_row_valid(tm, lp, seq), out, 0.0)


def _even_out(h, oa, ob, w_out, gain, w1, w3, w2, lp, seq, tm):
    rows = h.shape[0]
    wo = w_out.astype(BF16)
    row_spec = lambda w: pl.BlockSpec((tm, w), lambda i: (i, 0))
    return pl.pallas_call(
        functools.partial(_even_out_kernel, tm=tm, lp=lp, seq=seq),
        grid=(rows // tm,),
        in_specs=[row_spec(D_MODEL), row_spec(A_W), row_spec(B_W),
                  _const_spec((A_W, D_MODEL)), _const_spec((B_W, D_MODEL)), _const_spec((1, D_MODEL)),
                  _const_spec(w1.shape), _const_spec(w3.shape), _const_spec(w2.shape)],
        out_specs=row_spec(D_MODEL),
        out_shape=jax.ShapeDtypeStruct((rows, D_MODEL), F32),
        scratch_shapes=[pltpu.VMEM((tm, D_FF), BF16)],
        compiler_params=pltpu.CompilerParams(dimension_semantics=("arbitrary",),
                                             vmem_limit_bytes=VMEM_LIMIT),
        name="even_out_ffn",
    )(h, oa, ob, wo[:A_W], wo[A_W:], gain[None, :].astype(F32),
      w1.astype(BF16), w3.astype(BF16), w2.astype(BF16))


def _odd_in_kernel(h_ref, g_ref, wq_ref, wf_ref, wi_ref, wz_ref, lb_ref,
                   q_ref, k_ref, lf_ref, i_ref, z_ref, *, tm, lp, seq):
    xn = _rms(h_ref[...], g_ref[...]).astype(BF16)
    valid = _row_valid(tm, lp, seq)
    q_ref[...] = jnp.dot(xn, wq_ref[...], preferred_element_type=F32).astype(BF16)
    f = jnp.dot(xn, wf_ref[...], preferred_element_type=F32)
    lb = lb_ref[...]
    fg = lb + (1.0 - lb) * _sigmoid(f)
    k_ref[...] = jnp.where(valid, 1.0 - fg, 0.0).astype(BF16)
    lf_ref[...] = jnp.where(valid, jnp.log(fg), 0.0)
    i_ref[...] = jnp.dot(xn, wi_ref[...], preferred_element_type=F32).astype(BF16)
    z_ref[...] = jnp.dot(xn, wz_ref[...], preferred_element_type=F32).astype(BF16)


def _odd_in(h, gain, w_in, lb, lp, seq, tm):
    rows = h.shape[0]
    wb16 = w_in.astype(BF16)
    ws = [wb16[:, j * D_MODEL:(j + 1) * D_MODEL] for j in range(4)]
    row_spec = pl.BlockSpec((tm, D_MODEL), lambda i: (i, 0))
    wspec = _const_spec((D_MODEL, D_MODEL))
    vec = _const_spec((1, D_MODEL))
    sd = lambda dt: jax.ShapeDtypeStruct((rows, D_MODEL), dt)
    return pl.pallas_call(
        functools.partial(_odd_in_kernel, tm=tm, lp=lp, seq=seq),
        grid=(rows // tm,),
        in_specs=[row_spec, vec, wspec, wspec, wspec, wspec, vec],
        out_specs=(row_spec,) * 5,
        out_shape=(sd(BF16), sd(BF16), sd(F32), sd(BF16), sd(BF16)),
        compiler_params=pltpu.CompilerParams(dimension_semantics=("arbitrary",),
                                             vmem_limit_bytes=VMEM_LIMIT),
        name="odd_in_proj",
    )(h, gain[None, :].astype(F32), *ws, lb[None, :].astype(F32))


SUB = 8
HALVES = (32, 16, 8)


def _hgrn_kernel(q_ref, k_ref, lf_ref, i_ref, z_ref, og_ref, sel_ref, o_ref, s_ref, p_ref, kf_ref, gf_ref):
    n = ROW_TILE
    nb = q_ref.shape[0]
    step = pl.program_id(0)

    @pl.when(step == 0)
    def _():
        s_ref[...] = jnp.zeros_like(s_ref)

    r, c, same = _chunk_masks(n)
    incl = same & (r >= c)
    rrow = lax.broadcasted_iota(jnp.int32, (n, 1), 0)
    inner = rrow % SUB
    level_mask = [(r // (2 * half) == c // (2 * half)).astype(F32) for half in HALVES]
    diag_mask = (r // SUB == c // SUB).astype(F32)

    gc_alls = [_mask_dot(incl, lf_ref[b]) * LOG2E for b in range(nb)]
    for b in range(nb):
        kf_ref[b] = k_ref[b].astype(F32)
        gf_ref[b] = gc_alls[b]

    def block_rows(ref, b, sl, size, j):
        return jnp.concatenate([jnp.broadcast_to(ref[b, pl.ds(blk * size + j, 1), sl], (size, C_DK))
                                for blk in range(n // size)], axis=0)

    def products(b, h):
        sl = slice(h * C_DK, (h + 1) * C_DK)
        q = q_ref[b, :, sl].astype(F32)
        k = kf_ref[b, :, sl]
        gc = gc_alls[b][:, sl]
        gl = block_rows(gf_ref, b, sl, CHUNK, CHUNK - 1)
        qg = q * jnp.exp2(gc)
        kdec = k * jnp.exp2(gl - gc)
        parts = []
        for half in HALVES:
            gref = block_rows(gf_ref, b, sl, 2 * half, half - 1)
            upper = (rrow % (2 * half)) >= half
            qt = q * jnp.exp2(jnp.where(upper, gc - gref, -jnp.inf))
            kt = k * jnp.exp2(jnp.where(upper, -jnp.inf, gref - gc))
            parts.append(_bdot_nt(qt, kt))
        pbuf = p_ref.at[h % 2]
        for j in range(SUB):
            kj = block_rows(kf_ref, b, sl, SUB, j)
            gj = block_rows(gf_ref, b, sl, SUB, j)
            pbuf[:, j * C_DK:(j + 1) * C_DK] = (
                q * kj * jnp.exp2(jnp.where(inner >= j, gc - gj, -jnp.inf))).astype(BF16)
        diag = jnp.dot(pbuf[...], sel_ref[...], preferred_element_type=F32)
        return parts, diag, qg, kdec, gl

    def finish(b, h, parts, diag, qg, kdec, gl):
        sl = slice(h * C_DK, (h + 1) * C_DK)
        iv = i_ref[b, :, sl]
        att = diag_mask * diag
        for mask, part in zip(level_mask, parts):
            att = att + mask * part
        st = s_ref[b * C_HEADS + h]
        inter = []
        for ci in range(n // CHUNK):
            rs = slice(ci * CHUNK, (ci + 1) * CHUNK)
            inter.append(_bdot_nt(qg[rs], st))
            st = st * jnp.exp2(gl[ci * CHUNK:ci * CHUNK + 1, :]) + _bdot_tn(iv[rs], kdec[rs])
        s_ref[b * C_HEADS + h] = st
        o = jnp.concatenate(inter, axis=0) + _bdot(att, iv)
        o_ref[b, :, sl] = (_rms(o, og_ref[...]) * _silu(z_ref[b, :, sl].astype(F32))).astype(o_ref.dtype)

    for b in range(nb):
        for h in range(C_HEADS):
            finish(b, h, *products(b, h))


def _hgrn2(q, k, lf, iv, z, out_gain, lp):
    rows = q.shape[0]
    n, nb = ROW_TILE, rows // lp
    per_batch = lambda a: a.reshape(nb, lp, D_MODEL)
    row_spec = pl.BlockSpec((nb, n, D_MODEL), lambda i: (0, i, 0))
    sel = ((jnp.arange(SUB * C_DK)[:, None] // C_DK) == (jnp.arange(n)[None, :] % SUB)).astype(BF16)
    out = pl.pallas_call(
        _hgrn_kernel,
        grid=(lp // n,),
        in_specs=[row_spec] * 5 + [_const_spec((1, C_DK)), _const_spec(sel.shape)],
        out_specs=row_spec,
        out_shape=jax.ShapeDtypeStruct((nb, lp, D_MODEL), BF16),
        scratch_shapes=[pltpu.VMEM((nb * C_HEADS, C_DK, C_DK), F32), pltpu.VMEM((2, n, SUB * C_DK), BF16),
                        pltpu.VMEM((nb, n, D_MODEL), F32), pltpu.VMEM((nb, n, D_MODEL), F32)],
        compiler_params=pltpu.CompilerParams(dimension_semantics=("arbitrary",),
                                             vmem_limit_bytes=VMEM_LIMIT),
        name="hgrn2",
    )(per_batch(q), per_batch(k), per_batch(lf), per_batch(iv), per_batch(z), out_gain[None, :].astype(F32), sel)
    return out.reshape(rows, D_MODEL)


def _odd_out_kernel(h_ref, o_ref, wo_ref, g_ref, wr_ref, h3_ref, xn_ref, route_ref, *, tm, lp, seq):
    h3 = h_ref[...] + jnp.dot(o_ref[...], wo_ref[...], preferred_element_type=F32)
    h3 = jnp.where(_row_valid(tm, lp, seq), h3, 0.0)
    h3_ref[...] = h3
    xn = _rms(h3, g_ref[...])
    xn_ref[...] = xn
    logits = jnp.dot(xn.astype(BF16), wr_ref[...], preferred_element_type=F32)
    lane = lax.broadcasted_iota(jnp.int32, logits.shape, 1)
    logits = jnp.where(lane < N_EXPERTS, logits, -jnp.inf)
    v1 = jnp.max(logits, axis=-1, keepdims=True)
    e1 = jnp.min(jnp.where(logits == v1, lane, LANES), axis=-1, keepdims=True)
    rest = jnp.where(lane == e1, -jnp.inf, logits)
    v2 = jnp.max(rest, axis=-1, keepdims=True)
    e2 = jnp.min(jnp.where(rest == v2, lane, LANES), axis=-1, keepdims=True)
    t = jnp.exp(v2 - v1)
    g1 = 1.0 / (1.0 + t)
    g2 = t * g1
    route_ref[...] = jnp.where(lane == 0, e1.astype(F32),
                               jnp.where(lane == 1, e2.astype(F32),
                                         jnp.where(lane == 2, g1, jnp.where(lane == 3, g2, 0.0))))


def _odd_out(h, o, w_out, gain, router, lp, seq, tm):
    rows = h.shape[0]
    wr = jnp.pad(router.astype(BF16), ((0, 0), (0, LANES - N_EXPERTS)))
    row_spec = lambda w: pl.BlockSpec((tm, w), lambda i: (i, 0))
    return pl.pallas_call(
        functools.partial(_odd_out_kernel, tm=tm, lp=lp, seq=seq),
        grid=(rows // tm,),
        in_specs=[row_spec(D_MODEL), row_spec(D_MODEL), _const_spec((D_MODEL, D_MODEL)),
                  _const_spec((1, D_MODEL)), _const_spec((D_MODEL, LANES))],
        out_specs=(row_spec(D_MODEL), row_spec(D_MODEL), row_spec(LANES)),
        out_shape=(jax.ShapeDtypeStruct((rows, D_MODEL), F32), jax.ShapeDtypeStruct((rows, D_MODEL), F32),
                   jax.ShapeDtypeStruct((rows, LANES), F32)),
        compiler_params=pltpu.CompilerParams(dimension_semantics=("arbitrary",),
                                             vmem_limit_bytes=VMEM_LIMIT),
        name="odd_out_route",
    )(h, o, w_out.astype(BF16), gain[None, :].astype(F32), wr)


MOE_TILE = 512
MOE_FF_STEPS = 2


def _moe_plan(route, tile):
    n_tok = route.shape[0]
    flat_e = route[:, :2].astype(jnp.int32).reshape(-1)
    na = flat_e.shape[0]
    counts = jnp.sum(flat_e[:, None] == jnp.arange(N_EXPERTS, dtype=jnp.int32)[None, :], axis=0,
                     dtype=jnp.int32)
    order = jnp.argsort(flat_e, stable=True).astype(jnp.int32)
    first = jnp.cumsum(counts) - counts
    padded = (counts + tile - 1) // tile * tile
    pend = jnp.cumsum(padded)
    n_tiles = -(-(na + N_EXPERTS * (tile - 1)) // tile) + 2
    n_rows = n_tiles * tile
    tile_e = jnp.minimum(jnp.searchsorted(pend, jnp.arange(n_tiles, dtype=jnp.int32) * tile, side='right'),
                         N_EXPERTS - 1).astype(jnp.int32)
    row = jnp.arange(n_rows, dtype=jnp.int32)
    row_e = jnp.repeat(tile_e, tile)
    local = row - (pend - padded)[row_e]
    real = local < counts[row_e]
    pair = order[jnp.minimum(first[row_e] + local, na - 1)]
    row_tok = jnp.where(real, pair // 2, 0)
    dump = na + ((row // tile) % 2) * tile + row % tile
    row_dst = jnp.where(real, (pair % 2) * n_tok + pair // 2, dump)
    n_used = (pend[-1] // tile).astype(jnp.int32).reshape(1)
    return row_tok, row_dst, tile_e, n_used, n_tiles


def _moe_ffn_kernel(te_ref, nused_ref, tok0_ref, tokn_ref, dstp_ref, x_hbm, w1_ref, w3_ref, w2_ref,
                    y_hbm, xbuf, xb16, ybuf, gsem, ssem):
    i, f = pl.program_id(0), pl.program_id(1)
    tile = xb16.shape[0]
    slot = i % 2
    n_used = nused_ref[0]
    n_out = y_hbm.shape[0] - 2 * tile

    def gather_start(tok_ref, row, dst_slot):
        pltpu.make_async_copy(x_hbm.at[pl.ds(tok_ref[0, 0, row], 1), :],
                              xbuf.at[dst_slot, pl.ds(row, 1), :], gsem.at[dst_slot]).start()

    def scatter_start(row, src_slot):
        pltpu.make_async_copy(ybuf.at[src_slot, pl.ds(row, 1), :],
                              y_hbm.at[pl.ds(dstp_ref[0, 0, row], 1), :], ssem.at[src_slot]).start()

    def result_wait(s):
        pltpu.make_async_copy(ybuf.at[s], y_hbm.at[pl.ds(0, tile), :], ssem.at[s]).wait()

    @pl.when((i == 0) & (f == 0))
    def _():
        ybuf[...] = jnp.zeros(ybuf.shape, F32)
        for s in range(2):
            pltpu.make_async_copy(ybuf.at[s], y_hbm.at[pl.ds(n_out + s * tile, tile), :], ssem.at[s]).start()

        def issue(r, carry):
            gather_start(tok0_ref, r, 0)
            return carry
        lax.fori_loop(0, tile, issue, 0)

    @pl.when((f == 0) & (i >= n_used) & (i - 2 < n_used))
    def _():
        result_wait(slot)

    @pl.when((f == 0) & (i <= n_used))
    def _():
        pltpu.make_async_copy(x_hbm.at[pl.ds(0, tile), :], xbuf.at[slot], gsem.at[slot]).wait()
        xb16[...] = xbuf[slot].astype(BF16)

    def compute(step, par, first_tile):
        if step == 0:
            for r in range(tile):
                gather_start(tokn_ref, r, 1 - par)
        if step == MOE_FF_STEPS - 1 and not first_tile:
            for r in range(tile):
                scatter_start(r, 1 - par)
        x = xb16[...]
        a = jnp.dot(x, w1_ref[0], preferred_element_type=F32)
        b = jnp.dot(x, w3_ref[0], preferred_element_type=F32)
        y = jnp.dot((_silu(a) * b).astype(BF16), w2_ref[0], preferred_element_type=F32)
        if step == 0:
            result_wait(par)
            ybuf[par] = y
        else:
            ybuf[par] += y

    for step in range(MOE_FF_STEPS):
        @pl.when((i == 0) & (f == step))
        def _():
            compute(step, 0, True)

        for par in range(2):
            @pl.when((i > 0) & (i < n_used) & (f == step) & (slot == par))
            def _():
                compute(step, par, False)

    @pl.when((i == n_used) & (f == 0))
    def _():
        def issue(r, carry):
            scatter_start(r, 1 - slot)
            return carry
        lax.fori_loop(0, tile, issue, 0)


def _moe_ffn(xn, w1, w3, w2, row_tok, row_dst, tile_e, n_used, n_tiles, tile):
    n_tok = xn.shape[0]
    steps = MOE_FF_STEPS
    ff = D_FF_EXPERT // steps
    fi = lambda i, f, nu: jnp.where(i < nu[0], f, steps - 1)
    tok_spec = lambda imap: pl.BlockSpec((1, 1, tile), imap, memory_space=pltpu.SMEM)
    grid_spec = pltpu.PrefetchScalarGridSpec(
        num_scalar_prefetch=2,
        grid=(n_tiles, steps),
        in_specs=[tok_spec(lambda i, f, te, nu: (0, 0, 0)),
                  tok_spec(lambda i, f, te, nu: (jnp.minimum(i + 1, n_tiles - 1), 0, 0)),
                  tok_spec(lambda i, f, te, nu: (jnp.maximum(i - 1, 0), 0, 0)),
                  pl.BlockSpec(memory_space=pl.ANY),
                  pl.BlockSpec((1, D_MODEL, ff), lambda i, f, te, nu: (te[i], 0, fi(i, f, nu))),
                  pl.BlockSpec((1, D_MODEL, ff), lambda i, f, te, nu: (te[i], 0, fi(i, f, nu))),
                  pl.BlockSpec((1, ff, D_MODEL), lambda i, f, te, nu: (te[i], fi(i, f, nu), 0))],
        out_specs=pl.BlockSpec(memory_space=pl.ANY),
        scratch_shapes=[pltpu.VMEM((2, tile, D_MODEL), F32), pltpu.VMEM((tile, D_MODEL), BF16),
                        pltpu.VMEM((2, tile, D_MODEL), F32),
                        pltpu.SemaphoreType.DMA((2,)), pltpu.SemaphoreType.DMA((2,))],
    )
    tok3 = row_tok.reshape(n_tiles, 1, tile)
    return pl.pallas_call(
        _moe_ffn_kernel,
        grid_spec=grid_spec,
        out_shape=jax.ShapeDtypeStruct((2 * n_tok + 2 * tile, D_MODEL), F32),
        compiler_params=pltpu.CompilerParams(dimension_semantics=("arbitrary", "arbitrary"),
                                             vmem_limit_bytes=VMEM_LIMIT),
        name="moe_ffn",
    )(tile_e, n_used, tok3, tok3, row_dst.reshape(n_tiles, 1, tile), xn, w1, w3, w2)


def _moe_combine_kernel(h_ref, y0_ref, y1_ref, route_ref, o_ref):
    g = route_ref[...]
    o_ref[0] = h_ref[...] + g[:, 2:3] * y0_ref[...] + g[:, 3:4] * y1_ref[...]


def _moe_combine(h3, y, route, batch, seq, lp, tc):
    n_tok = h3.shape[0]
    frame_rows = lambda width, off: pl.BlockSpec(
        (pl.Element(tc), pl.Element(width)),
        lambda b, j: (pl.multiple_of(off + b * lp + CHUNK + j * tc, CHUNK), 0))
    return pl.pallas_call(
        _moe_combine_kernel,
        grid=(batch, seq // tc),
        in_specs=[frame_rows(D_MODEL, 0), frame_rows(D_MODEL, 0), frame_rows(D_MODEL, n_tok),
                  frame_rows(LANES, 0)],
        out_specs=pl.BlockSpec((1, tc, D_MODEL), lambda b, j: (b, j, 0)),
        out_shape=jax.ShapeDtypeStruct((batch, seq, D_MODEL), F32),
        compiler_params=pltpu.CompilerParams(dimension_semantics=("arbitrary", "arbitrary"),
                                             vmem_limit_bytes=VMEM_LIMIT),
        name="moe_combine",
    )(h3, y, y, route)


def _layout_rows(seq):
    return -(-(CHUNK + seq) // ROW_TILE) * ROW_TILE


def _proj_tile(lp):
    return next(t for t in (640, 512, 384, ROW_TILE) if lp % t == 0)


def kernel(x, meta_tokens, lb_logits, mix_norm, ffn_norm, e_w_in, a_q_gain, a_k_gain, a_lam_q1, a_lam_k1, a_lam_q2, a_lam_k2, a_sub_gain, b_conv, b_a_log, b_dt_bias, b_out_gain, e_w_out, ffn_w1, ffn_w3, ffn_w2, o_w_in, c_out_gain, o_w_out, router, moe_w1, moe_w3, moe_w2):
    batch, seq, _ = x.shape
    lp = _layout_rows(seq)
    rows = batch * lp
    tm = _proj_tile(lp)
    meta = jnp.broadcast_to(meta_tokens[None].astype(x.dtype), (batch, N_META, D_MODEL))
    h = jnp.concatenate([jnp.zeros((batch, FRONT, D_MODEL), x.dtype), meta, x,
                         jnp.zeros((batch, lp - CHUNK - seq, D_MODEL), x.dtype)], axis=1).reshape(rows, D_MODEL)

    q, k, v, bqkv, bz, gp = _even_in(h, mix_norm[0], e_w_in[0], a_q_gain[0], a_k_gain[0], lp, tm)
    lam_init = 0.8 - 0.6 * math.exp(-0.3 * 0)
    lam = (jnp.exp(jnp.sum(a_lam_q1[0].astype(F32) * a_lam_k1[0].astype(F32)))
           - jnp.exp(jnp.sum(a_lam_q2[0].astype(F32) * a_lam_k2[0].astype(F32))) + lam_init)
    o_a = _diff_attention(q, k, v, a_sub_gain[0], lam, lam_init, batch, lp, ROW_TILE)
    o_b = _gated_deltanet(bqkv, bz, gp, b_conv[0], b_a_log[0], b_dt_bias[0], b_out_gain[0], lp)
    h = _even_out(h, o_a, o_b, e_w_out[0], ffn_norm[0], ffn_w1[0], ffn_w3[0], ffn_w2[0], lp, seq, tm)

    lb_all = jnp.cumsum(jax.nn.softmax(lb_logits.astype(F32), axis=0), axis=0)
    lb = (lb_all - lb_all[0])[1]
    cq, ck, clf, ci, cz = _odd_in(h, mix_norm[1], o_w_in[0], lb, lp, seq, tm)
    o_c = _hgrn2(cq, ck, clf, ci, cz, c_out_gain[0], lp)
    h3, xn, route = _odd_out(h, o_c, o_w_out[0], ffn_norm[1], router[0], lp, seq, tm)
    row_tok, row_dst, tile_e, n_used, n_tiles = _moe_plan(route, MOE_TILE)
    y = _moe_ffn(xn, moe_w1[0].astype(BF16), moe_w3[0].astype(BF16), moe_w2[0].astype(BF16),
                 row_tok, row_dst, tile_e, n_used, n_tiles, MOE_TILE)
    return _moe_combine(h3, y, route, batch, seq, lp, ROW_TILE)
```

```python
import functools
import math

import jax
import jax.numpy as jnp
from jax import lax
from jax.experimental import pallas as pl
from jax.experimental.pallas import tpu as pltpu

D_MODEL = 1024
CHUNK = 64
N_META = 16
FRONT = CHUNK - N_META
EPS = 1e-6
ROPE_THETA = 500000.0
LOG2E = math.log2(math.e)

A_HEADS = 4
A_DK = 64
A_DV = 128
A_ROT = A_DK // 4
A_W = A_HEADS * A_DV

B_HEADS = 4
B_DK = 128
B_W = B_HEADS * B_DK
CONV_K = 4

C_HEADS = 8
C_DK = 128

D_FF = 2816
N_EXPERTS = 8
D_FF_EXPERT = 3584

LANES = 128
ROW_TILE = 256
VMEM_LIMIT = 56 * 1024 * 1024

BF16 = jnp.bfloat16
F32 = jnp.float32


def _const_spec(shape):
    return pl.BlockSpec(shape, lambda *_: (0,) * len(shape), pipeline_mode=pl.Buffered(1))


def _rms(x, gain):
    return x * lax.rsqrt(jnp.mean(x * x, axis=-1, keepdims=True) + EPS) * gain


def _silu(x):
    return x * (1.0 / (1.0 + jnp.exp(-x)))


def _sigmoid(x):
    return 1.0 / (1.0 + jnp.exp(-x))


def _row_valid(tile_rows, lp, seq):
    r = pl.program_id(0) * tile_rows + lax.broadcasted_iota(jnp.int32, (tile_rows, 1), 0)
    p = r % lp
    return (p >= FRONT) & (p < CHUNK + seq)


def _head_norm_rope(a, ones_bd, gain, cos, sin_lo, sin_hi):
    ssq = jnp.dot((a * a).astype(BF16), ones_bd, preferred_element_type=F32)
    y = a * lax.rsqrt(ssq * (1.0 / A_DK) + EPS) * gain
    up = pltpu.roll(y, A_W - A_ROT // 2, axis=1)
    dn = pltpu.roll(y, A_ROT // 2, axis=1)
    return y * cos + up * sin_lo + dn * sin_hi


def _even_in_kernel(h_ref, g_ref, wq_ref, wk_ref, wv_ref, wb_ref, wz_ref, wg_ref, bd_ref,
                    qg_ref, kg_ref, cos_ref, slo_ref, shi_ref,
                    q_ref, k_ref, v_ref, b_ref, z_ref, gp_ref):
    xn = _rms(h_ref[...], g_ref[...]).astype(BF16)
    rep = lambda t: jnp.concatenate([t] * (A_W // LANES), axis=1)
    cos, slo, shi = rep(cos_ref[...]), rep(slo_ref[...]), rep(shi_ref[...])
    bd = bd_ref[...]
    aq = jnp.dot(xn, wq_ref[...], preferred_element_type=F32)
    q_ref[...] = (_head_norm_rope(aq, bd, qg_ref[...], cos, slo, shi) * (A_DK ** -0.5 * LOG2E)).astype(BF16)
    ak = jnp.dot(xn, wk_ref[...], preferred_element_type=F32)
    k_ref[...] = _head_norm_rope(ak, bd, kg_ref[...], cos, slo, shi).astype(BF16)
    v_ref[...] = jnp.dot(xn, wv_ref[...], preferred_element_type=F32).astype(BF16)
    b_ref[...] = jnp.dot(xn, wb_ref[...], preferred_element_type=F32).astype(BF16)
    z_ref[...] = jnp.dot(xn, wz_ref[...], preferred_element_type=F32).astype(BF16)
    gp_ref[...] = jnp.dot(xn, wg_ref[...], preferred_element_type=F32)


def _even_in(h, gain, w_in, q_gain, k_gain, lp, tm):
    rows = h.shape[0]
    wb16 = w_in.astype(BF16)
    c0, c1, c2, c3, c4 = A_W, 2 * A_W, 3 * A_W, 3 * A_W + 3 * B_W, 3 * A_W + 4 * B_W
    wq, wk, wv, wb, wz = wb16[:, :c0], wb16[:, c0:c1], wb16[:, c1:c2], wb16[:, c2:c3], wb16[:, c3:c4]
    wg = jnp.pad(wb16[:, c4:], ((0, 0), (0, LANES - 2 * B_HEADS)))
    d = jnp.arange(A_W)
    ones_bd = (d[:, None] // A_DK == d[None, :] // A_DK).astype(BF16)
    half = A_ROT // 2
    inv_freq = ROPE_THETA ** (-jnp.arange(half, dtype=F32) * 2.0 / A_ROT)
    pos = (jnp.arange(lp) - FRONT).astype(F32)
    dim = jnp.arange(LANES) % A_DK
    ang = pos[:, None] * inv_freq[dim % half][None, :]
    cos_t = jnp.where(dim < A_ROT, jnp.cos(ang), 1.0)
    slo_t = jnp.where(dim < half, -jnp.sin(ang), 0.0)
    shi_t = jnp.where((dim >= half) & (dim < A_ROT), jnp.sin(ang), 0.0)
    tile8 = lambda g: jnp.tile(g.astype(F32), 2 * A_HEADS)[None, :]
    n_pos = lp // tm
    row_spec = lambda w: pl.BlockSpec((tm, w), lambda i: (i, 0))
    pos_spec = pl.BlockSpec((tm, LANES), lambda i: (i % n_pos, 0))
    outs = pl.pallas_call(
        _even_in_kernel,
        grid=(rows // tm,),
        in_specs=[row_spec(D_MODEL), _const_spec((1, D_MODEL)),
                  _const_spec(wq.shape), _const_spec(wk.shape), _const_spec(wv.shape),
                  _const_spec(wb.shape), _const_spec(wz.shape), _const_spec(wg.shape),
                  _const_spec(ones_bd.shape), _const_spec((1, A_W)), _const_spec((1, A_W)),
                  pos_spec, pos_spec, pos_spec],
        out_specs=(row_spec(A_W), row_spec(A_W), row_spec(A_W), row_spec(3 * B_W), row_spec(B_W),
                   row_spec(LANES)),
        out_shape=(jax.ShapeDtypeStruct((rows, A_W), BF16), jax.ShapeDtypeStruct((rows, A_W), BF16),
                   jax.ShapeDtypeStruct((rows, A_W), BF16), jax.ShapeDtypeStruct((rows, 3 * B_W), BF16),
                   jax.ShapeDtypeStruct((rows, B_W), BF16), jax.ShapeDtypeStruct((rows, LANES), F32)),
        compiler_params=pltpu.CompilerParams(dimension_semantics=("arbitrary",),
                                             vmem_limit_bytes=VMEM_LIMIT),
        name="even_in_proj",
    )(h, gain[None, :].astype(F32), wq, wk, wv, wb, wz, wg, ones_bd, tile8(q_gain), tile8(k_gain),
      cos_t, slo_t, shi_t)
    return outs


ATTN_WIDE = 1024


ATTN_HEADS = 2


def _attn_kernel(q_ref, k_ref, v_ref, sg_ref, lam_ref, o_ref, qs_ref, vx_ref, m_ref, acc_ref,
                 sa_ref, sb_ref, *, tq, width, lam_init):
    qi = pl.program_id(2)
    heads = range(ATTN_HEADS)
    lanes = lambda s: slice(s * LANES, (s + 1) * LANES)

    @pl.when(qi == 0)
    def _():
        for s in heads:
            vx_ref[s, :, :A_DV] = v_ref[:, lanes(s)]
            vx_ref[s, :, A_DV:] = jnp.ones((vx_ref.shape[1], A_DV), BF16)

    lane = lax.broadcasted_iota(jnp.int32, (tq, LANES), 1)
    zero = jnp.zeros((tq, LANES), BF16)
    for s in heads:
        q = q_ref[:, lanes(s)]
        qs_ref[s, :tq, :] = jnp.where(lane < A_DK, q, zero)
        qs_ref[s, tq:, :] = jnp.where(lane >= A_DK, q, zero)
    m_ref[...] = jnp.full(m_ref.shape, -jnp.inf, F32)
    acc_ref[...] = jnp.zeros(acc_ref.shape, F32)

    lp = k_ref.shape[0]
    qchunk = (qi * tq + lax.broadcasted_iota(jnp.int32, (2 * tq, 1), 0) % tq) // CHUNK

    def window(t):
        return pl.multiple_of(jnp.minimum(t * width, lp - width), tq)

    def scores(s_ref, t):
        for s in heads:
            s_ref[s] = lax.dot_general(qs_ref[s], k_ref[pl.ds(window(t), width), lanes(s)],
                                       (((1,), (1,)), ((), ())), preferred_element_type=F32)

    def softmax_pv(s_ref, t, mask):
        start = window(t)
        kpos = start + lax.broadcasted_iota(jnp.int32, (1, width), 1)
        kchunk = jnp.where((kpos >= t * width) & (kpos >= FRONT), kpos // CHUNK, lp)
        nblk = width // LANES

        def block(s, c):
            cs = slice(c * LANES, (c + 1) * LANES)
            if mask == "full":
                return jnp.where(kchunk[:, cs] <= qchunk, s_ref[s, :, cs], -jnp.inf)
            if mask == "pad" and c == 0:
                return jnp.where(kpos[:, cs] >= FRONT, s_ref[s, :, cs], -jnp.inf)
            return s_ref[s, :, cs]

        for s in heads:
            mx = functools.reduce(jnp.maximum, [block(s, c) for c in range(nblk)])
            m_old = m_ref[s]
            m_new = jnp.maximum(m_old, jnp.max(mx, axis=-1, keepdims=True))
            alpha = jnp.exp2(m_old - m_new)
            p = jnp.concatenate([jnp.exp2((block(s, c) - m_new).astype(BF16)) for c in range(nblk)], axis=1)
            pv = jnp.dot(p, vx_ref[s, pl.ds(start, width), :], preferred_element_type=F32)
            acc_ref[s, :, :A_DV] = alpha * acc_ref[s, :, :A_DV] + pv[:, :A_DV]
            acc_ref[s, :, A_DV:] = alpha * acc_ref[s, :, A_DV:] + pv[:, A_DV:]
            m_ref[s] = m_new

    n_tiles = (qi * tq + tq + width - 1) // width
    n_pairs = (n_tiles + 1) // 2

    def pair(j, masks):
        scores(sb_ref, 2 * j + 1)
        softmax_pv(sa_ref, 2 * j, masks[0])
        scores(sa_ref, 2 * j + 2)
        softmax_pv(sb_ref, 2 * j + 1, masks[1])

    scores(sa_ref, 0)

    @pl.when(n_pairs > 1)
    def _():
        pair(0, ("pad", "none"))

    def inner(j, carry):
        pair(j, ("none", "none"))
        return carry
    lax.fori_loop(1, n_pairs - 1, inner, 0)

    @pl.when(n_tiles == 2 * n_pairs)
    def _():
        scores(sb_ref, 2 * n_pairs - 1)
        softmax_pv(sa_ref, 2 * n_pairs - 2, "pad")
        softmax_pv(sb_ref, 2 * n_pairs - 1, "full")

    @pl.when(n_tiles < 2 * n_pairs)
    def _():
        softmax_pv(sa_ref, 2 * n_pairs - 2, "full")

    for s in heads:
        o = acc_ref[s, :, :A_DV] / acc_ref[s, :, A_DV:]
        o = o[:tq] - lam_ref[...] * o[tq:]
        o_ref[:, lanes(s)] = (_rms(o, sg_ref[...]) * (1.0 - lam_init)).astype(o_ref.dtype)


def _diff_attention(q, k, v, sub_gain, lam, lam_init, batch, lp, tq):
    nq = lp // tq
    hw = ATTN_HEADS * LANES
    kv_spec = pl.BlockSpec((lp, hw), lambda b, h, i: (b, h), pipeline_mode=pl.Buffered(1))
    q_spec = pl.BlockSpec((tq, hw), lambda b, h, i: (b * nq + i, h))
    vec = pl.BlockSpec((1, LANES), lambda b, h, i: (0, 0))
    width = min(ATTN_WIDE, lp)
    return pl.pallas_call(
        functools.partial(_attn_kernel, tq=tq, width=width, lam_init=lam_init),
        grid=(batch, A_HEADS // ATTN_HEADS, nq),
        in_specs=[q_spec, kv_spec, kv_spec, vec, vec],
        out_specs=q_spec,
        out_shape=jax.ShapeDtypeStruct(q.shape, BF16),
        scratch_shapes=[pltpu.VMEM((ATTN_HEADS, 2 * tq, LANES), BF16),
                        pltpu.VMEM((ATTN_HEADS, lp, 2 * A_DV), BF16),
                        pltpu.VMEM((ATTN_HEADS, 2 * tq, LANES), F32),
                        pltpu.VMEM((ATTN_HEADS, 2 * tq, 2 * A_DV), F32),
                        pltpu.VMEM((ATTN_HEADS, 2 * tq, width), F32),
                        pltpu.VMEM((ATTN_HEADS, 2 * tq, width), F32)],
        compiler_params=pltpu.CompilerParams(dimension_semantics=("arbitrary",) * 3,
                                             vmem_limit_bytes=VMEM_LIMIT),
        name="diff_attention",
    )(q, k, v, sub_gain[None, :].astype(F32), jnp.full((1, LANES), lam, F32))


def _chunk_masks(n):
    r = lax.broadcasted_iota(jnp.int32, (n, n), 0)
    c = lax.broadcasted_iota(jnp.int32, (n, n), 1)
    same = (r // CHUNK) == (c // CHUNK)
    return r, c, same


def _mask_dot(mask, x):
    hi = x.astype(BF16)
    rest = x - hi.astype(F32)
    mid = rest.astype(BF16)
    lo = (rest - mid.astype(F32)).astype(BF16)
    m = mask.astype(BF16)
    return jnp.dot(jnp.concatenate([m, m, m], axis=1), jnp.concatenate([hi, mid, lo], axis=0),
                   preferred_element_type=F32)


def _bdot(a, b):
    return jnp.dot(a.astype(BF16), b.astype(BF16), preferred_element_type=F32)


def _bdot_nt(a, b):
    return lax.dot_general(a.astype(BF16), b.astype(BF16), (((1,), (1,)), ((), ())),
                           preferred_element_type=F32)


def _bdot_tn(a, b):
    return lax.dot_general(a.astype(BF16), b.astype(BF16), (((0,), (0,)), ((), ())),
                           preferred_element_type=F32)


def _gdn_kernel(x_ref, z_ref, gp_ref, cw_ref, alog_ref, dtb_ref, og_ref, o_ref, xbuf_ref, s_ref):
    nb, n = x_ref.shape[0], ROW_TILE
    step = pl.program_id(0)

    @pl.when(step == 0)
    def _():
        xbuf_ref[:, :8, :] = jnp.zeros((nb, 8, 3 * B_W), F32)
        s_ref[...] = jnp.zeros_like(s_ref)

    r, c, same = _chunk_masks(n)
    incl = same & (r >= c)
    strict = same & (r > c)
    eye = (r == c).astype(F32)
    row = step * n + lax.broadcasted_iota(jnp.int32, (n, 1), 0)
    chunks = [slice(ci * CHUNK, (ci + 1) * CHUNK) for ci in range(n // CHUNK)]

    heads = range(nb * B_HEADS)
    p, t, rhs, qk, qg, kdec, decay = [], [], [], [], [], [], []
    for b in range(nb):
        xbuf_ref[b, 8:, :] = x_ref[b].astype(F32)
        conv = cw_ref[CONV_K - 1:CONV_K, :] * xbuf_ref[b, 8:, :]
        for j in range(CONV_K - 1):
            conv = conv + cw_ref[j:j + 1, :] * xbuf_ref[b, 5 + j:5 + j + n, :]
        xbuf_ref[b, :8, :] = xbuf_ref[b, n:n + 8, :]
        conv = _silu(conv)
        gp = gp_ref[b]
        beta_all = _sigmoid(gp)
        gpre = gp + dtb_ref[...]
        softplus = jnp.maximum(gpre, 0.0) + jnp.log(1.0 + jnp.exp(-jnp.abs(gpre)))
        g_all = jnp.where(row >= FRONT, -jnp.exp(alog_ref[...]) * softplus, 0.0)
        sums = _mask_dot(jnp.concatenate([incl, same], axis=0), g_all)
        gc_all, gl_all = sums[:n], sums[n:]
        gc_rows = gc_all.T
        for h in range(B_HEADS):
            qh = conv[:, h * B_DK:(h + 1) * B_DK]
            kh = conv[:, B_W + h * B_DK:B_W + (h + 1) * B_DK]
            vh = conv[:, 2 * B_W + h * B_DK:2 * B_W + (h + 1) * B_DK]
            qh = qh * lax.rsqrt(jnp.sum(qh * qh, axis=-1, keepdims=True) + EPS) * (B_DK ** -0.5)
            kh = kh * lax.rsqrt(jnp.sum(kh * kh, axis=-1, keepdims=True) + EPS)
            beta = beta_all[:, h:h + 1]
            gcol = gc_all[:, B_HEADS + h:B_HEADS + h + 1]
            glcol = gl_all[:, B_HEADS + h:B_HEADS + h + 1]
            grow = gc_rows[B_HEADS + h:B_HEADS + h + 1, :]
            dec = jnp.exp(jnp.where(incl, gcol - grow, -jnp.inf))
            kb = kh * beta
            egc = jnp.exp(gcol)
            p0 = -(_bdot_nt(kb, kh) * jnp.where(strict, dec, 0.0))
            p.append(p0)
            t.append(eye + p0)
            rhs.append(jnp.concatenate([vh * beta, kb * egc], axis=1))
            qk.append(_bdot_nt(qh, kh) * dec)
            qg.append(qh * egc)
            kdec.append(kh * jnp.exp(glcol - gcol))
            decay.append([jnp.exp(glcol[rs.stop - 1:rs.stop, :]) for rs in chunks])
    for _ in range(5):
        for h in heads:
            p[h] = _bdot(p[h], p[h])
            t[h] = t[h] + _bdot(t[h], p[h])
    sol = [_bdot(t[h], rhs[h]) for h in heads]
    u = [s[:, :B_DK] for s in sol]
    w = [s[:, B_DK:] for s in sol]
    kw = [[_bdot_tn(kdec[h][rs], w[h][rs]) for rs in chunks] for h in heads]
    ku = [[_bdot_tn(kdec[h][rs], u[h][rs]) for rs in chunks] for h in heads]
    state = [s_ref[h] for h in heads]
    before = [[] for _ in heads]
    for ci in range(len(chunks)):
        for h in heads:
            before[h].append(state[h])
            state[h] = state[h] * decay[h][ci] - _bdot(kw[h][ci], state[h]) + ku[h][ci]
    for h in heads:
        s_ref[h] = state[h]
        vnew, inter = [], []
        for ci, rs in enumerate(chunks):
            ws = _bdot(jnp.concatenate([w[h][rs], qg[h][rs]], axis=0), before[h][ci])
            vnew.append(u[h][rs] - ws[:CHUNK])
            inter.append(ws[CHUNK:])
        o = jnp.concatenate(inter, axis=0) + _bdot(qk[h], jnp.concatenate(vnew, axis=0))
        b, sl = h // B_HEADS, slice((h % B_HEADS) * B_DK, (h % B_HEADS + 1) * B_DK)
        o_ref[b, :, sl] = (_rms(o, og_ref[...]) * _silu(z_ref[b, :, sl].astype(F32))).astype(o_ref.dtype)


def _gated_deltanet(bqkv, bz, gp, conv_w, a_log, dt_bias, out_gain, lp):
    rows = bqkv.shape[0]
    n, nb = ROW_TILE, rows // lp
    head_vec = lambda p: jnp.zeros((1, LANES), F32).at[0, B_HEADS:2 * B_HEADS].set(p.astype(F32))
    per_batch = lambda a: a.reshape(nb, lp, a.shape[-1])
    row_spec = lambda w: pl.BlockSpec((nb, n, w), lambda i: (0, i, 0))
    out = pl.pallas_call(
        _gdn_kernel,
        grid=(lp // n,),
        in_specs=[row_spec(3 * B_W), row_spec(B_W), row_spec(LANES), _const_spec((CONV_K, 3 * B_W)),
                  _const_spec((1, LANES)), _const_spec((1, LANES)), _const_spec((1, B_DK))],
        out_specs=row_spec(B_W),
        out_shape=jax.ShapeDtypeStruct((nb, lp, B_W), BF16),
        scratch_shapes=[pltpu.VMEM((nb, n + 8, 3 * B_W), F32), pltpu.VMEM((nb * B_HEADS, B_DK, B_DK), F32)],
        compiler_params=pltpu.CompilerParams(dimension_semantics=("arbitrary",),
                                             vmem_limit_bytes=VMEM_LIMIT),
        name="gated_deltanet",
    )(per_batch(bqkv), per_batch(bz), per_batch(gp), conv_w.astype(F32), head_vec(a_log), head_vec(dt_bias),
      out_gain[None, :].astype(F32))
    return out.reshape(rows, B_W)


FF_COLS = 256


def _even_out_kernel(h_ref, oa_ref, ob_ref, woa_ref, wob_ref, g_ref, w1_ref, w3_ref, w2_ref, o_ref,
                     act_ref, *, tm, lp, seq):
    h1 = (h_ref[...] + jnp.dot(oa_ref[...], woa_ref[...], preferred_element_type=F32)
          + jnp.dot(ob_ref[...], wob_ref[...], preferred_element_type=F32))
    xn = _rms(h1, g_ref[...]).astype(BF16)
    for c in range(D_FF // FF_COLS):
        cs = slice(c * FF_COLS, (c + 1) * FF_COLS)
        a = jnp.dot(xn, w1_ref[:, cs], preferred_element_type=F32)
        b = jnp.dot(xn, w3_ref[:, cs], preferred_element_type=F32)
        act_ref[:, cs] = (_silu(a) * b).astype(BF16)
    out = h1 + jnp.dot(act_ref[...], w2_ref[...], preferred_element_type=F32)
    o_ref[...] = jnp.where(_row_valid(tm, lp, seq), out, 0.0)


def _even_out(h, oa, ob, w_out, gain, w1, w3, w2, lp, seq, tm):
    rows = h.shape[0]
    wo = w_out.astype(BF16)
    row_spec = lambda w: pl.BlockSpec((tm, w), lambda i: (i, 0))
    return pl.pallas_call(
        functools.partial(_even_out_kernel, tm=tm, lp=lp, seq=seq),
        grid=(rows // tm,),
        in_specs=[row_spec(D_MODEL), row_spec(A_W), row_spec(B_W),
                  _const_spec((A_W, D_MODEL)), _const_spec((B_W, D_MODEL)), _const_spec((1, D_MODEL)),
                  _const_spec(w1.shape), _const_spec(w3.shape), _const_spec(w2.shape)],
        out_specs=row_spec(D_MODEL),
        out_shape=jax.ShapeDtypeStruct((rows, D_MODEL), F32),
        scratch_shapes=[pltpu.VMEM((tm, D_FF), BF16)],
        compiler_params=pltpu.CompilerParams(dimension_semantics=("arbitrary",),
                                             vmem_limit_bytes=VMEM_LIMIT),
        name="even_out_ffn",
    )(h, oa, ob, wo[:A_W], wo[A_W:], gain[None, :].astype(F32),
      w1.astype(BF16), w3.astype(BF16), w2.astype(BF16))


def _odd_in_kernel(h_ref, g_ref, wq_ref, wf_ref, wi_ref, wz_ref, lb_ref,
                   q_ref, k_ref, lf_ref, i_ref, z_ref, *, tm, lp, seq):
    xn = _rms(h_ref[...], g_ref[...]).astype(BF16)
    valid = _row_valid(tm, lp, seq)
    q_ref[...] = jnp.dot(xn, wq_ref[...], preferred_element_type=F32).astype(BF16)
    f = jnp.dot(xn, wf_ref[...], preferred_element_type=F32)
    lb = lb_ref[...]
    fg = lb + (1.0 - lb) * _sigmoid(f)
    k_ref[...] = jnp.where(valid, 1.0 - fg, 0.0).astype(BF16)
    lf_ref[...] = jnp.where(valid, jnp.log(fg), 0.0)
    i_ref[...] = jnp.dot(xn, wi_ref[...], preferred_element_type=F32).astype(BF16)
    z_ref[...] = jnp.dot(xn, wz_ref[...], preferred_element_type=F32).astype(BF16)


def _odd_in(h, gain, w_in, lb, lp, seq, tm):
    rows = h.shape[0]
    wb16 = w_in.astype(BF16)
    ws = [wb16[:, j * D_MODEL:(j + 1) * D_MODEL] for j in range(4)]
    row_spec = pl.BlockSpec((tm, D_MODEL), lambda i: (i, 0))
    wspec = _const_spec((D_MODEL, D_MODEL))
    vec = _const_spec((1, D_MODEL))
    sd = lambda dt: jax.ShapeDtypeStruct((rows, D_MODEL), dt)
    return pl.pallas_call(
        functools.partial(_odd_in_kernel, tm=tm, lp=lp, seq=seq),
        grid=(rows // tm,),
        in_specs=[row_spec, vec, wspec, wspec, wspec, wspec, vec],
        out_specs=(row_spec,) * 5,
        out_shape=(sd(BF16), sd(BF16), sd(F32), sd(BF16), sd(BF16)),
        compiler_params=pltpu.CompilerParams(dimension_semantics=("arbitrary",),
                                             vmem_limit_bytes=VMEM_LIMIT),
        name="odd_in_proj",
    )(h, gain[None, :].astype(F32), *ws, lb[None, :].astype(F32))


SUB = 8
HALVES = (32, 16, 8)


def _hgrn_kernel(q_ref, k_ref, lf_ref, i_ref, z_ref, og_ref, sel_ref, o_ref, s_ref, p_ref, kf_ref, gf_ref):
    n = ROW_TILE
    nb = q_ref.shape[0]
    step = pl.program_id(0)

    @pl.when(step == 0)
    def _():
        s_ref[...] = jnp.zeros_like(s_ref)

    r, c, same = _chunk_masks(n)
    incl = same & (r >= c)
    rrow = lax.broadcasted_iota(jnp.int32, (n, 1), 0)
    inner = rrow % SUB
    level_mask = [(r // (2 * half) == c // (2 * half)).astype(F32) for half in HALVES]
    diag_mask = (r // SUB == c // SUB).astype(F32)

    gc_alls = [_mask_dot(incl, lf_ref[b]) * LOG2E for b in range(nb)]
    for b in range(nb):
        kf_ref[b] = k_ref[b].astype(F32)
        gf_ref[b] = gc_alls[b]

    def block_rows(ref, b, sl, size, j):
        return jnp.concatenate([jnp.broadcast_to(ref[b, pl.ds(blk * size + j, 1), sl], (size, C_DK))
                                for blk in range(n // size)], axis=0)

    def products(b, h):
        sl = slice(h * C_DK, (h + 1) * C_DK)
        q = q_ref[b, :, sl].astype(F32)
        k = kf_ref[b, :, sl]
        gc = gc_alls[b][:, sl]
        gl = block_rows(gf_ref, b, sl, CHUNK, CHUNK - 1)
        qg = q * jnp.exp2(gc)
        kdec = k * jnp.exp2(gl - gc)
        parts = []
        for half in HALVES:
            gref = block_rows(gf_ref, b, sl, 2 * half, half - 1)
            upper = (rrow % (2 * half)) >= half
            qt = q * jnp.exp2(jnp.where(upper, gc - gref, -jnp.inf))
            kt = k * jnp.exp2(jnp.where(upper, -jnp.inf, gref - gc))
            parts.append(_bdot_nt(qt, kt))
        pbuf = p_ref.at[h % 2]
        for j in range(SUB):
            kj = block_rows(kf_ref, b, sl, SUB, j)
            gj = block_rows(gf_ref, b, sl, SUB, j)
            pbuf[:, j * C_DK:(j + 1) * C_DK] = (
                q * kj * jnp.exp2(jnp.where(inner >= j, gc - gj, -jnp.inf))).astype(BF16)
        diag = jnp.dot(pbuf[...], sel_ref[...], preferred_element_type=F32)
        return parts, diag, qg, kdec, gl

    def finish(b, h, parts, diag, qg, kdec, gl):
        sl = slice(h * C_DK, (h + 1) * C_DK)
        iv = i_ref[b, :, sl]
        att = diag_mask * diag
        for mask, part in zip(level_mask, parts):
            att = att + mask * part
        st = s_ref[b * C_HEADS + h]
        inter = []
        for ci in range(n // CHUNK):
            rs = slice(ci * CHUNK, (ci + 1) * CHUNK)
            inter.append(_bdot_nt(qg[rs], st))
            st = st * jnp.exp2(gl[ci * CHUNK:ci * CHUNK + 1, :]) + _bdot_tn(iv[rs], kdec[rs])
        s_ref[b * C_HEADS + h] = st
        o = jnp.concatenate(inter, axis=0) + _bdot(att, iv)
        o_ref[b, :, sl] = (_rms(o, og_ref[...]) * _silu(z_ref[b, :, sl].astype(F32))).astype(o_ref.dtype)

    for b in range(nb):
        for h in range(C_HEADS):
            finish(b, h, *products(b, h))


def _hgrn2(q, k, lf, iv, z, out_gain, lp):
    rows = q.shape[0]
    n, nb = ROW_TILE, rows // lp
    per_batch = lambda a: a.reshape(nb, lp, D_MODEL)
    row_spec = pl.BlockSpec((nb, n, D_MODEL), lambda i: (0, i, 0))
    sel = ((jnp.arange(SUB * C_DK)[:, None] // C_DK) == (jnp.arange(n)[None, :] % SUB)).astype(BF16)
    out = pl.pallas_call(
        _hgrn_kernel,
        grid=(lp // n,),
        in_specs=[row_spec] * 5 + [_const_spec((1, C_DK)), _const_spec(sel.shape)],
        out_specs=row_spec,
        out_shape=jax.ShapeDtypeStruct((nb, lp, D_MODEL), BF16),
        scratch_shapes=[pltpu.VMEM((nb * C_HEADS, C_DK, C_DK), F32), pltpu.VMEM((2, n, SUB * C_DK), BF16),
                        pltpu.VMEM((nb, n, D_MODEL), F32), pltpu.VMEM((nb, n, D_MODEL), F32)],
        compiler_params=pltpu.CompilerParams(dimension_semantics=("arbitrary",),
                                             vmem_limit_bytes=VMEM_LIMIT),
        name="hgrn2",
    )(per_batch(q), per_batch(k), per_batch(lf), per_batch(iv), per_batch(z), out_gain[None, :].astype(F32), sel)
    return out.reshape(rows, D_MODEL)


def _odd_out_kernel(h_ref, o_ref, wo_ref, g_ref, wr_ref, h3_ref, xn_ref, route_ref, *, tm, lp, seq):
    h3 = h_ref[...] + jnp.dot(o_ref[...], wo_ref[...], preferred_element_type=F32)
    h3 = jnp.where(_row_valid(tm, lp, seq), h3, 0.0)
    h3_ref[...] = h3
    xn = _rms(h3, g_ref[...])
    xn_ref[...] = xn
    logits = jnp.dot(xn.astype(BF16), wr_ref[...], preferred_element_type=F32)
    lane = lax.broadcasted_iota(jnp.int32, logits.shape, 1)
    logits = jnp.where(lane < N_EXPERTS, logits, -jnp.inf)
    v1 = jnp.max(logits, axis=-1, keepdims=True)
    e1 = jnp.min(jnp.where(logits == v1, lane, LANES), axis=-1, keepdims=True)
    rest = jnp.where(lane == e1, -jnp.inf, logits)
    v2 = jnp.max(rest, axis=-1, keepdims=True)
    e2 = jnp.min(jnp.where(rest == v2, lane, LANES), axis=-1, keepdims=True)
    t = jnp.exp(v2 - v1)
    g1 = 1.0 / (1.0 + t)
    g2 = t * g1
    route_ref[...] = jnp.where(lane == 0, e1.astype(F32),
                               jnp.where(lane == 1, e2.astype(F32),
                                         jnp.where(lane == 2, g1, jnp.where(lane == 3, g2, 0.0))))


def _odd_out(h, o, w_out, gain, router, lp, seq, tm):
    rows = h.shape[0]
    wr = jnp.pad(router.astype(BF16), ((0, 0), (0, LANES - N_EXPERTS)))
    row_spec = lambda w: pl.BlockSpec((tm, w), lambda i: (i, 0))
    return pl.pallas_call(
        functools.partial(_odd_out_kernel, tm=tm, lp=lp, seq=seq),
        grid=(rows // tm,),
        in_specs=[row_spec(D_MODEL), row_spec(D_MODEL), _const_spec((D_MODEL, D_MODEL)),
                  _const_spec((1, D_MODEL)), _const_spec((D_MODEL, LANES))],
        out_specs=(row_spec(D_MODEL), row_spec(D_MODEL), row_spec(LANES)),
        out_shape=(jax.ShapeDtypeStruct((rows, D_MODEL), F32), jax.ShapeDtypeStruct((rows, D_MODEL), F32),
                   jax.ShapeDtypeStruct((rows, LANES), F32)),
        compiler_params=pltpu.CompilerParams(dimension_semantics=("arbitrary",),
                                             vmem_limit_bytes=VMEM_LIMIT),
        name="odd_out_route",
    )(h, o, w_out.astype(BF16), gain[None, :].astype(F32), wr)


MOE_TILE = 512
MOE_FF_STEPS = 2


def _moe_plan(route, tile):
    n_tok = route.shape[0]
    flat_e = route[:, :2].astype(jnp.int32).reshape(-1)
    na = flat_e.shape[0]
    counts = jnp.sum(flat_e[:, None] == jnp.arange(N_EXPERTS, dtype=jnp.int32)[None, :], axis=0,
                     dtype=jnp.int32)
    order = jnp.argsort(flat_e, stable=True).astype(jnp.int32)
    first = jnp.cumsum(counts) - counts
    padded = (counts + tile - 1) // tile * tile
    pend = jnp.cumsum(padded)
    n_tiles = -(-(na + N_EXPERTS * (tile - 1)) // tile) + 2
    n_rows = n_tiles * tile
    tile_e = jnp.minimum(jnp.searchsorted(pend, jnp.arange(n_tiles, dtype=jnp.int32) * tile, side='right'),
                         N_EXPERTS - 1).astype(jnp.int32)
    row = jnp.arange(n_rows, dtype=jnp.int32)
    row_e = jnp.repeat(tile_e, tile)
    local = row - (pend - padded)[row_e]
    real = local < counts[row_e]
    pair = order[jnp.minimum(first[row_e] + local, na - 1)]
    row_tok = jnp.where(real, pair // 2, 0)
    dump = na + ((row // tile) % 2) * tile + row % tile
    row_dst = jnp.where(real, (pair % 2) * n_tok + pair // 2, dump)
    n_used = (pend[-1] // tile).astype(jnp.int32).reshape(1)
    return row_tok, row_dst, tile_e, n_used, n_tiles


def _moe_ffn_kernel(te_ref, nused_ref, tok0_ref, tokn_ref, dstp_ref, x_hbm, w1_ref, w3_ref, w2_ref,
                    y_hbm, xbuf, xb16, ybuf, gsem, ssem):
    i, f = pl.program_id(0), pl.program_id(1)
    tile = xb16.shape[0]
    slot = i % 2
    n_used = nused_ref[0]
    n_out = y_hbm.shape[0] - 2 * tile

    def gather_start(tok_ref, row, dst_slot):
        pltpu.make_async_copy(x_hbm.at[pl.ds(tok_ref[0, 0, row], 1), :],
                              xbuf.at[dst_slot, pl.ds(row, 1), :], gsem.at[dst_slot]).start()

    def scatter_start(row, src_slot, priority=0):
        pltpu.make_async_copy(ybuf.at[src_slot, pl.ds(row, 1), :],
                              y_hbm.at[pl.ds(dstp_ref[0, 0, row], 1), :], ssem.at[src_slot]).start(priority)

    def result_wait(s):
        pltpu.make_async_copy(ybuf.at[s], y_hbm.at[pl.ds(0, tile), :], ssem.at[s]).wait()

    @pl.when((i == 0) & (f == 0))
    def _():
        ybuf[...] = jnp.zeros(ybuf.shape, F32)
        for s in range(2):
            pltpu.make_async_copy(ybuf.at[s], y_hbm.at[pl.ds(n_out + s * tile, tile), :], ssem.at[s]).start()

        def issue(r, carry):
            gather_start(tok0_ref, r, 0)
            return carry
        lax.fori_loop(0, tile, issue, 0)

    @pl.when((f == 0) & (i >= n_used) & (i - 2 < n_used))
    def _():
        result_wait(slot)

    @pl.when((f == 0) & (i <= n_used))
    def _():
        pltpu.make_async_copy(x_hbm.at[pl.ds(0, tile), :], xbuf.at[slot], gsem.at[slot]).wait()
        xb16[...] = xbuf[slot].astype(BF16)

    def compute(step, par, first_tile):
        if step == 0:
            for r in range(tile):
                gather_start(tokn_ref, r, 1 - par)
        if step == MOE_FF_STEPS - 1 and not first_tile:
            for r in range(tile):
                scatter_start(r, 1 - par, priority=r % 2)
        x = xb16[...]
        a = jnp.dot(x, w1_ref[0], preferred_element_type=F32)
        b = jnp.dot(x, w3_ref[0], preferred_element_type=F32)
        y = jnp.dot((_silu(a) * b).astype(BF16), w2_ref[0], preferred_element_type=F32)
        if step == 0:
            result_wait(par)
            ybuf[par] = y
        else:
            ybuf[par] += y

    for step in range(MOE_FF_STEPS):
        @pl.when((i == 0) & (f == step))
        def _():
            compute(step, 0, True)

        for par in range(2):
            @pl.when((i > 0) & (i < n_used) & (f == step) & (slot == par))
            def _():
                compute(step, par, False)

    @pl.when((i == n_used) & (f == 0))
    def _():
        def issue(r, carry):
            scatter_start(r, 1 - slot)
            return carry
        lax.fori_loop(0, tile, issue, 0)


def _moe_ffn(xn, w1, w3, w2, row_tok, row_dst, tile_e, n_used, n_tiles, tile):
    n_tok = xn.shape[0]
    steps = MOE_FF_STEPS
    ff = D_FF_EXPERT // steps
    fi = lambda i, f, nu: jnp.where(i < nu[0], f, steps - 1)
    tok_spec = lambda imap: pl.BlockSpec((1, 1, tile), imap, memory_space=pltpu.SMEM)
    grid_spec = pltpu.PrefetchScalarGridSpec(
        num_scalar_prefetch=2,
        grid=(n_tiles, steps),
        in_specs=[tok_spec(lambda i, f, te, nu: (0, 0, 0)),
                  tok_spec(lambda i, f, te, nu: (jnp.minimum(i + 1, n_tiles - 1), 0, 0)),
                  tok_spec(lambda i, f, te, nu: (jnp.maximum(i - 1, 0), 0, 0)),
                  pl.BlockSpec(memory_space=pl.ANY),
                  pl.BlockSpec((1, D_MODEL, ff), lambda i, f, te, nu: (te[i], 0, fi(i, f, nu))),
                  pl.BlockSpec((1, D_MODEL, ff), lambda i, f, te, nu: (te[i], 0, fi(i, f, nu))),
                  pl.BlockSpec((1, ff, D_MODEL), lambda i, f, te, nu: (te[i], fi(i, f, nu), 0))],
        out_specs=pl.BlockSpec(memory_space=pl.ANY),
        scratch_shapes=[pltpu.VMEM((2, tile, D_MODEL), F32), pltpu.VMEM((tile, D_MODEL), BF16),
                        pltpu.VMEM((2, tile, D_MODEL), F32),
                        pltpu.SemaphoreType.DMA((2,)), pltpu.SemaphoreType.DMA((2,))],
    )
    tok3 = row_tok.reshape(n_tiles, 1, tile)
    return pl.pallas_call(
        _moe_ffn_kernel,
        grid_spec=grid_spec,
        out_shape=jax.ShapeDtypeStruct((2 * n_tok + 2 * tile, D_MODEL), F32),
        compiler_params=pltpu.CompilerParams(dimension_semantics=("arbitrary", "arbitrary"),
                                             vmem_limit_bytes=VMEM_LIMIT),
        name="moe_ffn",
    )(tile_e, n_used, tok3, tok3, row_dst.reshape(n_tiles, 1, tile), xn, w1, w3, w2)


def _moe_combine_kernel(h_ref, y0_ref, y1_ref, route_ref, o_ref):
    g = route_ref[...]
    o_ref[0] = h_ref[...] + g[:, 2:3] * y0_ref[...] + g[:, 3:4] * y1_ref[...]


def _moe_combine(h3, y, route, batch, seq, lp, tc):
    n_tok = h3.shape[0]
    frame_rows = lambda width, off: pl.BlockSpec(
        (pl.Element(tc), pl.Element(width)),
        lambda b, j: (pl.multiple_of(off + b * lp + CHUNK + j * tc, CHUNK), 0))
    return pl.pallas_call(
        _moe_combine_kernel,
        grid=(batch, seq // tc),
        in_specs=[frame_rows(D_MODEL, 0), frame_rows(D_MODEL, 0), frame_rows(D_MODEL, n_tok),
                  frame_rows(LANES, 0)],
        out_specs=pl.BlockSpec((1, tc, D_MODEL), lambda b, j: (b, j, 0)),
        out_shape=jax.ShapeDtypeStruct((batch, seq, D_MODEL), F32),
        compiler_params=pltpu.CompilerParams(dimension_semantics=("arbitrary", "arbitrary"),
                                             vmem_limit_bytes=VMEM_LIMIT),
        name="moe_combine",
    )(h3, y, y, route)


def _layout_rows(seq):
    return -(-(CHUNK + seq) // ROW_TILE) * ROW_TILE


def _proj_tile(lp):
    return next(t for t in (640, 512, 384, ROW_TILE) if lp % t == 0)


def kernel(x, meta_tokens, lb_logits, mix_norm, ffn_norm, e_w_in, a_q_gain, a_k_gain, a_lam_q1, a_lam_k1, a_lam_q2, a_lam_k2, a_sub_gain, b_conv, b_a_log, b_dt_bias, b_out_gain, e_w_out, ffn_w1, ffn_w3, ffn_w2, o_w_in, c_out_gain, o_w_out, router, moe_w1, moe_w3, moe_w2):
    batch, seq, _ = x.shape
    lp = _layout_rows(seq)
    rows = batch * lp
    tm = _proj_tile(lp)
    meta = jnp.broadcast_to(meta_tokens[None].astype(x.dtype), (batch, N_META, D_MODEL))
    h = jnp.concatenate([jnp.zeros((batch, FRONT, D_MODEL), x.dtype), meta, x,
                         jnp.zeros((batch, lp - CHUNK - seq, D_MODEL), x.dtype)], axis=1).reshape(rows, D_MODEL)

    q, k, v, bqkv, bz, gp = _even_in(h, mix_norm[0], e_w_in[0], a_q_gain[0], a_k_gain[0], lp, tm)
    lam_init = 0.8 - 0.6 * math.exp(-0.3 * 0)
    lam = (jnp.exp(jnp.sum(a_lam_q1[0].astype(F32) * a_lam_k1[0].astype(F32)))
           - jnp.exp(jnp.sum(a_lam_q2[0].astype(F32) * a_lam_k2[0].astype(F32))) + lam_init)
    o_a = _diff_attention(q, k, v, a_sub_gain[0], lam, lam_init, batch, lp, ROW_TILE)
    o_b = _gated_deltanet(bqkv, bz, gp, b_conv[0], b_a_log[0], b_dt_bias[0], b_out_gain[0], lp)
    h = _even_out(h, o_a, o_b, e_w_out[0], ffn_norm[0], ffn_w1[0], ffn_w3[0], ffn_w2[0], lp, seq, tm)

    lb_all = jnp.cumsum(jax.nn.softmax(lb_logits.astype(F32), axis=0), axis=0)
    lb = (lb_all - lb_all[0])[1]
    cq, ck, clf, ci, cz = _odd_in(h, mix_norm[1], o_w_in[0], lb, lp, seq, tm)
    o_c = _hgrn2(cq, ck, clf, ci, cz, c_out_gain[0], lp)
    h3, xn, route = _odd_out(h, o_c, o_w_out[0], ffn_norm[1], router[0], lp, seq, tm)
    row_tok, row_dst, tile_e, n_used, n_tiles = _moe_plan(route, MOE_TILE)
    y = _moe_ffn(xn, moe_w1[0].astype(BF16), moe_w3[0].astype(BF16), moe_w2[0].astype(BF16),
                 row_tok, row_dst, tile_e, n_used, n_tiles, MOE_TILE)
    return _moe_combine(h3, y, route, batch, seq, lp, ROW_TILE)
```
